```python
import math
import jax
import jax.numpy as jnp
from jax import lax
import numpy as np

D_MODEL = 1024
BATCH = 8
SEQ = 2048
DEPTH = 1
DEC_BATCH = 32
DEC_SEQ = 32
PAST_LEN = 2048

CHUNK = 64
QBLK = 128
N_A = 8
HD_A = 64
N_B = 4
HD_B = 64
D_A = N_A * HD_A
D_B = N_B * 2 * HD_B
D_MIX = D_A + D_B
D_IN = 3 * D_A + N_A + 3 * D_B
D_FF = 2816
NORM_EPS = 1e-6
NEG_INF = -1e30
LAMBDA_STD = 0.1
ALIBI_SLOPES = (0.25, 0.0625, 0.015625, 0.00390625)

kernel_name = 'streaming_fox_diff_hybrid_step'


def rmsnorm(x, g):
    xf = x.astype(jnp.float32)
    y = xf * lax.rsqrt(jnp.mean(xf * xf, axis=-1, keepdims=True) + NORM_EPS)
    return (y * g.astype(jnp.float32)).astype(x.dtype)


def swiglu(x, w_in, w_out):
    g, u = jnp.split(x @ w_in, 2, axis=-1)
    return (jax.nn.silu(g) * u) @ w_out


def sweep_queries(block_fn, n_q):
    if n_q <= QBLK:
        return block_fn(0, n_q)
    n_blk = n_q // QBLK
    out = lax.map(lambda i: block_fn(i * QBLK, QBLK), jnp.arange(n_blk))
    out = jnp.moveaxis(out, 0, 1)
    return out.reshape(out.shape[0], n_q, *out.shape[3:])


def fox_attention(q, k, v, c_q, c_k, q_off):
    k_pos = jnp.arange(k.shape[1])
    c_k_t = jnp.swapaxes(c_k, 1, 2)
    scale = HD_A ** -0.5

    def block(start, size):
        qb = lax.dynamic_slice_in_dim(q, start, size, axis=1)
        cqb = jnp.swapaxes(lax.dynamic_slice_in_dim(c_q, start, size, axis=1), 1, 2)
        q_pos = q_off + start + jnp.arange(size)
        s = jnp.einsum('bqhd,bkhd->bhqk', qb, k).astype(jnp.float32) * scale
        s = s + (cqb[..., :, None] - c_k_t[..., None, :])
        s = jnp.where(q_pos[:, None] >= k_pos[None, :], s, NEG_INF)
        p = jax.nn.softmax(s, axis=-1).astype(v.dtype)
        return jnp.einsum('bhqk,bkhd->bqhd', p, v)

    return sweep_queries(block, q.shape[1])


def diff_attention(q, k, v, lam, q_off):
    k_pos = jnp.arange(k.shape[1])
    slopes = jnp.asarray(ALIBI_SLOPES, jnp.float32)
    scale = HD_B ** -0.5

    def block(start, size):
        qb = lax.dynamic_slice_in_dim(q, start, size, axis=1)
        q_pos = q_off + start + jnp.arange(size)
        dist = jnp.abs(q_pos[:, None] - k_pos[None, :]).astype(jnp.float32)
        bias = -slopes[:, None, None] * dist
        s = jnp.einsum('bqhed,bkhed->behqk', qb, k).astype(jnp.float32) * scale + bias
        mask = (q_pos // CHUNK)[:, None] >= (k_pos // CHUNK)[None, :]
        s = jnp.where(mask, s, NEG_INF)
        p = jax.nn.softmax(s, axis=-1)
        a = (p[:, 0] - lam * p[:, 1]).astype(v.dtype)
        return jnp.einsum('bhqk,bkhv->bqhv', a, v)

    return sweep_queries(block, q.shape[1])


def hybrid_layer(x, past_fk, past_fv, past_flf, past_dk, past_dv, lambda_init,
                 norm_ffn1, w_ffn1_in, w_ffn1_out, norm_mix, w_in, b_forget,
                 lambda_q1, lambda_k1, lambda_q2, lambda_k2, diff_subln, w_out,
                 norm_ffn2, w_ffn2_in, w_ffn2_out):
    B, T, _ = x.shape
    past = past_fk.shape[1]
    x = x + 0.5 * swiglu(rmsnorm(x, norm_ffn1), w_ffn1_in, w_ffn1_out)
    h = rmsnorm(x, norm_mix)
    proj = h @ w_in
    splits = [D_A, 2 * D_A, 3 * D_A, 3 * D_A + N_A, 3 * D_A + N_A + D_B, 3 * D_A + N_A + 2 * D_B]
    qa, ka, va, fa, qb, kb, vb = jnp.split(proj, splits, axis=-1)
    qa = qa.reshape(B, T, N_A, HD_A)
    ka = ka.reshape(B, T, N_A, HD_A)
    va = va.reshape(B, T, N_A, HD_A)
    logf = jax.nn.log_sigmoid((fa + b_forget).astype(jnp.float32))
    qb = qb.reshape(B, T, N_B, 2, HD_B)
    kb = kb.reshape(B, T, N_B, 2, HD_B)
    vb = vb.reshape(B, T, N_B, 2 * HD_B)
    k_all = jnp.concatenate([past_fk, ka], axis=1)
    v_all = jnp.concatenate([past_fv, va], axis=1)
    c_all = jnp.cumsum(jnp.concatenate([past_flf.astype(jnp.float32), logf], axis=1), axis=1)
    o_a = fox_attention(qa, k_all, v_all, c_all[:, past:], c_all, past)
    lam = (jnp.exp(jnp.sum(lambda_q1.astype(jnp.float32) * lambda_k1.astype(jnp.float32)))
           - jnp.exp(jnp.sum(lambda_q2.astype(jnp.float32) * lambda_k2.astype(jnp.float32)))
           + lambda_init)
    kb_all = jnp.concatenate([past_dk, kb], axis=1)
    vb_all = jnp.concatenate([past_dv, vb], axis=1)
    o_b = diff_attention(qb, kb_all, vb_all, lam, past)
    o_b = rmsnorm(o_b, diff_subln) * (1.0 - lambda_init)
    o = jnp.concatenate([o_a.reshape(B, T, D_A), o_b.reshape(B, T, D_B)], axis=-1)
    x = x + o @ w_out
    x = x + 0.5 * swiglu(rmsnorm(x, norm_ffn2), w_ffn2_in, w_ffn2_out)
    return x, ka, va, logf, kb, vb


def setup_inputs(seed: int = 0) -> dict:
    key = jax.random.key(seed)
    ks = jax.random.split(key, 24)
    f32 = jnp.float32

    def nrm(k, shape, scale=1.0):
        return jax.random.normal(k, shape, f32) * scale

    def gain(k, n):
        return 1.0 + 0.02 * jax.random.normal(k, (DEPTH, n), f32)

    return {
        'x_prompt': nrm(ks[0], (BATCH, SEQ, D_MODEL)),
        'x_sample': nrm(ks[1], (DEC_BATCH, DEC_SEQ, D_MODEL)),
        'cache_fox_k': nrm(ks[2], (DEPTH, DEC_BATCH, PAST_LEN, N_A, HD_A)),
        'cache_fox_v': nrm(ks[3], (DEPTH, DEC_BATCH, PAST_LEN, N_A, HD_A)),
        'cache_fox_logf': jax.nn.log_sigmoid(2.5 + nrm(ks[4], (DEPTH, DEC_BATCH, PAST_LEN, N_A))),
        'cache_diff_k': nrm(ks[5], (DEPTH, DEC_BATCH, PAST_LEN, N_B, 2, HD_B)),
        'cache_diff_v': nrm(ks[6], (DEPTH, DEC_BATCH, PAST_LEN, N_B, 2 * HD_B)),
        'norm_ffn1': gain(ks[7], D_MODEL),
        'w_ffn1_in': nrm(ks[8], (DEPTH, D_MODEL, 2 * D_FF), D_MODEL ** -0.5),
        'w_ffn1_out': nrm(ks[9], (DEPTH, D_FF, D_MODEL), D_FF ** -0.5),
        'norm_mix': gain(ks[10], D_MODEL),
        'w_in': nrm(ks[11], (DEPTH, D_MODEL, D_IN), D_MODEL ** -0.5),
        'b_forget': jax.random.uniform(ks[12], (DEPTH, N_A), f32, 1.0, 4.0),
        'lambda_q1': nrm(ks[13], (DEPTH, HD_B), LAMBDA_STD),
        'lambda_k1': nrm(ks[14], (DEPTH, HD_B), LAMBDA_STD),
        'lambda_q2': nrm(ks[15], (DEPTH, HD_B), LAMBDA_STD),
        'lambda_k2': nrm(ks[16], (DEPTH, HD_B), LAMBDA_STD),
        'diff_subln': gain(ks[17], 2 * HD_B),
        'w_out': nrm(ks[18], (DEPTH, D_MIX, D_MODEL), D_MIX ** -0.5),
        'norm_ffn2': gain(ks[19], D_MODEL),
        'w_ffn2_in': nrm(ks[20], (DEPTH, D_MODEL, 2 * D_FF), D_MODEL ** -0.5),
        'w_ffn2_out': nrm(ks[21], (DEPTH, D_FF, D_MODEL), D_FF ** -0.5),
        'norm_final': 1.0 + 0.02 * jax.random.normal(ks[22], (D_MODEL,), f32),
    }


def reference(x_prompt, x_sample, cache_fox_k, cache_fox_v, cache_fox_logf, cache_diff_k, cache_diff_v,
              norm_ffn1, w_ffn1_in, w_ffn1_out, norm_mix, w_in, b_forget,
              lambda_q1, lambda_k1, lambda_q2, lambda_k2, diff_subln, w_out,
              norm_ffn2, w_ffn2_in, w_ffn2_out, norm_final):
    B = x_prompt.shape[0]
    dt = x_prompt.dtype
    xp, xs = x_prompt, x_sample
    pk, pv, plf, pdk, pdv = [], [], [], [], []
    sk, sv, slf, sdk, sdv = [], [], [], [], []
    for l in range(DEPTH):
        lambda_init = 0.8 - 0.6 * math.exp(-0.3 * l)
        w = (norm_ffn1[l], w_ffn1_in[l], w_ffn1_out[l], norm_mix[l], w_in[l], b_forget[l],
             lambda_q1[l], lambda_k1[l], lambda_q2[l], lambda_k2[l], diff_subln[l], w_out[l],
             norm_ffn2[l], w_ffn2_in[l], w_ffn2_out[l])
        xp, ka, va, lfa, kb, vb = hybrid_layer(
            xp,
            jnp.zeros((B, 0, N_A, HD_A), dt), jnp.zeros((B, 0, N_A, HD_A), dt),
            jnp.zeros((B, 0, N_A), jnp.float32),
            jnp.zeros((B, 0, N_B, 2, HD_B), dt), jnp.zeros((B, 0, N_B, 2 * HD_B), dt),
            lambda_init, *w)
        pk.append(ka); pv.append(va); plf.append(lfa); pdk.append(kb); pdv.append(vb)
        xs, ka, va, lfa, kb, vb = hybrid_layer(
            xs, cache_fox_k[l], cache_fox_v[l], cache_fox_logf[l], cache_diff_k[l], cache_diff_v[l],
            lambda_init, *w)
        sk.append(ka); sv.append(va); slf.append(lfa); sdk.append(kb); sdv.append(vb)
    y_prompt = rmsnorm(xp, norm_final)
    y_sample = rmsnorm(xs, norm_final)
    return (y_prompt, y_sample,
            jnp.stack(pk), jnp.stack(pv), jnp.stack(plf), jnp.stack(pdk), jnp.stack(pdv),
            jnp.stack(sk), jnp.stack(sv), jnp.stack(slf), jnp.stack(sdk), jnp.stack(sdv))
```

```python
import functools
import math

import jax
import jax.numpy as jnp
from jax import lax
from jax.experimental import pallas as pl
from jax.experimental.pallas import tpu as pltpu

D_MODEL = 1024
D_FF = 2816
N_A = 8
HD_A = 64
N_B = 4
HD_B = 64
D_A = N_A * HD_A
D_B = N_B * 2 * HD_B
CHUNK = 64
NORM_EPS = 1e-6
NEG_INF = -1e30
ALIBI_SLOPES = (0.25, 0.0625, 0.015625, 0.00390625)

LANES = 128
FF_CHUNK = 1408
TOKEN_TILE = 512
ATTN_BLOCK = 256
SCAN_CHUNK = 256
VMEM_LIMIT = 56 * 1024 * 1024

F32 = jnp.float32
BF16 = jnp.bfloat16


def _params(n_axes):
    return pltpu.CompilerParams(dimension_semantics=("arbitrary",) * n_axes,
                                vmem_limit_bytes=VMEM_LIMIT)


def _resident(shape):
    return pl.BlockSpec(shape, lambda *_: (0,) * len(shape), pipeline_mode=pl.Buffered(1))


def _rms(xf, g):
    ms = jnp.mean(xf * xf, axis=-1, keepdims=True)
    return xf * lax.rsqrt(ms + NORM_EPS) * g


def _swiglu_residual(x, gn_ref, w1_ref, w2_ref):
    h = _rms(x, gn_ref[...]).astype(BF16)
    acc = None
    for c in range(D_FF // FF_CHUNK):
        lo = c * FF_CHUNK
        g = jnp.dot(h, w1_ref[:, lo:lo + FF_CHUNK], preferred_element_type=F32)
        u = jnp.dot(h, w1_ref[:, D_FF + lo:D_FF + lo + FF_CHUNK], preferred_element_type=F32)
        a = (g * jax.nn.sigmoid(g) * u).astype(BF16)
        d = jnp.dot(a, w2_ref[lo:lo + FF_CHUNK, :], preferred_element_type=F32)
        acc = d if acc is None else acc + d
    return x + 0.5 * acc


def _ffn1_kernel(x_ref, gn_ref, w1_ref, w2_ref, o_ref):
    o_ref[...] = _swiglu_residual(x_ref[...], gn_ref, w1_ref, w2_ref)


def _ffn1(x, gn, w1, w2):
    n = x.shape[0]
    tm = min(TOKEN_TILE, n)
    return pl.pallas_call(
        _ffn1_kernel,
        grid=(n // tm,),
        in_specs=[pl.BlockSpec((tm, D_MODEL), lambda i: (i, 0)),
                  _resident((1, D_MODEL)), _resident(w1.shape), _resident(w2.shape)],
        out_specs=pl.BlockSpec((tm, D_MODEL), lambda i: (i, 0)),
        out_shape=jax.ShapeDtypeStruct((n, D_MODEL), F32),
        compiler_params=_params(1),
        name="ffn1",
    )(x, gn, w1, w2)


def _post_kernel(x_ref, oa_ref, ob_ref, wo_ref, gn_ref, w1_ref, w2_ref, gf_ref, y_ref, *, final):
    x = x_ref[...]
    x = x + jnp.dot(oa_ref[...], wo_ref[:D_A, :], preferred_element_type=F32)
    x = x + jnp.dot(ob_ref[...], wo_ref[D_A:, :], preferred_element_type=F32)
    x = _swiglu_residual(x, gn_ref, w1_ref, w2_ref)
    y_ref[...] = _rms(x, gf_ref[...]) if final else x


def _post(x, oa, ob, wo, gn, w1, w2, gf, final):
    n = x.shape[0]
    tm = min(TOKEN_TILE, n)
    row = lambda i: (i, 0)
    return pl.pallas_call(
        functools.partial(_post_kernel, final=final),
        grid=(n // tm,),
        in_specs=[pl.BlockSpec((tm, D_MODEL), row), pl.BlockSpec((tm, D_A), row),
                  pl.BlockSpec((tm, D_B), row), _resident(wo.shape), _resident((1, D_MODEL)),
                  _resident(w1.shape), _resident(w2.shape), _resident((1, D_MODEL))],
        out_specs=pl.BlockSpec((tm, D_MODEL), row),
        out_shape=jax.ShapeDtypeStruct((n, D_MODEL), F32),
        compiler_params=_params(1),
        name="post",
    )(x, oa, ob, wo, gn, w1, w2, gf)


def _proj_kernel(x_ref, gn_ref, wa_ref, wf_ref, bf_ref, wb_ref,
                 ka_ref, va_ref, lf_ref, kb_ref, vb_ref,
                 qa16_ref, ka16_ref, va16_ref, qb16_ref, kb16_ref, vb16_ref):
    h = _rms(x_ref[...], gn_ref[...]).astype(BF16)
    pa = jnp.dot(h, wa_ref[...], preferred_element_type=F32)
    qa, ka, va = pa[:, :D_A], pa[:, D_A:2 * D_A], pa[:, 2 * D_A:]
    ka_ref[...] = ka
    va_ref[...] = va
    qa16_ref[...] = (qa * (HD_A ** -0.5)).astype(BF16)
    ka16_ref[...] = ka.astype(BF16)
    va16_ref[...] = va.astype(BF16)
    z = jnp.dot(h, wf_ref[...], preferred_element_type=F32) + bf_ref[...]
    logf = jnp.minimum(z, 0.0) - jnp.log1p(jnp.exp(-jnp.abs(z)))
    lf_ref[...] = logf[:, :N_A]
    pb = jnp.dot(h, wb_ref[...], preferred_element_type=F32)
    qb, kb, vb = pb[:, :D_B], pb[:, D_B:2 * D_B], pb[:, 2 * D_B:]
    kb_ref[...] = kb
    vb_ref[...] = vb
    qb16_ref[...] = (qb * (HD_B ** -0.5)).astype(BF16)
    kb16_ref[...] = kb.astype(BF16)
    vb16_ref[...] = vb.astype(BF16)


def _proj(x, gn, wa, wf, bf, wb):
    n = x.shape[0]
    tm = min(TOKEN_TILE, n)
    row = lambda i: (i, 0)
    wide = pl.BlockSpec((tm, D_A), row)
    f32o = jax.ShapeDtypeStruct((n, D_A), F32)
    b16o = jax.ShapeDtypeStruct((n, D_A), BF16)
    return pl.pallas_call(
        _proj_kernel,
        grid=(n // tm,),
        in_specs=[pl.BlockSpec((tm, D_MODEL), row), _resident((1, D_MODEL)), _resident(wa.shape),
                  _resident(wf.shape), _resident(bf.shape), _resident(wb.shape)],
        out_specs=[wide, wide, pl.BlockSpec((tm, N_A), row), wide, wide] + [wide] * 6,
        out_shape=[f32o, f32o, jax.ShapeDtypeStruct((n, N_A), F32), f32o, f32o] + [b16o] * 6,
        compiler_params=_params(1),
        name="proj",
    )(x, gn, wa, wf, bf, wb)


def _split3(x):
    hi = x.astype(BF16).astype(F32)
    r = x - hi
    mid = r.astype(BF16).astype(F32)
    lo = (r - mid).astype(BF16).astype(F32)
    return hi, mid, lo


def _cum_kernel(lf_ref, aug_ref, pad_ref, *, t):
    t_pad = pad_ref.shape[0]
    pad_ref[...] = jnp.zeros(pad_ref.shape, F32)
    pad_ref[0:t, 0:N_A] = lf_ref[0]
    r = lax.broadcasted_iota(jnp.int32, (SCAN_CHUNK, SCAN_CHUNK), 0)
    c = lax.broadcasted_iota(jnp.int32, (SCAN_CHUNK, SCAN_CHUNK), 1)
    tri = jnp.where(r >= c, 1.0, 0.0).astype(BF16)
    carry = jnp.zeros((1, LANES), F32)
    for s in range(t_pad // SCAN_CHUNK):
        lo_row = s * SCAN_CHUNK
        rows = min(SCAN_CHUNK, t - lo_row)
        hi, mid, lo = _split3(pad_ref[lo_row:lo_row + SCAN_CHUNK, :])
        cs = (jnp.dot(tri, hi.astype(BF16), preferred_element_type=F32)
              + jnp.dot(tri, mid.astype(BF16), preferred_element_type=F32)
              + jnp.dot(tri, lo.astype(BF16), preferred_element_type=F32)) + carry
        carry = cs[SCAN_CHUNK - 1:SCAN_CHUNK, :]
        nhi, nmid, nlo = _split3(-cs)
        aug = nhi + pltpu.roll(nmid, N_A, axis=1) + pltpu.roll(nlo, 2 * N_A, axis=1)
        aug_ref[0, lo_row:lo_row + rows, :] = aug[:rows].astype(BF16)


def _cum_aug(logf):
    b, t, _ = logf.shape
    t_pad = -(-t // SCAN_CHUNK) * SCAN_CHUNK
    return pl.pallas_call(
        functools.partial(_cum_kernel, t=t),
        grid=(b,),
        in_specs=[pl.BlockSpec((1, t, N_A), lambda i: (i, 0, 0))],
        out_specs=pl.BlockSpec((1, t, LANES), lambda i: (i, 0, 0)),
        out_shape=jax.ShapeDtypeStruct((b, t, LANES), BF16),
        scratch_shapes=[pltpu.VMEM((t_pad, LANES), F32)],
        compiler_params=_params(1),
        name="cum_aug",
    )(logf)


def _lane_halves(q):
    lane = lax.broadcasted_iota(jnp.int32, q.shape, 1)
    zero = jnp.zeros_like(q)
    return jnp.where(lane < HD_A, q, zero), jnp.where(lane >= HD_A, q, zero)


def _forget_selector(shape, head):
    lane = lax.broadcasted_iota(jnp.int32, shape, 1)
    hit = (lane == head) | (lane == N_A + head) | (lane == 2 * N_A + head)
    return jnp.where(hit, 1.0, 0.0).astype(BF16)


def _nt_dot(a, b):
    return lax.dot_general(a, b, (((1,), (1,)), ((), ())), preferred_element_type=F32)


def _softmax_step(s, v, m_ref, l_ref, acc_ref):
    m_prev = m_ref[...]
    m_new = jnp.maximum(m_prev, jnp.max(s, axis=1, keepdims=True))
    alpha = jnp.exp(m_prev - m_new)
    p = jnp.exp(s - m_new)
    l_ref[...] = alpha * l_ref[...] + jnp.sum(p, axis=1, keepdims=True)
    acc_ref[...] = alpha * acc_ref[...] + jnp.dot(p.astype(BF16), v, preferred_element_type=F32)
    m_ref[...] = m_new


def _softmax_init(m_ref, l_ref, acc_ref):
    m_ref[...] = jnp.full(m_ref.shape, NEG_INF, F32)
    l_ref[...] = jnp.zeros(l_ref.shape, F32)
    acc_ref[...] = jnp.zeros(acc_ref.shape, F32)


def _slope(head):
    s = jnp.float32(ALIBI_SLOPES[N_B - 1])
    for h in range(N_B - 2, -1, -1):
        s = jnp.where(head == h, jnp.float32(ALIBI_SLOPES[h]), s)
    return s


def _alibi_shifted(slope, q_pos, k_pos):
    return slope * (k_pos - 2.0 * jnp.maximum(k_pos - q_pos, 0.0))


def _diff_finish(o1, o2, lq1_ref, lk1_ref, lq2_ref, lk2_ref, sub_ref, lambda_init):
    lam = (jnp.exp(jnp.sum(lq1_ref[...] * lk1_ref[...], axis=1, keepdims=True))
           - jnp.exp(jnp.sum(lq2_ref[...] * lk2_ref[...], axis=1, keepdims=True)) + lambda_init)
    o = o1 - lam * o2
    return _rms(o, sub_ref[...]) * (1.0 - lambda_init)


def _fox_prompt_kernel(q_ref, k_ref, v_ref, aug_ref, o_ref, qs_ref, m_ref, l_ref, acc_ref):
    pair, i = pl.program_id(1), pl.program_id(2)
    tq = q_ref.shape[0]
    qa, qb = _lane_halves(q_ref[...])
    qs_ref[0:tq, 0:LANES] = qa
    qs_ref[tq:, 0:LANES] = qb
    qs_ref[0:tq, LANES:] = _forget_selector((tq, LANES), 2 * pair)
    qs_ref[tq:, LANES:] = _forget_selector((tq, LANES), 2 * pair + 1)
    _softmax_init(m_ref, l_ref, acc_ref)

    def scores(j):
        rows = pl.ds(pl.multiple_of(j * tq, tq), tq)
        kk = jnp.concatenate([k_ref[rows, :], aug_ref[0, rows, :]], axis=1)
        return _nt_dot(qs_ref[...], kk), v_ref[rows, :]

    def body(j, carry):
        s, v = scores(j)
        _softmax_step(s, v, m_ref, l_ref, acc_ref)
        return carry

    lax.fori_loop(0, i, body, 0)
    s, v = scores(i)
    r = lax.broadcasted_iota(jnp.int32, s.shape, 0)
    c = lax.broadcasted_iota(jnp.int32, s.shape, 1)
    q_idx = jnp.where(r >= tq, r - tq, r)
    s = jnp.where(q_idx >= c, s, NEG_INF)
    _softmax_step(s, v, m_ref, l_ref, acc_ref)
    o = acc_ref[...] / l_ref[...]
    lane = lax.broadcasted_iota(jnp.int32, (tq, LANES), 1)
    o_ref[...] = jnp.where(lane < HD_A, o[:tq], o[tq:]).astype(BF16)


def _fox_prompt(q16, k16, v16, aug, b, t):
    tq = ATTN_BLOCK
    nq = t // tq
    return pl.pallas_call(
        _fox_prompt_kernel,
        grid=(b, N_A // 2, nq),
        in_specs=[pl.BlockSpec((tq, LANES), lambda bi, p, i: (bi * nq + i, p)),
                  pl.BlockSpec((t, LANES), lambda bi, p, i: (bi, p)),
                  pl.BlockSpec((t, LANES), lambda bi, p, i: (bi, p)),
                  pl.BlockSpec((1, t, LANES), lambda bi, p, i: (bi, 0, 0))],
        out_specs=pl.BlockSpec((tq, LANES), lambda bi, p, i: (bi * nq + i, p)),
        out_shape=jax.ShapeDtypeStruct((b * t, D_A), BF16),
        scratch_shapes=[pltpu.VMEM((2 * tq, 2 * LANES), BF16), pltpu.VMEM((2 * tq, 1), F32),
                        pltpu.VMEM((2 * tq, 1), F32), pltpu.VMEM((2 * tq, LANES), F32)],
        compiler_params=_params(3),
        name="fox_prompt",
    )(q16, k16, v16, aug)


def _diff_prompt_kernel(q_ref, k_ref, v_ref, lq1_ref, lk1_ref, lq2_ref, lk2_ref, sub_ref, o_ref,
                        qs_ref, m_ref, l_ref, acc_ref, *, lambda_init):
    head, i = pl.program_id(1), pl.program_id(2)
    tq = q_ref.shape[0]
    q1, q2 = _lane_halves(q_ref[...])
    qs_ref[0:tq, :] = q1
    qs_ref[tq:, :] = q2
    _softmax_init(m_ref, l_ref, acc_ref)
    slope = _slope(head)

    def scores(j):
        rows = pl.ds(pl.multiple_of(j * tq, tq), tq)
        return _nt_dot(qs_ref[...], k_ref[rows, :]), v_ref[rows, :]

    def body(j, carry):
        s, v = scores(j)
        k_pos = (j * tq + lax.broadcasted_iota(jnp.int32, (1, tq), 1)).astype(F32)
        _softmax_step(s + slope * k_pos, v, m_ref, l_ref, acc_ref)
        return carry

    lax.fori_loop(0, i, body, 0)
    s, v = scores(i)
    r = lax.broadcasted_iota(jnp.int32, s.shape, 0)
    c = lax.broadcasted_iota(jnp.int32, s.shape, 1)
    q_idx = jnp.where(r >= tq, r - tq, r)
    q_pos = (i * tq + q_idx).astype(F32)
    k_pos = (i * tq + c).astype(F32)
    visible = (q_idx // CHUNK) >= (c // CHUNK)
    s = jnp.where(visible, s + _alibi_shifted(slope, q_pos, k_pos), NEG_INF)
    _softmax_step(s, v, m_ref, l_ref, acc_ref)
    o = acc_ref[...] / l_ref[...]
    o_ref[...] = _diff_finish(o[:tq], o[tq:], lq1_ref, lk1_ref, lq2_ref, lk2_ref, sub_ref,
                              lambda_init).astype(BF16)


def _diff_prompt(q16, k16, v16, lams, sub, b, t, lambda_init):
    tq = ATTN_BLOCK
    nq = t // tq
    small = [_resident((1, HD_B))] * 4 + [_resident((1, 2 * HD_B))]
    return pl.pallas_call(
        functools.partial(_diff_prompt_kernel, lambda_init=lambda_init),
        grid=(b, N_B, nq),
        in_specs=[pl.BlockSpec((tq, LANES), lambda bi, h, i: (bi * nq + i, h)),
                  pl.BlockSpec((t, LANES), lambda bi, h, i: (bi, h)),
                  pl.BlockSpec((t, LANES), lambda bi, h, i: (bi, h))] + small,
        out_specs=pl.BlockSpec((tq, LANES), lambda bi, h, i: (bi * nq + i, h)),
        out_shape=jax.ShapeDtypeStruct((b * t, D_B), BF16),
        scratch_shapes=[pltpu.VMEM((2 * tq, LANES), BF16), pltpu.VMEM((2 * tq, 1), F32),
                        pltpu.VMEM((2 * tq, 1), F32), pltpu.VMEM((2 * tq, LANES), F32)],
        compiler_params=_params(3),
        name="diff_prompt",
    )(q16, k16, v16, *lams, sub)


def _softmax_once(s, v):
    m = jnp.max(s, axis=1, keepdims=True)
    p = jnp.exp(s - m)
    return jnp.dot(p.astype(BF16), v, preferred_element_type=F32) / jnp.sum(p, axis=1, keepdims=True)


def _fill_keys(kk_ref, vv_ref, ck_ref, cv_ref, kn_ref, vn_ref, past, t_new):
    t_all = past + t_new
    kk_ref[0:past, 0:LANES] = ck_ref[0].astype(BF16)
    kk_ref[past:t_all, 0:LANES] = kn_ref[...]
    kk_ref[t_all:, :] = jnp.zeros((kk_ref.shape[0] - t_all, kk_ref.shape[1]), BF16)
    vv_ref[0:past, :] = cv_ref[0].astype(BF16)
    vv_ref[past:t_all, :] = vn_ref[...]
    vv_ref[t_all:, :] = jnp.zeros((vv_ref.shape[0] - t_all, LANES), BF16)


def _fox_cached_kernel(q_ref, ck_ref, cv_ref, kn_ref, vn_ref, aug_ref, o_ref, kk_ref, vv_ref):
    pair = pl.program_id(1)
    tq = q_ref.shape[0]
    past = ck_ref.shape[1]
    t_all = past + tq
    _fill_keys(kk_ref, vv_ref, ck_ref, cv_ref, kn_ref, vn_ref, past, tq)
    kk_ref[0:t_all, LANES:] = aug_ref[0]
    qa, qb = _lane_halves(q_ref[...])
    qs = jnp.concatenate(
        [jnp.concatenate([qa, _forget_selector((tq, LANES), 2 * pair)], axis=1),
         jnp.concatenate([qb, _forget_selector((tq, LANES), 2 * pair + 1)], axis=1)], axis=0)
    s = _nt_dot(qs, kk_ref[...])
    r = lax.broadcasted_iota(jnp.int32, s.shape, 0)
    c = lax.broadcasted_iota(jnp.int32, s.shape, 1)
    q_pos = past + jnp.where(r >= tq, r - tq, r)
    s = jnp.where(q_pos >= c, s, NEG_INF)
    o = _softmax_once(s, vv_ref[...])
    lane = lax.broadcasted_iota(jnp.int32, (tq, LANES), 1)
    o_ref[...] = jnp.where(lane < HD_A, o[:tq], o[tq:]).astype(BF16)


def _fox_cached(q16, ck, cv, kn16, vn16, aug, b, tq):
    past = ck.shape[1]
    t_pad = -(-(past + tq) // ATTN_BLOCK) * ATTN_BLOCK
    new = pl.BlockSpec((tq, LANES), lambda bi, p: (bi, p))
    old = pl.BlockSpec((1, past, LANES), lambda bi, p: (bi, 0, p))
    return pl.pallas_call(
        _fox_cached_kernel,
        grid=(b, N_A // 2),
        in_specs=[new, old, old, new, new,
                  pl.BlockSpec((1, past + tq, LANES), lambda bi, p: (bi, 0, 0))],
        out_specs=new,
        out_shape=jax.ShapeDtypeStruct((b * tq, D_A), BF16),
        scratch_shapes=[pltpu.VMEM((t_pad, 2 * LANES), BF16), pltpu.VMEM((t_pad, LANES), BF16)],
        compiler_params=_params(2),
        name="fox_cached",
    )(q16, ck, cv, kn16, vn16, aug)


def _diff_cached_kernel(q_ref, ck_ref, cv_ref, kn_ref, vn_ref, lq1_ref, lk1_ref, lq2_ref, lk2_ref,
                        sub_ref, o_ref, kk_ref, vv_ref, *, lambda_init):
    head = pl.program_id(1)
    tq = q_ref.shape[0]
    past = ck_ref.shape[1]
    t_all = past + tq
    _fill_keys(kk_ref, vv_ref, ck_ref, cv_ref, kn_ref, vn_ref, past, tq)
    q1, q2 = _lane_halves(q_ref[...])
    s = _nt_dot(jnp.concatenate([q1, q2], axis=0), kk_ref[...])
    r = lax.broadcasted_iota(jnp.int32, s.shape, 0)
    c = lax.broadcasted_iota(jnp.int32, s.shape, 1)
    q_idx = past + jnp.where(r >= tq, r - tq, r)
    visible = ((q_idx // CHUNK) >= (c // CHUNK)) & (c < t_all)
    bias = _alibi_shifted(_slope(head), q_idx.astype(F32), c.astype(F32))
    o = _softmax_once(jnp.where(visible, s + bias, NEG_INF), vv_ref[...])
    o_ref[...] = _diff_finish(o[:tq], o[tq:], lq1_ref, lk1_ref, lq2_ref, lk2_ref, sub_ref,
                              lambda_init).astype(BF16)


def _diff_cached(q16, ck, cv, kn16, vn16, lams, sub, b, tq, lambda_init):
    past = ck.shape[1]
    t_pad = -(-(past + tq) // ATTN_BLOCK) * ATTN_BLOCK
    new = pl.BlockSpec((tq, LANES), lambda bi, h: (bi, h))
    old = pl.BlockSpec((1, past, LANES), lambda bi, h: (bi, 0, h))
    small = [_resident((1, HD_B))] * 4 + [_resident((1, 2 * HD_B))]
    return pl.pallas_call(
        functools.partial(_diff_cached_kernel, lambda_init=lambda_init),
        grid=(b, N_B),
        in_specs=[new, old, old, new, new] + small,
        out_specs=new,
        out_shape=jax.ShapeDtypeStruct((b * tq, D_B), BF16),
        scratch_shapes=[pltpu.VMEM((t_pad, LANES), BF16), pltpu.VMEM((t_pad, LANES), BF16)],
        compiler_params=_params(2),
        name="diff_cached",
    )(q16, ck, cv, kn16, vn16, *lams, sub)


def _layer(x, cache, w, lambda_init):
    b, t, _ = x.shape
    n = b * t
    x1 = _ffn1(x.reshape(n, D_MODEL), w["gn1"], w["w1a"], w["w2a"])
    (ka, va, logf, kb, vb, qa16, ka16, va16, qb16, kb16, vb16) = _proj(
        x1, w["gmix"], w["wa"], w["wf"], w["bf"], w["wb"])
    logf = logf.reshape(b, t, N_A)
    if cache is None:
        aug = _cum_aug(logf)
        oa = _fox_prompt(qa16, ka16, va16, aug, b, t)
        ob = _diff_prompt(qb16, kb16, vb16, w["lams"], w["sub"], b, t, lambda_init)
    else:
        cfk, cfv, cflf, cdk, cdv = cache
        past = cfk.shape[1]
        aug = _cum_aug(jnp.concatenate([cflf.astype(F32), logf], axis=1))
        oa = _fox_cached(qa16, cfk.reshape(b, past, D_A), cfv.reshape(b, past, D_A), ka16, va16, aug, b, t)
        ob = _diff_cached(qb16, cdk.reshape(b, past, D_B), cdv.reshape(b, past, D_B), kb16, vb16,
                          w["lams"], w["sub"], b, t, lambda_init)
    news = (ka.reshape(b, t, N_A, HD_A), va.reshape(b, t, N_A, HD_A), logf,
            kb.reshape(b, t, N_B, 2, HD_B), vb.reshape(b, t, N_B, 2 * HD_B))
    return x1, oa, ob, news


def kernel(x_prompt, x_sample, cache_fox_k, cache_fox_v, cache_fox_logf, cache_diff_k, cache_diff_v,
           norm_ffn1, w_ffn1_in, w_ffn1_out, norm_mix, w_in, b_forget,
           lambda_q1, lambda_k1, lambda_q2, lambda_k2, diff_subln, w_out,
           norm_ffn2, w_ffn2_in, w_ffn2_out, norm_final):
    depth = w_in.shape[0]
    xp, xs = x_prompt, x_sample
    outs_p, outs_s = [], []
    gfin = norm_final.reshape(1, D_MODEL)
    for l in range(depth):
        lambda_init = 0.8 - 0.6 * math.exp(-0.3 * l)
        f_lo, f_hi = 3 * D_A, 3 * D_A + N_A
        w = {
            "gn1": norm_ffn1[l].reshape(1, D_MODEL),
            "w1a": w_ffn1_in[l].astype(BF16), "w2a": w_ffn1_out[l].astype(BF16),
            "gmix": norm_mix[l].reshape(1, D_MODEL),
            "wa": w_in[l][:, :f_lo].astype(BF16),
            "wf": jnp.pad(w_in[l][:, f_lo:f_hi], ((0, 0), (0, LANES - N_A))).astype(BF16),
            "bf": jnp.pad(b_forget[l], (0, LANES - N_A)).reshape(1, LANES),
            "wb": w_in[l][:, f_hi:].astype(BF16),
            "lams": [v[l].reshape(1, HD_B) for v in (lambda_q1, lambda_k1, lambda_q2, lambda_k2)],
            "sub": diff_subln[l].reshape(1, 2 * HD_B),
            "wo": w_out[l].astype(BF16),
            "gn2": norm_ffn2[l].reshape(1, D_MODEL),
            "w1b": w_ffn2_in[l].astype(BF16), "w2b": w_ffn2_out[l].astype(BF16),
        }
        streams = []
        for x, cache in ((xp, None),
                         (xs, (cache_fox_k[l], cache_fox_v[l], cache_fox_logf[l],
                               cache_diff_k[l], cache_diff_v[l]))):
            b, t, _ = x.shape
            x1, oa, ob, news = _layer(x, cache, w, lambda_init)
            y = _post(x1, oa, ob, w["wo"], w["gn2"], w["w1b"], w["w2b"], gfin, l == depth - 1)
            streams.append((y.reshape(b, t, D_MODEL), news))
        (xp, news_p), (xs, news_s) = streams
        outs_p.append(news_p)
        outs_s.append(news_s)
    stack = lambda outs, i: jnp.stack([o[i] for o in outs])
    return (xp, xs) + tuple(stack(outs_p, i) for i in range(5)) + tuple(stack(outs_s, i) for i in range(5))
```

```python
import functools
import math

import jax
import jax.numpy as jnp
from jax import lax
from jax.experimental import pallas as pl
from jax.experimental.pallas import tpu as pltpu

D_MODEL = 1024
D_FF = 2816
N_A = 8
HD_A = 64
N_B = 4
HD_B = 64
D_A = N_A * HD_A
D_B = N_B * 2 * HD_B
CHUNK = 64
NORM_EPS = 1e-6
NEG_INF = -1e30
ALIBI_SLOPES = (0.25, 0.0625, 0.015625, 0.00390625)

LANES = 128
FF_CHUNK = 1408
TOKEN_TILE = 512
ATTN_BLOCK = 256
AUG_ROWS = 32
VMEM_LIMIT = 56 * 1024 * 1024
N_PAIRS = N_A // 2

F32 = jnp.float32
BF16 = jnp.bfloat16


def _params(n_axes):
    return pltpu.CompilerParams(dimension_semantics=("arbitrary",) * n_axes,
                                vmem_limit_bytes=VMEM_LIMIT)


def _resident(shape):
    return pl.BlockSpec(shape, lambda *_: (0,) * len(shape), pipeline_mode=pl.Buffered(1))


def _rms(xf, g):
    ms = jnp.mean(xf * xf, axis=-1, keepdims=True)
    return xf * lax.rsqrt(ms + NORM_EPS) * g


def _swiglu_residual(x, gn_ref, w1_ref, w2_ref):
    h = _rms(x, gn_ref[...]).astype(BF16)
    acc = None
    for c in range(D_FF // FF_CHUNK):
        lo = c * FF_CHUNK
        g = jnp.dot(h, w1_ref[:, lo:lo + FF_CHUNK], preferred_element_type=F32)
        u = jnp.dot(h, w1_ref[:, D_FF + lo:D_FF + lo + FF_CHUNK], preferred_element_type=F32)
        a = (g * jax.nn.sigmoid(g) * u).astype(BF16)
        d = jnp.dot(a, w2_ref[lo:lo + FF_CHUNK, :], preferred_element_type=F32)
        acc = d if acc is None else acc + d
    return x + 0.5 * acc


def _ffn1_kernel(x_ref, gn_ref, w1_ref, w2_ref, o_ref):
    o_ref[...] = _swiglu_residual(x_ref[...], gn_ref, w1_ref, w2_ref)


def _ffn1(x, gn, w1, w2):
    n = x.shape[0]
    tm = min(TOKEN_TILE, n)
    return pl.pallas_call(
        _ffn1_kernel,
        grid=(n // tm,),
        in_specs=[pl.BlockSpec((tm, D_MODEL), lambda i: (i, 0)),
                  _resident((1, D_MODEL)), _resident(w1.shape), _resident(w2.shape)],
        out_specs=pl.BlockSpec((tm, D_MODEL), lambda i: (i, 0)),
        out_shape=jax.ShapeDtypeStruct((n, D_MODEL), F32),
        compiler_params=_params(1),
        name="ffn1",
    )(x, gn, w1, w2)


def _post_kernel(x_ref, oa_ref, ob_ref, wo_ref, gn_ref, w1_ref, w2_ref, gf_ref, y_ref, *, final):
    x = x_ref[...]
    x = x + jnp.dot(oa_ref[...], wo_ref[:D_A, :], preferred_element_type=F32)
    x = x + jnp.dot(ob_ref[...], wo_ref[D_A:, :], preferred_element_type=F32)
    x = _swiglu_residual(x, gn_ref, w1_ref, w2_ref)
    y_ref[...] = _rms(x, gf_ref[...]) if final else x


def _post(x, oa, ob, wo, gn, w1, w2, gf, final):
    n = x.shape[0]
    tm = min(TOKEN_TILE, n)
    row = lambda i: (i, 0)
    return pl.pallas_call(
        functools.partial(_post_kernel, final=final),
        grid=(n // tm,),
        in_specs=[pl.BlockSpec((tm, D_MODEL), row), pl.BlockSpec((tm, D_A), row),
                  pl.BlockSpec((tm, D_B), row), _resident(wo.shape), _resident((1, D_MODEL)),
                  _resident(w1.shape), _resident(w2.shape), _resident((1, D_MODEL))],
        out_specs=pl.BlockSpec((tm, D_MODEL), row),
        out_shape=jax.ShapeDtypeStruct((n, D_MODEL), F32),
        compiler_params=_params(1),
        name="post",
    )(x, oa, ob, wo, gn, w1, w2, gf)


def _with_ones(v, width):
    tm = v.shape[0]
    lane = lax.broadcasted_iota(jnp.int32, (tm, width), 1)
    ones = jnp.where(lane == 0, 1.0, 0.0).astype(BF16)
    parts = []
    for g in range(v.shape[1] // width):
        parts += [v[:, g * width:(g + 1) * width].astype(BF16), ones]
    return jnp.concatenate(parts, axis=1)


def _proj_kernel(x_ref, gn_ref, wa_ref, wf_ref, bf_ref, wb_ref,
                 kat_ref, vat_ref, lft_ref, kbt_ref, vb_ref,
                 qa16_ref, kat16_ref, va16_ref, qb16_ref, kbt16_ref, vb16_ref, *, seg, kblk):
    tm = x_ref.shape[0]
    h = _rms(x_ref[...], gn_ref[...]).astype(BF16)
    pa = jnp.dot(h, wa_ref[...], preferred_element_type=F32)
    pb = jnp.dot(h, wb_ref[...], preferred_element_type=F32)
    z = jnp.dot(h, wf_ref[...], preferred_element_type=F32) + bf_ref[...]
    logf = jnp.minimum(z, 0.0) - jnp.log1p(jnp.exp(-jnp.abs(z)))
    qa16_ref[...] = (pa[:, :D_A] * (HD_A ** -0.5)).astype(BF16)
    qb16_ref[...] = (pb[:, :D_B] * (HD_B ** -0.5)).astype(BF16)
    va16_ref[...] = _with_ones(pa[:, 2 * D_A:], LANES)
    vb = pb[:, 2 * D_B:]
    vb16_ref[...] = _with_ones(vb, LANES)
    for hh in range(N_B):
        vb_ref[pl.ds(hh, tm, stride=N_B), :] = vb[:, hh * LANES:(hh + 1) * LANES]
    kat = pa[:, D_A:2 * D_A].T
    vat = pa[:, 2 * D_A:].T
    kbt = pb[:, D_B:2 * D_B].T
    lft = logf.T
    for bb in range(tm // seg):
        cols = slice(bb * seg, (bb + 1) * seg)
        kat_ref[bb] = kat[:, cols]
        vat_ref[bb] = vat[:, cols]
        kbt_ref[bb] = kbt[:, cols]
        lft_ref[bb] = lft[:N_A, cols]
        for jj in range(seg // kblk):
            kc = slice(bb * seg + jj * kblk, bb * seg + (jj + 1) * kblk)
            kat16_ref[bb, jj] = kat[:, kc].astype(BF16)
            kbt16_ref[bb, jj] = kbt[:, kc].astype(BF16)


def _proj(x, gn, wa, wf, bf, wb, b, t):
    n = x.shape[0]
    tm = min(TOKEN_TILE, n)
    seg = min(tm, t)
    nb = tm // seg
    nt = t // seg
    kblk = min(ATTN_BLOCK, seg)
    row = lambda i: (i, 0)
    tok = lambda i: (i // nt, 0, i % nt)
    blk = lambda i: (i // nt, i % nt, 0, 0)
    f32t = jax.ShapeDtypeStruct((b, D_A, t), F32)
    b16t = jax.ShapeDtypeStruct((b, t // kblk, D_A, kblk), BF16)
    tspec = pl.BlockSpec((nb, D_A, seg), tok)
    kspec = pl.BlockSpec((nb, seg // kblk, D_A, kblk), blk)
    return pl.pallas_call(
        functools.partial(_proj_kernel, seg=seg, kblk=kblk),
        grid=(n // tm,),
        in_specs=[pl.BlockSpec((tm, D_MODEL), row), _resident((1, D_MODEL)), _resident(wa.shape),
                  _resident(wf.shape), _resident(bf.shape), _resident(wb.shape)],
        out_specs=[tspec, tspec, pl.BlockSpec((nb, N_A, seg), tok), tspec,
                   pl.BlockSpec((tm * N_B, LANES), row),
                   pl.BlockSpec((tm, D_A), row), kspec, pl.BlockSpec((tm, 2 * D_A), row),
                   pl.BlockSpec((tm, D_B), row), kspec, pl.BlockSpec((tm, 2 * D_B), row)],
        out_shape=[f32t, f32t, jax.ShapeDtypeStruct((b, N_A, t), F32), f32t,
                   jax.ShapeDtypeStruct((n * N_B, LANES), F32),
                   jax.ShapeDtypeStruct((n, D_A), BF16), b16t, jax.ShapeDtypeStruct((n, 2 * D_A), BF16),
                   jax.ShapeDtypeStruct((n, D_B), BF16), b16t, jax.ShapeDtypeStruct((n, 2 * D_B), BF16)],
        compiler_params=_params(1),
        name="proj",
    )(x, gn, wa, wf, bf, wb)


def _split3(x):
    hi = x.astype(BF16).astype(F32)
    r = x - hi
    mid = r.astype(BF16).astype(F32)
    lo = (r - mid).astype(BF16).astype(F32)
    return hi, mid, lo


def _cum_kernel(lft_ref, aug_ref, pad_ref, *, t, blocked):
    w = ATTN_BLOCK
    pad_ref[...] = jnp.zeros(pad_ref.shape, F32)
    pad_ref[:, 0:t] = lft_ref[0]
    r = lax.broadcasted_iota(jnp.int32, (w, w), 0)
    c = lax.broadcasted_iota(jnp.int32, (w, w), 1)
    tri = jnp.where(r <= c, 1.0, 0.0).astype(BF16)
    zero = jnp.zeros((N_A, w), F32)
    carry = jnp.zeros((N_A, 1), F32)
    for s in range(pad_ref.shape[1] // w):
        lo_col = s * w
        cols = min(w, t - lo_col)
        hi, mid, lo = _split3(pad_ref[:, lo_col:lo_col + w])
        x = jnp.concatenate([hi, mid, lo, zero], axis=0).astype(BF16)
        y = jnp.dot(x, tri, preferred_element_type=F32)
        cs = y[0:N_A] + y[N_A:2 * N_A] + y[2 * N_A:3 * N_A] + carry
        carry = cs[:, w - 1:w]
        nhi, nmid, nlo = _split3(-cs)
        aug = jnp.concatenate([nhi, nmid, nlo, zero], axis=0).astype(BF16)
        if blocked:
            aug_ref[0, s] = aug
        else:
            aug_ref[0, :, lo_col:lo_col + cols] = aug[:, :cols]


def _cum_aug(lft, blocked):
    b, _, t = lft.shape
    w = ATTN_BLOCK
    t_pad = -(-t // w) * w
    if blocked:
        out_shape = jax.ShapeDtypeStruct((b, t // w, AUG_ROWS, w), BF16)
        out_spec = pl.BlockSpec((1, t // w, AUG_ROWS, w), lambda i: (i, 0, 0, 0))
    else:
        out_shape = jax.ShapeDtypeStruct((b, AUG_ROWS, t), BF16)
        out_spec = pl.BlockSpec((1, AUG_ROWS, t), lambda i: (i, 0, 0))
    return pl.pallas_call(
        functools.partial(_cum_kernel, t=t, blocked=blocked),
        grid=(b,),
        in_specs=[pl.BlockSpec((1, N_A, t), lambda i: (i, 0, 0))],
        out_specs=out_spec,
        out_shape=out_shape,
        scratch_shapes=[pltpu.VMEM((N_A, t_pad), F32)],
        compiler_params=_params(1),
        name="cum_aug",
    )(lft)


def _lane_halves(q):
    lane = lax.broadcasted_iota(jnp.int32, q.shape, 1)
    zero = jnp.zeros_like(q)
    return jnp.where(lane < HD_A, q, zero), jnp.where(lane >= HD_A, q, zero)


def _forget_selector(shape, head):
    lane = lax.broadcasted_iota(jnp.int32, shape, 1)
    hit = (lane == head) | (lane == N_A + head) | (lane == 2 * N_A + head)
    return jnp.where(hit, 1.0, 0.0).astype(BF16)


def _alibi_shifted(slope, q_pos, k_pos):
    return slope * (k_pos - 2.0 * jnp.maximum(k_pos - q_pos, 0.0))


def _diff_finish(o1, o2, lam, sub_ref, lambda_init):
    return _rms(o1 - lam * o2, sub_ref[...]) * (1.0 - lambda_init)


def _lambda(lq1_ref, lk1_ref, lq2_ref, lk2_ref, lambda_init):
    return (jnp.exp(jnp.sum(lq1_ref[...] * lk1_ref[...], axis=1, keepdims=True))
            - jnp.exp(jnp.sum(lq2_ref[...] * lk2_ref[...], axis=1, keepdims=True)) + lambda_init)


def _attn_prompt_kernel(qa_ref, qb_ref, kat_ref, kbt_ref, va_ref, vb_ref, aug_ref,
                        lq1_ref, lk1_ref, lq2_ref, lk2_ref, sub_ref, oa_ref, ob_ref,
                        qsa_ref, qsb_ref, m_ref, acc_ref, *, lambda_init):
    i = pl.program_id(1)
    tq = qa_ref.shape[0]
    n_chain = N_PAIRS + N_B
    for p in range(N_PAIRS):
        lo, hi = _lane_halves(qa_ref[:, p * LANES:(p + 1) * LANES])
        qsa_ref[p, 0:tq, 0:LANES] = lo
        qsa_ref[p, tq:, 0:LANES] = hi
        qsa_ref[p, 0:tq, LANES:] = _forget_selector((tq, LANES), 2 * p)
        qsa_ref[p, tq:, LANES:] = _forget_selector((tq, LANES), 2 * p + 1)
    for h in range(N_B):
        lo, hi = _lane_halves(qb_ref[:, h * LANES:(h + 1) * LANES])
        qsb_ref[h, 0:tq, :] = lo
        qsb_ref[h, tq:, :] = hi
    m_ref[...] = jnp.full(m_ref.shape, NEG_INF, F32)
    acc_ref[...] = jnp.zeros(acc_ref.shape, F32)

    def step(j, diag):
        rows = pl.ds(pl.multiple_of(j * tq, tq), tq)
        aug = jnp.concatenate([aug_ref[0, j], jnp.zeros((LANES - AUG_ROWS, tq), BF16)], axis=0)
        k_pos = (j * tq + lax.broadcasted_iota(jnp.int32, (1, tq), 1)).astype(F32)
        if diag:
            r = lax.broadcasted_iota(jnp.int32, (2 * tq, tq), 0)
            c = lax.broadcasted_iota(jnp.int32, (2 * tq, tq), 1)
            q_idx = jnp.where(r >= tq, r - tq, r)
            causal = q_idx >= c
            visible = (q_idx // CHUNK) >= (c // CHUNK)
            shift = (c - 2 * jnp.maximum(c - q_idx, 0)).astype(F32) + (j * tq).astype(F32)
        for ch in range(n_chain):
            if ch < N_PAIRS:
                kk = jnp.concatenate([kat_ref[0, j, ch * LANES:(ch + 1) * LANES, :], aug], axis=0)
                s = jnp.dot(qsa_ref[ch], kk, preferred_element_type=F32)
                v = va_ref[rows, ch * 2 * LANES:(ch + 1) * 2 * LANES]
                if diag:
                    s = jnp.where(causal, s, NEG_INF)
            else:
                h = ch - N_PAIRS
                s = jnp.dot(qsb_ref[h], kbt_ref[0, j, h * LANES:(h + 1) * LANES, :],
                            preferred_element_type=F32)
                v = vb_ref[rows, h * 2 * LANES:(h + 1) * 2 * LANES]
                if diag:
                    s = jnp.where(visible, s + ALIBI_SLOPES[h] * shift, NEG_INF)
                else:
                    s = s + ALIBI_SLOPES[h] * k_pos
            m_prev = m_ref[ch]
            m_new = jnp.maximum(m_prev, jnp.max(s, axis=1, keepdims=True))
            alpha = jnp.exp(m_prev - m_new)
            p = jnp.exp(s - jnp.concatenate([m_new, m_new], axis=1))
            pv = jnp.dot(p.astype(BF16), v, preferred_element_type=F32)
            acc_ref[ch] = jnp.concatenate([alpha, alpha], axis=1) * acc_ref[ch] + pv
            m_ref[ch] = m_new

    def body(j, carry):
        step(j, False)
        return carry

    lax.fori_loop(0, i, body, 0)
    step(i, True)

    lane = lax.broadcasted_iota(jnp.int32, (tq, LANES), 1)
    lam = _lambda(lq1_ref, lk1_ref, lq2_ref, lk2_ref, lambda_init)
    for ch in range(n_chain):
        acc = acc_ref[ch]
        o = acc[:, :LANES] / acc[:, LANES:LANES + 1]
        if ch < N_PAIRS:
            oa_ref[:, ch * LANES:(ch + 1) * LANES] = jnp.where(lane < HD_A, o[:tq], o[tq:]).astype(BF16)
        else:
            h = ch - N_PAIRS
            ob_ref[:, h * LANES:(h + 1) * LANES] = _diff_finish(
                o[:tq], o[tq:], lam, sub_ref, lambda_init).astype(BF16)


def _attn_prompt(qa16, qb16, kat16, kbt16, va16, vb16, aug, lams, sub, b, t, lambda_init):
    tq = ATTN_BLOCK
    nq = t // tq
    qspec = pl.BlockSpec((tq, D_A), lambda bi, i: (bi * nq + i, 0))
    kspec = pl.BlockSpec((1, nq, D_A, tq), lambda bi, i: (bi, 0, 0, 0))
    vspec = pl.BlockSpec((t, 2 * D_A), lambda bi, i: (bi, 0))
    small = [_resident((1, HD_B))] * 4 + [_resident((1, 2 * HD_B))]
    n_chain = N_PAIRS + N_B
    return pl.pallas_call(
        functools.partial(_attn_prompt_kernel, lambda_init=lambda_init),
        grid=(b, nq),
        in_specs=[qspec, qspec, kspec, kspec, vspec, vspec,
                  pl.BlockSpec((1, nq, AUG_ROWS, tq), lambda bi, i: (bi, 0, 0, 0))] + small,
        out_specs=[qspec, qspec],
        out_shape=[jax.ShapeDtypeStruct((b * t, D_A), BF16), jax.ShapeDtypeStruct((b * t, D_B), BF16)],
        scratch_shapes=[pltpu.VMEM((N_PAIRS, 2 * tq, 2 * LANES), BF16),
                        pltpu.VMEM((N_B, 2 * tq, LANES), BF16),
                        pltpu.VMEM((n_chain, 2 * tq, LANES), F32),
                        pltpu.VMEM((n_chain, 2 * tq, 2 * LANES), F32)],
        compiler_params=_params(2),
        name="attn_prompt",
    )(qa16, qb16, kat16, kbt16, va16, vb16, aug, *lams, sub)


def _softmax_rows(s):
    m = jnp.max(s, axis=1, keepdims=True)
    p = jnp.exp(s - m)
    return p.astype(BF16), jnp.sum(p, axis=1, keepdims=True)


def _nt_dot(a, b):
    return lax.dot_general(a, b, (((1,), (1,)), ((), ())), preferred_element_type=F32)


def _attn_cached_kernel(qa_ref, qb_ref, ckat_ref, cvat_ref, ckbt_ref, cvb_ref,
                        kat_ref, vat_ref, kbt_ref, vb_ref, aug_ref,
                        lq1_ref, lk1_ref, lq2_ref, lk2_ref, sub_ref, oa_ref, ob_ref,
                        kka_ref, vva_ref, kkb_ref, vvb_ref, *, lambda_init):
    half = pl.program_id(1)
    tq = qa_ref.shape[0]
    past = ckat_ref.shape[2]
    t_all = past + tq
    t_pad = kka_ref.shape[2]
    n_sub = N_PAIRS // 2
    r = lax.broadcasted_iota(jnp.int32, (2 * tq, t_pad), 0)
    c = lax.broadcasted_iota(jnp.int32, (2 * tq, t_pad), 1)
    q_pos = past + jnp.where(r >= tq, r - tq, r)
    causal = q_pos >= c
    visible = ((q_pos // CHUNK) >= (c // CHUNK)) & (c < t_all)
    shift = (c - 2 * jnp.maximum(c - q_pos, 0)).astype(F32)
    lane = lax.broadcasted_iota(jnp.int32, (tq, LANES), 1)
    lam = _lambda(lq1_ref, lk1_ref, lq2_ref, lk2_ref, lambda_init)
    tail = t_pad - t_all

    for u in range(n_sub):
        rows = slice(u * LANES, (u + 1) * LANES)
        kka_ref[u, 0:LANES, 0:past] = ckat_ref[0, rows, :].astype(BF16)
        kka_ref[u, 0:LANES, past:t_all] = kat_ref[0, rows, :].astype(BF16)
        kka_ref[u, LANES:LANES + AUG_ROWS, 0:t_all] = aug_ref[0]
        kka_ref[u, LANES + AUG_ROWS:, 0:t_all] = jnp.zeros((LANES - AUG_ROWS, t_all), BF16)
        kka_ref[u, :, t_all:] = jnp.zeros((2 * LANES, tail), BF16)
        vva_ref[u, :, 0:past] = cvat_ref[0, rows, :].astype(BF16)
        vva_ref[u, :, past:t_all] = vat_ref[0, rows, :].astype(BF16)
        vva_ref[u, :, t_all:] = jnp.zeros((LANES, tail), BF16)
        pair = half * n_sub + u
        lo, hi = _lane_halves(qa_ref[:, rows])
        qs = jnp.concatenate(
            [jnp.concatenate([lo, _forget_selector((tq, LANES), 2 * pair)], axis=1),
             jnp.concatenate([hi, _forget_selector((tq, LANES), 2 * pair + 1)], axis=1)], axis=0)
        s = jnp.dot(qs, kka_ref[u], preferred_element_type=F32)
        p, l = _softmax_rows(jnp.where(causal, s, NEG_INF))
        o = _nt_dot(p, vva_ref[u]) / l
        oa_ref[:, rows] = jnp.where(lane < HD_A, o[:tq], o[tq:]).astype(BF16)

        h_loc = u
        kkb_ref[u, :, 0:past] = ckbt_ref[0, rows, :].astype(BF16)
        kkb_ref[u, :, past:t_all] = kbt_ref[0, rows, :].astype(BF16)
        kkb_ref[u, :, t_all:] = jnp.zeros((LANES, tail), BF16)
        head = half * n_sub + h_loc
        vvb_ref[u, 0:past, :] = cvb_ref[0, pl.ds(head, past, stride=N_B), :].astype(BF16)
        vvb_ref[u, past:t_all, :] = vb_ref[:, u * 2 * LANES:u * 2 * LANES + LANES]
        vvb_ref[u, t_all:, :] = jnp.zeros((tail, LANES), BF16)
        lo, hi = _lane_halves(qb_ref[:, rows])
        s = jnp.dot(jnp.concatenate([lo, hi], axis=0), kkb_ref[u], preferred_element_type=F32)
        slope = jnp.where(head == 0, ALIBI_SLOPES[0],
                          jnp.where(head == 1, ALIBI_SLOPES[1],
                                    jnp.where(head == 2, ALIBI_SLOPES[2], ALIBI_SLOPES[3])))
        p, l = _softmax_rows(jnp.where(visible, s + slope * shift, NEG_INF))
        o = jnp.dot(p, vvb_ref[u], preferred_element_type=F32) / l
        ob_ref[:, rows] = _diff_finish(o[:tq], o[tq:], lam, sub_ref, lambda_init).astype(BF16)


def _attn_cached(qa16, qb16, ckat, cvat, ckbt, cvb, kat, vat, kbt, vb16, aug, lams, sub, b, tq, lambda_init):
    past = ckat.shape[2]
    t_all = past + tq
    t_pad = -(-t_all // ATTN_BLOCK) * ATTN_BLOCK
    n_sub = N_PAIRS // 2
    hw = n_sub * LANES
    qspec = pl.BlockSpec((tq, hw), lambda bi, hf: (bi, hf))
    old_t = pl.BlockSpec((1, hw, past), lambda bi, hf: (bi, hf, 0))
    new_t = pl.BlockSpec((1, hw, tq), lambda bi, hf: (bi, hf, 0))
    small = [_resident((1, HD_B))] * 4 + [_resident((1, 2 * HD_B))]
    return pl.pallas_call(
        functools.partial(_attn_cached_kernel, lambda_init=lambda_init),
        grid=(b, 2),
        in_specs=[qspec, qspec, old_t, old_t, old_t,
                  pl.BlockSpec((1, past * N_B, LANES), lambda bi, hf: (bi, 0, 0)),
                  new_t, new_t, new_t,
                  pl.BlockSpec((tq, 2 * hw), lambda bi, hf: (bi, hf)),
                  pl.BlockSpec((1, AUG_ROWS, t_all), lambda bi, hf: (bi, 0, 0))] + small,
        out_specs=[qspec, qspec],
        out_shape=[jax.ShapeDtypeStruct((b * tq, D_A), BF16), jax.ShapeDtypeStruct((b * tq, D_B), BF16)],
        scratch_shapes=[pltpu.VMEM((n_sub, 2 * LANES, t_pad), BF16), pltpu.VMEM((n_sub, LANES, t_pad), BF16),
                        pltpu.VMEM((n_sub, LANES, t_pad), BF16), pltpu.VMEM((n_sub, t_pad, LANES), BF16)],
        compiler_params=_params(2),
        name="attn_cached",
    )(qa16, qb16, ckat, cvat, ckbt, cvb, kat, vat, kbt, vb16, aug, *lams, sub)


def _tokens_minor(x):
    b, t = x.shape[:2]
    return jnp.moveaxis(x.reshape(b, t, -1), 1, 2)


def _tokens_major(xt, tail):
    b, _, t = xt.shape
    return jnp.moveaxis(xt, 2, 1).reshape(b, t, *tail)


def _layer(x, cache, w, lambda_init):
    b, t, _ = x.shape
    n = b * t
    x1 = _ffn1(x.reshape(n, D_MODEL), w["gn1"], w["w1a"], w["w2a"])
    (kat, vat, lft, kbt, vb, qa16, kat16, va16, qb16, kbt16, vb16) = _proj(
        x1, w["gmix"], w["wa"], w["wf"], w["bf"], w["wb"], b, t)
    if cache is None:
        aug = _cum_aug(lft, blocked=True)
        oa, ob = _attn_prompt(qa16, qb16, kat16, kbt16, va16, vb16, aug, w["lams"], w["sub"], b, t, lambda_init)
    else:
        cfk, cfv, cflf, cdk, cdv = cache
        past = cfk.shape[1]
        aug = _cum_aug(jnp.concatenate([_tokens_minor(cflf.astype(F32)), lft], axis=2), blocked=False)
        oa, ob = _attn_cached(qa16, qb16, _tokens_minor(cfk), _tokens_minor(cfv), _tokens_minor(cdk),
                              cdv.reshape(b, past * N_B, 2 * HD_B), kat, vat, kbt, vb16, aug,
                              w["lams"], w["sub"], b, t, lambda_init)
    news = (_tokens_major(kat, (N_A, HD_A)), _tokens_major(vat, (N_A, HD_A)), _tokens_major(lft, (N_A,)),
            _tokens_major(kbt, (N_B, 2, HD_B)), vb.reshape(b, t, N_B, 2 * HD_B))
    return x1, oa, ob, news


def kernel(x_prompt, x_sample, cache_fox_k, cache_fox_v, cache_fox_logf, cache_diff_k, cache_diff_v,
           norm_ffn1, w_ffn1_in, w_ffn1_out, norm_mix, w_in, b_forget,
           lambda_q1, lambda_k1, lambda_q2, lambda_k2, diff_subln, w_out,
           norm_ffn2, w_ffn2_in, w_ffn2_out, norm_final):
    depth = w_in.shape[0]
    xp, xs = x_prompt, x_sample
    outs_p, outs_s = [], []
    gfin = norm_final.reshape(1, D_MODEL)
    for l in range(depth):
        lambda_init = 0.8 - 0.6 * math.exp(-0.3 * l)
        f_lo, f_hi = 3 * D_A, 3 * D_A + N_A
        w = {
            "gn1": norm_ffn1[l].reshape(1, D_MODEL),
            "w1a": w_ffn1_in[l].astype(BF16), "w2a": w_ffn1_out[l].astype(BF16),
            "gmix": norm_mix[l].reshape(1, D_MODEL),
            "wa": w_in[l][:, :f_lo].astype(BF16),
            "wf": jnp.pad(w_in[l][:, f_lo:f_hi], ((0, 0), (0, LANES - N_A))).astype(BF16),
            "bf": jnp.pad(b_forget[l], (0, LANES - N_A)).reshape(1, LANES),
            "wb": w_in[l][:, f_hi:].astype(BF16),
            "lams": [v[l].reshape(1, HD_B) for v in (lambda_q1, lambda_k1, lambda_q2, lambda_k2)],
            "sub": diff_subln[l].reshape(1, 2 * HD_B),
            "wo": w_out[l].astype(BF16),
            "gn2": norm_ffn2[l].reshape(1, D_MODEL),
            "w1b": w_ffn2_in[l].astype(BF16), "w2b": w_ffn2_out[l].astype(BF16),
        }
        streams = []
        for x, cache in ((xp, None),
                         (xs, (cache_fox_k[l], cache_fox_v[l], cache_fox_logf[l],
                               cache_diff_k[l], cache_diff_v[l]))):
            b, t, _ = x.shape
            x1, oa, ob, news = _layer(x, cache, w, lambda_init)
            y = _post(x1, oa, ob, w["wo"], w["gn2"], w["w1b"], w["w2b"], gfin, l == depth - 1)
            streams.append((y.reshape(b, t, D_MODEL), news))
        (xp, news_p), (xs, news_s) = streams
        outs_p.append(news_p)
        outs_s.append(news_s)
    stack = lambda outs, i: jnp.stack([o[i] for o in outs])
    return (xp, xs) + tuple(stack(outs_p, i) for i in range(5)) + tuple(stack(outs_s, i) for i in range(5))
```

```python
import functools
import math

import jax
import jax.numpy as jnp
from jax import lax
from jax.experimental import pallas as pl
from jax.experimental.pallas import tpu as pltpu

D_MODEL = 1024
D_FF = 2816
N_A = 8
HD_A = 64
N_B = 4
HD_B = 64
D_A = N_A * HD_A
D_B = N_B * 2 * HD_B
CHUNK = 64
NORM_EPS = 1e-6
NEG_INF = -1e30
ALIBI_SLOPES = (0.25, 0.0625, 0.015625, 0.00390625)

LOG2E = math.log2(math.e)
LANES = 128
FF_CHUNK = 1408
TOKEN_TILE = 512
ATTN_BLOCK = 256
AUG_ROWS = 32
VMEM_LIMIT = 56 * 1024 * 1024
N_PAIRS = N_A // 2

F32 = jnp.float32
BF16 = jnp.bfloat16


def _params(n_axes):
    return pltpu.CompilerParams(dimension_semantics=("arbitrary",) * n_axes,
                                vmem_limit_bytes=VMEM_LIMIT)


def _resident(shape):
    return pl.BlockSpec(shape, lambda *_: (0,) * len(shape), pipeline_mode=pl.Buffered(1))


def _rms(xf, g):
    ms = jnp.mean(xf * xf, axis=-1, keepdims=True)
    return xf * lax.rsqrt(ms + NORM_EPS) * g


def _swiglu_residual(x, gn_ref, w1_ref, w2_ref):
    h = _rms(x, gn_ref[...]).astype(BF16)
    acc = None
    for c in range(D_FF // FF_CHUNK):
        lo = c * FF_CHUNK
        g = jnp.dot(h, w1_ref[:, lo:lo + FF_CHUNK], preferred_element_type=F32)
        u = jnp.dot(h, w1_ref[:, D_FF + lo:D_FF + lo + FF_CHUNK], preferred_element_type=F32)
        a = (g * jax.nn.sigmoid(g) * u).astype(BF16)
        d = jnp.dot(a, w2_ref[lo:lo + FF_CHUNK, :], preferred_element_type=F32)
        acc = d if acc is None else acc + d
    return x + 0.5 * acc


def _ffn1_kernel(x_ref, gn_ref, w1_ref, w2_ref, o_ref):
    o_ref[...] = _swiglu_residual(x_ref[...], gn_ref, w1_ref, w2_ref)


def _ffn1(x, gn, w1, w2):
    n = x.shape[0]
    tm = min(TOKEN_TILE, n)
    return pl.pallas_call(
        _ffn1_kernel,
        grid=(n // tm,),
        in_specs=[pl.BlockSpec((tm, D_MODEL), lambda i: (i, 0)),
                  _resident((1, D_MODEL)), _resident(w1.shape), _resident(w2.shape)],
        out_specs=pl.BlockSpec((tm, D_MODEL), lambda i: (i, 0)),
        out_shape=jax.ShapeDtypeStruct((n, D_MODEL), F32),
        compiler_params=_params(1),
        name="ffn1",
    )(x, gn, w1, w2)


def _post_kernel(x_ref, oa_ref, ob_ref, wo_ref, gn_ref, w1_ref, w2_ref, gf_ref, y_ref, *, final):
    x = x_ref[...]
    x = x + jnp.dot(oa_ref[...], wo_ref[:D_A, :], preferred_element_type=F32)
    x = x + jnp.dot(ob_ref[...], wo_ref[D_A:, :], preferred_element_type=F32)
    x = _swiglu_residual(x, gn_ref, w1_ref, w2_ref)
    y_ref[...] = _rms(x, gf_ref[...]) if final else x


def _post(x, oa, ob, wo, gn, w1, w2, gf, final):
    n = x.shape[0]
    tm = min(TOKEN_TILE, n)
    row = lambda i: (i, 0)
    return pl.pallas_call(
        functools.partial(_post_kernel, final=final),
        grid=(n // tm,),
        in_specs=[pl.BlockSpec((tm, D_MODEL), row), pl.BlockSpec((tm, D_A), row),
                  pl.BlockSpec((tm, D_B), row), _resident(wo.shape), _resident((1, D_MODEL)),
                  _resident(w1.shape), _resident(w2.shape), _resident((1, D_MODEL))],
        out_specs=pl.BlockSpec((tm, D_MODEL), row),
        out_shape=jax.ShapeDtypeStruct((n, D_MODEL), F32),
        compiler_params=_params(1),
        name="post",
    )(x, oa, ob, wo, gn, w1, w2, gf)


def _with_ones(v, width):
    ones = jnp.ones((v.shape[0], width), BF16)
    parts = []
    for g in range(v.shape[1] // width):
        parts += [v[:, g * width:(g + 1) * width].astype(BF16), ones]
    return jnp.concatenate(parts, axis=1)


def _proj_kernel(x_ref, gn_ref, wa_ref, wf_ref, bf_ref, wb_ref,
                 kat_ref, vat_ref, lft_ref, kbt_ref, vb_ref,
                 qa16_ref, kat16_ref, va16_ref, qb16_ref, kbt16_ref, vb16_ref, *, seg, kblk):
    tm = x_ref.shape[0]
    h = _rms(x_ref[...], gn_ref[...]).astype(BF16)
    pa = jnp.dot(h, wa_ref[...], preferred_element_type=F32)
    pb = jnp.dot(h, wb_ref[...], preferred_element_type=F32)
    z = jnp.dot(h, wf_ref[...], preferred_element_type=F32) + bf_ref[...]
    logf = jnp.minimum(z, 0.0) - jnp.log1p(jnp.exp(-jnp.abs(z)))
    qa16_ref[...] = (pa[:, :D_A] * (HD_A ** -0.5 * LOG2E)).astype(BF16)
    qb16_ref[...] = (pb[:, :D_B] * (HD_B ** -0.5 * LOG2E)).astype(BF16)
    va16_ref[...] = _with_ones(pa[:, 2 * D_A:], LANES)
    vb = pb[:, 2 * D_B:]
    vb16_ref[...] = _with_ones(vb, LANES)
    for hh in range(N_B):
        vb_ref[pl.ds(hh, tm, stride=N_B), :] = vb[:, hh * LANES:(hh + 1) * LANES]
    kat = pa[:, D_A:2 * D_A].T
    vat = pa[:, 2 * D_A:].T
    kbt = pb[:, D_B:2 * D_B].T
    lft = logf.T
    for bb in range(tm // seg):
        cols = slice(bb * seg, (bb + 1) * seg)
        kat_ref[bb] = kat[:, cols]
        vat_ref[bb] = vat[:, cols]
        kbt_ref[bb] = kbt[:, cols]
        lft_ref[bb] = lft[:N_A, cols]
        for jj in range(seg // kblk):
            kc = slice(bb * seg + jj * kblk, bb * seg + (jj + 1) * kblk)
            kat16_ref[bb, jj] = kat[:, kc].astype(BF16)
            kbt16_ref[bb, jj] = kbt[:, kc].astype(BF16)


def _proj(x, gn, wa, wf, bf, wb, b, t):
    n = x.shape[0]
    tm = min(TOKEN_TILE, n)
    seg = min(tm, t)
    nb = tm // seg
    nt = t // seg
    kblk = min(ATTN_BLOCK, seg)
    row = lambda i: (i, 0)
    tok = lambda i: (i // nt, 0, i % nt)
    blk = lambda i: (i // nt, i % nt, 0, 0)
    f32t = jax.ShapeDtypeStruct((b, D_A, t), F32)
    b16t = jax.ShapeDtypeStruct((b, t // kblk, D_A, kblk), BF16)
    tspec = pl.BlockSpec((nb, D_A, seg), tok)
    kspec = pl.BlockSpec((nb, seg // kblk, D_A, kblk), blk)
    return pl.pallas_call(
        functools.partial(_proj_kernel, seg=seg, kblk=kblk),
        grid=(n // tm,),
        in_specs=[pl.BlockSpec((tm, D_MODEL), row), _resident((1, D_MODEL)), _resident(wa.shape),
                  _resident(wf.shape), _resident(bf.shape), _resident(wb.shape)],
        out_specs=[tspec, tspec, pl.BlockSpec((nb, N_A, seg), tok), tspec,
                   pl.BlockSpec((tm * N_B, LANES), row),
                   pl.BlockSpec((tm, D_A), row), kspec, pl.BlockSpec((tm, 2 * D_A), row),
                   pl.BlockSpec((tm, D_B), row), kspec, pl.BlockSpec((tm, 2 * D_B), row)],
        out_shape=[f32t, f32t, jax.ShapeDtypeStruct((b, N_A, t), F32), f32t,
                   jax.ShapeDtypeStruct((n * N_B, LANES), F32),
                   jax.ShapeDtypeStruct((n, D_A), BF16), b16t, jax.ShapeDtypeStruct((n, 2 * D_A), BF16),
                   jax.ShapeDtypeStruct((n, D_B), BF16), b16t, jax.ShapeDtypeStruct((n, 2 * D_B), BF16)],
        compiler_params=_params(1),
        name="proj",
    )(x, gn, wa, wf, bf, wb)


def _split3(x):
    hi = x.astype(BF16).astype(F32)
    r = x - hi
    mid = r.astype(BF16).astype(F32)
    lo = (r - mid).astype(BF16).astype(F32)
    return hi, mid, lo


def _cum_kernel(lft_ref, aug_ref, pad_ref, *, t, blocked):
    w = ATTN_BLOCK
    pad_ref[...] = jnp.zeros(pad_ref.shape, F32)
    pad_ref[:, 0:t] = lft_ref[0]
    r = lax.broadcasted_iota(jnp.int32, (w, w), 0)
    c = lax.broadcasted_iota(jnp.int32, (w, w), 1)
    tri = jnp.where(r <= c, 1.0, 0.0).astype(BF16)
    zero = jnp.zeros((N_A, w), F32)
    carry = jnp.zeros((N_A, 1), F32)
    for s in range(pad_ref.shape[1] // w):
        lo_col = s * w
        cols = min(w, t - lo_col)
        hi, mid, lo = _split3(pad_ref[:, lo_col:lo_col + w])
        x = jnp.concatenate([hi, mid, lo, zero], axis=0).astype(BF16)
        y = jnp.dot(x, tri, preferred_element_type=F32)
        cs = y[0:N_A] + y[N_A:2 * N_A] + y[2 * N_A:3 * N_A] + carry
        carry = cs[:, w - 1:w]
        nhi, nmid, nlo = _split3(cs * -LOG2E)
        aug = jnp.concatenate([nhi, nmid, nlo, zero], axis=0).astype(BF16)
        if blocked:
            aug_ref[0, s] = aug
        else:
            aug_ref[0, :, lo_col:lo_col + cols] = aug[:, :cols]


def _cum_aug(lft, blocked):
    b, _, t = lft.shape
    w = ATTN_BLOCK
    t_pad = -(-t // w) * w
    if blocked:
        out_shape = jax.ShapeDtypeStruct((b, t // w, AUG_ROWS, w), BF16)
        out_spec = pl.BlockSpec((1, t // w, AUG_ROWS, w), lambda i: (i, 0, 0, 0))
    else:
        out_shape = jax.ShapeDtypeStruct((b, AUG_ROWS, t), BF16)
        out_spec = pl.BlockSpec((1, AUG_ROWS, t), lambda i: (i, 0, 0))
    return pl.pallas_call(
        functools.partial(_cum_kernel, t=t, blocked=blocked),
        grid=(b,),
        in_specs=[pl.BlockSpec((1, N_A, t), lambda i: (i, 0, 0))],
        out_specs=out_spec,
        out_shape=out_shape,
        scratch_shapes=[pltpu.VMEM((N_A, t_pad), F32)],
        compiler_params=_params(1),
        name="cum_aug",
    )(lft)


def _lane_halves(q):
    lane = lax.broadcasted_iota(jnp.int32, q.shape, 1)
    zero = jnp.zeros_like(q)
    return jnp.where(lane < HD_A, q, zero), jnp.where(lane >= HD_A, q, zero)


def _piece_selector(shape, head, n_heads):
    lane = lax.broadcasted_iota(jnp.int32, shape, 1)
    hit = (lane == head) | (lane == n_heads + head) | (lane == 2 * n_heads + head)
    return jnp.where(hit, 1.0, 0.0).astype(BF16)


def _alibi_rows(k_pos):
    w = k_pos.shape[1]
    row = lax.broadcasted_iota(jnp.int32, (4 * N_B, 1), 0)
    head = row % N_B
    slope = jnp.zeros((4 * N_B, 1), F32)
    for h in range(N_B):
        slope = jnp.where(head == h, ALIBI_SLOPES[h] * LOG2E, slope)
    hi, mid, lo = _split3(slope * k_pos)
    piece = jnp.where(row < N_B, hi, jnp.where(row < 2 * N_B, mid, jnp.where(row < 3 * N_B, lo, 0.0)))
    return jnp.concatenate([piece.astype(BF16), jnp.zeros((LANES - 4 * N_B, w), BF16)], axis=0)


def _alibi_fold(slope, q_pos, k_pos):
    return (-2.0 * LOG2E * slope) * jnp.maximum(k_pos - q_pos, 0).astype(F32)


def _diff_finish(o1, o2, lam, sub_ref, lambda_init):
    return _rms(o1 - lam * o2, sub_ref[...]) * (1.0 - lambda_init)


def _lambda(lq1_ref, lk1_ref, lq2_ref, lk2_ref, lambda_init):
    return (jnp.exp(jnp.sum(lq1_ref[...] * lk1_ref[...], axis=1, keepdims=True))
            - jnp.exp(jnp.sum(lq2_ref[...] * lk2_ref[...], axis=1, keepdims=True)) + lambda_init)


def _attn_prompt_kernel(qa_ref, qb_ref, kat_ref, kbt_ref, va_ref, vb_ref, aug_ref,
                        lq1_ref, lk1_ref, lq2_ref, lk2_ref, sub_ref, oa_ref, ob_ref,
                        qs_ref, m_ref, acc_ref, tbl_ref, *, lambda_init):
    bi, i = pl.program_id(0), pl.program_id(1)
    tq = qa_ref.shape[0]
    n_chain = N_PAIRS + N_B

    @pl.when((bi == 0) & (i == 0))
    def _():
        r = lax.broadcasted_iota(jnp.int32, (2 * tq, tq), 0)
        c = lax.broadcasted_iota(jnp.int32, (2 * tq, tq), 1)
        q_idx = jnp.where(r >= tq, r - tq, r)
        tbl_ref[0] = jnp.where(q_idx >= c, 0.0, NEG_INF)
        visible = (q_idx // CHUNK) >= (c // CHUNK)
        for h in range(N_B):
            tbl_ref[1 + h] = jnp.where(visible, _alibi_fold(ALIBI_SLOPES[h], q_idx, c), NEG_INF)

    for ch in range(n_chain):
        if ch < N_PAIRS:
            lo, hi = _lane_halves(qa_ref[:, ch * LANES:(ch + 1) * LANES])
            sel_lo = _piece_selector((tq, LANES), 2 * ch, N_A)
            sel_hi = _piece_selector((tq, LANES), 2 * ch + 1, N_A)
        else:
            lo, hi = _lane_halves(qb_ref[:, (ch - N_PAIRS) * LANES:(ch - N_PAIRS + 1) * LANES])
            sel_lo = sel_hi = _piece_selector((tq, LANES), ch - N_PAIRS, N_B)
        qs_ref[ch, 0:tq, 0:LANES] = lo
        qs_ref[ch, tq:, 0:LANES] = hi
        qs_ref[ch, 0:tq, LANES:] = sel_lo
        qs_ref[ch, tq:, LANES:] = sel_hi
    m_ref[...] = jnp.full(m_ref.shape, NEG_INF, F32)
    acc_ref[...] = jnp.zeros(acc_ref.shape, F32)

    def step(j, diag):
        rows = pl.ds(pl.multiple_of(j * tq, tq), tq)
        aug_a = jnp.concatenate([aug_ref[0, j], jnp.zeros((LANES - AUG_ROWS, tq), BF16)], axis=0)
        aug_b = _alibi_rows((j * tq + lax.broadcasted_iota(jnp.int32, (1, tq), 1)).astype(F32))
        for ch in range(n_chain):
            if ch < N_PAIRS:
                kk = jnp.concatenate([kat_ref[0, j, ch * LANES:(ch + 1) * LANES, :], aug_a], axis=0)
                v = va_ref[rows, ch * 2 * LANES:(ch + 1) * 2 * LANES]
                tbl = 0
            else:
                h = ch - N_PAIRS
                kk = jnp.concatenate([kbt_ref[0, j, h * LANES:(h + 1) * LANES, :], aug_b], axis=0)
                v = vb_ref[rows, h * 2 * LANES:(h + 1) * 2 * LANES]
                tbl = 1 + h
            s = jnp.dot(qs_ref[ch], kk, preferred_element_type=F32)
            if diag:
                s = s + tbl_ref[tbl]
            m_prev = m_ref[ch]
            m_new = jnp.maximum(m_prev, jnp.max(s, axis=1, keepdims=True))
            alpha = jnp.exp2(m_prev - m_new)
            p = jnp.exp2(s - jnp.concatenate([m_new, m_new], axis=1))
            pv = jnp.dot(p.astype(BF16), v, preferred_element_type=F32)
            acc_ref[ch] = jnp.concatenate([alpha, alpha], axis=1) * acc_ref[ch] + pv
            m_ref[ch] = m_new

    def body(j, carry):
        step(j, False)
        return carry

    lax.fori_loop(0, i, body, 0)
    step(i, True)

    lane = lax.broadcasted_iota(jnp.int32, (tq, LANES), 1)
    lam = _lambda(lq1_ref, lk1_ref, lq2_ref, lk2_ref, lambda_init)
    for ch in range(n_chain):
        o = acc_ref[ch, :, 0:LANES] / acc_ref[ch, :, LANES:]
        if ch < N_PAIRS:
            oa_ref[:, ch * LANES:(ch + 1) * LANES] = jnp.where(lane < HD_A, o[:tq], o[tq:]).astype(BF16)
        else:
            h = ch - N_PAIRS
            ob_ref[:, h * LANES:(h + 1) * LANES] = _diff_finish(
                o[:tq], o[tq:], lam, sub_ref, lambda_init).astype(BF16)


def _attn_prompt(qa16, qb16, kat16, kbt16, va16, vb16, aug, lams, sub, b, t, lambda_init):
    tq = ATTN_BLOCK
    nq = t // tq
    qspec = pl.BlockSpec((tq, D_A), lambda bi, i: (bi * nq + i, 0))
    kspec = pl.BlockSpec((1, nq, D_A, tq), lambda bi, i: (bi, 0, 0, 0))
    vspec = pl.BlockSpec((t, 2 * D_A), lambda bi, i: (bi, 0))
    small = [_resident((1, HD_B))] * 4 + [_resident((1, 2 * HD_B))]
    n_chain = N_PAIRS + N_B
    return pl.pallas_call(
        functools.partial(_attn_prompt_kernel, lambda_init=lambda_init),
        grid=(b, nq),
        in_specs=[qspec, qspec, kspec, kspec, vspec, vspec,
                  pl.BlockSpec((1, nq, AUG_ROWS, tq), lambda bi, i: (bi, 0, 0, 0))] + small,
        out_specs=[qspec, qspec],
        out_shape=[jax.ShapeDtypeStruct((b * t, D_A), BF16), jax.ShapeDtypeStruct((b * t, D_B), BF16)],
        scratch_shapes=[pltpu.VMEM((n_chain, 2 * tq, 2 * LANES), BF16),
                        pltpu.VMEM((n_chain, 2 * tq, LANES), F32),
                        pltpu.VMEM((n_chain, 2 * tq, 2 * LANES), F32),
                        pltpu.VMEM((1 + N_B, 2 * tq, tq), F32)],
        compiler_params=_params(2),
        name="attn_prompt",
    )(qa16, qb16, kat16, kbt16, va16, vb16, aug, *lams, sub)


def _softmax_rows(s):
    m = jnp.max(s, axis=1, keepdims=True)
    p = jnp.exp2(s - m)
    return p.astype(BF16), jnp.sum(p, axis=1, keepdims=True)


def _nt_dot(a, b):
    return lax.dot_general(a, b, (((1,), (1,)), ((), ())), preferred_element_type=F32)


def _attn_cached_kernel(qa_ref, qb_ref, ckat_ref, cvat_ref, ckbt_ref, cvb_ref,
                        kat_ref, vat_ref, kbt_ref, vb_ref, aug_ref,
                        lq1_ref, lk1_ref, lq2_ref, lk2_ref, sub_ref, oa_ref, ob_ref,
                        kka_ref, vva_ref, kkb_ref, vvb_ref, *, lambda_init):
    half = pl.program_id(1)
    tq = qa_ref.shape[0]
    past = ckat_ref.shape[2]
    t_all = past + tq
    t_pad = kka_ref.shape[2]
    n_sub = N_PAIRS // 2
    r = lax.broadcasted_iota(jnp.int32, (2 * tq, t_pad), 0)
    c = lax.broadcasted_iota(jnp.int32, (2 * tq, t_pad), 1)
    q_pos = past + jnp.where(r >= tq, r - tq, r)
    causal = q_pos >= c
    visible = ((q_pos // CHUNK) >= (c // CHUNK)) & (c < t_all)
    shift = (c - 2 * jnp.maximum(c - q_pos, 0)).astype(F32) * LOG2E
    lane = lax.broadcasted_iota(jnp.int32, (tq, LANES), 1)
    lam = _lambda(lq1_ref, lk1_ref, lq2_ref, lk2_ref, lambda_init)
    tail = t_pad - t_all

    for u in range(n_sub):
        rows = slice(u * LANES, (u + 1) * LANES)
        kka_ref[u, 0:LANES, 0:past] = ckat_ref[0, rows, :].astype(BF16)
        kka_ref[u, 0:LANES, past:t_all] = kat_ref[0, rows, :].astype(BF16)
        kka_ref[u, LANES:LANES + AUG_ROWS, 0:t_all] = aug_ref[0]
        kka_ref[u, LANES + AUG_ROWS:, 0:t_all] = jnp.zeros((LANES - AUG_ROWS, t_all), BF16)
        kka_ref[u, :, t_all:] = jnp.zeros((2 * LANES, tail), BF16)
        vva_ref[u, :, 0:past] = cvat_ref[0, rows, :].astype(BF16)
        vva_ref[u, :, past:t_all] = vat_ref[0, rows, :].astype(BF16)
        vva_ref[u, :, t_all:] = jnp.zeros((LANES, tail), BF16)
        pair = half * n_sub + u
        lo, hi = _lane_halves(qa_ref[:, rows])
        qs = jnp.concatenate(
            [jnp.concatenate([lo, _piece_selector((tq, LANES), 2 * pair, N_A)], axis=1),
             jnp.concatenate([hi, _piece_selector((tq, LANES), 2 * pair + 1, N_A)], axis=1)], axis=0)
        s = jnp.dot(qs, kka_ref[u], preferred_element_type=F32)
        p, l = _softmax_rows(jnp.where(causal, s, NEG_INF))
        o = _nt_dot(p, vva_ref[u]) / l
        oa_ref[:, rows] = jnp.where(lane < HD_A, o[:tq], o[tq:]).astype(BF16)

        kkb_ref[u, :, 0:past] = ckbt_ref[0, rows, :].astype(BF16)
        kkb_ref[u, :, past:t_all] = kbt_ref[0, rows, :].astype(BF16)
        kkb_ref[u, :, t_all:] = jnp.zeros((LANES, tail), BF16)
        head = half * n_sub + u
        vvb_ref[u, 0:past, :] = cvb_ref[0, pl.ds(head, past, stride=N_B), :].astype(BF16)
        vvb_ref[u, past:t_all, :] = vb_ref[:, u * 2 * LANES:u * 2 * LANES + LANES]
        vvb_ref[u, t_all:, :] = jnp.zeros((tail, LANES), BF16)
        lo, hi = _lane_halves(qb_ref[:, rows])
        s = jnp.dot(jnp.concatenate([lo, hi], axis=0), kkb_ref[u], preferred_element_type=F32)
        slope = jnp.where(head == 0, ALIBI_SLOPES[0],
                          jnp.where(head == 1, ALIBI_SLOPES[1],
                                    jnp.where(head == 2, ALIBI_SLOPES[2], ALIBI_SLOPES[3])))
        p, l = _softmax_rows(jnp.where(visible, s + slope * shift, NEG_INF))
        o = jnp.dot(p, vvb_ref[u], preferred_element_type=F32) / l
        ob_ref[:, rows] = _diff_finish(o[:tq], o[tq:], lam, sub_ref, lambda_init).astype(BF16)


def _attn_cached(qa16, qb16, ckat, cvat, ckbt, cvb, kat, vat, kbt, vb16, aug, lams, sub, b, tq, lambda_init):
    past = ckat.shape[2]
    t_all = past + tq
    t_pad = -(-t_all // ATTN_BLOCK) * ATTN_BLOCK
    n_sub = N_PAIRS // 2
    hw = n_sub * LANES
    qspec = pl.BlockSpec((tq, hw), lambda bi, hf: (bi, hf))
    old_t = pl.BlockSpec((1, hw, past), lambda bi, hf: (bi, hf, 0))
    new_t = pl.BlockSpec((1, hw, tq), lambda bi, hf: (bi, hf, 0))
    small = [_resident((1, HD_B))] * 4 + [_resident((1, 2 * HD_B))]
    return pl.pallas_call(
        functools.partial(_attn_cached_kernel, lambda_init=lambda_init),
        grid=(b, 2),
        in_specs=[qspec, qspec, old_t, old_t, old_t,
                  pl.BlockSpec((1, past * N_B, LANES), lambda bi, hf: (bi, 0, 0)),
                  new_t, new_t, new_t,
                  pl.BlockSpec((tq, 2 * hw), lambda bi, hf: (bi, hf)),
                  pl.BlockSpec((1, AUG_ROWS, t_all), lambda bi, hf: (bi, 0, 0))] + small,
        out_specs=[qspec, qspec],
        out_shape=[jax.ShapeDtypeStruct((b * tq, D_A), BF16), jax.ShapeDtypeStruct((b * tq, D_B), BF16)],
        scratch_shapes=[pltpu.VMEM((n_sub, 2 * LANES, t_pad), BF16), pltpu.VMEM((n_sub, LANES, t_pad), BF16),
                        pltpu.VMEM((n_sub, LANES, t_pad), BF16), pltpu.VMEM((n_sub, t_pad, LANES), BF16)],
        compiler_params=_params(2),
        name="attn_cached",
    )(qa16, qb16, ckat, cvat, ckbt, cvb, kat, vat, kbt, vb16, aug, *lams, sub)


def _tokens_minor(x):
    b, t = x.shape[:2]
    return jnp.moveaxis(x.reshape(b, t, -1), 1, 2)


def _tokens_major(xt, tail):
    b, _, t = xt.shape
    return jnp.moveaxis(xt, 2, 1).reshape(b, t, *tail)


def _layer(x, cache, w, lambda_init):
    b, t, _ = x.shape
    n = b * t
    x1 = _ffn1(x.reshape(n, D_MODEL), w["gn1"], w["w1a"], w["w2a"])
    (kat, vat, lft, kbt, vb, qa16, kat16, va16, qb16, kbt16, vb16) = _proj(
        x1, w["gmix"], w["wa"], w["wf"], w["bf"], w["wb"], b, t)
    if cache is None:
        aug = _cum_aug(lft, blocked=True)
        oa, ob = _attn_prompt(qa16, qb16, kat16, kbt16, va16, vb16, aug, w["lams"], w["sub"], b, t, lambda_init)
    else:
        cfk, cfv, cflf, cdk, cdv = cache
        past = cfk.shape[1]
        aug = _cum_aug(jnp.concatenate([_tokens_minor(cflf.astype(F32)), lft], axis=2), blocked=False)
        oa, ob = _attn_cached(qa16, qb16, _tokens_minor(cfk), _tokens_minor(cfv), _tokens_minor(cdk),
                              cdv.reshape(b, past * N_B, 2 * HD_B), kat, vat, kbt, vb16, aug,
                              w["lams"], w["sub"], b, t, lambda_init)
    news = (_tokens_major(kat, (N_A, HD_A)), _tokens_major(vat, (N_A, HD_A)), _tokens_major(lft, (N_A,)),
            _tokens_major(kbt, (N_B, 2, HD_B)), vb.reshape(b, t, N_B, 2 * HD_B))
    return x1, oa, ob, news


def kernel(x_prompt, x_sample, cache_fox_k, cache_fox_v, cache_fox_logf, cache_diff_k, cache_diff_v,
           norm_ffn1, w_ffn1_in, w_ffn1_out, norm_mix, w_in, b_forget,
           lambda_q1, lambda_k1, lambda_q2, lambda_k2, diff_subln, w_out,
           norm_ffn2, w_ffn2_in, w_ffn2_out, norm_final):
    depth = w_in.shape[0]
    xp, xs = x_prompt, x_sample
    outs_p, outs_s = [], []
    gfin = norm_final.reshape(1, D_MODEL)
    for l in range(depth):
        lambda_init = 0.8 - 0.6 * math.exp(-0.3 * l)
        f_lo, f_hi = 3 * D_A, 3 * D_A + N_A
        w = {
            "gn1": norm_ffn1[l].reshape(1, D_MODEL),
            "w1a": w_ffn1_in[l].astype(BF16), "w2a": w_ffn1_out[l].astype(BF16),
            "gmix": norm_mix[l].reshape(1, D_MODEL),
            "wa": w_in[l][:, :f_lo].astype(BF16),
            "wf": jnp.pad(w_in[l][:, f_lo:f_hi], ((0, 0), (0, LANES - N_A))).astype(BF16),
            "bf": jnp.pad(b_forget[l], (0, LANES - N_A)).reshape(1, LANES),
            "wb": w_in[l][:, f_hi:].astype(BF16),
            "lams": [v[l].reshape(1, HD_B) for v in (lambda_q1, lambda_k1, lambda_q2, lambda_k2)],
            "sub": diff_subln[l].reshape(1, 2 * HD_B),
            "wo": w_out[l].astype(BF16),
            "gn2": norm_ffn2[l].reshape(1, D_MODEL),
            "w1b": w_ffn2_in[l].astype(BF16), "w2b": w_ffn2_out[l].astype(BF16),
        }
        streams = []
        for x, cache in ((xp, None),
                         (xs, (cache_fox_k[l], cache_fox_v[l], cache_fox_logf[l],
                               cache_diff_k[l], cache_diff_v[l]))):
            b, t, _ = x.shape
            x1, oa, ob, news = _layer(x, cache, w, lambda_init)
            y = _post(x1, oa, ob, w["wo"], w["gn2"], w["w1b"], w["w2b"], gfin, l == depth - 1)
            streams.append((y.reshape(b, t, D_MODEL), news))
        (xp, news_p), (xs, news_s) = streams
        outs_p.append(news_p)
        outs_s.append(news_s)
    stack = lambda outs, i: jnp.stack([o[i] for o in outs])
    return (xp, xs) + tuple(stack(outs_p, i) for i in range(5)) + tuple(stack(outs_s, i) for i in range(5))
```

```python
import functools
import math

import jax
import jax.numpy as jnp
from jax import lax
from jax.experimental import pallas as pl
from jax.experimental.pallas import tpu as pltpu

D_MODEL = 1024
D_FF = 2816
N_A = 8
HD_A = 64
N_B = 4
HD_B = 64
D_A = N_A * HD_A
D_B = N_B * 2 * HD_B
CHUNK = 64
NORM_EPS = 1e-6
NEG_INF = -1e30
ALIBI_SLOPES = (0.25, 0.0625, 0.015625, 0.00390625)

LOG2E = math.log2(math.e)
LANES = 128
MXU_WIDTH = 256
FF_CHUNKS = (1536, 1280)
assert sum(FF_CHUNKS) == D_FF and all(c % MXU_WIDTH == 0 for c in FF_CHUNKS)
SCAN_BATCH = 8
TOKEN_TILE = 512
ATTN_BLOCK = 256
AUG_ROWS = 32
VMEM_LIMIT = 56 * 1024 * 1024
N_PAIRS = N_A // 2

F32 = jnp.float32
BF16 = jnp.bfloat16


def _params(n_axes):
    return pltpu.CompilerParams(dimension_semantics=("arbitrary",) * n_axes,
                                vmem_limit_bytes=VMEM_LIMIT)


def _resident(shape):
    return pl.BlockSpec(shape, lambda *_: (0,) * len(shape), pipeline_mode=pl.Buffered(1))


def _rms(xf, g):
    ms = jnp.mean(xf * xf, axis=-1, keepdims=True)
    return xf * lax.rsqrt(ms + NORM_EPS) * g


def _swiglu_residual(x, gn_ref, w1_ref, w2_ref):
    h = _rms(x, gn_ref[...]).astype(BF16)
    acc = None
    lo = 0
    for width in FF_CHUNKS:
        g = jnp.dot(h, w1_ref[:, lo:lo + width], preferred_element_type=F32)
        u = jnp.dot(h, w1_ref[:, D_FF + lo:D_FF + lo + width], preferred_element_type=F32)
        a = (g * jax.nn.sigmoid(g) * u).astype(BF16)
        d = jnp.dot(a, w2_ref[lo:lo + width, :], preferred_element_type=F32)
        acc = d if acc is None else acc + d
        lo += width
    return x + 0.5 * acc


def _ffn1_kernel(x_ref, gn_ref, w1_ref, w2_ref, o_ref):
    o_ref[...] = _swiglu_residual(x_ref[...], gn_ref, w1_ref, w2_ref)


def _ffn1(x, gn, w1, w2):
    n = x.shape[0]
    tm = min(TOKEN_TILE, n)
    return pl.pallas_call(
        _ffn1_kernel,
        grid=(n // tm,),
        in_specs=[pl.BlockSpec((tm, D_MODEL), lambda i: (i, 0)),
                  _resident((1, D_MODEL)), _resident(w1.shape), _resident(w2.shape)],
        out_specs=pl.BlockSpec((tm, D_MODEL), lambda i: (i, 0)),
        out_shape=jax.ShapeDtypeStruct((n, D_MODEL), F32),
        compiler_params=_params(1),
        name="ffn1",
    )(x, gn, w1, w2)


def _post_kernel(x_ref, oa_ref, ob_ref, wo_ref, gn_ref, w1_ref, w2_ref, gf_ref, y_ref, *, final):
    x = x_ref[...]
    x = x + jnp.dot(oa_ref[...], wo_ref[:D_A, :], preferred_element_type=F32)
    x = x + jnp.dot(ob_ref[...], wo_ref[D_A:, :], preferred_element_type=F32)
    x = _swiglu_residual(x, gn_ref, w1_ref, w2_ref)
    y_ref[...] = _rms(x, gf_ref[...]) if final else x


def _post(x, oa, ob, wo, gn, w1, w2, gf, final):
    n = x.shape[0]
    tm = min(TOKEN_TILE, n)
    row = lambda i: (i, 0)
    return pl.pallas_call(
        functools.partial(_post_kernel, final=final),
        grid=(n // tm,),
        in_specs=[pl.BlockSpec((tm, D_MODEL), row), pl.BlockSpec((tm, D_A), row),
                  pl.BlockSpec((tm, D_B), row), _resident(wo.shape), _resident((1, D_MODEL)),
                  _resident(w1.shape), _resident(w2.shape), _resident((1, D_MODEL))],
        out_specs=pl.BlockSpec((tm, D_MODEL), row),
        out_shape=jax.ShapeDtypeStruct((n, D_MODEL), F32),
        compiler_params=_params(1),
        name="post",
    )(x, oa, ob, wo, gn, w1, w2, gf)


def _with_ones(v, width):
    ones = jnp.ones((v.shape[0], width), BF16)
    parts = []
    for g in range(v.shape[1] // width):
        parts += [v[:, g * width:(g + 1) * width].astype(BF16), ones]
    return jnp.concatenate(parts, axis=1)


def _proj_kernel(x_ref, gn_ref, wa_ref, wf_ref, bf_ref, wb_ref,
                 kat_ref, vat_ref, lft_ref, kbt_ref, vb_ref,
                 qa16_ref, kat16_ref, va16_ref, qb16_ref, kbt16_ref, vb16_ref, *, seg, kblk):
    tm = x_ref.shape[0]
    h = _rms(x_ref[...], gn_ref[...]).astype(BF16)
    pa = jnp.dot(h, wa_ref[...], preferred_element_type=F32)
    pb = jnp.dot(h, wb_ref[...], preferred_element_type=F32)
    z = jnp.dot(h, wf_ref[...], preferred_element_type=F32) + bf_ref[...]
    logf = jnp.minimum(z, 0.0) - jnp.log1p(jnp.exp(-jnp.abs(z)))
    qa16_ref[...] = (pa[:, :D_A] * (HD_A ** -0.5 * LOG2E)).astype(BF16)
    qb16_ref[...] = (pb[:, :D_B] * (HD_B ** -0.5 * LOG2E)).astype(BF16)
    va16_ref[...] = _with_ones(pa[:, 2 * D_A:], LANES)
    vb = pb[:, 2 * D_B:]
    vb16_ref[...] = _with_ones(vb, LANES)
    for hh in range(N_B):
        vb_ref[pl.ds(hh, tm, stride=N_B), :] = vb[:, hh * LANES:(hh + 1) * LANES]
    kat = pa[:, D_A:2 * D_A].T
    vat = pa[:, 2 * D_A:].T
    kbt = pb[:, D_B:2 * D_B].T
    lft = logf.T
    for bb in range(tm // seg):
        cols = slice(bb * seg, (bb + 1) * seg)
        kat_ref[bb] = kat[:, cols]
        vat_ref[bb] = vat[:, cols]
        kbt_ref[bb] = kbt[:, cols]
        lft_ref[bb] = lft[:N_A, cols]
        for jj in range(seg // kblk):
            kc = slice(bb * seg + jj * kblk, bb * seg + (jj + 1) * kblk)
            kat16_ref[bb, jj] = kat[:, kc].astype(BF16)
            kbt16_ref[bb, jj] = kbt[:, kc].astype(BF16)


def _proj(x, gn, wa, wf, bf, wb, b, t):
    n = x.shape[0]
    tm = min(TOKEN_TILE, n)
    seg = min(tm, t)
    nb = tm // seg
    nt = t // seg
    kblk = min(ATTN_BLOCK, seg)
    row = lambda i: (i, 0)
    tok = lambda i: (i // nt, 0, i % nt)
    blk = lambda i: (i // nt, i % nt, 0, 0)
    f32t = jax.ShapeDtypeStruct((b, D_A, t), F32)
    b16t = jax.ShapeDtypeStruct((b, t // kblk, D_A, kblk), BF16)
    tspec = pl.BlockSpec((nb, D_A, seg), tok)
    kspec = pl.BlockSpec((nb, seg // kblk, D_A, kblk), blk)
    return pl.pallas_call(
        functools.partial(_proj_kernel, seg=seg, kblk=kblk),
        grid=(n // tm,),
        in_specs=[pl.BlockSpec((tm, D_MODEL), row), _resident((1, D_MODEL)), _resident(wa.shape),
                  _resident(wf.shape), _resident(bf.shape), _resident(wb.shape)],
        out_specs=[tspec, tspec, pl.BlockSpec((nb, N_A, seg), tok), tspec,
                   pl.BlockSpec((tm * N_B, LANES), row),
                   pl.BlockSpec((tm, D_A), row), kspec, pl.BlockSpec((tm, 2 * D_A), row),
                   pl.BlockSpec((tm, D_B), row), kspec, pl.BlockSpec((tm, 2 * D_B), row)],
        out_shape=[f32t, f32t, jax.ShapeDtypeStruct((b, N_A, t), F32), f32t,
                   jax.ShapeDtypeStruct((n * N_B, LANES), F32),
                   jax.ShapeDtypeStruct((n, D_A), BF16), b16t, jax.ShapeDtypeStruct((n, 2 * D_A), BF16),
                   jax.ShapeDtypeStruct((n, D_B), BF16), b16t, jax.ShapeDtypeStruct((n, 2 * D_B), BF16)],
        compiler_params=_params(1),
        name="proj",
    )(x, gn, wa, wf, bf, wb)


def _split3(x):
    hi = x.astype(BF16).astype(F32)
    r = x - hi
    mid = r.astype(BF16).astype(F32)
    lo = (r - mid).astype(BF16).astype(F32)
    return hi, mid, lo


def _cum_kernel(lft_ref, aug_ref, pad_ref, *, t, blocked):
    w = ATTN_BLOCK
    nb = lft_ref.shape[0]
    rows = nb * N_A
    pad_ref[...] = jnp.zeros(pad_ref.shape, F32)
    for bb in range(nb):
        pad_ref[bb * N_A:(bb + 1) * N_A, 0:t] = lft_ref[bb]
    r = lax.broadcasted_iota(jnp.int32, (w, w), 0)
    c = lax.broadcasted_iota(jnp.int32, (w, w), 1)
    tri = jnp.where(r <= c, 1.0, 0.0).astype(BF16)
    zero = jnp.zeros((N_A, w), F32)
    carry = jnp.zeros((rows, 1), F32)
    for s in range(pad_ref.shape[1] // w):
        lo_col = s * w
        cols = min(w, t - lo_col)
        x = jnp.concatenate(_split3(pad_ref[:, lo_col:lo_col + w]), axis=0).astype(BF16)
        y = jnp.dot(x, tri, preferred_element_type=F32)
        cs = y[0:rows] + y[rows:2 * rows] + y[2 * rows:3 * rows] + carry
        carry = cs[:, w - 1:w]
        pieces = _split3(cs * -LOG2E)
        for bb in range(nb):
            sl = slice(bb * N_A, (bb + 1) * N_A)
            aug = jnp.concatenate([p[sl] for p in pieces] + [zero], axis=0).astype(BF16)
            if blocked:
                aug_ref[bb, s] = aug
            else:
                aug_ref[bb, :, lo_col:lo_col + cols] = aug[:, :cols]


def _cum_aug(lft, blocked):
    b, _, t = lft.shape
    w = ATTN_BLOCK
    t_pad = -(-t // w) * w
    nb = math.gcd(b, SCAN_BATCH)
    if blocked:
        out_shape = jax.ShapeDtypeStruct((b, t // w, AUG_ROWS, w), BF16)
        out_spec = pl.BlockSpec((nb, t // w, AUG_ROWS, w), lambda i: (i, 0, 0, 0))
    else:
        out_shape = jax.ShapeDtypeStruct((b, AUG_ROWS, t), BF16)
        out_spec = pl.BlockSpec((nb, AUG_ROWS, t), lambda i: (i, 0, 0))
    return pl.pallas_call(
        functools.partial(_cum_kernel, t=t, blocked=blocked),
        grid=(b // nb,),
        in_specs=[pl.BlockSpec((nb, N_A, t), lambda i: (i, 0, 0))],
        out_specs=out_spec,
        out_shape=out_shape,
        scratch_shapes=[pltpu.VMEM((nb * N_A, t_pad), F32)],
        compiler_params=_params(1),
        name="cum_aug",
    )(lft)


def _lane_halves(q):
    lane = lax.broadcasted_iota(jnp.int32, q.shape, 1)
    zero = jnp.zeros_like(q)
    return jnp.where(lane < HD_A, q, zero), jnp.where(lane >= HD_A, q, zero)


def _piece_selector(shape, head, n_heads):
    lane = lax.broadcasted_iota(jnp.int32, shape, 1)
    hit = (lane == head) | (lane == n_heads + head) | (lane == 2 * n_heads + head)
    return jnp.where(hit, 1.0, 0.0).astype(BF16)


def _alibi_rows(k_pos):
    w = k_pos.shape[1]
    row = lax.broadcasted_iota(jnp.int32, (4 * N_B, 1), 0)
    head = row % N_B
    slope = jnp.zeros((4 * N_B, 1), F32)
    for h in range(N_B):
        slope = jnp.where(head == h, ALIBI_SLOPES[h] * LOG2E, slope)
    hi, mid, lo = _split3(slope * k_pos)
    piece = jnp.where(row < N_B, hi, jnp.where(row < 2 * N_B, mid, jnp.where(row < 3 * N_B, lo, 0.0)))
    return jnp.concatenate([piece.astype(BF16), jnp.zeros((LANES - 4 * N_B, w), BF16)], axis=0)


def _alibi_fold(slope, q_pos, k_pos):
    return (-2.0 * LOG2E * slope) * jnp.maximum(k_pos - q_pos, 0).astype(F32)


def _diff_finish(o1, o2, lam, sub_ref, lambda_init):
    return _rms(o1 - lam * o2, sub_ref[...]) * (1.0 - lambda_init)


def _lambda(lq1_ref, lk1_ref, lq2_ref, lk2_ref, lambda_init):
    return (jnp.exp(jnp.sum(lq1_ref[...] * lk1_ref[...], axis=1, keepdims=True))
            - jnp.exp(jnp.sum(lq2_ref[...] * lk2_ref[...], axis=1, keepdims=True)) + lambda_init)


def _attn_prompt_kernel(qa_ref, qb_ref, kat_ref, kbt_ref, va_ref, vb_ref, aug_ref,
                        lq1_ref, lk1_ref, lq2_ref, lk2_ref, sub_ref, oa_ref, ob_ref,
                        qs_ref, m_ref, acc_ref, tbl_ref, *, lambda_init):
    bi, i = pl.program_id(0), pl.program_id(1)
    tq = qa_ref.shape[0]
    n_chain = N_PAIRS + N_B

    @pl.when((bi == 0) & (i == 0))
    def _():
        r = lax.broadcasted_iota(jnp.int32, (2 * tq, tq), 0)
        c = lax.broadcasted_iota(jnp.int32, (2 * tq, tq), 1)
        q_idx = jnp.where(r >= tq, r - tq, r)
        tbl_ref[0] = jnp.where(q_idx >= c, 0.0, NEG_INF)
        visible = (q_idx // CHUNK) >= (c // CHUNK)
        for h in range(N_B):
            tbl_ref[1 + h] = jnp.where(visible, _alibi_fold(ALIBI_SLOPES[h], q_idx, c), NEG_INF)
        for ch in range(n_chain):
            if ch < N_PAIRS:
                qs_ref[ch, 0:tq, LANES:] = _piece_selector((tq, LANES), 2 * ch, N_A)
                qs_ref[ch, tq:, LANES:] = _piece_selector((tq, LANES), 2 * ch + 1, N_A)
            else:
                sel = _piece_selector((tq, LANES), ch - N_PAIRS, N_B)
                qs_ref[ch, 0:tq, LANES:] = sel
                qs_ref[ch, tq:, LANES:] = sel

    for ch in range(n_chain):
        q_ref, g = (qa_ref, ch) if ch < N_PAIRS else (qb_ref, ch - N_PAIRS)
        lo, hi = _lane_halves(q_ref[:, g * LANES:(g + 1) * LANES])
        qs_ref[ch, 0:tq, 0:LANES] = lo
        qs_ref[ch, tq:, 0:LANES] = hi

    def step(j, first):
        rows = pl.ds(pl.multiple_of(j * tq, tq), tq)
        aug_a = jnp.concatenate([aug_ref[0, j], jnp.zeros((LANES - AUG_ROWS, tq), BF16)], axis=0)
        aug_b = _alibi_rows((j * tq + lax.broadcasted_iota(jnp.int32, (1, tq), 1)).astype(F32))
        for ch in range(n_chain):
            if ch < N_PAIRS:
                kk = jnp.concatenate([kat_ref[0, j, ch * LANES:(ch + 1) * LANES, :], aug_a], axis=0)
                v = va_ref[rows, ch * 2 * LANES:(ch + 1) * 2 * LANES]
                tbl = 0
            else:
                h = ch - N_PAIRS
                kk = jnp.concatenate([kbt_ref[0, j, h * LANES:(h + 1) * LANES, :], aug_b], axis=0)
                v = vb_ref[rows, h * 2 * LANES:(h + 1) * 2 * LANES]
                tbl = 1 + h
            s = jnp.dot(qs_ref[ch], kk, preferred_element_type=F32)
            if first:
                s = s + tbl_ref[tbl]
                m_new = jnp.broadcast_to(jnp.max(s, axis=1, keepdims=True), (2 * tq, LANES))
            else:
                m_prev = m_ref[ch]
                m_new = jnp.maximum(m_prev, jnp.max(s, axis=1, keepdims=True))
            p = jnp.exp2(s - jnp.concatenate([m_new, m_new], axis=1))
            pv = jnp.dot(p.astype(BF16), v, preferred_element_type=F32)
            if first:
                acc_ref[ch] = pv
            else:
                alpha = jnp.exp2(m_prev - m_new)
                acc_ref[ch] = jnp.concatenate([alpha, alpha], axis=1) * acc_ref[ch] + pv
            m_ref[ch] = m_new

    def body(j, carry):
        step(j, False)
        return carry

    step(i, True)
    lax.fori_loop(0, i, body, 0)

    lane = lax.broadcasted_iota(jnp.int32, (tq, LANES), 1)
    lam = _lambda(lq1_ref, lk1_ref, lq2_ref, lk2_ref, lambda_init)
    for ch in range(n_chain):
        o = acc_ref[ch, :, 0:LANES] / acc_ref[ch, :, LANES:]
        if ch < N_PAIRS:
            oa_ref[:, ch * LANES:(ch + 1) * LANES] = jnp.where(lane < HD_A, o[:tq], o[tq:]).astype(BF16)
        else:
            h = ch - N_PAIRS
            ob_ref[:, h * LANES:(h + 1) * LANES] = _diff_finish(
                o[:tq], o[tq:], lam, sub_ref, lambda_init).astype(BF16)


def _attn_prompt(qa16, qb16, kat16, kbt16, va16, vb16, aug, lams, sub, b, t, lambda_init):
    tq = ATTN_BLOCK
    nq = t // tq
    qspec = pl.BlockSpec((tq, D_A), lambda bi, i: (bi * nq + i, 0))
    kspec = pl.BlockSpec((1, nq, D_A, tq), lambda bi, i: (bi, 0, 0, 0))
    vspec = pl.BlockSpec((t, 2 * D_A), lambda bi, i: (bi, 0))
    small = [_resident((1, HD_B))] * 4 + [_resident((1, 2 * HD_B))]
    n_chain = N_PAIRS + N_B
    return pl.pallas_call(
        functools.partial(_attn_prompt_kernel, lambda_init=lambda_init),
        grid=(b, nq),
        in_specs=[qspec, qspec, kspec, kspec, vspec, vspec,
                  pl.BlockSpec((1, nq, AUG_ROWS, tq), lambda bi, i: (bi, 0, 0, 0))] + small,
        out_specs=[qspec, qspec],
        out_shape=[jax.ShapeDtypeStruct((b * t, D_A), BF16), jax.ShapeDtypeStruct((b * t, D_B), BF16)],
        scratch_shapes=[pltpu.VMEM((n_chain, 2 * tq, 2 * LANES), BF16),
                        pltpu.VMEM((n_chain, 2 * tq, LANES), F32),
                        pltpu.VMEM((n_chain, 2 * tq, 2 * LANES), F32),
                        pltpu.VMEM((1 + N_B, 2 * tq, tq), F32)],
        compiler_params=_params(2),
        name="attn_prompt",
    )(qa16, qb16, kat16, kbt16, va16, vb16, aug, *lams, sub)


def _softmax_rows(s):
    m = jnp.max(s, axis=1, keepdims=True)
    p = jnp.exp2(s - m)
    return p.astype(BF16), jnp.sum(p, axis=1, keepdims=True)


def _nt_dot(a, b):
    return lax.dot_general(a, b, (((1,), (1,)), ((), ())), preferred_element_type=F32)


def _attn_cached_kernel(qa_ref, qb_ref, ckat_ref, cvat_ref, ckbt_ref, cvb_ref,
                        kat_ref, vat_ref, kbt_ref, vb_ref, aug_ref,
                        lq1_ref, lk1_ref, lq2_ref, lk2_ref, sub_ref, oa_ref, ob_ref,
                        kka_ref, vva_ref, kkb_ref, vvb_ref, *, lambda_init):
    half = pl.program_id(1)
    tq = qa_ref.shape[0]
    past = ckat_ref.shape[2]
    t_all = past + tq
    t_pad = kka_ref.shape[2]
    n_sub = N_PAIRS // 2
    r = lax.broadcasted_iota(jnp.int32, (2 * tq, t_pad), 0)
    c = lax.broadcasted_iota(jnp.int32, (2 * tq, t_pad), 1)
    q_pos = past + jnp.where(r >= tq, r - tq, r)
    causal = q_pos >= c
    visible = ((q_pos // CHUNK) >= (c // CHUNK)) & (c < t_all)
    shift = (c - 2 * jnp.maximum(c - q_pos, 0)).astype(F32) * LOG2E
    lane = lax.broadcasted_iota(jnp.int32, (tq, LANES), 1)
    lam = _lambda(lq1_ref, lk1_ref, lq2_ref, lk2_ref, lambda_init)
    tail = t_pad - t_all

    for u in range(n_sub):
        rows = slice(u * LANES, (u + 1) * LANES)
        kka_ref[u, 0:LANES, 0:past] = ckat_ref[0, rows, :].astype(BF16)
        kka_ref[u, 0:LANES, past:t_all] = kat_ref[0, rows, :].astype(BF16)
        kka_ref[u, LANES:LANES + AUG_ROWS, 0:t_all] = aug_ref[0]
        kka_ref[u, LANES + AUG_ROWS:, 0:t_all] = jnp.zeros((LANES - AUG_ROWS, t_all), BF16)
        kka_ref[u, :, t_all:] = jnp.zeros((2 * LANES, tail), BF16)
        vva_ref[u, :, 0:past] = cvat_ref[0, rows, :].astype(BF16)
        vva_ref[u, :, past:t_all] = vat_ref[0, rows, :].astype(BF16)
        vva_ref[u, :, t_all:] = jnp.zeros((LANES, tail), BF16)
        pair = half * n_sub + u
        lo, hi = _lane_halves(qa_ref[:, rows])
        qs = jnp.concatenate(
            [jnp.concatenate([lo, _piece_selector((tq, LANES), 2 * pair, N_A)], axis=1),
             jnp.concatenate([hi, _piece_selector((tq, LANES), 2 * pair + 1, N_A)], axis=1)], axis=0)
        s = jnp.dot(qs, kka_ref[u], preferred_element_type=F32)
        p, l = _softmax_rows(jnp.where(causal, s, NEG_INF))
        o = _nt_dot(p, vva_ref[u]) / l
        oa_ref[:, rows] = jnp.where(lane < HD_A, o[:tq], o[tq:]).astype(BF16)

        kkb_ref[u, :, 0:past] = ckbt_ref[0, rows, :].astype(BF16)
        kkb_ref[u, :, past:t_all] = kbt_ref[0, rows, :].astype(BF16)
        kkb_ref[u, :, t_all:] = jnp.zeros((LANES, tail), BF16)
        head = half * n_sub + u
        vvb_ref[u, 0:past, :] = cvb_ref[0, pl.ds(head, past, stride=N_B), :].astype(BF16)
        vvb_ref[u, past:t_all, :] = vb_ref[:, u * 2 * LANES:u * 2 * LANES + LANES]
        vvb_ref[u, t_all:, :] = jnp.zeros((tail, LANES), BF16)
        lo, hi = _lane_halves(qb_ref[:, rows])
        s = jnp.dot(jnp.concatenate([lo, hi], axis=0), kkb_ref[u], preferred_element_type=F32)
        slope = jnp.where(head == 0, ALIBI_SLOPES[0],
                          jnp.where(head == 1, ALIBI_SLOPES[1],
                                    jnp.where(head == 2, ALIBI_SLOPES[2], ALIBI_SLOPES[3])))
        p, l = _softmax_rows(jnp.where(visible, s + slope * shift, NEG_INF))
        o = jnp.dot(p, vvb_ref[u], preferred_element_type=F32) / l
        ob_ref[:, rows] = _diff_finish(o[:tq], o[tq:], lam, sub_ref, lambda_init).astype(BF16)


def _attn_cached(qa16, qb16, ckat, cvat, ckbt, cvb, kat, vat, kbt, vb16, aug, lams, sub, b, tq, lambda_init):
    past = ckat.shape[2]
    t_all = past + tq
    t_pad = -(-t_all // ATTN_BLOCK) * ATTN_BLOCK
    n_sub = N_PAIRS // 2
    hw = n_sub * LANES
    qspec = pl.BlockSpec((tq, hw), lambda bi, hf: (bi, hf))
    old_t = pl.BlockSpec((1, hw, past), lambda bi, hf: (bi, hf, 0))
    new_t = pl.BlockSpec((1, hw, tq), lambda bi, hf: (bi, hf, 0))
    small = [_resident((1, HD_B))] * 4 + [_resident((1, 2 * HD_B))]
    return pl.pallas_call(
        functools.partial(_attn_cached_kernel, lambda_init=lambda_init),
        grid=(b, 2),
        in_specs=[qspec, qspec, old_t, old_t, old_t,
                  pl.BlockSpec((1, past * N_B, LANES), lambda bi, hf: (bi, 0, 0)),
                  new_t, new_t, new_t,
                  pl.BlockSpec((tq, 2 * hw), lambda bi, hf: (bi, hf)),
                  pl.BlockSpec((1, AUG_ROWS, t_all), lambda bi, hf: (bi, 0, 0))] + small,
        out_specs=[qspec, qspec],
        out_shape=[jax.ShapeDtypeStruct((b * tq, D_A), BF16), jax.ShapeDtypeStruct((b * tq, D_B), BF16)],
        scratch_shapes=[pltpu.VMEM((n_sub, 2 * LANES, t_pad), BF16), pltpu.VMEM((n_sub, LANES, t_pad), BF16),
                        pltpu.VMEM((n_sub, LANES, t_pad), BF16), pltpu.VMEM((n_sub, t_pad, LANES), BF16)],
        compiler_params=_params(2),
        name="attn_cached",
    )(qa16, qb16, ckat, cvat, ckbt, cvb, kat, vat, kbt, vb16, aug, *lams, sub)


def _tokens_minor(x):
    b, t = x.shape[:2]
    return jnp.moveaxis(x.reshape(b, t, -1), 1, 2)


def _tokens_major(xt, tail):
    b, _, t = xt.shape
    return jnp.moveaxis(xt, 2, 1).reshape(b, t, *tail)


def _layer(x, cache, w, lambda_init):
    b, t, _ = x.shape
    n = b * t
    x1 = _ffn1(x.reshape(n, D_MODEL), w["gn1"], w["w1a"], w["w2a"])
    (kat, vat, lft, kbt, vb, qa16, kat16, va16, qb16, kbt16, vb16) = _proj(
        x1, w["gmix"], w["wa"], w["wf"], w["bf"], w["wb"], b, t)
    if cache is None:
        aug = _cum_aug(lft, blocked=True)
        oa, ob = _attn_prompt(qa16, qb16, kat16, kbt16, va16, vb16, aug, w["lams"], w["sub"], b, t, lambda_init)
    else:
        cfk, cfv, cflf, cdk, cdv = cache
        past = cfk.shape[1]
        aug = _cum_aug(jnp.concatenate([_tokens_minor(cflf.astype(F32)), lft], axis=2), blocked=False)
        oa, ob = _attn_cached(qa16, qb16, _tokens_minor(cfk), _tokens_minor(cfv), _tokens_minor(cdk),
                              cdv.reshape(b, past * N_B, 2 * HD_B), kat, vat, kbt, vb16, aug,
                              w["lams"], w["sub"], b, t, lambda_init)
    news = (_tokens_major(kat, (N_A, HD_A)), _tokens_major(vat, (N_A, HD_A)), _tokens_major(lft, (N_A,)),
            _tokens_major(kbt, (N_B, 2, HD_B)), vb.reshape(b, t, N_B, 2 * HD_B))
    return x1, oa, ob, news


def kernel(x_prompt, x_sample, cache_fox_k, cache_fox_v, cache_fox_logf, cache_diff_k, cache_diff_v,
           norm_ffn1, w_ffn1_in, w_ffn1_out, norm_mix, w_in, b_forget,
           lambda_q1, lambda_k1, lambda_q2, lambda_k2, diff_subln, w_out,
           norm_ffn2, w_ffn2_in, w_ffn2_out, norm_final):
    depth = w_in.shape[0]
    xp, xs = x_prompt, x_sample
    outs_p, outs_s = [], []
    gfin = norm_final.reshape(1, D_MODEL)
    for l in range(depth):
        lambda_init = 0.8 - 0.6 * math.exp(-0.3 * l)
        f_lo, f_hi = 3 * D_A, 3 * D_A + N_A
        w = {
            "gn1": norm_ffn1[l].reshape(1, D_MODEL),
            "w1a": w_ffn1_in[l].astype(BF16), "w2a": w_ffn1_out[l].astype(BF16),
            "gmix": norm_mix[l].reshape(1, D_MODEL),
            "wa": w_in[l][:, :f_lo].astype(BF16),
            "wf": jnp.pad(w_in[l][:, f_lo:f_hi], ((0, 0), (0, LANES - N_A))).astype(BF16),
            "bf": jnp.pad(b_forget[l], (0, LANES - N_A)).reshape(1, LANES),
            "wb": w_in[l][:, f_hi:].astype(BF16),
            "lams": [v[l].reshape(1, HD_B) for v in (lambda_q1, lambda_k1, lambda_q2, lambda_k2)],
            "sub": diff_subln[l].reshape(1, 2 * HD_B),
            "wo": w_out[l].astype(BF16),
            "gn2": norm_ffn2[l].reshape(1, D_MODEL),
            "w1b": w_ffn2_in[l].astype(BF16), "w2b": w_ffn2_out[l].astype(BF16),
        }
        streams = []
        for x, cache in ((xp, None),
                         (xs, (cache_fox_k[l], cache_fox_v[l], cache_fox_logf[l],
                               cache_diff_k[l], cache_diff_v[l]))):
            b, t, _ = x.shape
            x1, oa, ob, news = _layer(x, cache, w, lambda_init)
            y = _post(x1, oa, ob, w["wo"], w["gn2"], w["w1b"], w["w2b"], gfin, l == depth - 1)
            streams.append((y.reshape(b, t, D_MODEL), news))
        (xp, news_p), (xs, news_s) = streams
        outs_p.append(news_p)
        outs_s.append(news_s)
    stack = lambda outs, i: jnp.stack([o[i] for o in outs])
    return (xp, xs) + tuple(stack(outs_p, i) for i in range(5)) + tuple(stack(outs_s, i) for i in range(5))
```

```python
import functools
import math

import jax
import jax.numpy as jnp
from jax import lax
from jax.experimental import pallas as pl
from jax.experimental.pallas import tpu as pltpu

D_MODEL = 1024
D_FF = 2816
N_A = 8
HD_A = 64
N_B = 4
HD_B = 64
D_A = N_A * HD_A
D_B = N_B * 2 * HD_B
CHUNK = 64
NORM_EPS = 1e-6
NEG_INF = -1e30
ALIBI_SLOPES = (0.25, 0.0625, 0.015625, 0.00390625)

LOG2E = math.log2(math.e)
LANES = 128
MXU_WIDTH = 256
FF_CHUNKS = (1536, 1280)
assert sum(FF_CHUNKS) == D_FF and all(c % MXU_WIDTH == 0 for c in FF_CHUNKS)
SCAN_BATCH = 8
TOKEN_TILE = 512
ATTN_BLOCK = 256
AUG_ROWS = 32
VMEM_LIMIT = 56 * 1024 * 1024
N_PAIRS = N_A // 2

F32 = jnp.float32
BF16 = jnp.bfloat16


def _params(n_axes):
    return pltpu.CompilerParams(dimension_semantics=("arbitrary",) * n_axes,
                                vmem_limit_bytes=VMEM_LIMIT)


def _resident(shape):
    return pl.BlockSpec(shape, lambda *_: (0,) * len(shape), pipeline_mode=pl.Buffered(1))


def _rms(xf, g):
    ms = jnp.mean(xf * xf, axis=-1, keepdims=True)
    return xf * lax.rsqrt(ms + NORM_EPS) * g


def _swiglu_residual(x, gn_ref, w1_ref, w2_ref):
    h = _rms(x, gn_ref[...]).astype(BF16)
    acc = None
    lo = 0
    for width in FF_CHUNKS:
        g = jnp.dot(h, w1_ref[:, lo:lo + width], preferred_element_type=F32)
        u = jnp.dot(h, w1_ref[:, D_FF + lo:D_FF + lo + width], preferred_element_type=F32)
        a = (g * jax.nn.sigmoid(g) * u).astype(BF16)
        d = jnp.dot(a, w2_ref[lo:lo + width, :], preferred_element_type=F32)
        acc = d if acc is None else acc + d
        lo += width
    return x + 0.5 * acc


def _ffn1_kernel(x_ref, gn_ref, w1_ref, w2_ref, o_ref):
    o_ref[...] = _swiglu_residual(x_ref[...], gn_ref, w1_ref, w2_ref)


def _ffn1(x, gn, w1, w2):
    n = x.shape[0]
    tm = min(TOKEN_TILE, n)
    return pl.pallas_call(
        _ffn1_kernel,
        grid=(n // tm,),
        in_specs=[pl.BlockSpec((tm, D_MODEL), lambda i: (i, 0)),
                  _resident((1, D_MODEL)), _resident(w1.shape), _resident(w2.shape)],
        out_specs=pl.BlockSpec((tm, D_MODEL), lambda i: (i, 0)),
        out_shape=jax.ShapeDtypeStruct((n, D_MODEL), F32),
        compiler_params=_params(1),
        name="ffn1",
    )(x, gn, w1, w2)


def _post_kernel(x_ref, oa_ref, ob_ref, wo_ref, gn_ref, w1_ref, w2_ref, gf_ref, y_ref, *, final):
    x = x_ref[...]
    x = x + jnp.dot(oa_ref[...], wo_ref[:D_A, :], preferred_element_type=F32)
    x = x + jnp.dot(ob_ref[...], wo_ref[D_A:, :], preferred_element_type=F32)
    x = _swiglu_residual(x, gn_ref, w1_ref, w2_ref)
    y_ref[...] = _rms(x, gf_ref[...]) if final else x


def _post(x, oa, ob, wo, gn, w1, w2, gf, final):
    n = x.shape[0]
    tm = min(TOKEN_TILE, n)
    row = lambda i: (i, 0)
    return pl.pallas_call(
        functools.partial(_post_kernel, final=final),
        grid=(n // tm,),
        in_specs=[pl.BlockSpec((tm, D_MODEL), row), pl.BlockSpec((tm, D_A), row),
                  pl.BlockSpec((tm, D_B), row), _resident(wo.shape), _resident((1, D_MODEL)),
                  _resident(w1.shape), _resident(w2.shape), _resident((1, D_MODEL))],
        out_specs=pl.BlockSpec((tm, D_MODEL), row),
        out_shape=jax.ShapeDtypeStruct((n, D_MODEL), F32),
        compiler_params=_params(1),
        name="post",
    )(x, oa, ob, wo, gn, w1, w2, gf)


def _with_ones(v, width):
    ones = jnp.ones((v.shape[0], width), BF16)
    parts = []
    for g in range(v.shape[1] // width):
        parts += [v[:, g * width:(g + 1) * width].astype(BF16), ones]
    return jnp.concatenate(parts, axis=1)


def _proj_kernel(x_ref, gn_ref, wa_ref, wf_ref, bf_ref, wb_ref,
                 kat_ref, vat_ref, lft_ref, kbt_ref, vb_ref,
                 qa16_ref, kat16_ref, va16_ref, qb16_ref, kbt16_ref, vb16_ref, *, seg, kblk):
    tm = x_ref.shape[0]
    h = _rms(x_ref[...], gn_ref[...]).astype(BF16)
    pa = jnp.dot(h, wa_ref[...], preferred_element_type=F32)
    pb = jnp.dot(h, wb_ref[...], preferred_element_type=F32)
    z = jnp.dot(h, wf_ref[...], preferred_element_type=F32) + bf_ref[...]
    logf = jnp.minimum(z, 0.0) - jnp.log1p(jnp.exp(-jnp.abs(z)))
    qa16_ref[...] = (pa[:, :D_A] * (HD_A ** -0.5 * LOG2E)).astype(BF16)
    qb16_ref[...] = (pb[:, :D_B] * (HD_B ** -0.5 * LOG2E)).astype(BF16)
    va16_ref[...] = _with_ones(pa[:, 2 * D_A:], LANES)
    vb = pb[:, 2 * D_B:]
    vb16_ref[...] = _with_ones(vb, LANES)
    for hh in range(N_B):
        vb_ref[pl.ds(hh, tm, stride=N_B), :] = vb[:, hh * LANES:(hh + 1) * LANES]
    kat = pa[:, D_A:2 * D_A].T
    vat = pa[:, 2 * D_A:].T
    kbt = pb[:, D_B:2 * D_B].T
    lft = logf.T
    for bb in range(tm // seg):
        cols = slice(bb * seg, (bb + 1) * seg)
        kat_ref[bb] = kat[:, cols]
        vat_ref[bb] = vat[:, cols]
        kbt_ref[bb] = kbt[:, cols]
        lft_ref[bb] = lft[:N_A, cols]
        for jj in range(seg // kblk):
            kc = slice(bb * seg + jj * kblk, bb * seg + (jj + 1) * kblk)
            kat16_ref[bb, jj] = kat[:, kc].astype(BF16)
            kbt16_ref[bb, jj] = kbt[:, kc].astype(BF16)


def _proj(x, gn, wa, wf, bf, wb, b, t):
    n = x.shape[0]
    tm = min(TOKEN_TILE, n)
    seg = min(tm, t)
    nb = tm // seg
    nt = t // seg
    kblk = min(ATTN_BLOCK, seg)
    row = lambda i: (i, 0)
    tok = lambda i: (i // nt, 0, i % nt)
    blk = lambda i: (i // nt, i % nt, 0, 0)
    f32t = jax.ShapeDtypeStruct((b, D_A, t), F32)
    b16t = jax.ShapeDtypeStruct((b, t // kblk, D_A, kblk), BF16)
    tspec = pl.BlockSpec((nb, D_A, seg), tok)
    kspec = pl.BlockSpec((nb, seg // kblk, D_A, kblk), blk)
    return pl.pallas_call(
        functools.partial(_proj_kernel, seg=seg, kblk=kblk),
        grid=(n // tm,),
        in_specs=[pl.BlockSpec((tm, D_MODEL), row), _resident((1, D_MODEL)), _resident(wa.shape),
                  _resident(wf.shape), _resident(bf.shape), _resident(wb.shape)],
        out_specs=[tspec, tspec, pl.BlockSpec((nb, N_A, seg), tok), tspec,
                   pl.BlockSpec((tm * N_B, LANES), row),
                   pl.BlockSpec((tm, D_A), row), kspec, pl.BlockSpec((tm, 2 * D_A), row),
                   pl.BlockSpec((tm, D_B), row), kspec, pl.BlockSpec((tm, 2 * D_B), row)],
        out_shape=[f32t, f32t, jax.ShapeDtypeStruct((b, N_A, t), F32), f32t,
                   jax.ShapeDtypeStruct((n * N_B, LANES), F32),
                   jax.ShapeDtypeStruct((n, D_A), BF16), b16t, jax.ShapeDtypeStruct((n, 2 * D_A), BF16),
                   jax.ShapeDtypeStruct((n, D_B), BF16), b16t, jax.ShapeDtypeStruct((n, 2 * D_B), BF16)],
        compiler_params=_params(1),
        name="proj",
    )(x, gn, wa, wf, bf, wb)


def _split3(x):
    hi = x.astype(BF16).astype(F32)
    r = x - hi
    mid = r.astype(BF16).astype(F32)
    lo = (r - mid).astype(BF16).astype(F32)
    return hi, mid, lo


def _cum_kernel(lft_ref, aug_ref, pad_ref, *, t, blocked):
    w = ATTN_BLOCK
    nb = lft_ref.shape[0]
    rows = nb * N_A
    pad_ref[...] = jnp.zeros(pad_ref.shape, F32)
    for bb in range(nb):
        pad_ref[bb * N_A:(bb + 1) * N_A, 0:t] = lft_ref[bb]
    r = lax.broadcasted_iota(jnp.int32, (w, w), 0)
    c = lax.broadcasted_iota(jnp.int32, (w, w), 1)
    tri = jnp.where(r <= c, 1.0, 0.0).astype(BF16)
    zero = jnp.zeros((N_A, w), F32)
    carry = jnp.zeros((rows, 1), F32)
    for s in range(pad_ref.shape[1] // w):
        lo_col = s * w
        cols = min(w, t - lo_col)
        x = jnp.concatenate(_split3(pad_ref[:, lo_col:lo_col + w]), axis=0).astype(BF16)
        y = jnp.dot(x, tri, preferred_element_type=F32)
        cs = y[0:rows] + y[rows:2 * rows] + y[2 * rows:3 * rows] + carry
        carry = cs[:, w - 1:w]
        pieces = _split3(cs * -LOG2E)
        for bb in range(nb):
            sl = slice(bb * N_A, (bb + 1) * N_A)
            aug = jnp.concatenate([p[sl] for p in pieces] + [zero], axis=0).astype(BF16)
            if blocked:
                aug_ref[bb, s] = aug
            else:
                aug_ref[bb, :, lo_col:lo_col + cols] = aug[:, :cols]


def _cum_aug(lft, blocked):
    b, _, t = lft.shape
    w = ATTN_BLOCK
    t_pad = -(-t // w) * w
    nb = math.gcd(b, SCAN_BATCH)
    if blocked:
        out_shape = jax.ShapeDtypeStruct((b, t // w, AUG_ROWS, w), BF16)
        out_spec = pl.BlockSpec((nb, t // w, AUG_ROWS, w), lambda i: (i, 0, 0, 0))
    else:
        out_shape = jax.ShapeDtypeStruct((b, AUG_ROWS, t), BF16)
        out_spec = pl.BlockSpec((nb, AUG_ROWS, t), lambda i: (i, 0, 0))
    return pl.pallas_call(
        functools.partial(_cum_kernel, t=t, blocked=blocked),
        grid=(b // nb,),
        in_specs=[pl.BlockSpec((nb, N_A, t), lambda i: (i, 0, 0))],
        out_specs=out_spec,
        out_shape=out_shape,
        scratch_shapes=[pltpu.VMEM((nb * N_A, t_pad), F32)],
        compiler_params=_params(1),
        name="cum_aug",
    )(lft)


def _lane_halves(q):
    lane = lax.broadcasted_iota(jnp.int32, q.shape, 1)
    zero = jnp.zeros_like(q)
    return jnp.where(lane < HD_A, q, zero), jnp.where(lane >= HD_A, q, zero)


def _piece_selector(shape, head, n_heads):
    lane = lax.broadcasted_iota(jnp.int32, shape, 1)
    hit = (lane == head) | (lane == n_heads + head) | (lane == 2 * n_heads + head)
    return jnp.where(hit, 1.0, 0.0).astype(BF16)


def _alibi_rows(k_pos):
    w = k_pos.shape[1]
    row = lax.broadcasted_iota(jnp.int32, (4 * N_B, 1), 0)
    head = row % N_B
    slope = jnp.zeros((4 * N_B, 1), F32)
    for h in range(N_B):
        slope = jnp.where(head == h, ALIBI_SLOPES[h] * LOG2E, slope)
    hi, mid, lo = _split3(slope * k_pos)
    piece = jnp.where(row < N_B, hi, jnp.where(row < 2 * N_B, mid, jnp.where(row < 3 * N_B, lo, 0.0)))
    return jnp.concatenate([piece.astype(BF16), jnp.zeros((LANES - 4 * N_B, w), BF16)], axis=0)


def _alibi_fold(slope, q_pos, k_pos):
    return (-2.0 * LOG2E * slope) * jnp.maximum(k_pos - q_pos, 0).astype(F32)


def _diff_finish(o1, o2, lam, sub_ref, lambda_init):
    return _rms(o1 - lam * o2, sub_ref[...]) * (1.0 - lambda_init)


def _lambda(lq1_ref, lk1_ref, lq2_ref, lk2_ref, lambda_init):
    return (jnp.exp(jnp.sum(lq1_ref[...] * lk1_ref[...], axis=1, keepdims=True))
            - jnp.exp(jnp.sum(lq2_ref[...] * lk2_ref[...], axis=1, keepdims=True)) + lambda_init)


def _attn_prompt_kernel(qa_ref, qb_ref, kat_ref, kbt_ref, va_ref, vb_ref, aug_ref,
                        lq1_ref, lk1_ref, lq2_ref, lk2_ref, sub_ref, oa_ref, ob_ref,
                        qs_ref, m_ref, acc_ref, p_ref, tbl_ref, *, lambda_init):
    bi, i = pl.program_id(0), pl.program_id(1)
    tq = qa_ref.shape[0]
    n_chain = N_PAIRS + N_B

    @pl.when((bi == 0) & (i == 0))
    def _():
        r = lax.broadcasted_iota(jnp.int32, (2 * tq, tq), 0)
        c = lax.broadcasted_iota(jnp.int32, (2 * tq, tq), 1)
        q_idx = jnp.where(r >= tq, r - tq, r)
        tbl_ref[0] = jnp.where(q_idx >= c, 0.0, NEG_INF)
        visible = (q_idx // CHUNK) >= (c // CHUNK)
        for h in range(N_B):
            tbl_ref[1 + h] = jnp.where(visible, _alibi_fold(ALIBI_SLOPES[h], q_idx, c), NEG_INF)
        for ch in range(n_chain):
            if ch < N_PAIRS:
                qs_ref[ch, 0:tq, LANES:] = _piece_selector((tq, LANES), 2 * ch, N_A)
                qs_ref[ch, tq:, LANES:] = _piece_selector((tq, LANES), 2 * ch + 1, N_A)
            else:
                sel = _piece_selector((tq, LANES), ch - N_PAIRS, N_B)
                qs_ref[ch, 0:tq, LANES:] = sel
                qs_ref[ch, tq:, LANES:] = sel

    for ch in range(n_chain):
        q_ref, g = (qa_ref, ch) if ch < N_PAIRS else (qb_ref, ch - N_PAIRS)
        lo, hi = _lane_halves(q_ref[:, g * LANES:(g + 1) * LANES])
        qs_ref[ch, 0:tq, 0:LANES] = lo
        qs_ref[ch, tq:, 0:LANES] = hi

    def chain_refs(ch):
        if ch < N_PAIRS:
            return kat_ref, va_ref, ch, 0
        return kbt_ref, vb_ref, ch - N_PAIRS, 1 + ch - N_PAIRS

    def values(ch, j):
        _, v_ref, g, _ = chain_refs(ch)
        return v_ref[pl.ds(pl.multiple_of(j * tq, tq), tq), g * 2 * LANES:(g + 1) * 2 * LANES]

    def step(j, j_pending):
        first = j_pending is None
        aug_a = jnp.concatenate([aug_ref[0, j], jnp.zeros((LANES - AUG_ROWS, tq), BF16)], axis=0)
        aug_b = _alibi_rows((j * tq + lax.broadcasted_iota(jnp.int32, (1, tq), 1)).astype(F32))
        for ch in range(n_chain):
            kt_ref, _, g, tbl = chain_refs(ch)
            kk = jnp.concatenate([kt_ref[0, j, g * LANES:(g + 1) * LANES, :], aug_b if ch >= N_PAIRS else aug_a],
                                 axis=0)
            s = jnp.dot(qs_ref[ch], kk, preferred_element_type=F32)
            if first:
                s = s + tbl_ref[tbl]
                m_new = jnp.broadcast_to(jnp.max(s, axis=1, keepdims=True), (2 * tq, LANES))
                acc_ref[ch] = jnp.zeros(acc_ref.shape[1:], F32)
            else:
                pv = jnp.dot(p_ref[ch], values(ch, j_pending), preferred_element_type=F32)
                m_prev = m_ref[ch]
                m_new = jnp.maximum(m_prev, jnp.max(s, axis=1, keepdims=True))
                alpha = jnp.exp2(m_prev - m_new)
                acc_ref[ch] = jnp.concatenate([alpha, alpha], axis=1) * (acc_ref[ch] + pv)
            p_ref[ch] = jnp.exp2(s - jnp.concatenate([m_new, m_new], axis=1)).astype(BF16)
            m_ref[ch] = m_new

    def body(j, carry):
        step(j, jnp.where(j == 0, i, j - 1))
        return carry

    step(i, None)
    lax.fori_loop(0, i, body, 0)
    j_last = jnp.maximum(i - 1, 0)

    lane = lax.broadcasted_iota(jnp.int32, (tq, LANES), 1)
    lam = _lambda(lq1_ref, lk1_ref, lq2_ref, lk2_ref, lambda_init)
    for ch in range(n_chain):
        acc = acc_ref[ch] + jnp.dot(p_ref[ch], values(ch, j_last), preferred_element_type=F32)
        o = acc[:, 0:LANES] / acc[:, LANES:]
        if ch < N_PAIRS:
            oa_ref[:, ch * LANES:(ch + 1) * LANES] = jnp.where(lane < HD_A, o[:tq], o[tq:]).astype(BF16)
        else:
            h = ch - N_PAIRS
            ob_ref[:, h * LANES:(h + 1) * LANES] = _diff_finish(
                o[:tq], o[tq:], lam, sub_ref, lambda_init).astype(BF16)


def _attn_prompt(qa16, qb16, kat16, kbt16, va16, vb16, aug, lams, sub, b, t, lambda_init):
    tq = ATTN_BLOCK
    nq = t // tq
    qspec = pl.BlockSpec((tq, D_A), lambda bi, i: (bi * nq + i, 0))
    kspec = pl.BlockSpec((1, nq, D_A, tq), lambda bi, i: (bi, 0, 0, 0))
    vspec = pl.BlockSpec((t, 2 * D_A), lambda bi, i: (bi, 0))
    small = [_resident((1, HD_B))] * 4 + [_resident((1, 2 * HD_B))]
    n_chain = N_PAIRS + N_B
    return pl.pallas_call(
        functools.partial(_attn_prompt_kernel, lambda_init=lambda_init),
        grid=(b, nq),
        in_specs=[qspec, qspec, kspec, kspec, vspec, vspec,
                  pl.BlockSpec((1, nq, AUG_ROWS, tq), lambda bi, i: (bi, 0, 0, 0))] + small,
        out_specs=[qspec, qspec],
        out_shape=[jax.ShapeDtypeStruct((b * t, D_A), BF16), jax.ShapeDtypeStruct((b * t, D_B), BF16)],
        scratch_shapes=[pltpu.VMEM((n_chain, 2 * tq, 2 * LANES), BF16),
                        pltpu.VMEM((n_chain, 2 * tq, LANES), F32),
                        pltpu.VMEM((n_chain, 2 * tq, 2 * LANES), F32),
                        pltpu.VMEM((n_chain, 2 * tq, tq), BF16),
                        pltpu.VMEM((1 + N_B, 2 * tq, tq), F32)],
        compiler_params=_params(2),
        name="attn_prompt",
    )(qa16, qb16, kat16, kbt16, va16, vb16, aug, *lams, sub)


def _softmax_rows(s):
    m = jnp.max(s, axis=1, keepdims=True)
    p = jnp.exp2(s - m)
    return p.astype(BF16), jnp.sum(p, axis=1, keepdims=True)


def _nt_dot(a, b):
    return lax.dot_general(a, b, (((1,), (1,)), ((), ())), preferred_element_type=F32)


def _attn_cached_kernel(qa_ref, qb_ref, ckat_ref, cvat_ref, ckbt_ref, cvb_ref,
                        kat_ref, vat_ref, kbt_ref, vb_ref, aug_ref,
                        lq1_ref, lk1_ref, lq2_ref, lk2_ref, sub_ref, oa_ref, ob_ref,
                        kka_ref, vva_ref, kkb_ref, vvb_ref, *, lambda_init):
    half = pl.program_id(1)
    tq = qa_ref.shape[0]
    past = ckat_ref.shape[2]
    t_all = past + tq
    t_pad = kka_ref.shape[2]
    n_sub = N_PAIRS // 2
    r = lax.broadcasted_iota(jnp.int32, (2 * tq, t_pad), 0)
    c = lax.broadcasted_iota(jnp.int32, (2 * tq, t_pad), 1)
    q_pos = past + jnp.where(r >= tq, r - tq, r)
    causal = q_pos >= c
    visible = ((q_pos // CHUNK) >= (c // CHUNK)) & (c < t_all)
    shift = (c - 2 * jnp.maximum(c - q_pos, 0)).astype(F32) * LOG2E
    lane = lax.broadcasted_iota(jnp.int32, (tq, LANES), 1)
    lam = _lambda(lq1_ref, lk1_ref, lq2_ref, lk2_ref, lambda_init)
    tail = t_pad - t_all

    for u in range(n_sub):
        rows = slice(u * LANES, (u + 1) * LANES)
        kka_ref[u, 0:LANES, 0:past] = ckat_ref[0, rows, :].astype(BF16)
        kka_ref[u, 0:LANES, past:t_all] = kat_ref[0, rows, :].astype(BF16)
        kka_ref[u, LANES:LANES + AUG_ROWS, 0:t_all] = aug_ref[0]
        kka_ref[u, LANES + AUG_ROWS:, 0:t_all] = jnp.zeros((LANES - AUG_ROWS, t_all), BF16)
        kka_ref[u, :, t_all:] = jnp.zeros((2 * LANES, tail), BF16)
        vva_ref[u, :, 0:past] = cvat_ref[0, rows, :].astype(BF16)
        vva_ref[u, :, past:t_all] = vat_ref[0, rows, :].astype(BF16)
        vva_ref[u, :, t_all:] = jnp.zeros((LANES, tail), BF16)
        pair = half * n_sub + u
        lo, hi = _lane_halves(qa_ref[:, rows])
        qs = jnp.concatenate(
            [jnp.concatenate([lo, _piece_selector((tq, LANES), 2 * pair, N_A)], axis=1),
             jnp.concatenate([hi, _piece_selector((tq, LANES), 2 * pair + 1, N_A)], axis=1)], axis=0)
        s = jnp.dot(qs, kka_ref[u], preferred_element_type=F32)
        p, l = _softmax_rows(jnp.where(causal, s, NEG_INF))
        o = _nt_dot(p, vva_ref[u]) / l
        oa_ref[:, rows] = jnp.where(lane < HD_A, o[:tq], o[tq:]).astype(BF16)

        kkb_ref[u, :, 0:past] = ckbt_ref[0, rows, :].astype(BF16)
        kkb_ref[u, :, past:t_all] = kbt_ref[0, rows, :].astype(BF16)
        kkb_ref[u, :, t_all:] = jnp.zeros((LANES, tail), BF16)
        head = half * n_sub + u
        vvb_ref[u, 0:past, :] = cvb_ref[0, pl.ds(head, past, stride=N_B), :].astype(BF16)
        vvb_ref[u, past:t_all, :] = vb_ref[:, u * 2 * LANES:u * 2 * LANES + LANES]
        vvb_ref[u, t_all:, :] = jnp.zeros((tail, LANES), BF16)
        lo, hi = _lane_halves(qb_ref[:, rows])
        s = jnp.dot(jnp.concatenate([lo, hi], axis=0), kkb_ref[u], preferred_element_type=F32)
        slope = jnp.where(head == 0, ALIBI_SLOPES[0],
                          jnp.where(head == 1, ALIBI_SLOPES[1],
                                    jnp.where(head == 2, ALIBI_SLOPES[2], ALIBI_SLOPES[3])))
        p, l = _softmax_rows(jnp.where(visible, s + slope * shift, NEG_INF))
        o = jnp.dot(p, vvb_ref[u], preferred_element_type=F32) / l
        ob_ref[:, rows] = _diff_finish(o[:tq], o[tq:], lam, sub_ref, lambda_init).astype(BF16)


def _attn_cached(qa16, qb16, ckat, cvat, ckbt, cvb, kat, vat, kbt, vb16, aug, lams, sub, b, tq, lambda_init):
    past = ckat.shape[2]
    t_all = past + tq
    t_pad = -(-t_all // ATTN_BLOCK) * ATTN_BLOCK
    n_sub = N_PAIRS // 2
    hw = n_sub * LANES
    qspec = pl.BlockSpec((tq, hw), lambda bi, hf: (bi, hf))
    old_t = pl.BlockSpec((1, hw, past), lambda bi, hf: (bi, hf, 0))
    new_t = pl.BlockSpec((1, hw, tq), lambda bi, hf: (bi, hf, 0))
    small = [_resident((1, HD_B))] * 4 + [_resident((1, 2 * HD_B))]
    return pl.pallas_call(
        functools.partial(_attn_cached_kernel, lambda_init=lambda_init),
        grid=(b, 2),
        in_specs=[qspec, qspec, old_t, old_t, old_t,
                  pl.BlockSpec((1, past * N_B, LANES), lambda bi, hf: (bi, 0, 0)),
                  new_t, new_t, new_t,
                  pl.BlockSpec((tq, 2 * hw), lambda bi, hf: (bi, hf)),
                  pl.BlockSpec((1, AUG_ROWS, t_all), lambda bi, hf: (bi, 0, 0))] + small,
        out_specs=[qspec, qspec],
        out_shape=[jax.ShapeDtypeStruct((b * tq, D_A), BF16), jax.ShapeDtypeStruct((b * tq, D_B), BF16)],
        scratch_shapes=[pltpu.VMEM((n_sub, 2 * LANES, t_pad), BF16), pltpu.VMEM((n_sub, LANES, t_pad), BF16),
                        pltpu.VMEM((n_sub, LANES, t_pad), BF16), pltpu.VMEM((n_sub, t_pad, LANES), BF16)],
        compiler_params=_params(2),
        name="attn_cached",
    )(qa16, qb16, ckat, cvat, ckbt, cvb, kat, vat, kbt, vb16, aug, *lams, sub)


def _tokens_minor(x):
    b, t = x.shape[:2]
    return jnp.moveaxis(x.reshape(b, t, -1), 1, 2)


def _tokens_major(xt, tail):
    b, _, t = xt.shape
    return jnp.moveaxis(xt, 2, 1).reshape(b, t, *tail)


def _layer(x, cache, w, lambda_init):
    b, t, _ = x.shape
    n = b * t
    x1 = _ffn1(x.reshape(n, D_MODEL), w["gn1"], w["w1a"], w["w2a"])
    (kat, vat, lft, kbt, vb, qa16, kat16, va16, qb16, kbt16, vb16) = _proj(
        x1, w["gmix"], w["wa"], w["wf"], w["bf"], w["wb"], b, t)
    if cache is None:
        aug = _cum_aug(lft, blocked=True)
        oa, ob = _attn_prompt(qa16, qb16, kat16, kbt16, va16, vb16, aug, w["lams"], w["sub"], b, t, lambda_init)
    else:
        cfk, cfv, cflf, cdk, cdv = cache
        past = cfk.shape[1]
        aug = _cum_aug(jnp.concatenate([_tokens_minor(cflf.astype(F32)), lft], axis=2), blocked=False)
        oa, ob = _attn_cached(qa16, qb16, _tokens_minor(cfk), _tokens_minor(cfv), _tokens_minor(cdk),
                              cdv.reshape(b, past * N_B, 2 * HD_B), kat, vat, kbt, vb16, aug,
                              w["lams"], w["sub"], b, t, lambda_init)
    news = (_tokens_major(kat, (N_A, HD_A)), _tokens_major(vat, (N_A, HD_A)), _tokens_major(lft, (N_A,)),
            _tokens_major(kbt, (N_B, 2, HD_B)), vb.reshape(b, t, N_B, 2 * HD_B))
    return x1, oa, ob, news


def kernel(x_prompt, x_sample, cache_fox_k, cache_fox_v, cache_fox_logf, cache_diff_k, cache_diff_v,
           norm_ffn1, w_ffn1_in, w_ffn1_out, norm_mix, w_in, b_forget,
           lambda_q1, lambda_k1, lambda_q2, lambda_k2, diff_subln, w_out,
           norm_ffn2, w_ffn2_in, w_ffn2_out, norm_final):
    depth = w_in.shape[0]
    xp, xs = x_prompt, x_sample
    outs_p, outs_s = [], []
    gfin = norm_final.reshape(1, D_MODEL)
    for l in range(depth):
        lambda_init = 0.8 - 0.6 * math.exp(-0.3 * l)
        f_lo, f_hi = 3 * D_A, 3 * D_A + N_A
        w = {
            "gn1": norm_ffn1[l].reshape(1, D_MODEL),
            "w1a": w_ffn1_in[l].astype(BF16), "w2a": w_ffn1_out[l].astype(BF16),
            "gmix": norm_mix[l].reshape(1, D_MODEL),
            "wa": w_in[l][:, :f_lo].astype(BF16),
            "wf": jnp.pad(w_in[l][:, f_lo:f_hi], ((0, 0), (0, LANES - N_A))).astype(BF16),
            "bf": jnp.pad(b_forget[l], (0, LANES - N_A)).reshape(1, LANES),
            "wb": w_in[l][:, f_hi:].astype(BF16),
            "lams": [v[l].reshape(1, HD_B) for v in (lambda_q1, lambda_k1, lambda_q2, lambda_k2)],
            "sub": diff_subln[l].reshape(1, 2 * HD_B),
            "wo": w_out[l].astype(BF16),
            "gn2": norm_ffn2[l].reshape(1, D_MODEL),
            "w1b": w_ffn2_in[l].astype(BF16), "w2b": w_ffn2_out[l].astype(BF16),
        }
        streams = []
        for x, cache in ((xp, None),
                         (xs, (cache_fox_k[l], cache_fox_v[l], cache_fox_logf[l],
                               cache_diff_k[l], cache_diff_v[l]))):
            b, t, _ = x.shape
            x1, oa, ob, news = _layer(x, cache, w, lambda_init)
            y = _post(x1, oa, ob, w["wo"], w["gn2"], w["w1b"], w["w2b"], gfin, l == depth - 1)
            streams.append((y.reshape(b, t, D_MODEL), news))
        (xp, news_p), (xs, news_s) = streams
        outs_p.append(news_p)
        outs_s.append(news_s)
    stack = lambda outs, i: jnp.stack([o[i] for o in outs])
    return (xp, xs) + tuple(stack(outs_p, i) for i in range(5)) + tuple(stack(outs_s, i) for i in range(5))
```

```python
import functools
import math

import jax
import jax.numpy as jnp
from jax import lax
from jax.experimental import pallas as pl
from jax.experimental.pallas import tpu as pltpu

D_MODEL = 1024
D_FF = 2816
N_A = 8
HD_A = 64
N_B = 4
HD_B = 64
D_A = N_A * HD_A
D_B = N_B * 2 * HD_B
CHUNK = 64
NORM_EPS = 1e-6
NEG_INF = -1e30
ALIBI_SLOPES = (0.25, 0.0625, 0.015625, 0.00390625)

LOG2E = math.log2(math.e)
LANES = 128
MXU_WIDTH = 256
FF_CHUNKS = (1536, 1280)
assert sum(FF_CHUNKS) == D_FF and all(c % MXU_WIDTH == 0 for c in FF_CHUNKS)
SCAN_BATCH = 8
TOKEN_TILE = 512
ATTN_BLOCK = 256
AUG_ROWS = 32
VMEM_LIMIT = 56 * 1024 * 1024
N_PAIRS = N_A // 2

F32 = jnp.float32
BF16 = jnp.bfloat16


def _params(n_axes):
    return pltpu.CompilerParams(dimension_semantics=("arbitrary",) * n_axes,
                                vmem_limit_bytes=VMEM_LIMIT)


def _resident(shape):
    return pl.BlockSpec(shape, lambda *_: (0,) * len(shape), pipeline_mode=pl.Buffered(1))


def _rms(xf, g):
    ms = jnp.mean(xf * xf, axis=-1, keepdims=True)
    return xf * lax.rsqrt(ms + NORM_EPS) * g


def _swiglu_residual(x, gn_ref, w1_ref, w2_ref):
    h = _rms(x, gn_ref[...]).astype(BF16)
    acc = None
    lo = 0
    for width in FF_CHUNKS:
        g = jnp.dot(h, w1_ref[:, lo:lo + width], preferred_element_type=F32)
        u = jnp.dot(h, w1_ref[:, D_FF + lo:D_FF + lo + width], preferred_element_type=F32)
        a = (g * jax.nn.sigmoid(g) * u).astype(BF16)
        d = jnp.dot(a, w2_ref[lo:lo + width, :], preferred_element_type=F32)
        acc = d if acc is None else acc + d
        lo += width
    return x + 0.5 * acc


def _ffn1_kernel(x_ref, gn_ref, w1_ref, w2_ref, o_ref):
    o_ref[...] = _swiglu_residual(x_ref[...], gn_ref, w1_ref, w2_ref)


def _ffn1(x, gn, w1, w2):
    n = x.shape[0]
    tm = min(TOKEN_TILE, n)
    return pl.pallas_call(
        _ffn1_kernel,
        grid=(n // tm,),
        in_specs=[pl.BlockSpec((tm, D_MODEL), lambda i: (i, 0)),
                  _resident((1, D_MODEL)), _resident(w1.shape), _resident(w2.shape)],
        out_specs=pl.BlockSpec((tm, D_MODEL), lambda i: (i, 0)),
        out_shape=jax.ShapeDtypeStruct((n, D_MODEL), F32),
        compiler_params=_params(1),
        name="ffn1",
    )(x, gn, w1, w2)


def _post_kernel(x_ref, oa_ref, ob_ref, wo_ref, gn_ref, w1_ref, w2_ref, gf_ref, y_ref, *, final):
    x = x_ref[...]
    x = x + jnp.dot(oa_ref[...], wo_ref[:D_A, :], preferred_element_type=F32)
    x = x + jnp.dot(ob_ref[...], wo_ref[D_A:, :], preferred_element_type=F32)
    x = _swiglu_residual(x, gn_ref, w1_ref, w2_ref)
    y_ref[...] = _rms(x, gf_ref[...]) if final else x


def _post(x, oa, ob, wo, gn, w1, w2, gf, final):
    n = x.shape[0]
    tm = min(TOKEN_TILE, n)
    row = lambda i: (i, 0)
    return pl.pallas_call(
        functools.partial(_post_kernel, final=final),
        grid=(n // tm,),
        in_specs=[pl.BlockSpec((tm, D_MODEL), row), pl.BlockSpec((tm, D_A), row),
                  pl.BlockSpec((tm, D_B), row), _resident(wo.shape), _resident((1, D_MODEL)),
                  _resident(w1.shape), _resident(w2.shape), _resident((1, D_MODEL))],
        out_specs=pl.BlockSpec((tm, D_MODEL), row),
        out_shape=jax.ShapeDtypeStruct((n, D_MODEL), F32),
        compiler_params=_params(1),
        name="post",
    )(x, oa, ob, wo, gn, w1, w2, gf)


def _with_ones(v, width):
    ones = jnp.ones((v.shape[0], width), BF16)
    parts = []
    for g in range(v.shape[1] // width):
        parts += [v[:, g * width:(g + 1) * width].astype(BF16), ones]
    return jnp.concatenate(parts, axis=1)


def _proj_kernel(x_ref, gn_ref, wa_ref, wf_ref, bf_ref, wb_ref,
                 kat_ref, vat_ref, lft_ref, kbt_ref, vb_ref,
                 qa16_ref, kat16_ref, va16_ref, qb16_ref, kbt16_ref, vb16_ref, *, seg, kblk):
    tm = x_ref.shape[0]
    h = _rms(x_ref[...], gn_ref[...]).astype(BF16)
    pa = jnp.dot(h, wa_ref[...], preferred_element_type=F32)
    pb = jnp.dot(h, wb_ref[...], preferred_element_type=F32)
    z = jnp.dot(h, wf_ref[...], preferred_element_type=F32) + bf_ref[...]
    logf = jnp.minimum(z, 0.0) - jnp.log1p(jnp.exp(-jnp.abs(z)))
    qa16_ref[...] = (pa[:, :D_A] * (HD_A ** -0.5 * LOG2E)).astype(BF16)
    qb16_ref[...] = (pb[:, :D_B] * (HD_B ** -0.5 * LOG2E)).astype(BF16)
    va16_ref[...] = _with_ones(pa[:, 2 * D_A:], LANES)
    vb = pb[:, 2 * D_B:]
    vb16_ref[...] = _with_ones(vb, LANES)
    for hh in range(N_B):
        vb_ref[pl.ds(hh, tm, stride=N_B), :] = vb[:, hh * LANES:(hh + 1) * LANES]
    kat = pa[:, D_A:2 * D_A].T
    vat = pa[:, 2 * D_A:].T
    kbt = pb[:, D_B:2 * D_B].T
    lft = logf.T
    for bb in range(tm // seg):
        cols = slice(bb * seg, (bb + 1) * seg)
        kat_ref[bb] = kat[:, cols]
        vat_ref[bb] = vat[:, cols]
        kbt_ref[bb] = kbt[:, cols]
        lft_ref[bb] = lft[:N_A, cols]
        for jj in range(seg // kblk):
            kc = slice(bb * seg + jj * kblk, bb * seg + (jj + 1) * kblk)
            kat16_ref[bb, jj] = kat[:, kc].astype(BF16)
            kbt16_ref[bb, jj] = kbt[:, kc].astype(BF16)


def _proj(x, gn, wa, wf, bf, wb, b, t):
    n = x.shape[0]
    tm = min(TOKEN_TILE, n)
    seg = min(tm, t)
    nb = tm // seg
    nt = t // seg
    kblk = min(ATTN_BLOCK, seg)
    row = lambda i: (i, 0)
    tok = lambda i: (i // nt, 0, i % nt)
    blk = lambda i: (i // nt, i % nt, 0, 0)
    f32t = jax.ShapeDtypeStruct((b, D_A, t), F32)
    b16t = jax.ShapeDtypeStruct((b, t // kblk, D_A, kblk), BF16)
    tspec = pl.BlockSpec((nb, D_A, seg), tok)
    kspec = pl.BlockSpec((nb, seg // kblk, D_A, kblk), blk)
    return pl.pallas_call(
        functools.partial(_proj_kernel, seg=seg, kblk=kblk),
        grid=(n // tm,),
        in_specs=[pl.BlockSpec((tm, D_MODEL), row), _resident((1, D_MODEL)), _resident(wa.shape),
                  _resident(wf.shape), _resident(bf.shape), _resident(wb.shape)],
        out_specs=[tspec, tspec, pl.BlockSpec((nb, N_A, seg), tok), tspec,
                   pl.BlockSpec((tm * N_B, LANES), row),
                   pl.BlockSpec((tm, D_A), row), kspec, pl.BlockSpec((tm, 2 * D_A), row),
                   pl.BlockSpec((tm, D_B), row), kspec, pl.BlockSpec((tm, 2 * D_B), row)],
        out_shape=[f32t, f32t, jax.ShapeDtypeStruct((b, N_A, t), F32), f32t,
                   jax.ShapeDtypeStruct((n * N_B, LANES), F32),
                   jax.ShapeDtypeStruct((n, D_A), BF16), b16t, jax.ShapeDtypeStruct((n, 2 * D_A), BF16),
                   jax.ShapeDtypeStruct((n, D_B), BF16), b16t, jax.ShapeDtypeStruct((n, 2 * D_B), BF16)],
        compiler_params=_params(1),
        name="proj",
    )(x, gn, wa, wf, bf, wb)


def _split3(x):
    hi = x.astype(BF16).astype(F32)
    r = x - hi
    mid = r.astype(BF16).astype(F32)
    lo = (r - mid).astype(BF16).astype(F32)
    return hi, mid, lo


def _cum_kernel(lft_ref, aug_ref, pad_ref, *, t, blocked):
    w = ATTN_BLOCK
    nb = lft_ref.shape[0]
    rows = nb * N_A
    pad_ref[...] = jnp.zeros(pad_ref.shape, F32)
    for bb in range(nb):
        pad_ref[bb * N_A:(bb + 1) * N_A, 0:t] = lft_ref[bb]
    r = lax.broadcasted_iota(jnp.int32, (w, w), 0)
    c = lax.broadcasted_iota(jnp.int32, (w, w), 1)
    tri = jnp.where(r <= c, 1.0, 0.0).astype(BF16)
    zero = jnp.zeros((N_A, w), F32)
    carry = jnp.zeros((rows, 1), F32)
    for s in range(pad_ref.shape[1] // w):
        lo_col = s * w
        cols = min(w, t - lo_col)
        x = jnp.concatenate(_split3(pad_ref[:, lo_col:lo_col + w]), axis=0).astype(BF16)
        y = jnp.dot(x, tri, preferred_element_type=F32)
        cs = y[0:rows] + y[rows:2 * rows] + y[2 * rows:3 * rows] + carry
        carry = cs[:, w - 1:w]
        pieces = _split3(cs * -LOG2E)
        for bb in range(nb):
            sl = slice(bb * N_A, (bb + 1) * N_A)
            aug = jnp.concatenate([p[sl] for p in pieces] + [zero], axis=0).astype(BF16)
            if blocked:
                aug_ref[bb, s] = aug
            else:
                aug_ref[bb, :, lo_col:lo_col + cols] = aug[:, :cols]


def _cum_aug(lft, blocked):
    b, _, t = lft.shape
    w = ATTN_BLOCK
    t_pad = -(-t // w) * w
    nb = math.gcd(b, SCAN_BATCH)
    if blocked:
        out_shape = jax.ShapeDtypeStruct((b, t // w, AUG_ROWS, w), BF16)
        out_spec = pl.BlockSpec((nb, t // w, AUG_ROWS, w), lambda i: (i, 0, 0, 0))
    else:
        out_shape = jax.ShapeDtypeStruct((b, AUG_ROWS, t), BF16)
        out_spec = pl.BlockSpec((nb, AUG_ROWS, t), lambda i: (i, 0, 0))
    return pl.pallas_call(
        functools.partial(_cum_kernel, t=t, blocked=blocked),
        grid=(b // nb,),
        in_specs=[pl.BlockSpec((nb, N_A, t), lambda i: (i, 0, 0))],
        out_specs=out_spec,
        out_shape=out_shape,
        scratch_shapes=[pltpu.VMEM((nb * N_A, t_pad), F32)],
        compiler_params=_params(1),
        name="cum_aug",
    )(lft)


def _lane_halves(q):
    lane = lax.broadcasted_iota(jnp.int32, q.shape, 1)
    zero = jnp.zeros_like(q)
    return jnp.where(lane < HD_A, q, zero), jnp.where(lane >= HD_A, q, zero)


def _piece_selector(shape, head, n_heads):
    lane = lax.broadcasted_iota(jnp.int32, shape, 1)
    hit = (lane == head) | (lane == n_heads + head) | (lane == 2 * n_heads + head)
    return jnp.where(hit, 1.0, 0.0).astype(BF16)


def _alibi_rows(k_pos):
    w = k_pos.shape[1]
    row = lax.broadcasted_iota(jnp.int32, (4 * N_B, 1), 0)
    head = row % N_B
    slope = jnp.zeros((4 * N_B, 1), F32)
    for h in range(N_B):
        slope = jnp.where(head == h, ALIBI_SLOPES[h] * LOG2E, slope)
    hi, mid, lo = _split3(slope * k_pos)
    piece = jnp.where(row < N_B, hi, jnp.where(row < 2 * N_B, mid, jnp.where(row < 3 * N_B, lo, 0.0)))
    return jnp.concatenate([piece.astype(BF16), jnp.zeros((LANES - 4 * N_B, w), BF16)], axis=0)


def _alibi_fold(slope, q_pos, k_pos):
    return (-2.0 * LOG2E * slope) * jnp.maximum(k_pos - q_pos, 0).astype(F32)


def _diff_finish(o1, o2, lam, sub_ref, lambda_init):
    return _rms(o1 - lam * o2, sub_ref[...]) * (1.0 - lambda_init)


def _lambda(lq1_ref, lk1_ref, lq2_ref, lk2_ref, lambda_init):
    return (jnp.exp(jnp.sum(lq1_ref[...] * lk1_ref[...], axis=1, keepdims=True))
            - jnp.exp(jnp.sum(lq2_ref[...] * lk2_ref[...], axis=1, keepdims=True)) + lambda_init)


def _attn_prompt_kernel(qa_ref, qb_ref, kat_ref, kbt_ref, va_ref, vb_ref, aug_ref,
                        lq1_ref, lk1_ref, lq2_ref, lk2_ref, sub_ref, oa_ref, ob_ref,
                        qs_ref, m_ref, acc_ref, tbl_ref, *, lambda_init):
    bi, i = pl.program_id(0), pl.program_id(1)
    tq = qa_ref.shape[0]
    n_chain = N_PAIRS + N_B

    @pl.when((bi == 0) & (i == 0))
    def _():
        r = lax.broadcasted_iota(jnp.int32, (2 * tq, tq), 0)
        c = lax.broadcasted_iota(jnp.int32, (2 * tq, tq), 1)
        q_idx = jnp.where(r >= tq, r - tq, r)
        tbl_ref[0] = jnp.where(q_idx >= c, 0.0, NEG_INF)
        visible = (q_idx // CHUNK) >= (c // CHUNK)
        for h in range(N_B):
            tbl_ref[1 + h] = jnp.where(visible, _alibi_fold(ALIBI_SLOPES[h], q_idx, c), NEG_INF)
        for ch in range(n_chain):
            if ch < N_PAIRS:
                qs_ref[ch, 0:tq, LANES:] = _piece_selector((tq, LANES), 2 * ch, N_A)
                qs_ref[ch, tq:, LANES:] = _piece_selector((tq, LANES), 2 * ch + 1, N_A)
            else:
                sel = _piece_selector((tq, LANES), ch - N_PAIRS, N_B)
                qs_ref[ch, 0:tq, LANES:] = sel
                qs_ref[ch, tq:, LANES:] = sel

    for ch in range(n_chain):
        q_ref, g = (qa_ref, ch) if ch < N_PAIRS else (qb_ref, ch - N_PAIRS)
        lo, hi = _lane_halves(q_ref[:, g * LANES:(g + 1) * LANES])
        qs_ref[ch, 0:tq, 0:LANES] = lo
        qs_ref[ch, tq:, 0:LANES] = hi

    def step(j, first):
        rows = pl.ds(pl.multiple_of(j * tq, tq), tq)
        aug_a = jnp.concatenate([aug_ref[0, j], jnp.zeros((LANES - AUG_ROWS, tq), BF16)], axis=0)
        aug_b = _alibi_rows((j * tq + lax.broadcasted_iota(jnp.int32, (1, tq), 1)).astype(F32))
        for ch in range(n_chain):
            if ch < N_PAIRS:
                kk = jnp.concatenate([kat_ref[0, j, ch * LANES:(ch + 1) * LANES, :], aug_a], axis=0)
                v = va_ref[rows, ch * 2 * LANES:(ch + 1) * 2 * LANES]
                tbl = 0
            else:
                h = ch - N_PAIRS
                kk = jnp.concatenate([kbt_ref[0, j, h * LANES:(h + 1) * LANES, :], aug_b], axis=0)
                v = vb_ref[rows, h * 2 * LANES:(h + 1) * 2 * LANES]
                tbl = 1 + h
            s = jnp.dot(qs_ref[ch], kk, preferred_element_type=F32)
            if first:
                s = s + tbl_ref[tbl]
                m_new = jnp.broadcast_to(jnp.max(s, axis=1, keepdims=True), (2 * tq, LANES))
            else:
                m_prev = m_ref[ch]
                m_new = jnp.maximum(m_prev, jnp.max(s, axis=1, keepdims=True))
            p = jnp.exp2(s - jnp.concatenate([m_new, m_new], axis=1))
            pv = jnp.dot(p.astype(BF16), v, preferred_element_type=F32)
            if first:
                acc_ref[ch] = pv
            else:
                alpha = jnp.exp2(m_prev - m_new)
                acc_ref[ch] = jnp.concatenate([alpha, alpha], axis=1) * acc_ref[ch] + pv
            m_ref[ch] = m_new

    def body(j, carry):
        step(j, False)
        return carry

    step(i, True)
    lax.fori_loop(0, i, body, 0)

    lane = lax.broadcasted_iota(jnp.int32, (tq, LANES), 1)
    lam = _lambda(lq1_ref, lk1_ref, lq2_ref, lk2_ref, lambda_init)
    for ch in range(n_chain):
        o = acc_ref[ch, :, 0:LANES] / acc_ref[ch, :, LANES:]
        if ch < N_PAIRS:
            oa_ref[:, ch * LANES:(ch + 1) * LANES] = jnp.where(lane < HD_A, o[:tq], o[tq:]).astype(BF16)
        else:
            h = ch - N_PAIRS
            ob_ref[:, h * LANES:(h + 1) * LANES] = _diff_finish(
                o[:tq], o[tq:], lam, sub_ref, lambda_init).astype(BF16)


def _attn_prompt(qa16, qb16, kat16, kbt16, va16, vb16, aug, lams, sub, b, t, lambda_init):
    tq = ATTN_BLOCK
    nq = t // tq
    qspec = pl.BlockSpec((tq, D_A), lambda bi, i: (bi * nq + i, 0))
    kspec = pl.BlockSpec((1, nq, D_A, tq), lambda bi, i: (bi, 0, 0, 0))
    vspec = pl.BlockSpec((t, 2 * D_A), lambda bi, i: (bi, 0))
    small = [_resident((1, HD_B))] * 4 + [_resident((1, 2 * HD_B))]
    n_chain = N_PAIRS + N_B
    return pl.pallas_call(
        functools.partial(_attn_prompt_kernel, lambda_init=lambda_init),
        grid=(b, nq),
        in_specs=[qspec, qspec, kspec, kspec, vspec, vspec,
                  pl.BlockSpec((1, nq, AUG_ROWS, tq), lambda bi, i: (bi, 0, 0, 0))] + small,
        out_specs=[qspec, qspec],
        out_shape=[jax.ShapeDtypeStruct((b * t, D_A), BF16), jax.ShapeDtypeStruct((b * t, D_B), BF16)],
        scratch_shapes=[pltpu.VMEM((n_chain, 2 * tq, 2 * LANES), BF16),
                        pltpu.VMEM((n_chain, 2 * tq, LANES), F32),
                        pltpu.VMEM((n_chain, 2 * tq, 2 * LANES), F32),
                        pltpu.VMEM((1 + N_B, 2 * tq, tq), F32)],
        compiler_params=_params(2),
        name="attn_prompt",
    )(qa16, qb16, kat16, kbt16, va16, vb16, aug, *lams, sub)


def _nt_dot(a, b):
    return lax.dot_general(a, b, (((1,), (1,)), ((), ())), preferred_element_type=F32)


def _attn_cached_kernel(qa_ref, qb_ref, ckat_ref, cvat_ref, ckbt_ref, cvb_ref,
                        kat_ref, vat_ref, kbt_ref, vb_ref, aug_ref,
                        lq1_ref, lk1_ref, lq2_ref, lk2_ref, sub_ref, oa_ref, ob_ref,
                        kn_ref, vn_ref, *, lambda_init):
    tq = qa_ref.shape[0]
    past = ckat_ref.shape[2]
    t_all = past + tq
    n_chain = N_PAIRS + N_B
    r = lax.broadcasted_iota(jnp.int32, (2 * tq, LANES), 0)
    c = lax.broadcasted_iota(jnp.int32, (2 * tq, LANES), 1)
    q_idx = jnp.where(r >= tq, r - tq, r)
    is_key = c < tq
    causal_tbl = jnp.where((q_idx >= c) & is_key, 0.0, NEG_INF)
    visible = (((past + q_idx) // CHUNK) >= ((past + c) // CHUNK)) & is_key
    lane = lax.broadcasted_iota(jnp.int32, (tq, LANES), 1)
    lam = _lambda(lq1_ref, lk1_ref, lq2_ref, lk2_ref, lambda_init)
    aug_c = jnp.concatenate([aug_ref[0, :, 0:past], jnp.zeros((LANES - AUG_ROWS, past), BF16)], axis=0)
    alibi_c = _alibi_rows(lax.broadcasted_iota(jnp.int32, (1, past), 1).astype(F32))
    alibi_n = _alibi_rows((past + lax.broadcasted_iota(jnp.int32, (1, LANES), 1)).astype(F32))
    kn_ref[...] = jnp.zeros(kn_ref.shape, BF16)
    vn_ref[...] = jnp.zeros(vn_ref.shape, BF16)

    for ch in range(n_chain):
        fox = ch < N_PAIRS
        g = ch if fox else ch - N_PAIRS
        rows = slice(g * LANES, (g + 1) * LANES)
        if fox:
            lo, hi = _lane_halves(qa_ref[:, rows])
            sel_lo = _piece_selector((tq, LANES), 2 * g, N_A)
            sel_hi = _piece_selector((tq, LANES), 2 * g + 1, N_A)
            k_old = jnp.concatenate([ckat_ref[0, rows, :].astype(BF16), aug_c], axis=0)
            kn_ref[ch, 0:LANES, 0:tq] = kat_ref[0, rows, :].astype(BF16)
            kn_ref[ch, LANES:LANES + AUG_ROWS, 0:tq] = aug_ref[0, :, past:t_all]
            tbl = causal_tbl
        else:
            lo, hi = _lane_halves(qb_ref[:, rows])
            sel_lo = sel_hi = _piece_selector((tq, LANES), g, N_B)
            k_old = jnp.concatenate([ckbt_ref[0, rows, :].astype(BF16), alibi_c], axis=0)
            kn_ref[ch, 0:LANES, 0:tq] = kbt_ref[0, rows, :].astype(BF16)
            kn_ref[ch, LANES:, :] = alibi_n
            tbl = jnp.where(visible, _alibi_fold(ALIBI_SLOPES[g], q_idx, c), NEG_INF)
        qs = jnp.concatenate([jnp.concatenate([lo, sel_lo], axis=1), jnp.concatenate([hi, sel_hi], axis=1)], axis=0)
        s_old = jnp.dot(qs, k_old, preferred_element_type=F32)
        s_new = jnp.dot(qs, kn_ref[ch], preferred_element_type=F32) + tbl
        m = jnp.maximum(jnp.max(s_old, axis=1, keepdims=True), jnp.max(s_new, axis=1, keepdims=True))
        p_old = jnp.exp2(s_old - m)
        p_new = jnp.exp2(s_new - m)
        l = jnp.sum(p_old, axis=1, keepdims=True) + jnp.sum(p_new, axis=1, keepdims=True)
        if fox:
            vn_ref[ch, :, 0:tq] = vat_ref[0, rows, :].astype(BF16)
            o = (_nt_dot(p_old.astype(BF16), cvat_ref[0, rows, :].astype(BF16))
                 + _nt_dot(p_new.astype(BF16), vn_ref[ch])) / l
            oa_ref[:, rows] = jnp.where(lane < HD_A, o[:tq], o[tq:]).astype(BF16)
        else:
            vn_ref[ch, 0:tq, :] = vb_ref[:, g * 2 * LANES:g * 2 * LANES + LANES]
            v_old = cvb_ref[0, pl.ds(g, past, stride=N_B), :].astype(BF16)
            o = (jnp.dot(p_old.astype(BF16), v_old, preferred_element_type=F32)
                 + jnp.dot(p_new.astype(BF16), vn_ref[ch], preferred_element_type=F32)) / l
            ob_ref[:, rows] = _diff_finish(o[:tq], o[tq:], lam, sub_ref, lambda_init).astype(BF16)


def _attn_cached(qa16, qb16, ckat, cvat, ckbt, cvb, kat, vat, kbt, vb16, aug, lams, sub, b, tq, lambda_init):
    past = ckat.shape[2]
    t_all = past + tq
    n_chain = N_PAIRS + N_B
    qspec = pl.BlockSpec((tq, D_A), lambda bi: (bi, 0))
    old_t = pl.BlockSpec((1, D_A, past), lambda bi: (bi, 0, 0))
    new_t = pl.BlockSpec((1, D_A, tq), lambda bi: (bi, 0, 0))
    small = [_resident((1, HD_B))] * 4 + [_resident((1, 2 * HD_B))]
    return pl.pallas_call(
        functools.partial(_attn_cached_kernel, lambda_init=lambda_init),
        grid=(b,),
        in_specs=[qspec, qspec, old_t, old_t, old_t,
                  pl.BlockSpec((1, past * N_B, LANES), lambda bi: (bi, 0, 0)),
                  new_t, new_t, new_t,
                  pl.BlockSpec((tq, 2 * D_B), lambda bi: (bi, 0)),
                  pl.BlockSpec((1, AUG_ROWS, t_all), lambda bi: (bi, 0, 0))] + small,
        out_specs=[qspec, qspec],
        out_shape=[jax.ShapeDtypeStruct((b * tq, D_A), BF16), jax.ShapeDtypeStruct((b * tq, D_B), BF16)],
        scratch_shapes=[pltpu.VMEM((n_chain, 2 * LANES, LANES), BF16), pltpu.VMEM((n_chain, LANES, LANES), BF16)],
        compiler_params=_params(1),
        name="attn_cached",
    )(qa16, qb16, ckat, cvat, ckbt, cvb, kat, vat, kbt, vb16, aug, *lams, sub)


def _tokens_minor(x):
    b, t = x.shape[:2]
    return jnp.moveaxis(x.reshape(b, t, -1), 1, 2)


def _tokens_major(xt, tail):
    b, _, t = xt.shape
    return jnp.moveaxis(xt, 2, 1).reshape(b, t, *tail)


def _layer(x, cache, w, lambda_init):
    b, t, _ = x.shape
    n = b * t
    x1 = _ffn1(x.reshape(n, D_MODEL), w["gn1"], w["w1a"], w["w2a"])
    (kat, vat, lft, kbt, vb, qa16, kat16, va16, qb16, kbt16, vb16) = _proj(
        x1, w["gmix"], w["wa"], w["wf"], w["bf"], w["wb"], b, t)
    if cache is None:
        aug = _cum_aug(lft, blocked=True)
        oa, ob = _attn_prompt(qa16, qb16, kat16, kbt16, va16, vb16, aug, w["lams"], w["sub"], b, t, lambda_init)
    else:
        cfk, cfv, cflf, cdk, cdv = cache
        past = cfk.shape[1]
        aug = _cum_aug(jnp.concatenate([_tokens_minor(cflf.astype(F32)), lft], axis=2), blocked=False)
        oa, ob = _attn_cached(qa16, qb16, _tokens_minor(cfk), _tokens_minor(cfv), _tokens_minor(cdk),
                              cdv.reshape(b, past * N_B, 2 * HD_B), kat, vat, kbt, vb16, aug,
                              w["lams"], w["sub"], b, t, lambda_init)
    news = (_tokens_major(kat, (N_A, HD_A)), _tokens_major(vat, (N_A, HD_A)), _tokens_major(lft, (N_A,)),
            _tokens_major(kbt, (N_B, 2, HD_B)), vb.reshape(b, t, N_B, 2 * HD_B))
    return x1, oa, ob, news


def kernel(x_prompt, x_sample, cache_fox_k, cache_fox_v, cache_fox_logf, cache_diff_k, cache_diff_v,
           norm_ffn1, w_ffn1_in, w_ffn1_out, norm_mix, w_in, b_forget,
           lambda_q1, lambda_k1, lambda_q2, lambda_k2, diff_subln, w_out,
           norm_ffn2, w_ffn2_in, w_ffn2_out, norm_final):
    depth = w_in.shape[0]
    xp, xs = x_prompt, x_sample
    outs_p, outs_s = [], []
    gfin = norm_final.reshape(1, D_MODEL)
    for l in range(depth):
        lambda_init = 0.8 - 0.6 * math.exp(-0.3 * l)
        f_lo, f_hi = 3 * D_A, 3 * D_A + N_A
        w = {
            "gn1": norm_ffn1[l].reshape(1, D_MODEL),
            "w1a": w_ffn1_in[l].astype(BF16), "w2a": w_ffn1_out[l].astype(BF16),
            "gmix": norm_mix[l].reshape(1, D_MODEL),
            "wa": w_in[l][:, :f_lo].astype(BF16),
            "wf": jnp.pad(w_in[l][:, f_lo:f_hi], ((0, 0), (0, LANES - N_A))).astype(BF16),
            "bf": jnp.pad(b_forget[l], (0, LANES - N_A)).reshape(1, LANES),
            "wb": w_in[l][:, f_hi:].astype(BF16),
            "lams": [v[l].reshape(1, HD_B) for v in (lambda_q1, lambda_k1, lambda_q2, lambda_k2)],
            "sub": diff_subln[l].reshape(1, 2 * HD_B),
            "wo": w_out[l].astype(BF16),
            "gn2": norm_ffn2[l].reshape(1, D_MODEL),
            "w1b": w_ffn2_in[l].astype(BF16), "w2b": w_ffn2_out[l].astype(BF16),
        }
        streams = []
        for x, cache in ((xp, None),
                         (xs, (cache_fox_k[l], cache_fox_v[l], cache_fox_logf[l],
                               cache_diff_k[l], cache_diff_v[l]))):
            b, t, _ = x.shape
            x1, oa, ob, news = _layer(x, cache, w, lambda_init)
            y = _post(x1, oa, ob, w["wo"], w["gn2"], w["w1b"], w["w2b"], gfin, l == depth - 1)
            streams.append((y.reshape(b, t, D_MODEL), news))
        (xp, news_p), (xs, news_s) = streams
        outs_p.append(news_p)
        outs_s.append(news_s)
    stack = lambda outs, i: jnp.stack([o[i] for o in outs])
    return (xp, xs) + tuple(stack(outs_p, i) for i in range(5)) + tuple(stack(outs_s, i) for i in range(5))
```

```python
import functools
import math

import jax
import jax.numpy as jnp
from jax import lax
from jax.experimental import pallas as pl
from jax.experimental.pallas import tpu as pltpu

D_MODEL = 1024
D_FF = 2816
N_A = 8
HD_A = 64
N_B = 4
HD_B = 64
D_A = N_A * HD_A
D_B = N_B * 2 * HD_B
CHUNK = 64
NORM_EPS = 1e-6
NEG_INF = -1e30
ALIBI_SLOPES = (0.25, 0.0625, 0.015625, 0.00390625)

LOG2E = math.log2(math.e)
LANES = 128
MXU_WIDTH = 256
FF_CHUNKS = (1536, 1280)
assert sum(FF_CHUNKS) == D_FF and all(c % MXU_WIDTH == 0 for c in FF_CHUNKS)
SCAN_BATCH = 8
TOKEN_TILE = 512
ATTN_BLOCK = 256
AUG_ROWS = 32
VMEM_LIMIT = 56 * 1024 * 1024
N_PAIRS = N_A // 2

F32 = jnp.float32
BF16 = jnp.bfloat16


def _params(n_axes):
    return pltpu.CompilerParams(dimension_semantics=("arbitrary",) * n_axes,
                                vmem_limit_bytes=VMEM_LIMIT)


def _resident(shape):
    return pl.BlockSpec(shape, lambda *_: (0,) * len(shape), pipeline_mode=pl.Buffered(1))


def _rms(xf, g):
    ms = jnp.mean(xf * xf, axis=-1, keepdims=True)
    return xf * lax.rsqrt(ms + NORM_EPS) * g


def _swiglu_residual(x, gn_ref, w1_ref, w2_ref):
    h = _rms(x, gn_ref[...]).astype(BF16)
    acc = None
    lo = 0
    for width in FF_CHUNKS:
        g = jnp.dot(h, w1_ref[:, lo:lo + width], preferred_element_type=F32)
        u = jnp.dot(h, w1_ref[:, D_FF + lo:D_FF + lo + width], preferred_element_type=F32)
        a = (g * jax.nn.sigmoid(g) * u).astype(BF16)
        d = jnp.dot(a, w2_ref[lo:lo + width, :], preferred_element_type=F32)
        acc = d if acc is None else acc + d
        lo += width
    return x + 0.5 * acc


def _ffn1_kernel(x_ref, gn_ref, w1_ref, w2_ref, o_ref):
    o_ref[...] = _swiglu_residual(x_ref[...], gn_ref, w1_ref, w2_ref)


def _ffn1(x, gn, w1, w2):
    n = x.shape[0]
    tm = min(TOKEN_TILE, n)
    return pl.pallas_call(
        _ffn1_kernel,
        grid=(n // tm,),
        in_specs=[pl.BlockSpec((tm, D_MODEL), lambda i: (i, 0)),
                  _resident((1, D_MODEL)), _resident(w1.shape), _resident(w2.shape)],
        out_specs=pl.BlockSpec((tm, D_MODEL), lambda i: (i, 0)),
        out_shape=jax.ShapeDtypeStruct((n, D_MODEL), F32),
        compiler_params=_params(1),
        name="ffn1",
    )(x, gn, w1, w2)


def _post_kernel(x_ref, oa_ref, ob_ref, wo_ref, gn_ref, w1_ref, w2_ref, gf_ref, y_ref, *, final):
    x = x_ref[...]
    x = x + jnp.dot(oa_ref[...], wo_ref[:D_A, :], preferred_element_type=F32)
    x = x + jnp.dot(ob_ref[...], wo_ref[D_A:, :], preferred_element_type=F32)
    x = _swiglu_residual(x, gn_ref, w1_ref, w2_ref)
    y_ref[...] = _rms(x, gf_ref[...]) if final else x


def _post(x, oa, ob, wo, gn, w1, w2, gf, final):
    n = x.shape[0]
    tm = min(TOKEN_TILE, n)
    row = lambda i: (i, 0)
    return pl.pallas_call(
        functools.partial(_post_kernel, final=final),
        grid=(n // tm,),
        in_specs=[pl.BlockSpec((tm, D_MODEL), row), pl.BlockSpec((tm, D_A), row),
                  pl.BlockSpec((tm, D_B), row), _resident(wo.shape), _resident((1, D_MODEL)),
                  _resident(w1.shape), _resident(w2.shape), _resident((1, D_MODEL))],
        out_specs=pl.BlockSpec((tm, D_MODEL), row),
        out_shape=jax.ShapeDtypeStruct((n, D_MODEL), F32),
        compiler_params=_params(1),
        name="post",
    )(x, oa, ob, wo, gn, w1, w2, gf)


def _proj_kernel(x_ref, gn_ref, wa_ref, wf_ref, bf_ref, wb_ref, *out_refs, seg, kblk, tokens_minor):
    tm = x_ref.shape[0]
    h = _rms(x_ref[...], gn_ref[...]).astype(BF16)
    pa = jnp.dot(h, wa_ref[...], preferred_element_type=F32)
    pb = jnp.dot(h, wb_ref[...], preferred_element_type=F32)
    z = jnp.dot(h, wf_ref[...], preferred_element_type=F32) + bf_ref[...]
    lft = (jnp.minimum(z, 0.0) - jnp.log1p(jnp.exp(-jnp.abs(z)))).T
    ka, va = pa[:, D_A:2 * D_A], pa[:, 2 * D_A:]
    kb, vb = pb[:, D_B:2 * D_B], pb[:, 2 * D_B:]
    qa16 = (pa[:, :D_A] * (HD_A ** -0.5 * LOG2E)).astype(BF16)
    qb16 = (pb[:, :D_B] * (HD_B ** -0.5 * LOG2E)).astype(BF16)
    if tokens_minor:
        kat_ref, vat_ref, lft_ref, kbt_ref, vb_ref, qa16_ref, kat16_ref, va16_ref, qb16_ref, kbt16_ref, vb16_ref = out_refs
        kat, vat, kbt = ka.T, va.T, kb.T
    else:
        ka_ref, va_ref, lft_ref, kb_ref, vb_ref, qa16_ref, ka16_ref, va16_ref, qb16_ref, kb16_ref, vb16_ref = out_refs
        ka16_ref[...] = ka.astype(BF16)
        kb16_ref[...] = kb.astype(BF16)
        for hh in range(N_A):
            cols = slice(hh * HD_A, (hh + 1) * HD_A)
            ka_ref[pl.ds(hh, tm, stride=N_A), :] = ka[:, cols]
            va_ref[pl.ds(hh, tm, stride=N_A), :] = va[:, cols]
            kb_ref[pl.ds(hh, tm, stride=N_A), :] = kb[:, cols]
    qa16_ref[...] = qa16
    qb16_ref[...] = qb16
    va16_ref[...] = va.astype(BF16)
    vb16_ref[...] = vb.astype(BF16)
    for hh in range(N_B):
        vb_ref[pl.ds(hh, tm, stride=N_B), :] = vb[:, hh * LANES:(hh + 1) * LANES]
    for bb in range(tm // seg):
        cols = slice(bb * seg, (bb + 1) * seg)
        lft_ref[bb] = lft[:N_A, cols]
        if tokens_minor:
            kat_ref[bb] = kat[:, cols]
            vat_ref[bb] = vat[:, cols]
            kbt_ref[bb] = kbt[:, cols]
            for jj in range(seg // kblk):
                kc = slice(bb * seg + jj * kblk, bb * seg + (jj + 1) * kblk)
                kat16_ref[bb, jj] = kat[:, kc].astype(BF16)
                kbt16_ref[bb, jj] = kbt[:, kc].astype(BF16)


def _proj(x, gn, wa, wf, bf, wb, b, t, tokens_minor):
    n = x.shape[0]
    tm = min(TOKEN_TILE, n)
    seg = min(tm, t)
    nb = tm // seg
    nt = t // seg
    kblk = min(ATTN_BLOCK, seg)
    row = lambda i: (i, 0)
    tok = lambda i: (i // nt, 0, i % nt)
    blk = lambda i: (i // nt, i % nt, 0, 0)
    b16 = (jax.ShapeDtypeStruct((n, D_A), BF16), pl.BlockSpec((tm, D_A), row))
    lft = (jax.ShapeDtypeStruct((b, N_A, t), F32), pl.BlockSpec((nb, N_A, seg), tok))
    vb = (jax.ShapeDtypeStruct((n * N_B, LANES), F32), pl.BlockSpec((tm * N_B, LANES), row))
    if tokens_minor:
        kv = (jax.ShapeDtypeStruct((b, D_A, t), F32), pl.BlockSpec((nb, D_A, seg), tok))
        k16 = (jax.ShapeDtypeStruct((b, t // kblk, D_A, kblk), BF16),
               pl.BlockSpec((nb, seg // kblk, D_A, kblk), blk))
    else:
        kv = (jax.ShapeDtypeStruct((n * N_A, HD_A), F32), pl.BlockSpec((tm * N_A, HD_A), row))
        k16 = b16
    outs = [kv, kv, lft, kv, vb, b16, k16, b16, b16, k16, b16]
    return pl.pallas_call(
        functools.partial(_proj_kernel, seg=seg, kblk=kblk, tokens_minor=tokens_minor),
        grid=(n // tm,),
        in_specs=[pl.BlockSpec((tm, D_MODEL), row), _resident((1, D_MODEL)), _resident(wa.shape),
                  _resident(wf.shape), _resident(bf.shape), _resident(wb.shape)],
        out_specs=[o[1] for o in outs],
        out_shape=[o[0] for o in outs],
        compiler_params=_params(1),
        name="proj",
    )(x, gn, wa, wf, bf, wb)


def _split3(x):
    hi = x.astype(BF16).astype(F32)
    r = x - hi
    mid = r.astype(BF16).astype(F32)
    lo = (r - mid).astype(BF16).astype(F32)
    return hi, mid, lo


def _cum_kernel(lft_ref, aug_ref, pad_ref, *, t, blocked):
    w = ATTN_BLOCK
    nb = lft_ref.shape[0]
    rows = nb * N_A
    pad_ref[...] = jnp.zeros(pad_ref.shape, F32)
    for bb in range(nb):
        pad_ref[bb * N_A:(bb + 1) * N_A, 0:t] = lft_ref[bb]
    r = lax.broadcasted_iota(jnp.int32, (w, w), 0)
    c = lax.broadcasted_iota(jnp.int32, (w, w), 1)
    tri = jnp.where(r <= c, 1.0, 0.0).astype(BF16)
    zero = jnp.zeros((N_A, w), F32)
    carry = jnp.zeros((rows, 1), F32)
    for s in range(pad_ref.shape[1] // w):
        lo_col = s * w
        cols = min(w, t - lo_col)
        x = jnp.concatenate(_split3(pad_ref[:, lo_col:lo_col + w]), axis=0).astype(BF16)
        y = jnp.dot(x, tri, preferred_element_type=F32)
        cs = y[0:rows] + y[rows:2 * rows] + y[2 * rows:3 * rows] + carry
        carry = cs[:, w - 1:w]
        pieces = _split3(cs * -LOG2E)
        for bb in range(nb):
            sl = slice(bb * N_A, (bb + 1) * N_A)
            aug = jnp.concatenate([p[sl] for p in pieces] + [zero], axis=0).astype(BF16)
            if blocked:
                aug_ref[bb, s] = aug
            else:
                aug_ref[bb, :, lo_col:lo_col + cols] = aug[:, :cols]


def _cum_aug(lft, blocked):
    b, _, t = lft.shape
    w = ATTN_BLOCK
    t_pad = -(-t // w) * w
    nb = math.gcd(b, SCAN_BATCH)
    if blocked:
        out_shape = jax.ShapeDtypeStruct((b, t // w, AUG_ROWS, w), BF16)
        out_spec = pl.BlockSpec((nb, t // w, AUG_ROWS, w), lambda i: (i, 0, 0, 0))
    else:
        out_shape = jax.ShapeDtypeStruct((b, AUG_ROWS, t), BF16)
        out_spec = pl.BlockSpec((nb, AUG_ROWS, t), lambda i: (i, 0, 0))
    return pl.pallas_call(
        functools.partial(_cum_kernel, t=t, blocked=blocked),
        grid=(b // nb,),
        in_specs=[pl.BlockSpec((nb, N_A, t), lambda i: (i, 0, 0))],
        out_specs=out_spec,
        out_shape=out_shape,
        scratch_shapes=[pltpu.VMEM((nb * N_A, t_pad), F32)],
        compiler_params=_params(1),
        name="cum_aug",
    )(lft)


def _lane_halves(q):
    lane = lax.broadcasted_iota(jnp.int32, q.shape, 1)
    zero = jnp.zeros_like(q)
    return jnp.where(lane < HD_A, q, zero), jnp.where(lane >= HD_A, q, zero)


def _piece_selector(shape, head, n_heads):
    lane = lax.broadcasted_iota(jnp.int32, shape, 1)
    hit = (lane == head) | (lane == n_heads + head) | (lane == 2 * n_heads + head)
    return jnp.where(hit, 1.0, 0.0).astype(BF16)


def _alibi_rows(k_pos):
    w = k_pos.shape[1]
    row = lax.broadcasted_iota(jnp.int32, (4 * N_B, 1), 0)
    head = row % N_B
    slope = jnp.zeros((4 * N_B, 1), F32)
    for h in range(N_B):
        slope = jnp.where(head == h, ALIBI_SLOPES[h] * LOG2E, slope)
    hi, mid, lo = _split3(slope * k_pos)
    piece = jnp.where(row < N_B, hi, jnp.where(row < 2 * N_B, mid, jnp.where(row < 3 * N_B, lo, 0.0)))
    return jnp.concatenate([piece.astype(BF16), jnp.zeros((LANES - 4 * N_B, w), BF16)], axis=0)


def _alibi_fold(slope, q_pos, k_pos):
    return (-2.0 * LOG2E * slope) * jnp.maximum(k_pos - q_pos, 0).astype(F32)


def _diff_finish(o1, o2, lam, sub_ref, lambda_init):
    return _rms(o1 - lam * o2, sub_ref[...]) * (1.0 - lambda_init)


def _lambda(lq1_ref, lk1_ref, lq2_ref, lk2_ref, lambda_init):
    return (jnp.exp(jnp.sum(lq1_ref[...] * lk1_ref[...], axis=1, keepdims=True))
            - jnp.exp(jnp.sum(lq2_ref[...] * lk2_ref[...], axis=1, keepdims=True)) + lambda_init)


def _attn_prompt_kernel(qa_ref, qb_ref, kat_ref, kbt_ref, va_ref, vb_ref, aug_ref,
                        lq1_ref, lk1_ref, lq2_ref, lk2_ref, sub_ref, oa_ref, ob_ref,
                        qs_ref, m_ref, acc_ref, tbl_ref, *, lambda_init):
    bi, i = pl.program_id(0), pl.program_id(1)
    tq = qa_ref.shape[0]
    n_chain = N_PAIRS + N_B

    @pl.when((bi == 0) & (i == 0))
    def _():
        r = lax.broadcasted_iota(jnp.int32, (2 * tq, tq), 0)
        c = lax.broadcasted_iota(jnp.int32, (2 * tq, tq), 1)
        q_idx = jnp.where(r >= tq, r - tq, r)
        tbl_ref[0] = jnp.where(q_idx >= c, 0.0, NEG_INF)
        visible = (q_idx // CHUNK) >= (c // CHUNK)
        for h in range(N_B):
            tbl_ref[1 + h] = jnp.where(visible, _alibi_fold(ALIBI_SLOPES[h], q_idx, c), NEG_INF)
        for ch in range(n_chain):
            if ch < N_PAIRS:
                qs_ref[ch, 0:tq, LANES:] = _piece_selector((tq, LANES), 2 * ch, N_A)
                qs_ref[ch, tq:, LANES:] = _piece_selector((tq, LANES), 2 * ch + 1, N_A)
            else:
                sel = _piece_selector((tq, LANES), ch - N_PAIRS, N_B)
                qs_ref[ch, 0:tq, LANES:] = sel
                qs_ref[ch, tq:, LANES:] = sel

    for ch in range(n_chain):
        q_ref, g = (qa_ref, ch) if ch < N_PAIRS else (qb_ref, ch - N_PAIRS)
        lo, hi = _lane_halves(q_ref[:, g * LANES:(g + 1) * LANES])
        qs_ref[ch, 0:tq, 0:LANES] = lo
        qs_ref[ch, tq:, 0:LANES] = hi

    ones = jnp.ones((tq, LANES), BF16)

    def step(j, first):
        rows = pl.ds(pl.multiple_of(j * tq, tq), tq)
        aug_a = jnp.concatenate([aug_ref[0, j], jnp.zeros((LANES - AUG_ROWS, tq), BF16)], axis=0)
        aug_b = _alibi_rows((j * tq + lax.broadcasted_iota(jnp.int32, (1, tq), 1)).astype(F32))
        for ch in range(n_chain):
            if ch < N_PAIRS:
                kk = jnp.concatenate([kat_ref[0, j, ch * LANES:(ch + 1) * LANES, :], aug_a], axis=0)
                v = va_ref[rows, ch * LANES:(ch + 1) * LANES]
                tbl = 0
            else:
                h = ch - N_PAIRS
                kk = jnp.concatenate([kbt_ref[0, j, h * LANES:(h + 1) * LANES, :], aug_b], axis=0)
                v = vb_ref[rows, h * LANES:(h + 1) * LANES]
                tbl = 1 + h
            s = jnp.dot(qs_ref[ch], kk, preferred_element_type=F32)
            if first:
                s = s + tbl_ref[tbl]
                m_new = jnp.broadcast_to(jnp.max(s, axis=1, keepdims=True), (2 * tq, LANES))
            else:
                m_prev = m_ref[ch]
                m_new = jnp.maximum(m_prev, jnp.max(s, axis=1, keepdims=True))
            p = jnp.exp2(s - jnp.concatenate([m_new, m_new], axis=1))
            pv = jnp.dot(p.astype(BF16), jnp.concatenate([v, ones], axis=1), preferred_element_type=F32)
            if first:
                acc_ref[ch] = pv
            else:
                alpha = jnp.exp2(m_prev - m_new)
                acc_ref[ch] = jnp.concatenate([alpha, alpha], axis=1) * acc_ref[ch] + pv
            m_ref[ch] = m_new

    def body(j, carry):
        step(j, False)
        return carry

    step(i, True)
    lax.fori_loop(0, i, body, 0)

    lane = lax.broadcasted_iota(jnp.int32, (tq, LANES), 1)
    lam = _lambda(lq1_ref, lk1_ref, lq2_ref, lk2_ref, lambda_init)
    for ch in range(n_chain):
        o = acc_ref[ch, :, 0:LANES] / acc_ref[ch, :, LANES:]
        if ch < N_PAIRS:
            oa_ref[:, ch * LANES:(ch + 1) * LANES] = jnp.where(lane < HD_A, o[:tq], o[tq:]).astype(BF16)
        else:
            h = ch - N_PAIRS
            ob_ref[:, h * LANES:(h + 1) * LANES] = _diff_finish(
                o[:tq], o[tq:], lam, sub_ref, lambda_init).astype(BF16)


def _attn_prompt(qa16, qb16, kat16, kbt16, va16, vb16, aug, lams, sub, b, t, lambda_init):
    tq = ATTN_BLOCK
    nq = t // tq
    qspec = pl.BlockSpec((tq, D_A), lambda bi, i: (bi * nq + i, 0))
    kspec = pl.BlockSpec((1, nq, D_A, tq), lambda bi, i: (bi, 0, 0, 0))
    vspec = pl.BlockSpec((t, D_A), lambda bi, i: (bi, 0))
    small = [_resident((1, HD_B))] * 4 + [_resident((1, 2 * HD_B))]
    n_chain = N_PAIRS + N_B
    return pl.pallas_call(
        functools.partial(_attn_prompt_kernel, lambda_init=lambda_init),
        grid=(b, nq),
        in_specs=[qspec, qspec, kspec, kspec, vspec, vspec,
                  pl.BlockSpec((1, nq, AUG_ROWS, tq), lambda bi, i: (bi, 0, 0, 0))] + small,
        out_specs=[qspec, qspec],
        out_shape=[jax.ShapeDtypeStruct((b * t, D_A), BF16), jax.ShapeDtypeStruct((b * t, D_B), BF16)],
        scratch_shapes=[pltpu.VMEM((n_chain, 2 * tq, 2 * LANES), BF16),
                        pltpu.VMEM((n_chain, 2 * tq, LANES), F32),
                        pltpu.VMEM((n_chain, 2 * tq, 2 * LANES), F32),
                        pltpu.VMEM((1 + N_B, 2 * tq, tq), F32)],
        compiler_params=_params(2),
        name="attn_prompt",
    )(qa16, qb16, kat16, kbt16, va16, vb16, aug, *lams, sub)


def _nt_dot(a, b):
    return lax.dot_general(a, b, (((1,), (1,)), ((), ())), preferred_element_type=F32)


def _attn_cached_kernel(qa_ref, qb_ref, ckat_ref, cvat_ref, ckbt_ref, cvb_ref,
                        ka_ref, va_ref, kb_ref, vb_ref, aug_ref,
                        lq1_ref, lk1_ref, lq2_ref, lk2_ref, sub_ref, oa_ref, ob_ref,
                        kn_ref, *, lambda_init):
    tq = qa_ref.shape[0]
    past = ckat_ref.shape[2]
    t_all = past + tq
    n_chain = N_PAIRS + N_B
    r = lax.broadcasted_iota(jnp.int32, (2 * tq, LANES), 0)
    c = lax.broadcasted_iota(jnp.int32, (2 * tq, LANES), 1)
    q_idx = jnp.where(r >= tq, r - tq, r)
    is_key = c < tq
    causal_tbl = jnp.where((q_idx >= c) & is_key, 0.0, NEG_INF)
    visible = (((past + q_idx) // CHUNK) >= ((past + c) // CHUNK)) & is_key
    lane = lax.broadcasted_iota(jnp.int32, (tq, LANES), 1)
    lam = _lambda(lq1_ref, lk1_ref, lq2_ref, lk2_ref, lambda_init)
    aug_c = jnp.concatenate([aug_ref[0, :, 0:past], jnp.zeros((LANES - AUG_ROWS, past), BF16)], axis=0)
    alibi_c = _alibi_rows(lax.broadcasted_iota(jnp.int32, (1, past), 1).astype(F32))
    alibi_n = _alibi_rows((past + lax.broadcasted_iota(jnp.int32, (1, LANES), 1)).astype(F32))
    kn_ref[...] = jnp.zeros(kn_ref.shape, BF16)
    row_pad = jnp.zeros((LANES - tq, LANES), BF16)

    def new_keys_t(k_new):
        return jnp.concatenate([k_new, row_pad], axis=0).astype(F32).T.astype(BF16)

    for ch in range(n_chain):
        fox = ch < N_PAIRS
        g = ch if fox else ch - N_PAIRS
        rows = slice(g * LANES, (g + 1) * LANES)
        if fox:
            lo, hi = _lane_halves(qa_ref[:, rows])
            sel_lo = _piece_selector((tq, LANES), 2 * g, N_A)
            sel_hi = _piece_selector((tq, LANES), 2 * g + 1, N_A)
            k_old = jnp.concatenate([ckat_ref[0, rows, :].astype(BF16), aug_c], axis=0)
            kn_ref[ch, 0:LANES, :] = new_keys_t(ka_ref[:, rows])
            kn_ref[ch, LANES:LANES + AUG_ROWS, 0:tq] = aug_ref[0, :, past:t_all]
            tbl = causal_tbl
        else:
            lo, hi = _lane_halves(qb_ref[:, rows])
            sel_lo = sel_hi = _piece_selector((tq, LANES), g, N_B)
            k_old = jnp.concatenate([ckbt_ref[0, rows, :].astype(BF16), alibi_c], axis=0)
            kn_ref[ch, 0:LANES, :] = new_keys_t(kb_ref[:, rows])
            kn_ref[ch, LANES:, :] = alibi_n
            tbl = jnp.where(visible, _alibi_fold(ALIBI_SLOPES[g], q_idx, c), NEG_INF)
        qs = jnp.concatenate([jnp.concatenate([lo, sel_lo], axis=1), jnp.concatenate([hi, sel_hi], axis=1)], axis=0)
        s_old = jnp.dot(qs, k_old, preferred_element_type=F32)
        s_new = jnp.dot(qs, kn_ref[ch], preferred_element_type=F32) + tbl
        m = jnp.maximum(jnp.max(s_old, axis=1, keepdims=True), jnp.max(s_new, axis=1, keepdims=True))
        p_old = jnp.exp2(s_old - m)
        p_new = jnp.exp2(s_new - m)
        l = jnp.sum(p_old, axis=1, keepdims=True) + jnp.sum(p_new, axis=1, keepdims=True)
        v_new = jnp.concatenate([(va_ref if fox else vb_ref)[:, rows], row_pad], axis=0)
        o_new = jnp.dot(p_new.astype(BF16), v_new, preferred_element_type=F32)
        if fox:
            o = (_nt_dot(p_old.astype(BF16), cvat_ref[0, rows, :].astype(BF16)) + o_new) / l
            oa_ref[:, rows] = jnp.where(lane < HD_A, o[:tq], o[tq:]).astype(BF16)
        else:
            v_old = cvb_ref[0, pl.ds(g, past, stride=N_B), :].astype(BF16)
            o = (jnp.dot(p_old.astype(BF16), v_old, preferred_element_type=F32) + o_new) / l
            ob_ref[:, rows] = _diff_finish(o[:tq], o[tq:], lam, sub_ref, lambda_init).astype(BF16)


def _attn_cached(qa16, qb16, ckat, cvat, ckbt, cvb, ka16, va16, kb16, vb16, aug, lams, sub, b, tq, lambda_init):
    past = ckat.shape[2]
    t_all = past + tq
    n_chain = N_PAIRS + N_B
    qspec = pl.BlockSpec((tq, D_A), lambda bi: (bi, 0))
    old_t = pl.BlockSpec((1, D_A, past), lambda bi: (bi, 0, 0))
    small = [_resident((1, HD_B))] * 4 + [_resident((1, 2 * HD_B))]
    return pl.pallas_call(
        functools.partial(_attn_cached_kernel, lambda_init=lambda_init),
        grid=(b,),
        in_specs=[qspec, qspec, old_t, old_t, old_t,
                  pl.BlockSpec((1, past * N_B, LANES), lambda bi: (bi, 0, 0)),
                  qspec, qspec, qspec, qspec,
                  pl.BlockSpec((1, AUG_ROWS, t_all), lambda bi: (bi, 0, 0))] + small,
        out_specs=[qspec, qspec],
        out_shape=[jax.ShapeDtypeStruct((b * tq, D_A), BF16), jax.ShapeDtypeStruct((b * tq, D_B), BF16)],
        scratch_shapes=[pltpu.VMEM((n_chain, 2 * LANES, LANES), BF16)],
        compiler_params=_params(1),
        name="attn_cached",
    )(qa16, qb16, ckat, cvat, ckbt, cvb, ka16, va16, kb16, vb16, aug, *lams, sub)


def _tokens_minor(x):
    b, t = x.shape[:2]
    return jnp.moveaxis(x.reshape(b, t, -1), 1, 2)


def _tokens_major(xt, tail):
    b, _, t = xt.shape
    return jnp.moveaxis(xt, 2, 1).reshape(b, t, *tail)


def _layer(x, cache, w, lambda_init):
    b, t, _ = x.shape
    n = b * t
    x1 = _ffn1(x.reshape(n, D_MODEL), w["gn1"], w["w1a"], w["w2a"])
    (k_a, v_a, lft, k_b, vb, qa16, ka16, va16, qb16, kb16, vb16) = _proj(
        x1, w["gmix"], w["wa"], w["wf"], w["bf"], w["wb"], b, t, tokens_minor=cache is None)
    if cache is None:
        aug = _cum_aug(lft, blocked=True)
        oa, ob = _attn_prompt(qa16, qb16, ka16, kb16, va16, vb16, aug, w["lams"], w["sub"], b, t, lambda_init)
        news = (_tokens_major(k_a, (N_A, HD_A)), _tokens_major(v_a, (N_A, HD_A)), _tokens_major(lft, (N_A,)),
                _tokens_major(k_b, (N_B, 2, HD_B)), vb.reshape(b, t, N_B, 2 * HD_B))
    else:
        cfk, cfv, cflf, cdk, cdv = cache
        past = cfk.shape[1]
        aug = _cum_aug(jnp.concatenate([_tokens_minor(cflf.astype(F32)), lft], axis=2), blocked=False)
        oa, ob = _attn_cached(qa16, qb16, _tokens_minor(cfk), _tokens_minor(cfv), _tokens_minor(cdk),
                              cdv.reshape(b, past * N_B, 2 * HD_B), ka16, va16, kb16, vb16, aug,
                              w["lams"], w["sub"], b, t, lambda_init)
        news = (k_a.reshape(b, t, N_A, HD_A), v_a.reshape(b, t, N_A, HD_A), _tokens_major(lft, (N_A,)),
                k_b.reshape(b, t, N_B, 2, HD_B), vb.reshape(b, t, N_B, 2 * HD_B))
    return x1, oa, ob, news


def kernel(x_prompt, x_sample, cache_fox_k, cache_fox_v, cache_fox_logf, cache_diff_k, cache_diff_v,
           norm_ffn1, w_ffn1_in, w_ffn1_out, norm_mix, w_in, b_forget,
           lambda_q1, lambda_k1, lambda_q2, lambda_k2, diff_subln, w_out,
           norm_ffn2, w_ffn2_in, w_ffn2_out, norm_final):
    depth = w_in.shape[0]
    xp, xs = x_prompt, x_sample
    outs_p, outs_s = [], []
    gfin = norm_final.reshape(1, D_MODEL)
    for l in range(depth):
        lambda_init = 0.8 - 0.6 * math.exp(-0.3 * l)
        f_lo, f_hi = 3 * D_A, 3 * D_A + N_A
        w = {
            "gn1": norm_ffn1[l].reshape(1, D_MODEL),
            "w1a": w_ffn1_in[l].astype(BF16), "w2a": w_ffn1_out[l].astype(BF16),
            "gmix": norm_mix[l].reshape(1, D_MODEL),
            "wa": w_in[l][:, :f_lo].astype(BF16),
            "wf": jnp.pad(w_in[l][:, f_lo:f_hi], ((0, 0), (0, LANES - N_A))).astype(BF16),
            "bf": jnp.pad(b_forget[l], (0, LANES - N_A)).reshape(1, LANES),
            "wb": w_in[l][:, f_hi:].astype(BF16),
            "lams": [v[l].reshape(1, HD_B) for v in (lambda_q1, lambda_k1, lambda_q2, lambda_k2)],
            "sub": diff_subln[l].reshape(1, 2 * HD_B),
            "wo": w_out[l].astype(BF16),
            "gn2": norm_ffn2[l].reshape(1, D_MODEL),
            "w1b": w_ffn2_in[l].astype(BF16), "w2b": w_ffn2_out[l].astype(BF16),
        }
        streams = []
        for x, cache in ((xp, None),
                         (xs, (cache_fox_k[l], cache_fox_v[l], cache_fox_logf[l],
                               cache_diff_k[l], cache_diff_v[l]))):
            b, t, _ = x.shape
            x1, oa, ob, news = _layer(x, cache, w, lambda_init)
            y = _post(x1, oa, ob, w["wo"], w["gn2"], w["w1b"], w["w2b"], gfin, l == depth - 1)
            streams.append((y.reshape(b, t, D_MODEL), news))
        (xp, news_p), (xs, news_s) = streams
        outs_p.append(news_p)
        outs_s.append(news_s)
    stack = lambda outs, i: jnp.stack([o[i] for o in outs])
    return (xp, xs) + tuple(stack(outs_p, i) for i in range(5)) + tuple(stack(outs_s, i) for i in range(5))
```

```python
import functools
import math

import jax
import jax.numpy as jnp
from jax import lax
from jax.experimental import pallas as pl
from jax.experimental.pallas import tpu as pltpu

D_MODEL = 1024
D_FF = 2816
N_A = 8
HD_A = 64
N_B = 4
HD_B = 64
D_A = N_A * HD_A
D_B = N_B * 2 * HD_B
CHUNK = 64
NORM_EPS = 1e-6
NEG_INF = -1e30
ALIBI_SLOPES = (0.25, 0.0625, 0.015625, 0.00390625)

LOG2E = math.log2(math.e)
LANES = 128
MXU_WIDTH = 256
FF_CHUNKS = (1536, 1280)
assert sum(FF_CHUNKS) == D_FF and all(c % MXU_WIDTH == 0 for c in FF_CHUNKS)
SCAN_BATCH = 8
TOKEN_TILE = 512
ATTN_BLOCK = 256
KEY_BLOCKS_PER_STEP = 2
AUG_ROWS = 32
VMEM_LIMIT = 56 * 1024 * 1024
N_PAIRS = N_A // 2

F32 = jnp.float32
BF16 = jnp.bfloat16


def _params(n_axes):
    return pltpu.CompilerParams(dimension_semantics=("arbitrary",) * n_axes,
                                vmem_limit_bytes=VMEM_LIMIT)


def _resident(shape):
    return pl.BlockSpec(shape, lambda *_: (0,) * len(shape), pipeline_mode=pl.Buffered(1))


def _rms(xf, g):
    ms = jnp.mean(xf * xf, axis=-1, keepdims=True)
    return xf * lax.rsqrt(ms + NORM_EPS) * g


def _swiglu_residual(x, gn_ref, w1_ref, w2_ref):
    h = _rms(x, gn_ref[...]).astype(BF16)
    acc = None
    lo = 0
    for width in FF_CHUNKS:
        g = jnp.dot(h, w1_ref[:, lo:lo + width], preferred_element_type=F32)
        u = jnp.dot(h, w1_ref[:, D_FF + lo:D_FF + lo + width], preferred_element_type=F32)
        a = (g * jax.nn.sigmoid(g) * u).astype(BF16)
        d = jnp.dot(a, w2_ref[lo:lo + width, :], preferred_element_type=F32)
        acc = d if acc is None else acc + d
        lo += width
    return x + 0.5 * acc


def _ffn1_kernel(x_ref, gn_ref, w1_ref, w2_ref, o_ref):
    o_ref[...] = _swiglu_residual(x_ref[...], gn_ref, w1_ref, w2_ref)


def _ffn1(x, gn, w1, w2):
    n = x.shape[0]
    tm = min(TOKEN_TILE, n)
    return pl.pallas_call(
        _ffn1_kernel,
        grid=(n // tm,),
        in_specs=[pl.BlockSpec((tm, D_MODEL), lambda i: (i, 0)),
                  _resident((1, D_MODEL)), _resident(w1.shape), _resident(w2.shape)],
        out_specs=pl.BlockSpec((tm, D_MODEL), lambda i: (i, 0)),
        out_shape=jax.ShapeDtypeStruct((n, D_MODEL), F32),
        compiler_params=_params(1),
        name="ffn1",
    )(x, gn, w1, w2)


def _post_kernel(x_ref, oa_ref, ob_ref, wo_ref, gn_ref, w1_ref, w2_ref, gf_ref, y_ref, *, final):
    x = x_ref[...]
    x = x + jnp.dot(oa_ref[...], wo_ref[:D_A, :], preferred_element_type=F32)
    x = x + jnp.dot(ob_ref[...], wo_ref[D_A:, :], preferred_element_type=F32)
    x = _swiglu_residual(x, gn_ref, w1_ref, w2_ref)
    y_ref[...] = _rms(x, gf_ref[...]) if final else x


def _post(x, oa, ob, wo, gn, w1, w2, gf, final):
    n = x.shape[0]
    tm = min(TOKEN_TILE, n)
    row = lambda i: (i, 0)
    return pl.pallas_call(
        functools.partial(_post_kernel, final=final),
        grid=(n // tm,),
        in_specs=[pl.BlockSpec((tm, D_MODEL), row), pl.BlockSpec((tm, D_A), row),
                  pl.BlockSpec((tm, D_B), row), _resident(wo.shape), _resident((1, D_MODEL)),
                  _resident(w1.shape), _resident(w2.shape), _resident((1, D_MODEL))],
        out_specs=pl.BlockSpec((tm, D_MODEL), row),
        out_shape=jax.ShapeDtypeStruct((n, D_MODEL), F32),
        compiler_params=_params(1),
        name="post",
    )(x, oa, ob, wo, gn, w1, w2, gf)


def _proj_kernel(x_ref, gn_ref, wa_ref, wf_ref, bf_ref, wb_ref, *out_refs, seg, kblk, tokens_minor):
    tm = x_ref.shape[0]
    h = _rms(x_ref[...], gn_ref[...]).astype(BF16)
    pa = jnp.dot(h, wa_ref[...], preferred_element_type=F32)
    pb = jnp.dot(h, wb_ref[...], preferred_element_type=F32)
    z = jnp.dot(h, wf_ref[...], preferred_element_type=F32) + bf_ref[...]
    lft = (jnp.minimum(z, 0.0) - jnp.log1p(jnp.exp(-jnp.abs(z)))).T
    ka, va = pa[:, D_A:2 * D_A], pa[:, 2 * D_A:]
    kb, vb = pb[:, D_B:2 * D_B], pb[:, 2 * D_B:]
    qa16 = (pa[:, :D_A] * (HD_A ** -0.5 * LOG2E)).astype(BF16)
    qb16 = (pb[:, :D_B] * (HD_B ** -0.5 * LOG2E)).astype(BF16)
    if tokens_minor:
        kat_ref, vat_ref, lft_ref, kbt_ref, vb_ref, qa16_ref, kat16_ref, va16_ref, qb16_ref, kbt16_ref, vb16_ref = out_refs
        kat, vat, kbt = ka.T, va.T, kb.T
    else:
        ka_ref, va_ref, lft_ref, kb_ref, vb_ref, qa16_ref, ka16_ref, va16_ref, qb16_ref, kb16_ref, vb16_ref = out_refs
        ka16_ref[...] = ka.astype(BF16)
        kb16_ref[...] = kb.astype(BF16)
        for hh in range(N_A):
            cols = slice(hh * HD_A, (hh + 1) * HD_A)
            ka_ref[pl.ds(hh, tm, stride=N_A), :] = ka[:, cols]
            va_ref[pl.ds(hh, tm, stride=N_A), :] = va[:, cols]
            kb_ref[pl.ds(hh, tm, stride=N_A), :] = kb[:, cols]
    qa16_ref[...] = qa16
    qb16_ref[...] = qb16
    va16_ref[...] = va.astype(BF16)
    vb16_ref[...] = vb.astype(BF16)
    for hh in range(N_B):
        vb_ref[pl.ds(hh, tm, stride=N_B), :] = vb[:, hh * LANES:(hh + 1) * LANES]
    for bb in range(tm // seg):
        cols = slice(bb * seg, (bb + 1) * seg)
        lft_ref[bb] = lft[:N_A, cols]
        if tokens_minor:
            kat_ref[bb] = kat[:, cols]
            vat_ref[bb] = vat[:, cols]
            kbt_ref[bb] = kbt[:, cols]
            for jj in range(seg // kblk):
                kc = slice(bb * seg + jj * kblk, bb * seg + (jj + 1) * kblk)
                kat16_ref[bb, jj] = kat[:, kc].astype(BF16)
                kbt16_ref[bb, jj] = kbt[:, kc].astype(BF16)


def _proj(x, gn, wa, wf, bf, wb, b, t, tokens_minor):
    n = x.shape[0]
    tm = min(TOKEN_TILE, n)
    seg = min(tm, t)
    nb = tm // seg
    nt = t // seg
    kblk = min(ATTN_BLOCK, seg)
    row = lambda i: (i, 0)
    tok = lambda i: (i // nt, 0, i % nt)
    blk = lambda i: (i // nt, i % nt, 0, 0)
    b16 = (jax.ShapeDtypeStruct((n, D_A), BF16), pl.BlockSpec((tm, D_A), row))
    lft = (jax.ShapeDtypeStruct((b, N_A, t), F32), pl.BlockSpec((nb, N_A, seg), tok))
    vb = (jax.ShapeDtypeStruct((n * N_B, LANES), F32), pl.BlockSpec((tm * N_B, LANES), row))
    if tokens_minor:
        kv = (jax.ShapeDtypeStruct((b, D_A, t), F32), pl.BlockSpec((nb, D_A, seg), tok))
        k16 = (jax.ShapeDtypeStruct((b, t // kblk, D_A, kblk), BF16),
               pl.BlockSpec((nb, seg // kblk, D_A, kblk), blk))
    else:
        kv = (jax.ShapeDtypeStruct((n * N_A, HD_A), F32), pl.BlockSpec((tm * N_A, HD_A), row))
        k16 = b16
    outs = [kv, kv, lft, kv, vb, b16, k16, b16, b16, k16, b16]
    return pl.pallas_call(
        functools.partial(_proj_kernel, seg=seg, kblk=kblk, tokens_minor=tokens_minor),
        grid=(n // tm,),
        in_specs=[pl.BlockSpec((tm, D_MODEL), row), _resident((1, D_MODEL)), _resident(wa.shape),
                  _resident(wf.shape), _resident(bf.shape), _resident(wb.shape)],
        out_specs=[o[1] for o in outs],
        out_shape=[o[0] for o in outs],
        compiler_params=_params(1),
        name="proj",
    )(x, gn, wa, wf, bf, wb)


def _split3(x):
    hi = x.astype(BF16).astype(F32)
    r = x - hi
    mid = r.astype(BF16).astype(F32)
    lo = (r - mid).astype(BF16).astype(F32)
    return hi, mid, lo


def _cum_kernel(lft_ref, aug_ref, pad_ref, *, t, blocked):
    w = ATTN_BLOCK
    nb = lft_ref.shape[0]
    rows = nb * N_A
    pad_ref[...] = jnp.zeros(pad_ref.shape, F32)
    for bb in range(nb):
        pad_ref[bb * N_A:(bb + 1) * N_A, 0:t] = lft_ref[bb]
    r = lax.broadcasted_iota(jnp.int32, (w, w), 0)
    c = lax.broadcasted_iota(jnp.int32, (w, w), 1)
    tri = jnp.where(r <= c, 1.0, 0.0).astype(BF16)
    zero = jnp.zeros((N_A, w), F32)
    carry = jnp.zeros((rows, 1), F32)
    for s in range(pad_ref.shape[1] // w):
        lo_col = s * w
        cols = min(w, t - lo_col)
        x = jnp.concatenate(_split3(pad_ref[:, lo_col:lo_col + w]), axis=0).astype(BF16)
        y = jnp.dot(x, tri, preferred_element_type=F32)
        cs = y[0:rows] + y[rows:2 * rows] + y[2 * rows:3 * rows] + carry
        carry = cs[:, w - 1:w]
        pieces = _split3(cs * -LOG2E)
        for bb in range(nb):
            sl = slice(bb * N_A, (bb + 1) * N_A)
            aug = jnp.concatenate([p[sl] for p in pieces] + [zero], axis=0).astype(BF16)
            if blocked:
                aug_ref[bb, s] = aug
            else:
                aug_ref[bb, :, lo_col:lo_col + cols] = aug[:, :cols]


def _cum_aug(lft, blocked):
    b, _, t = lft.shape
    w = ATTN_BLOCK
    t_pad = -(-t // w) * w
    nb = math.gcd(b, SCAN_BATCH)
    if blocked:
        out_shape = jax.ShapeDtypeStruct((b, t // w, AUG_ROWS, w), BF16)
        out_spec = pl.BlockSpec((nb, t // w, AUG_ROWS, w), lambda i: (i, 0, 0, 0))
    else:
        out_shape = jax.ShapeDtypeStruct((b, AUG_ROWS, t), BF16)
        out_spec = pl.BlockSpec((nb, AUG_ROWS, t), lambda i: (i, 0, 0))
    return pl.pallas_call(
        functools.partial(_cum_kernel, t=t, blocked=blocked),
        grid=(b // nb,),
        in_specs=[pl.BlockSpec((nb, N_A, t), lambda i: (i, 0, 0))],
        out_specs=out_spec,
        out_shape=out_shape,
        scratch_shapes=[pltpu.VMEM((nb * N_A, t_pad), F32)],
        compiler_params=_params(1),
        name="cum_aug",
    )(lft)


def _lane_halves(q):
    lane = lax.broadcasted_iota(jnp.int32, q.shape, 1)
    zero = jnp.zeros_like(q)
    return jnp.where(lane < HD_A, q, zero), jnp.where(lane >= HD_A, q, zero)


def _piece_selector(shape, head, n_heads):
    lane = lax.broadcasted_iota(jnp.int32, shape, 1)
    hit = (lane == head) | (lane == n_heads + head) | (lane == 2 * n_heads + head)
    return jnp.where(hit, 1.0, 0.0).astype(BF16)


def _alibi_rows(k_pos):
    w = k_pos.shape[1]
    row = lax.broadcasted_iota(jnp.int32, (4 * N_B, 1), 0)
    head = row % N_B
    slope = jnp.zeros((4 * N_B, 1), F32)
    for h in range(N_B):
        slope = jnp.where(head == h, ALIBI_SLOPES[h] * LOG2E, slope)
    hi, mid, lo = _split3(slope * k_pos)
    piece = jnp.where(row < N_B, hi, jnp.where(row < 2 * N_B, mid, jnp.where(row < 3 * N_B, lo, 0.0)))
    return jnp.concatenate([piece.astype(BF16), jnp.zeros((LANES - 4 * N_B, w), BF16)], axis=0)


def _alibi_fold(slope, q_pos, k_pos):
    return (-2.0 * LOG2E * slope) * jnp.maximum(k_pos - q_pos, 0).astype(F32)


def _diff_finish(o1, o2, lam, sub_ref, lambda_init):
    return _rms(o1 - lam * o2, sub_ref[...]) * (1.0 - lambda_init)


def _lambda(lq1_ref, lk1_ref, lq2_ref, lk2_ref, lambda_init):
    return (jnp.exp(jnp.sum(lq1_ref[...] * lk1_ref[...], axis=1, keepdims=True))
            - jnp.exp(jnp.sum(lq2_ref[...] * lk2_ref[...], axis=1, keepdims=True)) + lambda_init)


def _attn_prompt_kernel(qa_ref, qb_ref, kat_ref, kbt_ref, va_ref, vb_ref, aug_ref,
                        lq1_ref, lk1_ref, lq2_ref, lk2_ref, sub_ref, oa_ref, ob_ref,
                        qs_ref, m_ref, acc_ref, tbl_ref, *, lambda_init):
    bi, i = pl.program_id(0), pl.program_id(1)
    tq = qa_ref.shape[0]
    n_chain = N_PAIRS + N_B

    @pl.when((bi == 0) & (i == 0))
    def _():
        r = lax.broadcasted_iota(jnp.int32, (2 * tq, tq), 0)
        c = lax.broadcasted_iota(jnp.int32, (2 * tq, tq), 1)
        q_idx = jnp.where(r >= tq, r - tq, r)
        tbl_ref[0] = jnp.where(q_idx >= c, 0.0, NEG_INF)
        visible = (q_idx // CHUNK) >= (c // CHUNK)
        for h in range(N_B):
            tbl_ref[1 + h] = jnp.where(visible, _alibi_fold(ALIBI_SLOPES[h], q_idx, c), NEG_INF)
        for ch in range(n_chain):
            if ch < N_PAIRS:
                qs_ref[ch, 0:tq, LANES:] = _piece_selector((tq, LANES), 2 * ch, N_A)
                qs_ref[ch, tq:, LANES:] = _piece_selector((tq, LANES), 2 * ch + 1, N_A)
            else:
                sel = _piece_selector((tq, LANES), ch - N_PAIRS, N_B)
                qs_ref[ch, 0:tq, LANES:] = sel
                qs_ref[ch, tq:, LANES:] = sel

    for ch in range(n_chain):
        q_ref, g = (qa_ref, ch) if ch < N_PAIRS else (qb_ref, ch - N_PAIRS)
        lo, hi = _lane_halves(q_ref[:, g * LANES:(g + 1) * LANES])
        qs_ref[ch, 0:tq, 0:LANES] = lo
        qs_ref[ch, tq:, 0:LANES] = hi

    def step(j, first, width=1):
        wk = width * tq
        rows = pl.ds(pl.multiple_of(j * tq, tq), wk)
        blocks = [j + d for d in range(width)]
        ones = jnp.ones((wk, LANES), BF16)
        aug_a = jnp.concatenate([jnp.concatenate([aug_ref[0, jb] for jb in blocks], axis=1),
                                 jnp.zeros((LANES - AUG_ROWS, wk), BF16)], axis=0)
        aug_b = _alibi_rows((j * tq + lax.broadcasted_iota(jnp.int32, (1, wk), 1)).astype(F32))
        for ch in range(n_chain):
            if ch < N_PAIRS:
                kt_ref, v_ref, g, aug, tbl = kat_ref, va_ref, ch, aug_a, 0
            else:
                kt_ref, v_ref, g, aug, tbl = kbt_ref, vb_ref, ch - N_PAIRS, aug_b, 1 + ch - N_PAIRS
            kt = jnp.concatenate([kt_ref[0, jb, g * LANES:(g + 1) * LANES, :] for jb in blocks], axis=1)
            s = jnp.dot(qs_ref[ch], jnp.concatenate([kt, aug], axis=0), preferred_element_type=F32)
            v = v_ref[rows, g * LANES:(g + 1) * LANES]
            if first:
                s = s + tbl_ref[tbl]
                m_new = jnp.broadcast_to(jnp.max(s, axis=1, keepdims=True), (2 * tq, LANES))
            else:
                m_prev = m_ref[ch]
                m_new = jnp.maximum(m_prev, jnp.max(s, axis=1, keepdims=True))
            p = jnp.exp2(s - jnp.concatenate([m_new] * (wk // LANES), axis=1))
            pv = jnp.dot(p.astype(BF16), jnp.concatenate([v, ones], axis=1), preferred_element_type=F32)
            if first:
                acc_ref[ch] = pv
            else:
                alpha = jnp.exp2(m_prev - m_new)
                acc_ref[ch] = jnp.concatenate([alpha, alpha], axis=1) * acc_ref[ch] + pv
            m_ref[ch] = m_new

    def wide_body(jj, carry):
        step(jj * KEY_BLOCKS_PER_STEP, False, KEY_BLOCKS_PER_STEP)
        return carry

    def single_body(j, carry):
        step(j, False)
        return carry

    step(i, True)
    n_wide = i // KEY_BLOCKS_PER_STEP
    lax.fori_loop(0, n_wide, wide_body, 0)
    lax.fori_loop(n_wide * KEY_BLOCKS_PER_STEP, i, single_body, 0)

    lane = lax.broadcasted_iota(jnp.int32, (tq, LANES), 1)
    lam = _lambda(lq1_ref, lk1_ref, lq2_ref, lk2_ref, lambda_init)
    for ch in range(n_chain):
        o = acc_ref[ch, :, 0:LANES] / acc_ref[ch, :, LANES:]
        if ch < N_PAIRS:
            oa_ref[:, ch * LANES:(ch + 1) * LANES] = jnp.where(lane < HD_A, o[:tq], o[tq:]).astype(BF16)
        else:
            h = ch - N_PAIRS
            ob_ref[:, h * LANES:(h + 1) * LANES] = _diff_finish(
                o[:tq], o[tq:], lam, sub_ref, lambda_init).astype(BF16)


def _attn_prompt(qa16, qb16, kat16, kbt16, va16, vb16, aug, lams, sub, b, t, lambda_init):
    tq = ATTN_BLOCK
    nq = t // tq
    qspec = pl.BlockSpec((tq, D_A), lambda bi, i: (bi * nq + i, 0))
    kspec = pl.BlockSpec((1, nq, D_A, tq), lambda bi, i: (bi, 0, 0, 0))
    vspec = pl.BlockSpec((t, D_A), lambda bi, i: (bi, 0))
    small = [_resident((1, HD_B))] * 4 + [_resident((1, 2 * HD_B))]
    n_chain = N_PAIRS + N_B
    return pl.pallas_call(
        functools.partial(_attn_prompt_kernel, lambda_init=lambda_init),
        grid=(b, nq),
        in_specs=[qspec, qspec, kspec, kspec, vspec, vspec,
                  pl.BlockSpec((1, nq, AUG_ROWS, tq), lambda bi, i: (bi, 0, 0, 0))] + small,
        out_specs=[qspec, qspec],
        out_shape=[jax.ShapeDtypeStruct((b * t, D_A), BF16), jax.ShapeDtypeStruct((b * t, D_B), BF16)],
        scratch_shapes=[pltpu.VMEM((n_chain, 2 * tq, 2 * LANES), BF16),
                        pltpu.VMEM((n_chain, 2 * tq, LANES), F32),
                        pltpu.VMEM((n_chain, 2 * tq, 2 * LANES), F32),
                        pltpu.VMEM((1 + N_B, 2 * tq, tq), F32)],
        compiler_params=_params(2),
        name="attn_prompt",
    )(qa16, qb16, kat16, kbt16, va16, vb16, aug, *lams, sub)


def _nt_dot(a, b):
    return lax.dot_general(a, b, (((1,), (1,)), ((), ())), preferred_element_type=F32)


def _attn_cached_kernel(qa_ref, qb_ref, ckat_ref, cvat_ref, ckbt_ref, cvb_ref,
                        ka_ref, va_ref, kb_ref, vb_ref, aug_ref,
                        lq1_ref, lk1_ref, lq2_ref, lk2_ref, sub_ref, oa_ref, ob_ref,
                        kn_ref, *, lambda_init):
    tq = qa_ref.shape[0]
    past = ckat_ref.shape[2]
    t_all = past + tq
    n_chain = N_PAIRS + N_B
    r = lax.broadcasted_iota(jnp.int32, (2 * tq, LANES), 0)
    c = lax.broadcasted_iota(jnp.int32, (2 * tq, LANES), 1)
    q_idx = jnp.where(r >= tq, r - tq, r)
    is_key = c < tq
    causal_tbl = jnp.where((q_idx >= c) & is_key, 0.0, NEG_INF)
    visible = (((past + q_idx) // CHUNK) >= ((past + c) // CHUNK)) & is_key
    lane = lax.broadcasted_iota(jnp.int32, (tq, LANES), 1)
    lam = _lambda(lq1_ref, lk1_ref, lq2_ref, lk2_ref, lambda_init)
    aug_c = jnp.concatenate([aug_ref[0, :, 0:past], jnp.zeros((LANES - AUG_ROWS, past), BF16)], axis=0)
    alibi_c = _alibi_rows(lax.broadcasted_iota(jnp.int32, (1, past), 1).astype(F32))
    alibi_n = _alibi_rows((past + lax.broadcasted_iota(jnp.int32, (1, LANES), 1)).astype(F32))
    kn_ref[...] = jnp.zeros(kn_ref.shape, BF16)
    row_pad = jnp.zeros((LANES - tq, LANES), BF16)

    def new_keys_t(k_new):
        return jnp.concatenate([k_new, row_pad], axis=0).astype(F32).T.astype(BF16)

    for ch in range(n_chain):
        fox = ch < N_PAIRS
        g = ch if fox else ch - N_PAIRS
        rows = slice(g * LANES, (g + 1) * LANES)
        if fox:
            lo, hi = _lane_halves(qa_ref[:, rows])
            sel_lo = _piece_selector((tq, LANES), 2 * g, N_A)
            sel_hi = _piece_selector((tq, LANES), 2 * g + 1, N_A)
            k_old = jnp.concatenate([ckat_ref[0, rows, :].astype(BF16), aug_c], axis=0)
            kn_ref[ch, 0:LANES, :] = new_keys_t(ka_ref[:, rows])
            kn_ref[ch, LANES:LANES + AUG_ROWS, 0:tq] = aug_ref[0, :, past:t_all]
            tbl = causal_tbl
        else:
            lo, hi = _lane_halves(qb_ref[:, rows])
            sel_lo = sel_hi = _piece_selector((tq, LANES), g, N_B)
            k_old = jnp.concatenate([ckbt_ref[0, rows, :].astype(BF16), alibi_c], axis=0)
            kn_ref[ch, 0:LANES, :] = new_keys_t(kb_ref[:, rows])
            kn_ref[ch, LANES:, :] = alibi_n
            tbl = jnp.where(visible, _alibi_fold(ALIBI_SLOPES[g], q_idx, c), NEG_INF)
        qs = jnp.concatenate([jnp.concatenate([lo, sel_lo], axis=1), jnp.concatenate([hi, sel_hi], axis=1)], axis=0)
        s_old = jnp.dot(qs, k_old, preferred_element_type=F32)
        s_new = jnp.dot(qs, kn_ref[ch], preferred_element_type=F32) + tbl
        m = jnp.maximum(jnp.max(s_old, axis=1, keepdims=True), jnp.max(s_new, axis=1, keepdims=True))
        p_old = jnp.exp2(s_old - m)
        p_new = jnp.exp2(s_new - m)
        l = jnp.sum(p_old, axis=1, keepdims=True) + jnp.sum(p_new, axis=1, keepdims=True)
        v_new = jnp.concatenate([(va_ref if fox else vb_ref)[:, rows], row_pad], axis=0)
        o_new = jnp.dot(p_new.astype(BF16), v_new, preferred_element_type=F32)
        if fox:
            o = (_nt_dot(p_old.astype(BF16), cvat_ref[0, rows, :].astype(BF16)) + o_new) / l
            oa_ref[:, rows] = jnp.where(lane < HD_A, o[:tq], o[tq:]).astype(BF16)
        else:
            v_old = cvb_ref[0, pl.ds(g, past, stride=N_B), :].astype(BF16)
            o = (jnp.dot(p_old.astype(BF16), v_old, preferred_element_type=F32) + o_new) / l
            ob_ref[:, rows] = _diff_finish(o[:tq], o[tq:], lam, sub_ref, lambda_init).astype(BF16)


def _attn_cached(qa16, qb16, ckat, cvat, ckbt, cvb, ka16, va16, kb16, vb16, aug, lams, sub, b, tq, lambda_init):
    past = ckat.shape[2]
    t_all = past + tq
    n_chain = N_PAIRS + N_B
    qspec = pl.BlockSpec((tq, D_A), lambda bi: (bi, 0))
    old_t = pl.BlockSpec((1, D_A, past), lambda bi: (bi, 0, 0))
    small = [_resident((1, HD_B))] * 4 + [_resident((1, 2 * HD_B))]
    return pl.pallas_call(
        functools.partial(_attn_cached_kernel, lambda_init=lambda_init),
        grid=(b,),
        in_specs=[qspec, qspec, old_t, old_t, old_t,
                  pl.BlockSpec((1, past * N_B, LANES), lambda bi: (bi, 0, 0)),
                  qspec, qspec, qspec, qspec,
                  pl.BlockSpec((1, AUG_ROWS, t_all), lambda bi: (bi, 0, 0))] + small,
        out_specs=[qspec, qspec],
        out_shape=[jax.ShapeDtypeStruct((b * tq, D_A), BF16), jax.ShapeDtypeStruct((b * tq, D_B), BF16)],
        scratch_shapes=[pltpu.VMEM((n_chain, 2 * LANES, LANES), BF16)],
        compiler_params=_params(1),
        name="attn_cached",
    )(qa16, qb16, ckat, cvat, ckbt, cvb, ka16, va16, kb16, vb16, aug, *lams, sub)


def _tokens_minor(x):
    b, t = x.shape[:2]
    return jnp.moveaxis(x.reshape(b, t, -1), 1, 2)


def _tokens_major(xt, tail):
    b, _, t = xt.shape
    return jnp.moveaxis(xt, 2, 1).reshape(b, t, *tail)


def _layer(x, cache, w, lambda_init):
    b, t, _ = x.shape
    n = b * t
    x1 = _ffn1(x.reshape(n, D_MODEL), w["gn1"], w["w1a"], w["w2a"])
    (k_a, v_a, lft, k_b, vb, qa16, ka16, va16, qb16, kb16, vb16) = _proj(
        x1, w["gmix"], w["wa"], w["wf"], w["bf"], w["wb"], b, t, tokens_minor=cache is None)
    if cache is None:
        aug = _cum_aug(lft, blocked=True)
        oa, ob = _attn_prompt(qa16, qb16, ka16, kb16, va16, vb16, aug, w["lams"], w["sub"], b, t, lambda_init)
        news = (_tokens_major(k_a, (N_A, HD_A)), _tokens_major(v_a, (N_A, HD_A)), _tokens_major(lft, (N_A,)),
                _tokens_major(k_b, (N_B, 2, HD_B)), vb.reshape(b, t, N_B, 2 * HD_B))
    else:
        cfk, cfv, cflf, cdk, cdv = cache
        past = cfk.shape[1]
        aug = _cum_aug(jnp.concatenate([_tokens_minor(cflf.astype(F32)), lft], axis=2), blocked=False)
        oa, ob = _attn_cached(qa16, qb16, _tokens_minor(cfk), _tokens_minor(cfv), _tokens_minor(cdk),
                              cdv.reshape(b, past * N_B, 2 * HD_B), ka16, va16, kb16, vb16, aug,
                              w["lams"], w["sub"], b, t, lambda_init)
        news = (k_a.reshape(b, t, N_A, HD_A), v_a.reshape(b, t, N_A, HD_A), _tokens_major(lft, (N_A,)),
                k_b.reshape(b, t, N_B, 2, HD_B), vb.reshape(b, t, N_B, 2 * HD_B))
    return x1, oa, ob, news


def kernel(x_prompt, x_sample, cache_fox_k, cache_fox_v, cache_fox_logf, cache_diff_k, cache_diff_v,
           norm_ffn1, w_ffn1_in, w_ffn1_out, norm_mix, w_in, b_forget,
           lambda_q1, lambda_k1, lambda_q2, lambda_k2, diff_subln, w_out,
           norm_ffn2, w_ffn2_in, w_ffn2_out, norm_final):
    depth = w_in.shape[0]
    xp, xs = x_prompt, x_sample
    outs_p, outs_s = [], []
    gfin = norm_final.reshape(1, D_MODEL)
    for l in range(depth):
        lambda_init = 0.8 - 0.6 * math.exp(-0.3 * l)
        f_lo, f_hi = 3 * D_A, 3 * D_A + N_A
        w = {
            "gn1": norm_ffn1[l].reshape(1, D_MODEL),
            "w1a": w_ffn1_in[l].astype(BF16), "w2a": w_ffn1_out[l].astype(BF16),
            "gmix": norm_mix[l].reshape(1, D_MODEL),
            "wa": w_in[l][:, :f_lo].astype(BF16),
            "wf": jnp.pad(w_in[l][:, f_lo:f_hi], ((0, 0), (0, LANES - N_A))).astype(BF16),
            "bf": jnp.pad(b_forget[l], (0, LANES - N_A)).reshape(1, LANES),
            "wb": w_in[l][:, f_hi:].astype(BF16),
            "lams": [v[l].reshape(1, HD_B) for v in (lambda_q1, lambda_k1, lambda_q2, lambda_k2)],
            "sub": diff_subln[l].reshape(1, 2 * HD_B),
            "wo": w_out[l].astype(BF16),
            "gn2": norm_ffn2[l].reshape(1, D_MODEL),
            "w1b": w_ffn2_in[l].astype(BF16), "w2b": w_ffn2_out[l].astype(BF16),
        }
        streams = []
        for x, cache in ((xp, None),
                         (xs, (cache_fox_k[l], cache_fox_v[l], cache_fox_logf[l],
                               cache_diff_k[l], cache_diff_v[l]))):
            b, t, _ = x.shape
            x1, oa, ob, news = _layer(x, cache, w, lambda_init)
            y = _post(x1, oa, ob, w["wo"], w["gn2"], w["w1b"], w["w2b"], gfin, l == depth - 1)
            streams.append((y.reshape(b, t, D_MODEL), news))
        (xp, news_p), (xs, news_s) = streams
        outs_p.append(news_p)
        outs_s.append(news_s)
    stack = lambda outs, i: jnp.stack([o[i] for o in outs])
    return (xp, xs) + tuple(stack(outs_p, i) for i in range(5)) + tuple(stack(outs_s, i) for i in range(5))
```

```python
import functools
import math

import jax
import jax.numpy as jnp
from jax import lax
from jax.experimental import pallas as pl
from jax.experimental.pallas import tpu as pltpu

D_MODEL = 1024
D_FF = 2816
N_A = 8
HD_A = 64
N_B = 4
HD_B = 64
D_A = N_A * HD_A
D_B = N_B * 2 * HD_B
CHUNK = 64
NORM_EPS = 1e-6
NEG_INF = -1e30
ALIBI_SLOPES = (0.25, 0.0625, 0.015625, 0.00390625)

LOG2E = math.log2(math.e)
LANES = 128
MXU_WIDTH = 256
FF_CHUNKS = (1536, 1280)
assert sum(FF_CHUNKS) == D_FF and all(c % MXU_WIDTH == 0 for c in FF_CHUNKS)
SCAN_BATCH = 8
CAST_BYTES = 1 << 20
TOKEN_TILE = 512
ATTN_BLOCK = 256
KEY_BLOCKS_PER_STEP = 2
AUG_ROWS = 32
VMEM_LIMIT = 56 * 1024 * 1024
N_PAIRS = N_A // 2

F32 = jnp.float32
BF16 = jnp.bfloat16


def _params(n_axes):
    return pltpu.CompilerParams(dimension_semantics=("arbitrary",) * n_axes,
                                vmem_limit_bytes=VMEM_LIMIT)


def _resident(shape):
    return pl.BlockSpec(shape, lambda *_: (0,) * len(shape), pipeline_mode=pl.Buffered(1))


def _rms(xf, g):
    ms = jnp.mean(xf * xf, axis=-1, keepdims=True)
    return xf * lax.rsqrt(ms + NORM_EPS) * g


def _swiglu_residual(x, gn_ref, w1_ref, w2_ref):
    h = _rms(x, gn_ref[...]).astype(BF16)
    acc = None
    lo = 0
    for width in FF_CHUNKS:
        g = jnp.dot(h, w1_ref[:, lo:lo + width], preferred_element_type=F32)
        u = jnp.dot(h, w1_ref[:, D_FF + lo:D_FF + lo + width], preferred_element_type=F32)
        a = (g * jax.nn.sigmoid(g) * u).astype(BF16)
        d = jnp.dot(a, w2_ref[lo:lo + width, :], preferred_element_type=F32)
        acc = d if acc is None else acc + d
        lo += width
    return x + 0.5 * acc


def _cast_rows(rows_total, cols):
    rows = max(16, CAST_BYTES // (4 * cols) // 16 * 16)
    while rows_total % rows:
        rows -= 16
    return rows


def _stage_shape(shapes):
    return {c: (2, max(_cast_rows(r, cc) for r, cc in shapes if cc == c), c) for _, c in shapes}


def _cast_weight(w_hbm, w16_ref, stage_ref, sem_ref):
    r_total, cols = w_hbm.shape
    rows = _cast_rows(r_total, cols)
    n = r_total // rows

    def copy(c):
        return pltpu.make_async_copy(w_hbm.at[pl.ds(c * rows, rows), :],
                                     stage_ref.at[c % 2, pl.ds(0, rows), :], sem_ref.at[c % 2])

    copy(0).start()
    for c in range(n):
        if c + 1 < n:
            copy(c + 1).start()
        copy(c).wait()
        w16_ref[pl.ds(c * rows, rows), :] = stage_ref[c % 2, 0:rows, :].astype(BF16)


def _weights_prologue(i, w_hbm_refs, w16_refs, w16_hbm_refs, stages, sem_in, sem_out):
    def publish(k):
        return pltpu.make_async_copy(w16_refs[k], w16_hbm_refs[k], sem_out.at[k])

    @pl.when(i == 0)
    def _():
        for w_hbm, w16 in zip(w_hbm_refs, w16_refs):
            _cast_weight(w_hbm, w16, stages[w_hbm.shape[1]], sem_in)
        for k in range(len(w16_refs)):
            publish(k).start()

    return publish


def _weights_epilogue(i, publish, n_weights):
    @pl.when(i == pl.num_programs(0) - 1)
    def _():
        for k in range(n_weights):
            publish(k).wait()


def _weight_plumbing(weights, cast):
    if not cast:
        return [_resident(w.shape) for w in weights], [], [], []
    any_spec = pl.BlockSpec(memory_space=pl.ANY)
    shapes = [w.shape for w in weights]
    stages = _stage_shape(shapes)
    scratch = ([pltpu.VMEM(sh, BF16) for sh in shapes] + [pltpu.VMEM(stages[c], F32) for c in sorted(stages)]
               + [pltpu.SemaphoreType.DMA((2,)), pltpu.SemaphoreType.DMA((len(weights),))])
    return ([any_spec] * len(weights), [any_spec] * len(weights),
            [jax.ShapeDtypeStruct(sh, BF16) for sh in shapes], scratch)


def _split_weight_refs(refs, n_weights, shapes):
    w16 = refs[:n_weights]
    cols = sorted({c for _, c in shapes})
    stage_refs = refs[n_weights:n_weights + len(cols)]
    sem_in, sem_out = refs[n_weights + len(cols):]
    return w16, dict(zip(cols, stage_refs)), sem_in, sem_out


def _ffn1_kernel(x_ref, gn_ref, w1_in, w2_in, o_ref, *rest, cast):
    if cast:
        i = pl.program_id(0)
        w_hbm = (w1_in, w2_in)
        w16, stages, sem_in, sem_out = _split_weight_refs(rest[2:], 2, [w.shape for w in w_hbm])
        publish = _weights_prologue(i, w_hbm, w16, rest[:2], stages, sem_in, sem_out)
        o_ref[...] = _swiglu_residual(x_ref[...], gn_ref, *w16)
        _weights_epilogue(i, publish, 2)
    else:
        o_ref[...] = _swiglu_residual(x_ref[...], gn_ref, w1_in, w2_in)


def _ffn1(x, gn, w1, w2, cast):
    n = x.shape[0]
    tm = min(TOKEN_TILE, n)
    w_specs, w_out_specs, w_out_shapes, scratch = _weight_plumbing([w1, w2], cast)
    out = pl.pallas_call(
        functools.partial(_ffn1_kernel, cast=cast),
        grid=(n // tm,),
        in_specs=[pl.BlockSpec((tm, D_MODEL), lambda i: (i, 0)), _resident((1, D_MODEL))] + w_specs,
        out_specs=[pl.BlockSpec((tm, D_MODEL), lambda i: (i, 0))] + w_out_specs,
        out_shape=[jax.ShapeDtypeStruct((n, D_MODEL), F32)] + w_out_shapes,
        scratch_shapes=scratch,
        compiler_params=_params(1),
        name="ffn1",
    )(x, gn, w1, w2)
    return out if cast else out[0]


def _post_kernel(x_ref, oa_ref, ob_ref, gn_ref, gf_ref, wo_in, w1_in, w2_in, y_ref, *rest, final, cast):
    if cast:
        i = pl.program_id(0)
        w_hbm = (wo_in, w1_in, w2_in)
        w16, stages, sem_in, sem_out = _split_weight_refs(rest[3:], 3, [w.shape for w in w_hbm])
        publish = _weights_prologue(i, w_hbm, w16, rest[:3], stages, sem_in, sem_out)
        wo_ref, w1_ref, w2_ref = w16
    else:
        wo_ref, w1_ref, w2_ref = wo_in, w1_in, w2_in
    x = x_ref[...]
    x = x + jnp.dot(oa_ref[...], wo_ref[:D_A, :], preferred_element_type=F32)
    x = x + jnp.dot(ob_ref[...], wo_ref[D_A:, :], preferred_element_type=F32)
    x = _swiglu_residual(x, gn_ref, w1_ref, w2_ref)
    y_ref[...] = _rms(x, gf_ref[...]) if final else x
    if cast:
        _weights_epilogue(i, publish, 3)


def _post(x, oa, ob, gn, gf, wo, w1, w2, final, cast):
    n = x.shape[0]
    tm = min(TOKEN_TILE, n)
    row = lambda i: (i, 0)
    w_specs, w_out_specs, w_out_shapes, scratch = _weight_plumbing([wo, w1, w2], cast)
    out = pl.pallas_call(
        functools.partial(_post_kernel, final=final, cast=cast),
        grid=(n // tm,),
        in_specs=[pl.BlockSpec((tm, D_MODEL), row), pl.BlockSpec((tm, D_A), row), pl.BlockSpec((tm, D_B), row),
                  _resident((1, D_MODEL)), _resident((1, D_MODEL))] + w_specs,
        out_specs=[pl.BlockSpec((tm, D_MODEL), row)] + w_out_specs,
        out_shape=[jax.ShapeDtypeStruct((n, D_MODEL), F32)] + w_out_shapes,
        scratch_shapes=scratch,
        compiler_params=_params(1),
        name="post",
    )(x, oa, ob, gn, gf, wo, w1, w2)
    return out if cast else out[0]


def _proj_kernel(x_ref, gn_ref, wa_ref, wf_ref, bf_ref, wb_ref, *out_refs, seg, kblk, tokens_minor):
    tm = x_ref.shape[0]
    h = _rms(x_ref[...], gn_ref[...]).astype(BF16)
    pa = jnp.dot(h, wa_ref[...], preferred_element_type=F32)
    pb = jnp.dot(h, wb_ref[...], preferred_element_type=F32)
    z = jnp.dot(h, wf_ref[...], preferred_element_type=F32) + bf_ref[...]
    lft = (jnp.minimum(z, 0.0) - jnp.log1p(jnp.exp(-jnp.abs(z)))).T
    ka, va = pa[:, D_A:2 * D_A], pa[:, 2 * D_A:]
    kb, vb = pb[:, D_B:2 * D_B], pb[:, 2 * D_B:]
    qa16 = (pa[:, :D_A] * (HD_A ** -0.5 * LOG2E)).astype(BF16)
    qb16 = (pb[:, :D_B] * (HD_B ** -0.5 * LOG2E)).astype(BF16)
    if tokens_minor:
        kat_ref, vat_ref, lft_ref, kbt_ref, vb_ref, qa16_ref, kat16_ref, va16_ref, qb16_ref, kbt16_ref, vb16_ref = out_refs
        kat, vat, kbt = ka.T, va.T, kb.T
    else:
        ka_ref, va_ref, lft_ref, kb_ref, vb_ref, qa16_ref, ka16_ref, va16_ref, qb16_ref, kb16_ref, vb16_ref = out_refs
        ka16_ref[...] = ka.astype(BF16)
        kb16_ref[...] = kb.astype(BF16)
        for hh in range(N_A):
            cols = slice(hh * HD_A, (hh + 1) * HD_A)
            ka_ref[pl.ds(hh, tm, stride=N_A), :] = ka[:, cols]
            va_ref[pl.ds(hh, tm, stride=N_A), :] = va[:, cols]
            kb_ref[pl.ds(hh, tm, stride=N_A), :] = kb[:, cols]
    qa16_ref[...] = qa16
    qb16_ref[...] = qb16
    va16_ref[...] = va.astype(BF16)
    vb16_ref[...] = vb.astype(BF16)
    for hh in range(N_B):
        vb_ref[pl.ds(hh, tm, stride=N_B), :] = vb[:, hh * LANES:(hh + 1) * LANES]
    for bb in range(tm // seg):
        cols = slice(bb * seg, (bb + 1) * seg)
        lft_ref[bb] = lft[:N_A, cols]
        if tokens_minor:
            kat_ref[bb] = kat[:, cols]
            vat_ref[bb] = vat[:, cols]
            kbt_ref[bb] = kbt[:, cols]
            for jj in range(seg // kblk):
                kc = slice(bb * seg + jj * kblk, bb * seg + (jj + 1) * kblk)
                kat16_ref[bb, jj] = kat[:, kc].astype(BF16)
                kbt16_ref[bb, jj] = kbt[:, kc].astype(BF16)


def _proj(x, gn, wa, wf, bf, wb, b, t, tokens_minor):
    n = x.shape[0]
    tm = min(TOKEN_TILE, n)
    seg = min(tm, t)
    nb = tm // seg
    nt = t // seg
    kblk = min(ATTN_BLOCK, seg)
    row = lambda i: (i, 0)
    tok = lambda i: (i // nt, 0, i % nt)
    blk = lambda i: (i // nt, i % nt, 0, 0)
    b16 = (jax.ShapeDtypeStruct((n, D_A), BF16), pl.BlockSpec((tm, D_A), row))
    lft = (jax.ShapeDtypeStruct((b, N_A, t), F32), pl.BlockSpec((nb, N_A, seg), tok))
    vb = (jax.ShapeDtypeStruct((n * N_B, LANES), F32), pl.BlockSpec((tm * N_B, LANES), row))
    if tokens_minor:
        kv = (jax.ShapeDtypeStruct((b, D_A, t), F32), pl.BlockSpec((nb, D_A, seg), tok))
        k16 = (jax.ShapeDtypeStruct((b, t // kblk, D_A, kblk), BF16),
               pl.BlockSpec((nb, seg // kblk, D_A, kblk), blk))
    else:
        kv = (jax.ShapeDtypeStruct((n * N_A, HD_A), F32), pl.BlockSpec((tm * N_A, HD_A), row))
        k16 = b16
    outs = [kv, kv, lft, kv, vb, b16, k16, b16, b16, k16, b16]
    return pl.pallas_call(
        functools.partial(_proj_kernel, seg=seg, kblk=kblk, tokens_minor=tokens_minor),
        grid=(n // tm,),
        in_specs=[pl.BlockSpec((tm, D_MODEL), row), _resident((1, D_MODEL)), _resident(wa.shape),
                  _resident(wf.shape), _resident(bf.shape), _resident(wb.shape)],
        out_specs=[o[1] for o in outs],
        out_shape=[o[0] for o in outs],
        compiler_params=_params(1),
        name="proj",
    )(x, gn, wa, wf, bf, wb)


def _split3(x):
    hi = x.astype(BF16).astype(F32)
    r = x - hi
    mid = r.astype(BF16).astype(F32)
    lo = (r - mid).astype(BF16).astype(F32)
    return hi, mid, lo


def _cum_kernel(lft_ref, aug_ref, pad_ref, *, t, blocked):
    w = ATTN_BLOCK
    nb = lft_ref.shape[0]
    rows = nb * N_A
    pad_ref[...] = jnp.zeros(pad_ref.shape, F32)
    for bb in range(nb):
        pad_ref[bb * N_A:(bb + 1) * N_A, 0:t] = lft_ref[bb]
    r = lax.broadcasted_iota(jnp.int32, (w, w), 0)
    c = lax.broadcasted_iota(jnp.int32, (w, w), 1)
    tri = jnp.where(r <= c, 1.0, 0.0).astype(BF16)
    zero = jnp.zeros((N_A, w), F32)
    carry = jnp.zeros((rows, 1), F32)
    for s in range(pad_ref.shape[1] // w):
        lo_col = s * w
        cols = min(w, t - lo_col)
        x = jnp.concatenate(_split3(pad_ref[:, lo_col:lo_col + w]), axis=0).astype(BF16)
        y = jnp.dot(x, tri, preferred_element_type=F32)
        cs = y[0:rows] + y[rows:2 * rows] + y[2 * rows:3 * rows] + carry
        carry = cs[:, w - 1:w]
        pieces = _split3(cs * -LOG2E)
        for bb in range(nb):
            sl = slice(bb * N_A, (bb + 1) * N_A)
            aug = jnp.concatenate([p[sl] for p in pieces] + [zero], axis=0).astype(BF16)
            if blocked:
                aug_ref[bb, s] = aug
            else:
                aug_ref[bb, :, lo_col:lo_col + cols] = aug[:, :cols]


def _cum_aug(lft, blocked):
    b, _, t = lft.shape
    w = ATTN_BLOCK
    t_pad = -(-t // w) * w
    nb = math.gcd(b, SCAN_BATCH)
    if blocked:
        out_shape = jax.ShapeDtypeStruct((b, t // w, AUG_ROWS, w), BF16)
        out_spec = pl.BlockSpec((nb, t // w, AUG_ROWS, w), lambda i: (i, 0, 0, 0))
    else:
        out_shape = jax.ShapeDtypeStruct((b, AUG_ROWS, t), BF16)
        out_spec = pl.BlockSpec((nb, AUG_ROWS, t), lambda i: (i, 0, 0))
    return pl.pallas_call(
        functools.partial(_cum_kernel, t=t, blocked=blocked),
        grid=(b // nb,),
        in_specs=[pl.BlockSpec((nb, N_A, t), lambda i: (i, 0, 0))],
        out_specs=out_spec,
        out_shape=out_shape,
        scratch_shapes=[pltpu.VMEM((nb * N_A, t_pad), F32)],
        compiler_params=_params(1),
        name="cum_aug",
    )(lft)


def _lane_halves(q):
    lane = lax.broadcasted_iota(jnp.int32, q.shape, 1)
    zero = jnp.zeros_like(q)
    return jnp.where(lane < HD_A, q, zero), jnp.where(lane >= HD_A, q, zero)


def _piece_selector(shape, head, n_heads):
    lane = lax.broadcasted_iota(jnp.int32, shape, 1)
    hit = (lane == head) | (lane == n_heads + head) | (lane == 2 * n_heads + head)
    return jnp.where(hit, 1.0, 0.0).astype(BF16)


def _alibi_rows(k_pos):
    w = k_pos.shape[1]
    row = lax.broadcasted_iota(jnp.int32, (4 * N_B, 1), 0)
    head = row % N_B
    slope = jnp.zeros((4 * N_B, 1), F32)
    for h in range(N_B):
        slope = jnp.where(head == h, ALIBI_SLOPES[h] * LOG2E, slope)
    hi, mid, lo = _split3(slope * k_pos)
    piece = jnp.where(row < N_B, hi, jnp.where(row < 2 * N_B, mid, jnp.where(row < 3 * N_B, lo, 0.0)))
    return jnp.concatenate([piece.astype(BF16), jnp.zeros((LANES - 4 * N_B, w), BF16)], axis=0)


def _alibi_fold(slope, q_pos, k_pos):
    return (-2.0 * LOG2E * slope) * jnp.maximum(k_pos - q_pos, 0).astype(F32)


def _diff_finish(o1, o2, lam, sub_ref, lambda_init):
    return _rms(o1 - lam * o2, sub_ref[...]) * (1.0 - lambda_init)


def _lambda(lq1_ref, lk1_ref, lq2_ref, lk2_ref, lambda_init):
    return (jnp.exp(jnp.sum(lq1_ref[...] * lk1_ref[...], axis=1, keepdims=True))
            - jnp.exp(jnp.sum(lq2_ref[...] * lk2_ref[...], axis=1, keepdims=True)) + lambda_init)


def _attn_prompt_kernel(qa_ref, qb_ref, kat_ref, kbt_ref, va_ref, vb_ref, aug_ref,
                        lq1_ref, lk1_ref, lq2_ref, lk2_ref, sub_ref, oa_ref, ob_ref,
                        qs_ref, m_ref, acc_ref, tbl_ref, *, lambda_init):
    bi, i = pl.program_id(0), pl.program_id(1)
    tq = qa_ref.shape[0]
    n_chain = N_PAIRS + N_B

    @pl.when((bi == 0) & (i == 0))
    def _():
        r = lax.broadcasted_iota(jnp.int32, (2 * tq, tq), 0)
        c = lax.broadcasted_iota(jnp.int32, (2 * tq, tq), 1)
        q_idx = jnp.where(r >= tq, r - tq, r)
        tbl_ref[0] = jnp.where(q_idx >= c, 0.0, NEG_INF)
        visible = (q_idx // CHUNK) >= (c // CHUNK)
        for h in range(N_B):
            tbl_ref[1 + h] = jnp.where(visible, _alibi_fold(ALIBI_SLOPES[h], q_idx, c), NEG_INF)
        for ch in range(n_chain):
            if ch < N_PAIRS:
                qs_ref[ch, 0:tq, LANES:] = _piece_selector((tq, LANES), 2 * ch, N_A)
                qs_ref[ch, tq:, LANES:] = _piece_selector((tq, LANES), 2 * ch + 1, N_A)
            else:
                sel = _piece_selector((tq, LANES), ch - N_PAIRS, N_B)
                qs_ref[ch, 0:tq, LANES:] = sel
                qs_ref[ch, tq:, LANES:] = sel

    for ch in range(n_chain):
        q_ref, g = (qa_ref, ch) if ch < N_PAIRS else (qb_ref, ch - N_PAIRS)
        lo, hi = _lane_halves(q_ref[:, g * LANES:(g + 1) * LANES])
        qs_ref[ch, 0:tq, 0:LANES] = lo
        qs_ref[ch, tq:, 0:LANES] = hi

    def step(j, first, width=1):
        wk = width * tq
        rows = pl.ds(pl.multiple_of(j * tq, tq), wk)
        blocks = [j + d for d in range(width)]
        ones = jnp.ones((wk, LANES), BF16)
        aug_a = jnp.concatenate([jnp.concatenate([aug_ref[0, jb] for jb in blocks], axis=1),
                                 jnp.zeros((LANES - AUG_ROWS, wk), BF16)], axis=0)
        aug_b = _alibi_rows((j * tq + lax.broadcasted_iota(jnp.int32, (1, wk), 1)).astype(F32))
        for ch in range(n_chain):
            if ch < N_PAIRS:
                kt_ref, v_ref, g, aug, tbl = kat_ref, va_ref, ch, aug_a, 0
            else:
                kt_ref, v_ref, g, aug, tbl = kbt_ref, vb_ref, ch - N_PAIRS, aug_b, 1 + ch - N_PAIRS
            kt = jnp.concatenate([kt_ref[0, jb, g * LANES:(g + 1) * LANES, :] for jb in blocks], axis=1)
            s = jnp.dot(qs_ref[ch], jnp.concatenate([kt, aug], axis=0), preferred_element_type=F32)
            v = v_ref[rows, g * LANES:(g + 1) * LANES]
            if first:
                s = s + tbl_ref[tbl]
                m_new = jnp.broadcast_to(jnp.max(s, axis=1, keepdims=True), (2 * tq, LANES))
            else:
                m_prev = m_ref[ch]
                m_new = jnp.maximum(m_prev, jnp.max(s, axis=1, keepdims=True))
            p = jnp.exp2(s - jnp.concatenate([m_new] * (wk // LANES), axis=1))
            pv = jnp.dot(p.astype(BF16), jnp.concatenate([v, ones], axis=1), preferred_element_type=F32)
            if first:
                acc_ref[ch] = pv
            else:
                alpha = jnp.exp2(m_prev - m_new)
                acc_ref[ch] = jnp.concatenate([alpha, alpha], axis=1) * acc_ref[ch] + pv
            m_ref[ch] = m_new

    def wide_body(jj, carry):
        step(jj * KEY_BLOCKS_PER_STEP, False, KEY_BLOCKS_PER_STEP)
        return carry

    def single_body(j, carry):
        step(j, False)
        return carry

    step(i, True)
    n_wide = i // KEY_BLOCKS_PER_STEP
    lax.fori_loop(0, n_wide, wide_body, 0)
    lax.fori_loop(n_wide * KEY_BLOCKS_PER_STEP, i, single_body, 0)

    lane = lax.broadcasted_iota(jnp.int32, (tq, LANES), 1)
    lam = _lambda(lq1_ref, lk1_ref, lq2_ref, lk2_ref, lambda_init)
    for ch in range(n_chain):
        o = acc_ref[ch, :, 0:LANES] / acc_ref[ch, :, LANES:]
        if ch < N_PAIRS:
            oa_ref[:, ch * LANES:(ch + 1) * LANES] = jnp.where(lane < HD_A, o[:tq], o[tq:]).astype(BF16)
        else:
            h = ch - N_PAIRS
            ob_ref[:, h * LANES:(h + 1) * LANES] = _diff_finish(
                o[:tq], o[tq:], lam, sub_ref, lambda_init).astype(BF16)


def _attn_prompt(qa16, qb16, kat16, kbt16, va16, vb16, aug, lams, sub, b, t, lambda_init):
    tq = ATTN_BLOCK
    nq = t // tq
    qspec = pl.BlockSpec((tq, D_A), lambda bi, i: (bi * nq + i, 0))
    kspec = pl.BlockSpec((1, nq, D_A, tq), lambda bi, i: (bi, 0, 0, 0))
    vspec = pl.BlockSpec((t, D_A), lambda bi, i: (bi, 0))
    small = [_resident((1, HD_B))] * 4 + [_resident((1, 2 * HD_B))]
    n_chain = N_PAIRS + N_B
    return pl.pallas_call(
        functools.partial(_attn_prompt_kernel, lambda_init=lambda_init),
        grid=(b, nq),
        in_specs=[qspec, qspec, kspec, kspec, vspec, vspec,
                  pl.BlockSpec((1, nq, AUG_ROWS, tq), lambda bi, i: (bi, 0, 0, 0))] + small,
        out_specs=[qspec, qspec],
        out_shape=[jax.ShapeDtypeStruct((b * t, D_A), BF16), jax.ShapeDtypeStruct((b * t, D_B), BF16)],
        scratch_shapes=[pltpu.VMEM((n_chain, 2 * tq, 2 * LANES), BF16),
                        pltpu.VMEM((n_chain, 2 * tq, LANES), F32),
                        pltpu.VMEM((n_chain, 2 * tq, 2 * LANES), F32),
                        pltpu.VMEM((1 + N_B, 2 * tq, tq), F32)],
        compiler_params=_params(2),
        name="attn_prompt",
    )(qa16, qb16, kat16, kbt16, va16, vb16, aug, *lams, sub)


def _nt_dot(a, b):
    return lax.dot_general(a, b, (((1,), (1,)), ((), ())), preferred_element_type=F32)


def _attn_cached_kernel(qa_ref, qb_ref, ckat_ref, cvat_ref, ckbt_ref, cvb_ref,
                        ka_ref, va_ref, kb_ref, vb_ref, aug_ref,
                        lq1_ref, lk1_ref, lq2_ref, lk2_ref, sub_ref, oa_ref, ob_ref,
                        kn_ref, *, lambda_init):
    tq = qa_ref.shape[0]
    past = ckat_ref.shape[2]
    t_all = past + tq
    n_chain = N_PAIRS + N_B
    r = lax.broadcasted_iota(jnp.int32, (2 * tq, LANES), 0)
    c = lax.broadcasted_iota(jnp.int32, (2 * tq, LANES), 1)
    q_idx = jnp.where(r >= tq, r - tq, r)
    is_key = c < tq
    causal_tbl = jnp.where((q_idx >= c) & is_key, 0.0, NEG_INF)
    visible = (((past + q_idx) // CHUNK) >= ((past + c) // CHUNK)) & is_key
    lane = lax.broadcasted_iota(jnp.int32, (tq, LANES), 1)
    lam = _lambda(lq1_ref, lk1_ref, lq2_ref, lk2_ref, lambda_init)
    aug_c = jnp.concatenate([aug_ref[0, :, 0:past], jnp.zeros((LANES - AUG_ROWS, past), BF16)], axis=0)
    alibi_c = _alibi_rows(lax.broadcasted_iota(jnp.int32, (1, past), 1).astype(F32))
    alibi_n = _alibi_rows((past + lax.broadcasted_iota(jnp.int32, (1, LANES), 1)).astype(F32))
    kn_ref[...] = jnp.zeros(kn_ref.shape, BF16)
    row_pad = jnp.zeros((LANES - tq, LANES), BF16)

    def new_keys_t(k_new):
        return jnp.concatenate([k_new, row_pad], axis=0).astype(F32).T.astype(BF16)

    for ch in range(n_chain):
        fox = ch < N_PAIRS
        g = ch if fox else ch - N_PAIRS
        rows = slice(g * LANES, (g + 1) * LANES)
        if fox:
            lo, hi = _lane_halves(qa_ref[:, rows])
            sel_lo = _piece_selector((tq, LANES), 2 * g, N_A)
            sel_hi = _piece_selector((tq, LANES), 2 * g + 1, N_A)
            k_old = jnp.concatenate([ckat_ref[0, rows, :].astype(BF16), aug_c], axis=0)
            kn_ref[ch, 0:LANES, :] = new_keys_t(ka_ref[:, rows])
            kn_ref[ch, LANES:LANES + AUG_ROWS, 0:tq] = aug_ref[0, :, past:t_all]
            tbl = causal_tbl
        else:
            lo, hi = _lane_halves(qb_ref[:, rows])
            sel_lo = sel_hi = _piece_selector((tq, LANES), g, N_B)
            k_old = jnp.concatenate([ckbt_ref[0, rows, :].astype(BF16), alibi_c], axis=0)
            kn_ref[ch, 0:LANES, :] = new_keys_t(kb_ref[:, rows])
            kn_ref[ch, LANES:, :] = alibi_n
            tbl = jnp.where(visible, _alibi_fold(ALIBI_SLOPES[g], q_idx, c), NEG_INF)
        qs = jnp.concatenate([jnp.concatenate([lo, sel_lo], axis=1), jnp.concatenate([hi, sel_hi], axis=1)], axis=0)
        s_old = jnp.dot(qs, k_old, preferred_element_type=F32)
        s_new = jnp.dot(qs, kn_ref[ch], preferred_element_type=F32) + tbl
        m = jnp.maximum(jnp.max(s_old, axis=1, keepdims=True), jnp.max(s_new, axis=1, keepdims=True))
        p_old = jnp.exp2(s_old - m)
        p_new = jnp.exp2(s_new - m)
        l = jnp.sum(p_old, axis=1, keepdims=True) + jnp.sum(p_new, axis=1, keepdims=True)
        v_new = jnp.concatenate([(va_ref if fox else vb_ref)[:, rows], row_pad], axis=0)
        o_new = jnp.dot(p_new.astype(BF16), v_new, preferred_element_type=F32)
        if fox:
            o = (_nt_dot(p_old.astype(BF16), cvat_ref[0, rows, :].astype(BF16)) + o_new) / l
            oa_ref[:, rows] = jnp.where(lane < HD_A, o[:tq], o[tq:]).astype(BF16)
        else:
            v_old = cvb_ref[0, pl.ds(g, past, stride=N_B), :].astype(BF16)
            o = (jnp.dot(p_old.astype(BF16), v_old, preferred_element_type=F32) + o_new) / l
            ob_ref[:, rows] = _diff_finish(o[:tq], o[tq:], lam, sub_ref, lambda_init).astype(BF16)


def _attn_cached(qa16, qb16, ckat, cvat, ckbt, cvb, ka16, va16, kb16, vb16, aug, lams, sub, b, tq, lambda_init):
    past = ckat.shape[2]
    t_all = past + tq
    n_chain = N_PAIRS + N_B
    qspec = pl.BlockSpec((tq, D_A), lambda bi: (bi, 0))
    old_t = pl.BlockSpec((1, D_A, past), lambda bi: (bi, 0, 0))
    small = [_resident((1, HD_B))] * 4 + [_resident((1, 2 * HD_B))]
    return pl.pallas_call(
        functools.partial(_attn_cached_kernel, lambda_init=lambda_init),
        grid=(b,),
        in_specs=[qspec, qspec, old_t, old_t, old_t,
                  pl.BlockSpec((1, past * N_B, LANES), lambda bi: (bi, 0, 0)),
                  qspec, qspec, qspec, qspec,
                  pl.BlockSpec((1, AUG_ROWS, t_all), lambda bi: (bi, 0, 0))] + small,
        out_specs=[qspec, qspec],
        out_shape=[jax.ShapeDtypeStruct((b * tq, D_A), BF16), jax.ShapeDtypeStruct((b * tq, D_B), BF16)],
        scratch_shapes=[pltpu.VMEM((n_chain, 2 * LANES, LANES), BF16)],
        compiler_params=_params(1),
        name="attn_cached",
    )(qa16, qb16, ckat, cvat, ckbt, cvb, ka16, va16, kb16, vb16, aug, *lams, sub)


def _tokens_minor(x):
    b, t = x.shape[:2]
    return jnp.moveaxis(x.reshape(b, t, -1), 1, 2)


def _tokens_major(xt, tail):
    b, _, t = xt.shape
    return jnp.moveaxis(xt, 2, 1).reshape(b, t, *tail)


def _layer(x, cache, w, lambda_init, final):
    b, t, _ = x.shape
    n = b * t
    cast = w["ffn16"] is None
    if cast:
        x1, w1a, w2a = _ffn1(x.reshape(n, D_MODEL), w["gn1"], w["w1a"], w["w2a"], cast=True)
    else:
        w1a, w2a, wo, w1b, w2b = w["ffn16"]
        x1 = _ffn1(x.reshape(n, D_MODEL), w["gn1"], w1a, w2a, cast=False)
    (k_a, v_a, lft, k_b, vb, qa16, ka16, va16, qb16, kb16, vb16) = _proj(
        x1, w["gmix"], w["wa"], w["wf"], w["bf"], w["wb"], b, t, tokens_minor=cache is None)
    if cache is None:
        aug = _cum_aug(lft, blocked=True)
        oa, ob = _attn_prompt(qa16, qb16, ka16, kb16, va16, vb16, aug, w["lams"], w["sub"], b, t, lambda_init)
        news = (_tokens_major(k_a, (N_A, HD_A)), _tokens_major(v_a, (N_A, HD_A)), _tokens_major(lft, (N_A,)),
                _tokens_major(k_b, (N_B, 2, HD_B)), vb.reshape(b, t, N_B, 2 * HD_B))
    else:
        cfk, cfv, cflf, cdk, cdv = cache
        past = cfk.shape[1]
        aug = _cum_aug(jnp.concatenate([_tokens_minor(cflf.astype(F32)), lft], axis=2), blocked=False)
        oa, ob = _attn_cached(qa16, qb16, _tokens_minor(cfk), _tokens_minor(cfv), _tokens_minor(cdk),
                              cdv.reshape(b, past * N_B, 2 * HD_B), ka16, va16, kb16, vb16, aug,
                              w["lams"], w["sub"], b, t, lambda_init)
        news = (k_a.reshape(b, t, N_A, HD_A), v_a.reshape(b, t, N_A, HD_A), _tokens_major(lft, (N_A,)),
                k_b.reshape(b, t, N_B, 2, HD_B), vb.reshape(b, t, N_B, 2 * HD_B))
    if cast:
        y, wo, w1b, w2b = _post(x1, oa, ob, w["gn2"], w["gfin"], w["wo"], w["w1b"], w["w2b"], final, cast=True)
        w["ffn16"] = (w1a, w2a, wo, w1b, w2b)
    else:
        y = _post(x1, oa, ob, w["gn2"], w["gfin"], wo, w1b, w2b, final, cast=False)
    return y.reshape(b, t, D_MODEL), news


def kernel(x_prompt, x_sample, cache_fox_k, cache_fox_v, cache_fox_logf, cache_diff_k, cache_diff_v,
           norm_ffn1, w_ffn1_in, w_ffn1_out, norm_mix, w_in, b_forget,
           lambda_q1, lambda_k1, lambda_q2, lambda_k2, diff_subln, w_out,
           norm_ffn2, w_ffn2_in, w_ffn2_out, norm_final):
    depth = w_in.shape[0]
    xp, xs = x_prompt, x_sample
    outs_p, outs_s = [], []
    gfin = norm_final.reshape(1, D_MODEL)
    for l in range(depth):
        lambda_init = 0.8 - 0.6 * math.exp(-0.3 * l)
        f_lo, f_hi = 3 * D_A, 3 * D_A + N_A
        w = {
            "gn1": norm_ffn1[l].reshape(1, D_MODEL),
            "w1a": w_ffn1_in[l], "w2a": w_ffn1_out[l],
            "gmix": norm_mix[l].reshape(1, D_MODEL),
            "wa": w_in[l][:, :f_lo].astype(BF16),
            "wf": jnp.pad(w_in[l][:, f_lo:f_hi], ((0, 0), (0, LANES - N_A))).astype(BF16),
            "bf": jnp.pad(b_forget[l], (0, LANES - N_A)).reshape(1, LANES),
            "wb": w_in[l][:, f_hi:].astype(BF16),
            "lams": [v[l].reshape(1, HD_B) for v in (lambda_q1, lambda_k1, lambda_q2, lambda_k2)],
            "sub": diff_subln[l].reshape(1, 2 * HD_B),
            "wo": w_out[l],
            "gn2": norm_ffn2[l].reshape(1, D_MODEL),
            "w1b": w_ffn2_in[l], "w2b": w_ffn2_out[l],
            "gfin": gfin,
            "ffn16": None,
        }
        streams = []
        for x, cache in ((xp, None),
                         (xs, (cache_fox_k[l], cache_fox_v[l], cache_fox_logf[l],
                               cache_diff_k[l], cache_diff_v[l]))):
            streams.append(_layer(x, cache, w, lambda_init, l == depth - 1))
        (xp, news_p), (xs, news_s) = streams
        outs_p.append(news_p)
        outs_s.append(news_s)
    stack = lambda outs, i: jnp.stack([o[i] for o in outs])
    return (xp, xs) + tuple(stack(outs_p, i) for i in range(5)) + tuple(stack(outs_s, i) for i in range(5))
```

```python
import functools
import math

import jax
import jax.numpy as jnp
from jax import lax
from jax.experimental import pallas as pl
from jax.experimental.pallas import tpu as pltpu

D_MODEL = 1024
D_FF = 2816
N_A = 8
HD_A = 64
N_B = 4
HD_B = 64
D_A = N_A * HD_A
D_B = N_B * 2 * HD_B
CHUNK = 64
NORM_EPS = 1e-6
NEG_INF = -1e30
ALIBI_SLOPES = (0.25, 0.0625, 0.015625, 0.00390625)

LOG2E = math.log2(math.e)
LANES = 128
MXU_WIDTH = 256
FF_CHUNKS = (1536, 1280)
assert sum(FF_CHUNKS) == D_FF and all(c % MXU_WIDTH == 0 for c in FF_CHUNKS)
SCAN_BATCH = 8
CAST_BYTES = 3 << 20
TOKEN_TILE = 512
ATTN_BLOCK = 256
KEY_BLOCKS_PER_STEP = 2
AUG_ROWS = 32
VMEM_LIMIT = 56 * 1024 * 1024
N_PAIRS = N_A // 2

F32 = jnp.float32
BF16 = jnp.bfloat16


def _params(n_axes):
    return pltpu.CompilerParams(dimension_semantics=("arbitrary",) * n_axes,
                                vmem_limit_bytes=VMEM_LIMIT)


def _resident(shape):
    return pl.BlockSpec(shape, lambda *_: (0,) * len(shape), pipeline_mode=pl.Buffered(1))


def _rms(xf, g):
    ms = jnp.mean(xf * xf, axis=-1, keepdims=True)
    return xf * lax.rsqrt(ms + NORM_EPS) * g


def _swiglu_residual(x, gn_ref, w1_ref, w2_ref):
    h = _rms(x, gn_ref[...]).astype(BF16)
    acc = None
    lo = 0
    for width in FF_CHUNKS:
        g = jnp.dot(h, w1_ref[:, lo:lo + width], preferred_element_type=F32)
        u = jnp.dot(h, w1_ref[:, D_FF + lo:D_FF + lo + width], preferred_element_type=F32)
        a = (g * jax.nn.sigmoid(g) * u).astype(BF16)
        d = jnp.dot(a, w2_ref[lo:lo + width, :], preferred_element_type=F32)
        acc = d if acc is None else acc + d
        lo += width
    return x + 0.5 * acc


def _cast_rows(rows_total, cols):
    rows = max(16, CAST_BYTES // (4 * cols) // 16 * 16)
    while rows_total % rows:
        rows -= 16
    return rows


def _stage_shape(shapes):
    return {c: (2, max(_cast_rows(r, cc) for r, cc in shapes if cc == c), c) for _, c in shapes}


def _cast_weight(w_hbm, w16_ref, stage_ref, sem_ref):
    r_total, cols = w_hbm.shape
    rows = _cast_rows(r_total, cols)
    n = r_total // rows

    def copy(c):
        return pltpu.make_async_copy(w_hbm.at[pl.ds(c * rows, rows), :],
                                     stage_ref.at[c % 2, pl.ds(0, rows), :], sem_ref.at[c % 2])

    copy(0).start()
    for c in range(n):
        if c + 1 < n:
            copy(c + 1).start()
        copy(c).wait()
        w16_ref[pl.ds(c * rows, rows), :] = stage_ref[c % 2, 0:rows, :].astype(BF16)


def _weights_prologue(i, w_hbm_refs, w16_refs, w16_hbm_refs, stages, sem_in, sem_out):
    def publish(k):
        return pltpu.make_async_copy(w16_refs[k], w16_hbm_refs[k], sem_out.at[k])

    @pl.when(i == 0)
    def _():
        for w_hbm, w16 in zip(w_hbm_refs, w16_refs):
            _cast_weight(w_hbm, w16, stages[w_hbm.shape[1]], sem_in)
        for k in range(len(w16_refs)):
            publish(k).start()

    return publish


def _weights_epilogue(i, publish, n_weights):
    @pl.when(i == pl.num_programs(0) - 1)
    def _():
        for k in range(n_weights):
            publish(k).wait()


def _weight_plumbing(weights, cast):
    if not cast:
        return [_resident(w.shape) for w in weights], [], [], []
    any_spec = pl.BlockSpec(memory_space=pl.ANY)
    shapes = [w.shape for w in weights]
    stages = _stage_shape(shapes)
    scratch = ([pltpu.VMEM(sh, BF16) for sh in shapes] + [pltpu.VMEM(stages[c], F32) for c in sorted(stages)]
               + [pltpu.SemaphoreType.DMA((2,)), pltpu.SemaphoreType.DMA((len(weights),))])
    return ([any_spec] * len(weights), [any_spec] * len(weights),
            [jax.ShapeDtypeStruct(sh, BF16) for sh in shapes], scratch)


def _split_weight_refs(refs, n_weights, shapes):
    w16 = refs[:n_weights]
    cols = sorted({c for _, c in shapes})
    stage_refs = refs[n_weights:n_weights + len(cols)]
    sem_in, sem_out = refs[n_weights + len(cols):]
    return w16, dict(zip(cols, stage_refs)), sem_in, sem_out


def _ffn1_kernel(x_ref, gn_ref, w1_in, w2_in, o_ref, *rest, cast):
    if cast:
        i = pl.program_id(0)
        w_hbm = (w1_in, w2_in)
        w16, stages, sem_in, sem_out = _split_weight_refs(rest[2:], 2, [w.shape for w in w_hbm])
        publish = _weights_prologue(i, w_hbm, w16, rest[:2], stages, sem_in, sem_out)
        o_ref[...] = _swiglu_residual(x_ref[...], gn_ref, *w16)
        _weights_epilogue(i, publish, 2)
    else:
        o_ref[...] = _swiglu_residual(x_ref[...], gn_ref, w1_in, w2_in)


def _ffn1(x, gn, w1, w2, cast):
    n = x.shape[0]
    tm = min(TOKEN_TILE, n)
    w_specs, w_out_specs, w_out_shapes, scratch = _weight_plumbing([w1, w2], cast)
    out = pl.pallas_call(
        functools.partial(_ffn1_kernel, cast=cast),
        grid=(n // tm,),
        in_specs=[pl.BlockSpec((tm, D_MODEL), lambda i: (i, 0)), _resident((1, D_MODEL))] + w_specs,
        out_specs=[pl.BlockSpec((tm, D_MODEL), lambda i: (i, 0))] + w_out_specs,
        out_shape=[jax.ShapeDtypeStruct((n, D_MODEL), F32)] + w_out_shapes,
        scratch_shapes=scratch,
        compiler_params=_params(1),
        name="ffn1",
    )(x, gn, w1, w2)
    return out if cast else out[0]


def _post_kernel(x_ref, oa_ref, ob_ref, gn_ref, gf_ref, wo_in, w1_in, w2_in, y_ref, *rest, final, cast):
    if cast:
        i = pl.program_id(0)
        w_hbm = (wo_in, w1_in, w2_in)
        w16, stages, sem_in, sem_out = _split_weight_refs(rest[3:], 3, [w.shape for w in w_hbm])
        publish = _weights_prologue(i, w_hbm, w16, rest[:3], stages, sem_in, sem_out)
        wo_ref, w1_ref, w2_ref = w16
    else:
        wo_ref, w1_ref, w2_ref = wo_in, w1_in, w2_in
    x = x_ref[...]
    x = x + jnp.dot(oa_ref[...], wo_ref[:D_A, :], preferred_element_type=F32)
    x = x + jnp.dot(ob_ref[...], wo_ref[D_A:, :], preferred_element_type=F32)
    x = _swiglu_residual(x, gn_ref, w1_ref, w2_ref)
    y_ref[...] = _rms(x, gf_ref[...]) if final else x
    if cast:
        _weights_epilogue(i, publish, 3)


def _post(x, oa, ob, gn, gf, wo, w1, w2, final, cast):
    n = x.shape[0]
    tm = min(TOKEN_TILE, n)
    row = lambda i: (i, 0)
    w_specs, w_out_specs, w_out_shapes, scratch = _weight_plumbing([wo, w1, w2], cast)
    out = pl.pallas_call(
        functools.partial(_post_kernel, final=final, cast=cast),
        grid=(n // tm,),
        in_specs=[pl.BlockSpec((tm, D_MODEL), row), pl.BlockSpec((tm, D_A), row), pl.BlockSpec((tm, D_B), row),
                  _resident((1, D_MODEL)), _resident((1, D_MODEL))] + w_specs,
        out_specs=[pl.BlockSpec((tm, D_MODEL), row)] + w_out_specs,
        out_shape=[jax.ShapeDtypeStruct((n, D_MODEL), F32)] + w_out_shapes,
        scratch_shapes=scratch,
        compiler_params=_params(1),
        name="post",
    )(x, oa, ob, gn, gf, wo, w1, w2)
    return out if cast else out[0]


def _proj_kernel(x_ref, gn_ref, wa_ref, wf_ref, bf_ref, wb_ref, *out_refs, seg, kblk, tokens_minor):
    tm = x_ref.shape[0]
    h = _rms(x_ref[...], gn_ref[...]).astype(BF16)
    pa = jnp.dot(h, wa_ref[...], preferred_element_type=F32)
    pb = jnp.dot(h, wb_ref[...], preferred_element_type=F32)
    z = jnp.dot(h, wf_ref[...], preferred_element_type=F32) + bf_ref[...]
    lft = (jnp.minimum(z, 0.0) - jnp.log1p(jnp.exp(-jnp.abs(z)))).T
    ka, va = pa[:, D_A:2 * D_A], pa[:, 2 * D_A:]
    kb, vb = pb[:, D_B:2 * D_B], pb[:, 2 * D_B:]
    qa16 = (pa[:, :D_A] * (HD_A ** -0.5 * LOG2E)).astype(BF16)
    qb16 = (pb[:, :D_B] * (HD_B ** -0.5 * LOG2E)).astype(BF16)
    if tokens_minor:
        kat_ref, vat_ref, lft_ref, kbt_ref, vb_ref, qa16_ref, kat16_ref, va16_ref, qb16_ref, kbt16_ref, vb16_ref = out_refs
        kat, vat, kbt = ka.T, va.T, kb.T
    else:
        ka_ref, va_ref, lft_ref, kb_ref, vb_ref, qa16_ref, ka16_ref, va16_ref, qb16_ref, kb16_ref, vb16_ref = out_refs
        ka16_ref[...] = ka.astype(BF16)
        kb16_ref[...] = kb.astype(BF16)
        for hh in range(N_A):
            cols = slice(hh * HD_A, (hh + 1) * HD_A)
            ka_ref[pl.ds(hh, tm, stride=N_A), :] = ka[:, cols]
            va_ref[pl.ds(hh, tm, stride=N_A), :] = va[:, cols]
            kb_ref[pl.ds(hh, tm, stride=N_A), :] = kb[:, cols]
    qa16_ref[...] = qa16
    qb16_ref[...] = qb16
    va16_ref[...] = va.astype(BF16)
    vb16_ref[...] = vb.astype(BF16)
    for hh in range(N_B):
        vb_ref[pl.ds(hh, tm, stride=N_B), :] = vb[:, hh * LANES:(hh + 1) * LANES]
    for bb in range(tm // seg):
        cols = slice(bb * seg, (bb + 1) * seg)
        lft_ref[bb] = lft[:N_A, cols]
        if tokens_minor:
            kat_ref[bb] = kat[:, cols]
            vat_ref[bb] = vat[:, cols]
            kbt_ref[bb] = kbt[:, cols]
            for jj in range(seg // kblk):
                kc = slice(bb * seg + jj * kblk, bb * seg + (jj + 1) * kblk)
                kat16_ref[bb, jj] = kat[:, kc].astype(BF16)
                kbt16_ref[bb, jj] = kbt[:, kc].astype(BF16)


def _proj(x, gn, wa, wf, bf, wb, b, t, tokens_minor):
    n = x.shape[0]
    tm = min(TOKEN_TILE, n)
    seg = min(tm, t)
    nb = tm // seg
    nt = t // seg
    kblk = min(ATTN_BLOCK, seg)
    row = lambda i: (i, 0)
    tok = lambda i: (i // nt, 0, i % nt)
    blk = lambda i: (i // nt, i % nt, 0, 0)
    b16 = (jax.ShapeDtypeStruct((n, D_A), BF16), pl.BlockSpec((tm, D_A), row))
    lft = (jax.ShapeDtypeStruct((b, N_A, t), F32), pl.BlockSpec((nb, N_A, seg), tok))
    vb = (jax.ShapeDtypeStruct((n * N_B, LANES), F32), pl.BlockSpec((tm * N_B, LANES), row))
    if tokens_minor:
        kv = (jax.ShapeDtypeStruct((b, D_A, t), F32), pl.BlockSpec((nb, D_A, seg), tok))
        k16 = (jax.ShapeDtypeStruct((b, t // kblk, D_A, kblk), BF16),
               pl.BlockSpec((nb, seg // kblk, D_A, kblk), blk))
    else:
        kv = (jax.ShapeDtypeStruct((n * N_A, HD_A), F32), pl.BlockSpec((tm * N_A, HD_A), row))
        k16 = b16
    outs = [kv, kv, lft, kv, vb, b16, k16, b16, b16, k16, b16]
    return pl.pallas_call(
        functools.partial(_proj_kernel, seg=seg, kblk=kblk, tokens_minor=tokens_minor),
        grid=(n // tm,),
        in_specs=[pl.BlockSpec((tm, D_MODEL), row), _resident((1, D_MODEL)), _resident(wa.shape),
                  _resident(wf.shape), _resident(bf.shape), _resident(wb.shape)],
        out_specs=[o[1] for o in outs],
        out_shape=[o[0] for o in outs],
        compiler_params=_params(1),
        name="proj",
    )(x, gn, wa, wf, bf, wb)


def _split3(x):
    hi = x.astype(BF16).astype(F32)
    r = x - hi
    mid = r.astype(BF16).astype(F32)
    lo = (r - mid).astype(BF16).astype(F32)
    return hi, mid, lo


def _cum_kernel(lft_ref, aug_ref, pad_ref, *, t, blocked):
    w = ATTN_BLOCK
    nb = lft_ref.shape[0]
    rows = nb * N_A
    pad_ref[...] = jnp.zeros(pad_ref.shape, F32)
    for bb in range(nb):
        pad_ref[bb * N_A:(bb + 1) * N_A, 0:t] = lft_ref[bb]
    r = lax.broadcasted_iota(jnp.int32, (w, w), 0)
    c = lax.broadcasted_iota(jnp.int32, (w, w), 1)
    tri = jnp.where(r <= c, 1.0, 0.0).astype(BF16)
    zero = jnp.zeros((N_A, w), F32)
    carry = jnp.zeros((rows, 1), F32)
    for s in range(pad_ref.shape[1] // w):
        lo_col = s * w
        cols = min(w, t - lo_col)
        x = jnp.concatenate(_split3(pad_ref[:, lo_col:lo_col + w]), axis=0).astype(BF16)
        y = jnp.dot(x, tri, preferred_element_type=F32)
        cs = y[0:rows] + y[rows:2 * rows] + y[2 * rows:3 * rows] + carry
        carry = cs[:, w - 1:w]
        pieces = _split3(cs * -LOG2E)
        for bb in range(nb):
            sl = slice(bb * N_A, (bb + 1) * N_A)
            aug = jnp.concatenate([p[sl] for p in pieces] + [zero], axis=0).astype(BF16)
            if blocked:
                aug_ref[bb, s] = aug
            else:
                aug_ref[bb, :, lo_col:lo_col + cols] = aug[:, :cols]


def _cum_aug(lft, blocked):
    b, _, t = lft.shape
    w = ATTN_BLOCK
    t_pad = -(-t // w) * w
    nb = math.gcd(b, SCAN_BATCH)
    if blocked:
        out_shape = jax.ShapeDtypeStruct((b, t // w, AUG_ROWS, w), BF16)
        out_spec = pl.BlockSpec((nb, t // w, AUG_ROWS, w), lambda i: (i, 0, 0, 0))
    else:
        out_shape = jax.ShapeDtypeStruct((b, AUG_ROWS, t), BF16)
        out_spec = pl.BlockSpec((nb, AUG_ROWS, t), lambda i: (i, 0, 0))
    return pl.pallas_call(
        functools.partial(_cum_kernel, t=t, blocked=blocked),
        grid=(b // nb,),
        in_specs=[pl.BlockSpec((nb, N_A, t), lambda i: (i, 0, 0))],
        out_specs=out_spec,
        out_shape=out_shape,
        scratch_shapes=[pltpu.VMEM((nb * N_A, t_pad), F32)],
        compiler_params=_params(1),
        name="cum_aug",
    )(lft)


def _lane_halves(q):
    lane = lax.broadcasted_iota(jnp.int32, q.shape, 1)
    zero = jnp.zeros_like(q)
    return jnp.where(lane < HD_A, q, zero), jnp.where(lane >= HD_A, q, zero)


def _piece_selector(shape, head, n_heads):
    lane = lax.broadcasted_iota(jnp.int32, shape, 1)
    hit = (lane == head) | (lane == n_heads + head) | (lane == 2 * n_heads + head)
    return jnp.where(hit, 1.0, 0.0).astype(BF16)


def _alibi_rows(k_pos):
    w = k_pos.shape[1]
    row = lax.broadcasted_iota(jnp.int32, (4 * N_B, 1), 0)
    head = row % N_B
    slope = jnp.zeros((4 * N_B, 1), F32)
    for h in range(N_B):
        slope = jnp.where(head == h, ALIBI_SLOPES[h] * LOG2E, slope)
    hi, mid, lo = _split3(slope * k_pos)
    piece = jnp.where(row < N_B, hi, jnp.where(row < 2 * N_B, mid, jnp.where(row < 3 * N_B, lo, 0.0)))
    return jnp.concatenate([piece.astype(BF16), jnp.zeros((LANES - 4 * N_B, w), BF16)], axis=0)


def _alibi_fold(slope, q_pos, k_pos):
    return (-2.0 * LOG2E * slope) * jnp.maximum(k_pos - q_pos, 0).astype(F32)


def _diff_finish(o1, o2, lam, sub_ref, lambda_init):
    return _rms(o1 - lam * o2, sub_ref[...]) * (1.0 - lambda_init)


def _lambda(lq1_ref, lk1_ref, lq2_ref, lk2_ref, lambda_init):
    return (jnp.exp(jnp.sum(lq1_ref[...] * lk1_ref[...], axis=1, keepdims=True))
            - jnp.exp(jnp.sum(lq2_ref[...] * lk2_ref[...], axis=1, keepdims=True)) + lambda_init)


def _attn_prompt_kernel(qa_ref, qb_ref, kat_ref, kbt_ref, va_ref, vb_ref, aug_ref,
                        lq1_ref, lk1_ref, lq2_ref, lk2_ref, sub_ref, oa_ref, ob_ref,
                        qs_ref, m_ref, acc_ref, tbl_ref, *, lambda_init):
    bi, i = pl.program_id(0), pl.program_id(1)
    tq = qa_ref.shape[0]
    n_chain = N_PAIRS + N_B

    @pl.when((bi == 0) & (i == 0))
    def _():
        r = lax.broadcasted_iota(jnp.int32, (2 * tq, tq), 0)
        c = lax.broadcasted_iota(jnp.int32, (2 * tq, tq), 1)
        q_idx = jnp.where(r >= tq, r - tq, r)
        tbl_ref[0] = jnp.where(q_idx >= c, 0.0, NEG_INF)
        visible = (q_idx // CHUNK) >= (c // CHUNK)
        for h in range(N_B):
            tbl_ref[1 + h] = jnp.where(visible, _alibi_fold(ALIBI_SLOPES[h], q_idx, c), NEG_INF)
        for ch in range(n_chain):
            if ch < N_PAIRS:
                qs_ref[ch, 0:tq, LANES:] = _piece_selector((tq, LANES), 2 * ch, N_A)
                qs_ref[ch, tq:, LANES:] = _piece_selector((tq, LANES), 2 * ch + 1, N_A)
            else:
                sel = _piece_selector((tq, LANES), ch - N_PAIRS, N_B)
                qs_ref[ch, 0:tq, LANES:] = sel
                qs_ref[ch, tq:, LANES:] = sel

    for ch in range(n_chain):
        q_ref, g = (qa_ref, ch) if ch < N_PAIRS else (qb_ref, ch - N_PAIRS)
        lo, hi = _lane_halves(q_ref[:, g * LANES:(g + 1) * LANES])
        qs_ref[ch, 0:tq, 0:LANES] = lo
        qs_ref[ch, tq:, 0:LANES] = hi

    def step(j, first, width=1):
        wk = width * tq
        rows = pl.ds(pl.multiple_of(j * tq, tq), wk)
        blocks = [j + d for d in range(width)]
        ones = jnp.ones((wk, LANES), BF16)
        aug_a = jnp.concatenate([jnp.concatenate([aug_ref[0, jb] for jb in blocks], axis=1),
                                 jnp.zeros((LANES - AUG_ROWS, wk), BF16)], axis=0)
        aug_b = _alibi_rows((j * tq + lax.broadcasted_iota(jnp.int32, (1, wk), 1)).astype(F32))
        for ch in range(n_chain):
            if ch < N_PAIRS:
                kt_ref, v_ref, g, aug, tbl = kat_ref, va_ref, ch, aug_a, 0
            else:
                kt_ref, v_ref, g, aug, tbl = kbt_ref, vb_ref, ch - N_PAIRS, aug_b, 1 + ch - N_PAIRS
            kt = jnp.concatenate([kt_ref[0, jb, g * LANES:(g + 1) * LANES, :] for jb in blocks], axis=1)
            s = jnp.dot(qs_ref[ch], jnp.concatenate([kt, aug], axis=0), preferred_element_type=F32)
            v = v_ref[rows, g * LANES:(g + 1) * LANES]
            if first:
                s = s + tbl_ref[tbl]
                m_new = jnp.broadcast_to(jnp.max(s, axis=1, keepdims=True), (2 * tq, LANES))
            else:
                m_prev = m_ref[ch]
                m_new = jnp.maximum(m_prev, jnp.max(s, axis=1, keepdims=True))
            p = jnp.exp2(s - jnp.concatenate([m_new] * (wk // LANES), axis=1))
            pv = jnp.dot(p.astype(BF16), jnp.concatenate([v, ones], axis=1), preferred_element_type=F32)
            if first:
                acc_ref[ch] = pv
            else:
                alpha = jnp.exp2(m_prev - m_new)
                acc_ref[ch] = jnp.concatenate([alpha, alpha], axis=1) * acc_ref[ch] + pv
            m_ref[ch] = m_new

    def wide_body(jj, carry):
        step(jj * KEY_BLOCKS_PER_STEP, False, KEY_BLOCKS_PER_STEP)
        return carry

    def single_body(j, carry):
        step(j, False)
        return carry

    step(i, True)
    n_wide = i // KEY_BLOCKS_PER_STEP
    lax.fori_loop(0, n_wide, wide_body, 0)
    lax.fori_loop(n_wide * KEY_BLOCKS_PER_STEP, i, single_body, 0)

    lane = lax.broadcasted_iota(jnp.int32, (tq, LANES), 1)
    lam = _lambda(lq1_ref, lk1_ref, lq2_ref, lk2_ref, lambda_init)
    for ch in range(n_chain):
        o = acc_ref[ch, :, 0:LANES] / acc_ref[ch, :, LANES:]
        if ch < N_PAIRS:
            oa_ref[:, ch * LANES:(ch + 1) * LANES] = jnp.where(lane < HD_A, o[:tq], o[tq:]).astype(BF16)
        else:
            h = ch - N_PAIRS
            ob_ref[:, h * LANES:(h + 1) * LANES] = _diff_finish(
                o[:tq], o[tq:], lam, sub_ref, lambda_init).astype(BF16)


def _attn_prompt(qa16, qb16, kat16, kbt16, va16, vb16, aug, lams, sub, b, t, lambda_init):
    tq = ATTN_BLOCK
    nq = t // tq
    qspec = pl.BlockSpec((tq, D_A), lambda bi, i: (bi * nq + i, 0))
    kspec = pl.BlockSpec((1, nq, D_A, tq), lambda bi, i: (bi, 0, 0, 0))
    vspec = pl.BlockSpec((t, D_A), lambda bi, i: (bi, 0))
    small = [_resident((1, HD_B))] * 4 + [_resident((1, 2 * HD_B))]
    n_chain = N_PAIRS + N_B
    return pl.pallas_call(
        functools.partial(_attn_prompt_kernel, lambda_init=lambda_init),
        grid=(b, nq),
        in_specs=[qspec, qspec, kspec, kspec, vspec, vspec,
                  pl.BlockSpec((1, nq, AUG_ROWS, tq), lambda bi, i: (bi, 0, 0, 0))] + small,
        out_specs=[qspec, qspec],
        out_shape=[jax.ShapeDtypeStruct((b * t, D_A), BF16), jax.ShapeDtypeStruct((b * t, D_B), BF16)],
        scratch_shapes=[pltpu.VMEM((n_chain, 2 * tq, 2 * LANES), BF16),
                        pltpu.VMEM((n_chain, 2 * tq, LANES), F32),
                        pltpu.VMEM((n_chain, 2 * tq, 2 * LANES), F32),
                        pltpu.VMEM((1 + N_B, 2 * tq, tq), F32)],
        compiler_params=_params(2),
        name="attn_prompt",
    )(qa16, qb16, kat16, kbt16, va16, vb16, aug, *lams, sub)


def _nt_dot(a, b):
    return lax.dot_general(a, b, (((1,), (1,)), ((), ())), preferred_element_type=F32)


def _attn_cached_kernel(qa_ref, qb_ref, ckat_ref, cvat_ref, ckbt_ref, cvb_ref,
                        ka_ref, va_ref, kb_ref, vb_ref, aug_ref,
                        lq1_ref, lk1_ref, lq2_ref, lk2_ref, sub_ref, oa_ref, ob_ref,
                        kn_ref, *, lambda_init):
    tq = qa_ref.shape[0]
    past = ckat_ref.shape[2]
    t_all = past + tq
    n_chain = N_PAIRS + N_B
    r = lax.broadcasted_iota(jnp.int32, (2 * tq, LANES), 0)
    c = lax.broadcasted_iota(jnp.int32, (2 * tq, LANES), 1)
    q_idx = jnp.where(r >= tq, r - tq, r)
    is_key = c < tq
    causal_tbl = jnp.where((q_idx >= c) & is_key, 0.0, NEG_INF)
    visible = (((past + q_idx) // CHUNK) >= ((past + c) // CHUNK)) & is_key
    lane = lax.broadcasted_iota(jnp.int32, (tq, LANES), 1)
    lam = _lambda(lq1_ref, lk1_ref, lq2_ref, lk2_ref, lambda_init)
    aug_c = jnp.concatenate([aug_ref[0, :, 0:past], jnp.zeros((LANES - AUG_ROWS, past), BF16)], axis=0)
    alibi_c = _alibi_rows(lax.broadcasted_iota(jnp.int32, (1, past), 1).astype(F32))
    alibi_n = _alibi_rows((past + lax.broadcasted_iota(jnp.int32, (1, LANES), 1)).astype(F32))
    kn_ref[...] = jnp.zeros(kn_ref.shape, BF16)
    row_pad = jnp.zeros((LANES - tq, LANES), BF16)

    def new_keys_t(k_new):
        return jnp.concatenate([k_new, row_pad], axis=0).astype(F32).T.astype(BF16)

    for ch in range(n_chain):
        fox = ch < N_PAIRS
        g = ch if fox else ch - N_PAIRS
        rows = slice(g * LANES, (g + 1) * LANES)
        if fox:
            lo, hi = _lane_halves(qa_ref[:, rows])
            sel_lo = _piece_selector((tq, LANES), 2 * g, N_A)
            sel_hi = _piece_selector((tq, LANES), 2 * g + 1, N_A)
            k_old = jnp.concatenate([ckat_ref[0, rows, :].astype(BF16), aug_c], axis=0)
            kn_ref[ch, 0:LANES, :] = new_keys_t(ka_ref[:, rows])
            kn_ref[ch, LANES:LANES + AUG_ROWS, 0:tq] = aug_ref[0, :, past:t_all]
            tbl = causal_tbl
        else:
            lo, hi = _lane_halves(qb_ref[:, rows])
            sel_lo = sel_hi = _piece_selector((tq, LANES), g, N_B)
            k_old = jnp.concatenate([ckbt_ref[0, rows, :].astype(BF16), alibi_c], axis=0)
            kn_ref[ch, 0:LANES, :] = new_keys_t(kb_ref[:, rows])
            kn_ref[ch, LANES:, :] = alibi_n
            tbl = jnp.where(visible, _alibi_fold(ALIBI_SLOPES[g], q_idx, c), NEG_INF)
        qs = jnp.concatenate([jnp.concatenate([lo, sel_lo], axis=1), jnp.concatenate([hi, sel_hi], axis=1)], axis=0)
        s_old = jnp.dot(qs, k_old, preferred_element_type=F32)
        s_new = jnp.dot(qs, kn_ref[ch], preferred_element_type=F32) + tbl
        m = jnp.maximum(jnp.max(s_old, axis=1, keepdims=True), jnp.max(s_new, axis=1, keepdims=True))
        p_old = jnp.exp2(s_old - m)
        p_new = jnp.exp2(s_new - m)
        l = jnp.sum(p_old, axis=1, keepdims=True) + jnp.sum(p_new, axis=1, keepdims=True)
        v_new = jnp.concatenate([(va_ref if fox else vb_ref)[:, rows], row_pad], axis=0)
        o_new = jnp.dot(p_new.astype(BF16), v_new, preferred_element_type=F32)
        if fox:
            o = (_nt_dot(p_old.astype(BF16), cvat_ref[0, rows, :].astype(BF16)) + o_new) / l
            oa_ref[:, rows] = jnp.where(lane < HD_A, o[:tq], o[tq:]).astype(BF16)
        else:
            v_old = cvb_ref[0, pl.ds(g, past, stride=N_B), :].astype(BF16)
            o = (jnp.dot(p_old.astype(BF16), v_old, preferred_element_type=F32) + o_new) / l
            ob_ref[:, rows] = _diff_finish(o[:tq], o[tq:], lam, sub_ref, lambda_init).astype(BF16)


def _attn_cached(qa16, qb16, ckat, cvat, ckbt, cvb, ka16, va16, kb16, vb16, aug, lams, sub, b, tq, lambda_init):
    past = ckat.shape[2]
    t_all = past + tq
    n_chain = N_PAIRS + N_B
    qspec = pl.BlockSpec((tq, D_A), lambda bi: (bi, 0))
    old_t = pl.BlockSpec((1, D_A, past), lambda bi: (bi, 0, 0))
    small = [_resident((1, HD_B))] * 4 + [_resident((1, 2 * HD_B))]
    return pl.pallas_call(
        functools.partial(_attn_cached_kernel, lambda_init=lambda_init),
        grid=(b,),
        in_specs=[qspec, qspec, old_t, old_t, old_t,
                  pl.BlockSpec((1, past * N_B, LANES), lambda bi: (bi, 0, 0)),
                  qspec, qspec, qspec, qspec,
                  pl.BlockSpec((1, AUG_ROWS, t_all), lambda bi: (bi, 0, 0))] + small,
        out_specs=[qspec, qspec],
        out_shape=[jax.ShapeDtypeStruct((b * tq, D_A), BF16), jax.ShapeDtypeStruct((b * tq, D_B), BF16)],
        scratch_shapes=[pltpu.VMEM((n_chain, 2 * LANES, LANES), BF16)],
        compiler_params=_params(1),
        name="attn_cached",
    )(qa16, qb16, ckat, cvat, ckbt, cvb, ka16, va16, kb16, vb16, aug, *lams, sub)


def _tokens_minor(x):
    b, t = x.shape[:2]
    return jnp.moveaxis(x.reshape(b, t, -1), 1, 2)


def _tokens_major(xt, tail):
    b, _, t = xt.shape
    return jnp.moveaxis(xt, 2, 1).reshape(b, t, *tail)


def _layer(x, cache, w, lambda_init, final):
    b, t, _ = x.shape
    n = b * t
    cast = w["ffn16"] is None
    if cast:
        x1, w1a, w2a = _ffn1(x.reshape(n, D_MODEL), w["gn1"], w["w1a"], w["w2a"], cast=True)
    else:
        w1a, w2a, wo, w1b, w2b = w["ffn16"]
        x1 = _ffn1(x.reshape(n, D_MODEL), w["gn1"], w1a, w2a, cast=False)
    (k_a, v_a, lft, k_b, vb, qa16, ka16, va16, qb16, kb16, vb16) = _proj(
        x1, w["gmix"], w["wa"], w["wf"], w["bf"], w["wb"], b, t, tokens_minor=cache is None)
    if cache is None:
        aug = _cum_aug(lft, blocked=True)
        oa, ob = _attn_prompt(qa16, qb16, ka16, kb16, va16, vb16, aug, w["lams"], w["sub"], b, t, lambda_init)
        news = (_tokens_major(k_a, (N_A, HD_A)), _tokens_major(v_a, (N_A, HD_A)), _tokens_major(lft, (N_A,)),
                _tokens_major(k_b, (N_B, 2, HD_B)), vb.reshape(b, t, N_B, 2 * HD_B))
    else:
        cfk, cfv, cflf, cdk, cdv = cache
        past = cfk.shape[1]
        aug = _cum_aug(jnp.concatenate([_tokens_minor(cflf.astype(F32)), lft], axis=2), blocked=False)
        oa, ob = _attn_cached(qa16, qb16, _tokens_minor(cfk), _tokens_minor(cfv), _tokens_minor(cdk),
                              cdv.reshape(b, past * N_B, 2 * HD_B), ka16, va16, kb16, vb16, aug,
                              w["lams"], w["sub"], b, t, lambda_init)
        news = (k_a.reshape(b, t, N_A, HD_A), v_a.reshape(b, t, N_A, HD_A), _tokens_major(lft, (N_A,)),
                k_b.reshape(b, t, N_B, 2, HD_B), vb.reshape(b, t, N_B, 2 * HD_B))
    if cast:
        y, wo, w1b, w2b = _post(x1, oa, ob, w["gn2"], w["gfin"], w["wo"], w["w1b"], w["w2b"], final, cast=True)
        w["ffn16"] = (w1a, w2a, wo, w1b, w2b)
    else:
        y = _post(x1, oa, ob, w["gn2"], w["gfin"], wo, w1b, w2b, final, cast=False)
    return y.reshape(b, t, D_MODEL), news


def kernel(x_prompt, x_sample, cache_fox_k, cache_fox_v, cache_fox_logf, cache_diff_k, cache_diff_v,
           norm_ffn1, w_ffn1_in, w_ffn1_out, norm_mix, w_in, b_forget,
           lambda_q1, lambda_k1, lambda_q2, lambda_k2, diff_subln, w_out,
           norm_ffn2, w_ffn2_in, w_ffn2_out, norm_final):
    depth = w_in.shape[0]
    xp, xs = x_prompt, x_sample
    outs_p, outs_s = [], []
    gfin = norm_final.reshape(1, D_MODEL)
    for l in range(depth):
        lambda_init = 0.8 - 0.6 * math.exp(-0.3 * l)
        f_lo, f_hi = 3 * D_A, 3 * D_A + N_A
        w = {
            "gn1": norm_ffn1[l].reshape(1, D_MODEL),
            "w1a": w_ffn1_in[l], "w2a": w_ffn1_out[l],
            "gmix": norm_mix[l].reshape(1, D_MODEL),
            "wa": w_in[l][:, :f_lo].astype(BF16),
            "wf": jnp.pad(w_in[l][:, f_lo:f_hi], ((0, 0), (0, LANES - N_A))).astype(BF16),
            "bf": jnp.pad(b_forget[l], (0, LANES - N_A)).reshape(1, LANES),
            "wb": w_in[l][:, f_hi:].astype(BF16),
            "lams": [v[l].reshape(1, HD_B) for v in (lambda_q1, lambda_k1, lambda_q2, lambda_k2)],
            "sub": diff_subln[l].reshape(1, 2 * HD_B),
            "wo": w_out[l],
            "gn2": norm_ffn2[l].reshape(1, D_MODEL),
            "w1b": w_ffn2_in[l], "w2b": w_ffn2_out[l],
            "gfin": gfin,
            "ffn16": None,
        }
        streams = []
        for x, cache in ((xp, None),
                         (xs, (cache_fox_k[l], cache_fox_v[l], cache_fox_logf[l],
                               cache_diff_k[l], cache_diff_v[l]))):
            streams.append(_layer(x, cache, w, lambda_init, l == depth - 1))
        (xp, news_p), (xs, news_s) = streams
        outs_p.append(news_p)
        outs_s.append(news_s)
    stack = lambda outs, i: jnp.stack([o[i] for o in outs])
    return (xp, xs) + tuple(stack(outs_p, i) for i in range(5)) + tuple(stack(outs_s, i) for i in range(5))
```

```python
import functools
import math

import jax
import jax.numpy as jnp
from jax import lax
from jax.experimental import pallas as pl
from jax.experimental.pallas import tpu as pltpu

D_MODEL = 1024
D_FF = 2816
N_A = 8
HD_A = 64
N_B = 4
HD_B = 64
D_A = N_A * HD_A
D_B = N_B * 2 * HD_B
CHUNK = 64
NORM_EPS = 1e-6
NEG_INF = -1e30
ALIBI_SLOPES = (0.25, 0.0625, 0.015625, 0.00390625)

LOG2E = math.log2(math.e)
LANES = 128
MXU_WIDTH = 256
FF_CHUNKS = (1536, 1280)
assert sum(FF_CHUNKS) == D_FF and all(c % MXU_WIDTH == 0 for c in FF_CHUNKS)
SCAN_BATCH = 8
CAST_BYTES = 3 << 20
TOKEN_TILE = 512
ATTN_BLOCK = 256
KEY_BLOCKS_PER_STEP = 2
AUG_ROWS = 32
VMEM_LIMIT = 56 * 1024 * 1024
N_PAIRS = N_A // 2

F32 = jnp.float32
BF16 = jnp.bfloat16


def _params(n_axes):
    return pltpu.CompilerParams(dimension_semantics=("arbitrary",) * n_axes,
                                vmem_limit_bytes=VMEM_LIMIT)


def _resident(shape):
    return pl.BlockSpec(shape, lambda *_: (0,) * len(shape), pipeline_mode=pl.Buffered(1))


def _rms(xf, g):
    ms = jnp.mean(xf * xf, axis=-1, keepdims=True)
    return xf * lax.rsqrt(ms + NORM_EPS) * g


def _swiglu_residual(x, gn_ref, w1_ref, w2_ref):
    h = _rms(x, gn_ref[...]).astype(BF16)
    acc = None
    lo = 0
    for width in FF_CHUNKS:
        g = jnp.dot(h, w1_ref[:, lo:lo + width], preferred_element_type=F32)
        u = jnp.dot(h, w1_ref[:, D_FF + lo:D_FF + lo + width], preferred_element_type=F32)
        a = (g * jax.nn.sigmoid(g) * u).astype(BF16)
        d = jnp.dot(a, w2_ref[lo:lo + width, :], preferred_element_type=F32)
        acc = d if acc is None else acc + d
        lo += width
    return x + 0.5 * acc


def _cast_rows(rows_total, cols):
    rows = max(16, CAST_BYTES // (4 * cols) // 16 * 16)
    while rows_total % rows:
        rows -= 16
    return rows


def _stage_shape(shapes):
    return {c: (2, max(_cast_rows(r, cc) for r, cc in shapes if cc == c), c) for _, c in shapes}


def _cast_weight(w_hbm, w16_ref, stage_ref, sem_ref):
    r_total, cols = w_hbm.shape
    rows = _cast_rows(r_total, cols)
    n = r_total // rows

    def copy(c):
        return pltpu.make_async_copy(w_hbm.at[pl.ds(c * rows, rows), :],
                                     stage_ref.at[c % 2, pl.ds(0, rows), :], sem_ref.at[c % 2])

    copy(0).start()
    for c in range(n):
        if c + 1 < n:
            copy(c + 1).start()
        copy(c).wait()
        w16_ref[pl.ds(c * rows, rows), :] = stage_ref[c % 2, 0:rows, :].astype(BF16)


def _weights_prologue(i, w_hbm_refs, w16_refs, w16_hbm_refs, stages, sem_in, sem_out):
    def publish(k):
        return pltpu.make_async_copy(w16_refs[k], w16_hbm_refs[k], sem_out.at[k])

    @pl.when(i == 0)
    def _():
        for w_hbm, w16 in zip(w_hbm_refs, w16_refs):
            _cast_weight(w_hbm, w16, stages[w_hbm.shape[1]], sem_in)
        for k in range(len(w16_refs)):
            publish(k).start()

    return publish


def _weights_epilogue(i, publish, n_weights):
    @pl.when(i == pl.num_programs(0) - 1)
    def _():
        for k in range(n_weights):
            publish(k).wait()


def _weight_plumbing(weights, cast):
    if not cast:
        return [_resident(w.shape) for w in weights], [], [], []
    any_spec = pl.BlockSpec(memory_space=pl.ANY)
    shapes = [w.shape for w in weights]
    stages = _stage_shape(shapes)
    scratch = ([pltpu.VMEM(sh, BF16) for sh in shapes] + [pltpu.VMEM(stages[c], F32) for c in sorted(stages)]
               + [pltpu.SemaphoreType.DMA((2,)), pltpu.SemaphoreType.DMA((len(weights),))])
    return ([any_spec] * len(weights), [any_spec] * len(weights),
            [jax.ShapeDtypeStruct(sh, BF16) for sh in shapes], scratch)


def _split_weight_refs(refs, n_weights, shapes):
    w16 = refs[:n_weights]
    cols = sorted({c for _, c in shapes})
    stage_refs = refs[n_weights:n_weights + len(cols)]
    sem_in, sem_out = refs[n_weights + len(cols):]
    return w16, dict(zip(cols, stage_refs)), sem_in, sem_out


def _ffn1_kernel(x_ref, gn_ref, w1_in, w2_in, o_ref, *rest, cast):
    if cast:
        i = pl.program_id(0)
        w_hbm = (w1_in, w2_in)
        w16, stages, sem_in, sem_out = _split_weight_refs(rest[2:], 2, [w.shape for w in w_hbm])
        publish = _weights_prologue(i, w_hbm, w16, rest[:2], stages, sem_in, sem_out)
        o_ref[...] = _swiglu_residual(x_ref[...], gn_ref, *w16)
        _weights_epilogue(i, publish, 2)
    else:
        o_ref[...] = _swiglu_residual(x_ref[...], gn_ref, w1_in, w2_in)


def _ffn1(x, gn, w1, w2, cast):
    n = x.shape[0]
    tm = min(TOKEN_TILE, n)
    w_specs, w_out_specs, w_out_shapes, scratch = _weight_plumbing([w1, w2], cast)
    out = pl.pallas_call(
        functools.partial(_ffn1_kernel, cast=cast),
        grid=(n // tm,),
        in_specs=[pl.BlockSpec((tm, D_MODEL), lambda i: (i, 0)), _resident((1, D_MODEL))] + w_specs,
        out_specs=[pl.BlockSpec((tm, D_MODEL), lambda i: (i, 0))] + w_out_specs,
        out_shape=[jax.ShapeDtypeStruct((n, D_MODEL), F32)] + w_out_shapes,
        scratch_shapes=scratch,
        compiler_params=_params(1),
        name="ffn1",
    )(x, gn, w1, w2)
    return out if cast else out[0]


def _post_kernel(x_ref, oa_ref, ob_ref, gn_ref, gf_ref, wo_in, w1_in, w2_in, y_ref, *rest, final, cast):
    if cast:
        i = pl.program_id(0)
        w_hbm = (wo_in, w1_in, w2_in)
        w16, stages, sem_in, sem_out = _split_weight_refs(rest[3:], 3, [w.shape for w in w_hbm])
        publish = _weights_prologue(i, w_hbm, w16, rest[:3], stages, sem_in, sem_out)
        wo_ref, w1_ref, w2_ref = w16
    else:
        wo_ref, w1_ref, w2_ref = wo_in, w1_in, w2_in
    x = x_ref[...]
    x = x + jnp.dot(oa_ref[...], wo_ref[:D_A, :], preferred_element_type=F32)
    x = x + jnp.dot(ob_ref[...], wo_ref[D_A:, :], preferred_element_type=F32)
    x = _swiglu_residual(x, gn_ref, w1_ref, w2_ref)
    y_ref[...] = _rms(x, gf_ref[...]) if final else x
    if cast:
        _weights_epilogue(i, publish, 3)


def _post(x, oa, ob, gn, gf, wo, w1, w2, final, cast):
    n = x.shape[0]
    tm = min(TOKEN_TILE, n)
    row = lambda i: (i, 0)
    w_specs, w_out_specs, w_out_shapes, scratch = _weight_plumbing([wo, w1, w2], cast)
    out = pl.pallas_call(
        functools.partial(_post_kernel, final=final, cast=cast),
        grid=(n // tm,),
        in_specs=[pl.BlockSpec((tm, D_MODEL), row), pl.BlockSpec((tm, D_A), row), pl.BlockSpec((tm, D_B), row),
                  _resident((1, D_MODEL)), _resident((1, D_MODEL))] + w_specs,
        out_specs=[pl.BlockSpec((tm, D_MODEL), row)] + w_out_specs,
        out_shape=[jax.ShapeDtypeStruct((n, D_MODEL), F32)] + w_out_shapes,
        scratch_shapes=scratch,
        compiler_params=_params(1),
        name="post",
    )(x, oa, ob, gn, gf, wo, w1, w2)
    return out if cast else out[0]


def _proj_kernel(x_ref, gn_ref, wa_ref, wf_ref, bf_ref, wb_ref, *out_refs, seg, kblk, tokens_minor):
    tm = x_ref.shape[0]
    h = _rms(x_ref[...], gn_ref[...]).astype(BF16)
    pa = jnp.dot(h, wa_ref[...], preferred_element_type=F32)
    pb = jnp.dot(h, wb_ref[...], preferred_element_type=F32)
    z = jnp.dot(h, wf_ref[...], preferred_element_type=F32) + bf_ref[...]
    lft = (jnp.minimum(z, 0.0) - jnp.log1p(jnp.exp(-jnp.abs(z)))).T
    ka, va = pa[:, D_A:2 * D_A], pa[:, 2 * D_A:]
    kb, vb = pb[:, D_B:2 * D_B], pb[:, 2 * D_B:]
    qa16 = (pa[:, :D_A] * (HD_A ** -0.5 * LOG2E)).astype(BF16)
    qb16 = (pb[:, :D_B] * (HD_B ** -0.5 * LOG2E)).astype(BF16)
    if tokens_minor:
        kat_ref, vat_ref, lft_ref, kbt_ref, vb_ref, qa16_ref, kat16_ref, va16_ref, qb16_ref, kbt16_ref, vb16_ref = out_refs
        kat, vat, kbt = ka.T, va.T, kb.T
    else:
        ka_ref, va_ref, lft_ref, kb_ref, vb_ref, qa16_ref, ka16_ref, va16_ref, qb16_ref, kb16_ref, vb16_ref = out_refs
        ka16_ref[...] = ka.astype(BF16)
        kb16_ref[...] = kb.astype(BF16)
        for hh in range(N_A):
            cols = slice(hh * HD_A, (hh + 1) * HD_A)
            ka_ref[pl.ds(hh, tm, stride=N_A), :] = ka[:, cols]
            va_ref[pl.ds(hh, tm, stride=N_A), :] = va[:, cols]
            kb_ref[pl.ds(hh, tm, stride=N_A), :] = kb[:, cols]
    qa16_ref[...] = qa16
    qb16_ref[...] = qb16
    va16_ref[...] = va.astype(BF16)
    vb16_ref[...] = vb.astype(BF16)
    for hh in range(N_B):
        vb_ref[pl.ds(hh, tm, stride=N_B), :] = vb[:, hh * LANES:(hh + 1) * LANES]
    for bb in range(tm // seg):
        cols = slice(bb * seg, (bb + 1) * seg)
        lft_ref[bb] = lft[:N_A, cols]
        if tokens_minor:
            kat_ref[bb] = kat[:, cols]
            vat_ref[bb] = vat[:, cols]
            kbt_ref[bb] = kbt[:, cols]
            for jj in range(seg // kblk):
                kc = slice(bb * seg + jj * kblk, bb * seg + (jj + 1) * kblk)
                kat16_ref[bb, jj] = kat[:, kc].astype(BF16)
                kbt16_ref[bb, jj] = kbt[:, kc].astype(BF16)


def _proj(x, gn, wa, wf, bf, wb, b, t, tokens_minor):
    n = x.shape[0]
    tm = min(TOKEN_TILE, n)
    seg = min(tm, t)
    nb = tm // seg
    nt = t // seg
    kblk = min(ATTN_BLOCK, seg)
    row = lambda i: (i, 0)
    tok = lambda i: (i // nt, 0, i % nt)
    blk = lambda i: (i // nt, i % nt, 0, 0)
    b16 = (jax.ShapeDtypeStruct((n, D_A), BF16), pl.BlockSpec((tm, D_A), row))
    lft = (jax.ShapeDtypeStruct((b, N_A, t), F32), pl.BlockSpec((nb, N_A, seg), tok))
    vb = (jax.ShapeDtypeStruct((n * N_B, LANES), F32), pl.BlockSpec((tm * N_B, LANES), row))
    if tokens_minor:
        kv = (jax.ShapeDtypeStruct((b, D_A, t), F32), pl.BlockSpec((nb, D_A, seg), tok))
        k16 = (jax.ShapeDtypeStruct((b, t // kblk, D_A, kblk), BF16),
               pl.BlockSpec((nb, seg // kblk, D_A, kblk), blk))
    else:
        kv = (jax.ShapeDtypeStruct((n * N_A, HD_A), F32), pl.BlockSpec((tm * N_A, HD_A), row))
        k16 = b16
    outs = [kv, kv, lft, kv, vb, b16, k16, b16, b16, k16, b16]
    return pl.pallas_call(
        functools.partial(_proj_kernel, seg=seg, kblk=kblk, tokens_minor=tokens_minor),
        grid=(n // tm,),
        in_specs=[pl.BlockSpec((tm, D_MODEL), row), _resident((1, D_MODEL)), _resident(wa.shape),
                  _resident(wf.shape), _resident(bf.shape), _resident(wb.shape)],
        out_specs=[o[1] for o in outs],
        out_shape=[o[0] for o in outs],
        compiler_params=_params(1),
        name="proj",
    )(x, gn, wa, wf, bf, wb)


def _split3(x):
    hi = x.astype(BF16).astype(F32)
    r = x - hi
    mid = r.astype(BF16).astype(F32)
    lo = (r - mid).astype(BF16).astype(F32)
    return hi, mid, lo


def _cum_kernel(lft_ref, aug_ref, pad_ref, *, t, blocked):
    w = ATTN_BLOCK
    nb = lft_ref.shape[0]
    rows = nb * N_A
    pad_ref[...] = jnp.zeros(pad_ref.shape, F32)
    for bb in range(nb):
        pad_ref[bb * N_A:(bb + 1) * N_A, 0:t] = lft_ref[bb]
    r = lax.broadcasted_iota(jnp.int32, (w, w), 0)
    c = lax.broadcasted_iota(jnp.int32, (w, w), 1)
    tri = jnp.where(r <= c, 1.0, 0.0).astype(BF16)
    zero = jnp.zeros((N_A, w), F32)
    carry = jnp.zeros((rows, 1), F32)
    for s in range(pad_ref.shape[1] // w):
        lo_col = s * w
        cols = min(w, t - lo_col)
        x = jnp.concatenate(_split3(pad_ref[:, lo_col:lo_col + w]), axis=0).astype(BF16)
        y = jnp.dot(x, tri, preferred_element_type=F32)
        cs = y[0:rows] + y[rows:2 * rows] + y[2 * rows:3 * rows] + carry
        carry = cs[:, w - 1:w]
        pieces = _split3(cs * -LOG2E)
        for bb in range(nb):
            sl = slice(bb * N_A, (bb + 1) * N_A)
            aug = jnp.concatenate([p[sl] for p in pieces] + [zero], axis=0).astype(BF16)
            if blocked:
                aug_ref[bb, s] = aug
            else:
                aug_ref[bb, :, lo_col:lo_col + cols] = aug[:, :cols]


def _cum_aug(lft, blocked):
    b, _, t = lft.shape
    w = ATTN_BLOCK
    t_pad = -(-t // w) * w
    nb = math.gcd(b, SCAN_BATCH)
    if blocked:
        out_shape = jax.ShapeDtypeStruct((b, t // w, AUG_ROWS, w), BF16)
        out_spec = pl.BlockSpec((nb, t // w, AUG_ROWS, w), lambda i: (i, 0, 0, 0))
    else:
        out_shape = jax.ShapeDtypeStruct((b, AUG_ROWS, t), BF16)
        out_spec = pl.BlockSpec((nb, AUG_ROWS, t), lambda i: (i, 0, 0))
    return pl.pallas_call(
        functools.partial(_cum_kernel, t=t, blocked=blocked),
        grid=(b // nb,),
        in_specs=[pl.BlockSpec((nb, N_A, t), lambda i: (i, 0, 0))],
        out_specs=out_spec,
        out_shape=out_shape,
        scratch_shapes=[pltpu.VMEM((nb * N_A, t_pad), F32)],
        compiler_params=_params(1),
        name="cum_aug",
    )(lft)


def _lane_halves(q):
    lane = lax.broadcasted_iota(jnp.int32, q.shape, 1)
    zero = jnp.zeros_like(q)
    return jnp.where(lane < HD_A, q, zero), jnp.where(lane >= HD_A, q, zero)


def _piece_selector(shape, head, n_heads):
    lane = lax.broadcasted_iota(jnp.int32, shape, 1)
    hit = (lane == head) | (lane == n_heads + head) | (lane == 2 * n_heads + head)
    return jnp.where(hit, 1.0, 0.0).astype(BF16)


def _alibi_rows(k_pos):
    w = k_pos.shape[1]
    row = lax.broadcasted_iota(jnp.int32, (4 * N_B, 1), 0)
    head = row % N_B
    slope = jnp.zeros((4 * N_B, 1), F32)
    for h in range(N_B):
        slope = jnp.where(head == h, ALIBI_SLOPES[h] * LOG2E, slope)
    hi, mid, lo = _split3(slope * k_pos)
    piece = jnp.where(row < N_B, hi, jnp.where(row < 2 * N_B, mid, jnp.where(row < 3 * N_B, lo, 0.0)))
    return jnp.concatenate([piece.astype(BF16), jnp.zeros((LANES - 4 * N_B, w), BF16)], axis=0)


def _alibi_fold(slope, q_pos, k_pos):
    return (-2.0 * LOG2E * slope) * jnp.maximum(k_pos - q_pos, 0).astype(F32)


def _diff_finish(o1, o2, lam, sub_ref, lambda_init):
    return _rms(o1 - lam * o2, sub_ref[...]) * (1.0 - lambda_init)


def _lambda(lq1_ref, lk1_ref, lq2_ref, lk2_ref, lambda_init):
    return (jnp.exp(jnp.sum(lq1_ref[...] * lk1_ref[...], axis=1, keepdims=True))
            - jnp.exp(jnp.sum(lq2_ref[...] * lk2_ref[...], axis=1, keepdims=True)) + lambda_init)


def _attn_prompt_kernel(qa_ref, qb_ref, kat_ref, kbt_ref, va_ref, vb_ref, aug_ref,
                        lq1_ref, lk1_ref, lq2_ref, lk2_ref, sub_ref, oa_ref, ob_ref,
                        qs_ref, m_ref, acc_ref, tbl_ref, *, lambda_init):
    tq = ATTN_BLOCK
    n_chain = N_PAIRS + N_B

    @pl.when(pl.program_id(0) == 0)
    def _():
        r = lax.broadcasted_iota(jnp.int32, (2 * tq, tq), 0)
        c = lax.broadcasted_iota(jnp.int32, (2 * tq, tq), 1)
        q_idx = jnp.where(r >= tq, r - tq, r)
        tbl_ref[0] = jnp.where(q_idx >= c, 0.0, NEG_INF)
        visible = (q_idx // CHUNK) >= (c // CHUNK)
        for h in range(N_B):
            tbl_ref[1 + h] = jnp.where(visible, _alibi_fold(ALIBI_SLOPES[h], q_idx, c), NEG_INF)
        for ch in range(n_chain):
            if ch < N_PAIRS:
                qs_ref[ch, 0:tq, LANES:] = _piece_selector((tq, LANES), 2 * ch, N_A)
                qs_ref[ch, tq:, LANES:] = _piece_selector((tq, LANES), 2 * ch + 1, N_A)
            else:
                sel = _piece_selector((tq, LANES), ch - N_PAIRS, N_B)
                qs_ref[ch, 0:tq, LANES:] = sel
                qs_ref[ch, tq:, LANES:] = sel

    def step(j, first, width=1):
        wk = width * tq
        rows = pl.ds(pl.multiple_of(j * tq, tq), wk)
        blocks = [j + d for d in range(width)]
        ones = jnp.ones((wk, LANES), BF16)
        aug_a = jnp.concatenate([jnp.concatenate([aug_ref[0, jb] for jb in blocks], axis=1),
                                 jnp.zeros((LANES - AUG_ROWS, wk), BF16)], axis=0)
        aug_b = _alibi_rows((j * tq + lax.broadcasted_iota(jnp.int32, (1, wk), 1)).astype(F32))
        for ch in range(n_chain):
            if ch < N_PAIRS:
                kt_ref, v_ref, g, aug, tbl = kat_ref, va_ref, ch, aug_a, 0
            else:
                kt_ref, v_ref, g, aug, tbl = kbt_ref, vb_ref, ch - N_PAIRS, aug_b, 1 + ch - N_PAIRS
            kt = jnp.concatenate([kt_ref[0, jb, g * LANES:(g + 1) * LANES, :] for jb in blocks], axis=1)
            s = jnp.dot(qs_ref[ch], jnp.concatenate([kt, aug], axis=0), preferred_element_type=F32)
            v = v_ref[rows, g * LANES:(g + 1) * LANES]
            if first:
                s = s + tbl_ref[tbl]
                m_new = jnp.broadcast_to(jnp.max(s, axis=1, keepdims=True), (2 * tq, LANES))
            else:
                m_prev = m_ref[ch]
                m_new = jnp.maximum(m_prev, jnp.max(s, axis=1, keepdims=True))
            p = jnp.exp2(s - jnp.concatenate([m_new] * (wk // LANES), axis=1))
            pv = jnp.dot(p.astype(BF16), jnp.concatenate([v, ones], axis=1), preferred_element_type=F32)
            if first:
                acc_ref[ch] = pv
            else:
                alpha = jnp.exp2(m_prev - m_new)
                acc_ref[ch] = jnp.concatenate([alpha, alpha], axis=1) * acc_ref[ch] + pv
            m_ref[ch] = m_new

    def wide_body(jj, carry):
        step(jj * KEY_BLOCKS_PER_STEP, False, KEY_BLOCKS_PER_STEP)
        return carry

    def single_body(j, carry):
        step(j, False)
        return carry

    lane = lax.broadcasted_iota(jnp.int32, (tq, LANES), 1)
    lam = _lambda(lq1_ref, lk1_ref, lq2_ref, lk2_ref, lambda_init)

    def query_block(i, carry):
        q_rows = pl.ds(pl.multiple_of(i * tq, tq), tq)
        for ch in range(n_chain):
            q_ref, g = (qa_ref, ch) if ch < N_PAIRS else (qb_ref, ch - N_PAIRS)
            lo, hi = _lane_halves(q_ref[q_rows, g * LANES:(g + 1) * LANES])
            qs_ref[ch, 0:tq, 0:LANES] = lo
            qs_ref[ch, tq:, 0:LANES] = hi
        step(i, True)
        n_wide = i // KEY_BLOCKS_PER_STEP
        lax.fori_loop(0, n_wide, wide_body, 0)
        lax.fori_loop(n_wide * KEY_BLOCKS_PER_STEP, i, single_body, 0)
        for ch in range(n_chain):
            o = acc_ref[ch, :, 0:LANES] / acc_ref[ch, :, LANES:]
            if ch < N_PAIRS:
                oa_ref[q_rows, ch * LANES:(ch + 1) * LANES] = jnp.where(lane < HD_A, o[:tq], o[tq:]).astype(BF16)
            else:
                h = ch - N_PAIRS
                ob_ref[q_rows, h * LANES:(h + 1) * LANES] = _diff_finish(
                    o[:tq], o[tq:], lam, sub_ref, lambda_init).astype(BF16)
        return carry

    lax.fori_loop(0, qa_ref.shape[0] // tq, query_block, 0)


def _attn_prompt(qa16, qb16, kat16, kbt16, va16, vb16, aug, lams, sub, b, t, lambda_init):
    tq = ATTN_BLOCK
    nq = t // tq
    kspec = pl.BlockSpec((1, nq, D_A, tq), lambda bi: (bi, 0, 0, 0))
    vspec = pl.BlockSpec((t, D_A), lambda bi: (bi, 0))
    small = [_resident((1, HD_B))] * 4 + [_resident((1, 2 * HD_B))]
    n_chain = N_PAIRS + N_B
    return pl.pallas_call(
        functools.partial(_attn_prompt_kernel, lambda_init=lambda_init),
        grid=(b,),
        in_specs=[vspec, vspec, kspec, kspec, vspec, vspec,
                  pl.BlockSpec((1, nq, AUG_ROWS, tq), lambda bi: (bi, 0, 0, 0))] + small,
        out_specs=[vspec, vspec],
        out_shape=[jax.ShapeDtypeStruct((b * t, D_A), BF16), jax.ShapeDtypeStruct((b * t, D_B), BF16)],
        scratch_shapes=[pltpu.VMEM((n_chain, 2 * tq, 2 * LANES), BF16),
                        pltpu.VMEM((n_chain, 2 * tq, LANES), F32),
                        pltpu.VMEM((n_chain, 2 * tq, 2 * LANES), F32),
                        pltpu.VMEM((1 + N_B, 2 * tq, tq), F32)],
        compiler_params=_params(1),
        name="attn_prompt",
    )(qa16, qb16, kat16, kbt16, va16, vb16, aug, *lams, sub)


def _nt_dot(a, b):
    return lax.dot_general(a, b, (((1,), (1,)), ((), ())), preferred_element_type=F32)


def _attn_cached_kernel(qa_ref, qb_ref, ckat_ref, cvat_ref, ckbt_ref, cvb_ref,
                        ka_ref, va_ref, kb_ref, vb_ref, aug_ref,
                        lq1_ref, lk1_ref, lq2_ref, lk2_ref, sub_ref, oa_ref, ob_ref,
                        kn_ref, *, lambda_init):
    tq = qa_ref.shape[0]
    past = ckat_ref.shape[2]
    t_all = past + tq
    n_chain = N_PAIRS + N_B
    r = lax.broadcasted_iota(jnp.int32, (2 * tq, LANES), 0)
    c = lax.broadcasted_iota(jnp.int32, (2 * tq, LANES), 1)
    q_idx = jnp.where(r >= tq, r - tq, r)
    is_key = c < tq
    causal_tbl = jnp.where((q_idx >= c) & is_key, 0.0, NEG_INF)
    visible = (((past + q_idx) // CHUNK) >= ((past + c) // CHUNK)) & is_key
    lane = lax.broadcasted_iota(jnp.int32, (tq, LANES), 1)
    lam = _lambda(lq1_ref, lk1_ref, lq2_ref, lk2_ref, lambda_init)
    aug_c = jnp.concatenate([aug_ref[0, :, 0:past], jnp.zeros((LANES - AUG_ROWS, past), BF16)], axis=0)
    alibi_c = _alibi_rows(lax.broadcasted_iota(jnp.int32, (1, past), 1).astype(F32))
    alibi_n = _alibi_rows((past + lax.broadcasted_iota(jnp.int32, (1, LANES), 1)).astype(F32))
    kn_ref[...] = jnp.zeros(kn_ref.shape, BF16)
    row_pad = jnp.zeros((LANES - tq, LANES), BF16)

    def new_keys_t(k_new):
        return jnp.concatenate([k_new, row_pad], axis=0).astype(F32).T.astype(BF16)

    for ch in range(n_chain):
        fox = ch < N_PAIRS
        g = ch if fox else ch - N_PAIRS
        rows = slice(g * LANES, (g + 1) * LANES)
        if fox:
            lo, hi = _lane_halves(qa_ref[:, rows])
            sel_lo = _piece_selector((tq, LANES), 2 * g, N_A)
            sel_hi = _piece_selector((tq, LANES), 2 * g + 1, N_A)
            k_old = jnp.concatenate([ckat_ref[0, rows, :].astype(BF16), aug_c], axis=0)
            kn_ref[ch, 0:LANES, :] = new_keys_t(ka_ref[:, rows])
            kn_ref[ch, LANES:LANES + AUG_ROWS, 0:tq] = aug_ref[0, :, past:t_all]
            tbl = causal_tbl
        else:
            lo, hi = _lane_halves(qb_ref[:, rows])
            sel_lo = sel_hi = _piece_selector((tq, LANES), g, N_B)
            k_old = jnp.concatenate([ckbt_ref[0, rows, :].astype(BF16), alibi_c], axis=0)
            kn_ref[ch, 0:LANES, :] = new_keys_t(kb_ref[:, rows])
            kn_ref[ch, LANES:, :] = alibi_n
            tbl = jnp.where(visible, _alibi_fold(ALIBI_SLOPES[g], q_idx, c), NEG_INF)
        qs = jnp.concatenate([jnp.concatenate([lo, sel_lo], axis=1), jnp.concatenate([hi, sel_hi], axis=1)], axis=0)
        s_old = jnp.dot(qs, k_old, preferred_element_type=F32)
        s_new = jnp.dot(qs, kn_ref[ch], preferred_element_type=F32) + tbl
        m = jnp.maximum(jnp.max(s_old, axis=1, keepdims=True), jnp.max(s_new, axis=1, keepdims=True))
        p_old = jnp.exp2(s_old - m)
        p_new = jnp.exp2(s_new - m)
        l = jnp.sum(p_old, axis=1, keepdims=True) + jnp.sum(p_new, axis=1, keepdims=True)
        v_new = jnp.concatenate([(va_ref if fox else vb_ref)[:, rows], row_pad], axis=0)
        o_new = jnp.dot(p_new.astype(BF16), v_new, preferred_element_type=F32)
        if fox:
            o = (_nt_dot(p_old.astype(BF16), cvat_ref[0, rows, :].astype(BF16)) + o_new) / l
            oa_ref[:, rows] = jnp.where(lane < HD_A, o[:tq], o[tq:]).astype(BF16)
        else:
            v_old = cvb_ref[0, pl.ds(g, past, stride=N_B), :].astype(BF16)
            o = (jnp.dot(p_old.astype(BF16), v_old, preferred_element_type=F32) + o_new) / l
            ob_ref[:, rows] = _diff_finish(o[:tq], o[tq:], lam, sub_ref, lambda_init).astype(BF16)


def _attn_cached(qa16, qb16, ckat, cvat, ckbt, cvb, ka16, va16, kb16, vb16, aug, lams, sub, b, tq, lambda_init):
    past = ckat.shape[2]
    t_all = past + tq
    n_chain = N_PAIRS + N_B
    qspec = pl.BlockSpec((tq, D_A), lambda bi: (bi, 0))
    old_t = pl.BlockSpec((1, D_A, past), lambda bi: (bi, 0, 0))
    small = [_resident((1, HD_B))] * 4 + [_resident((1, 2 * HD_B))]
    return pl.pallas_call(
        functools.partial(_attn_cached_kernel, lambda_init=lambda_init),
        grid=(b,),
        in_specs=[qspec, qspec, old_t, old_t, old_t,
                  pl.BlockSpec((1, past * N_B, LANES), lambda bi: (bi, 0, 0)),
                  qspec, qspec, qspec, qspec,
                  pl.BlockSpec((1, AUG_ROWS, t_all), lambda bi: (bi, 0, 0))] + small,
        out_specs=[qspec, qspec],
        out_shape=[jax.ShapeDtypeStruct((b * tq, D_A), BF16), jax.ShapeDtypeStruct((b * tq, D_B), BF16)],
        scratch_shapes=[pltpu.VMEM((n_chain, 2 * LANES, LANES), BF16)],
        compiler_params=_params(1),
        name="attn_cached",
    )(qa16, qb16, ckat, cvat, ckbt, cvb, ka16, va16, kb16, vb16, aug, *lams, sub)


def _tokens_minor(x):
    b, t = x.shape[:2]
    return jnp.moveaxis(x.reshape(b, t, -1), 1, 2)


def _tokens_major(xt, tail):
    b, _, t = xt.shape
    return jnp.moveaxis(xt, 2, 1).reshape(b, t, *tail)


def _layer(x, cache, w, lambda_init, final):
    b, t, _ = x.shape
    n = b * t
    cast = w["ffn16"] is None
    if cast:
        x1, w1a, w2a = _ffn1(x.reshape(n, D_MODEL), w["gn1"], w["w1a"], w["w2a"], cast=True)
    else:
        w1a, w2a, wo, w1b, w2b = w["ffn16"]
        x1 = _ffn1(x.reshape(n, D_MODEL), w["gn1"], w1a, w2a, cast=False)
    (k_a, v_a, lft, k_b, vb, qa16, ka16, va16, qb16, kb16, vb16) = _proj(
        x1, w["gmix"], w["wa"], w["wf"], w["bf"], w["wb"], b, t, tokens_minor=cache is None)
    if cache is None:
        aug = _cum_aug(lft, blocked=True)
        oa, ob = _attn_prompt(qa16, qb16, ka16, kb16, va16, vb16, aug, w["lams"], w["sub"], b, t, lambda_init)
        news = (_tokens_major(k_a, (N_A, HD_A)), _tokens_major(v_a, (N_A, HD_A)), _tokens_major(lft, (N_A,)),
                _tokens_major(k_b, (N_B, 2, HD_B)), vb.reshape(b, t, N_B, 2 * HD_B))
    else:
        cfk, cfv, cflf, cdk, cdv = cache
        past = cfk.shape[1]
        aug = _cum_aug(jnp.concatenate([_tokens_minor(cflf.astype(F32)), lft], axis=2), blocked=False)
        oa, ob = _attn_cached(qa16, qb16, _tokens_minor(cfk), _tokens_minor(cfv), _tokens_minor(cdk),
                              cdv.reshape(b, past * N_B, 2 * HD_B), ka16, va16, kb16, vb16, aug,
                              w["lams"], w["sub"], b, t, lambda_init)
        news = (k_a.reshape(b, t, N_A, HD_A), v_a.reshape(b, t, N_A, HD_A), _tokens_major(lft, (N_A,)),
                k_b.reshape(b, t, N_B, 2, HD_B), vb.reshape(b, t, N_B, 2 * HD_B))
    if cast:
        y, wo, w1b, w2b = _post(x1, oa, ob, w["gn2"], w["gfin"], w["wo"], w["w1b"], w["w2b"], final, cast=True)
        w["ffn16"] = (w1a, w2a, wo, w1b, w2b)
    else:
        y = _post(x1, oa, ob, w["gn2"], w["gfin"], wo, w1b, w2b, final, cast=False)
    return y.reshape(b, t, D_MODEL), news


def kernel(x_prompt, x_sample, cache_fox_k, cache_fox_v, cache_fox_logf, cache_diff_k, cache_diff_v,
           norm_ffn1, w_ffn1_in, w_ffn1_out, norm_mix, w_in, b_forget,
           lambda_q1, lambda_k1, lambda_q2, lambda_k2, diff_subln, w_out,
           norm_ffn2, w_ffn2_in, w_ffn2_out, norm_final):
    depth = w_in.shape[0]
    xp, xs = x_prompt, x_sample
    outs_p, outs_s = [], []
    gfin = norm_final.reshape(1, D_MODEL)
    for l in range(depth):
        lambda_init = 0.8 - 0.6 * math.exp(-0.3 * l)
        f_lo, f_hi = 3 * D_A, 3 * D_A + N_A
        w = {
            "gn1": norm_ffn1[l].reshape(1, D_MODEL),
            "w1a": w_ffn1_in[l], "w2a": w_ffn1_out[l],
            "gmix": norm_mix[l].reshape(1, D_MODEL),
            "wa": w_in[l][:, :f_lo].astype(BF16),
            "wf": jnp.pad(w_in[l][:, f_lo:f_hi], ((0, 0), (0, LANES - N_A))).astype(BF16),
            "bf": jnp.pad(b_forget[l], (0, LANES - N_A)).reshape(1, LANES),
            "wb": w_in[l][:, f_hi:].astype(BF16),
            "lams": [v[l].reshape(1, HD_B) for v in (lambda_q1, lambda_k1, lambda_q2, lambda_k2)],
            "sub": diff_subln[l].reshape(1, 2 * HD_B),
            "wo": w_out[l],
            "gn2": norm_ffn2[l].reshape(1, D_MODEL),
            "w1b": w_ffn2_in[l], "w2b": w_ffn2_out[l],
            "gfin": gfin,
            "ffn16": None,
        }
        streams = []
        for x, cache in ((xp, None),
                         (xs, (cache_fox_k[l], cache_fox_v[l], cache_fox_logf[l],
                               cache_diff_k[l], cache_diff_v[l]))):
            streams.append(_layer(x, cache, w, lambda_init, l == depth - 1))
        (xp, news_p), (xs, news_s) = streams
        outs_p.append(news_p)
        outs_s.append(news_s)
    stack = lambda outs, i: jnp.stack([o[i] for o in outs])
    return (xp, xs) + tuple(stack(outs_p, i) for i in range(5)) + tuple(stack(outs_s, i) for i in range(5))
```

```python
import functools
import math

import jax
import jax.numpy as jnp
from jax import lax
from jax.experimental import pallas as pl
from jax.experimental.pallas import tpu as pltpu

D_MODEL = 1024
D_FF = 2816
N_A = 8
HD_A = 64
N_B = 4
HD_B = 64
D_A = N_A * HD_A
D_B = N_B * 2 * HD_B
CHUNK = 64
NORM_EPS = 1e-6
NEG_INF = -1e30
ALIBI_SLOPES = (0.25, 0.0625, 0.015625, 0.00390625)

LOG2E = math.log2(math.e)
LANES = 128
MXU_WIDTH = 256
FF_CHUNKS = (1536, 1280)
assert sum(FF_CHUNKS) == D_FF and all(c % MXU_WIDTH == 0 for c in FF_CHUNKS)
SCAN_BATCH = 8
CAST_BYTES = 3 << 20
TOKEN_TILE = 512
ATTN_BLOCK = 256
KEY_BLOCKS_PER_STEP = 2
Q_BLOCKS_PER_STEP = 2
assert KEY_BLOCKS_PER_STEP == Q_BLOCKS_PER_STEP == 2
AUG_ROWS = 32
VMEM_LIMIT = 56 * 1024 * 1024
N_PAIRS = N_A // 2

F32 = jnp.float32
BF16 = jnp.bfloat16


def _params(n_axes):
    return pltpu.CompilerParams(dimension_semantics=("arbitrary",) * n_axes,
                                vmem_limit_bytes=VMEM_LIMIT)


def _resident(shape):
    return pl.BlockSpec(shape, lambda *_: (0,) * len(shape), pipeline_mode=pl.Buffered(1))


def _rms(xf, g):
    ms = jnp.mean(xf * xf, axis=-1, keepdims=True)
    return xf * lax.rsqrt(ms + NORM_EPS) * g


def _swiglu_residual(x, gn_ref, w1_ref, w2_ref):
    h = _rms(x, gn_ref[...]).astype(BF16)
    acc = None
    lo = 0
    for width in FF_CHUNKS:
        g = jnp.dot(h, w1_ref[:, lo:lo + width], preferred_element_type=F32)
        u = jnp.dot(h, w1_ref[:, D_FF + lo:D_FF + lo + width], preferred_element_type=F32)
        a = (g * jax.nn.sigmoid(g) * u).astype(BF16)
        d = jnp.dot(a, w2_ref[lo:lo + width, :], preferred_element_type=F32)
        acc = d if acc is None else acc + d
        lo += width
    return x + 0.5 * acc


def _cast_rows(rows_total, cols):
    rows = max(16, CAST_BYTES // (4 * cols) // 16 * 16)
    while rows_total % rows:
        rows -= 16
    return rows


def _stage_shape(shapes):
    return {c: (2, max(_cast_rows(r, cc) for r, cc in shapes if cc == c), c) for _, c in shapes}


def _cast_weight(w_hbm, w16_ref, stage_ref, sem_ref):
    r_total, cols = w_hbm.shape
    rows = _cast_rows(r_total, cols)
    n = r_total // rows

    def copy(c):
        return pltpu.make_async_copy(w_hbm.at[pl.ds(c * rows, rows), :],
                                     stage_ref.at[c % 2, pl.ds(0, rows), :], sem_ref.at[c % 2])

    copy(0).start()
    for c in range(n):
        if c + 1 < n:
            copy(c + 1).start()
        copy(c).wait()
        w16_ref[pl.ds(c * rows, rows), :] = stage_ref[c % 2, 0:rows, :].astype(BF16)


def _weights_prologue(i, w_hbm_refs, w16_refs, w16_hbm_refs, stages, sem_in, sem_out):
    def publish(k):
        return pltpu.make_async_copy(w16_refs[k], w16_hbm_refs[k], sem_out.at[k])

    @pl.when(i == 0)
    def _():
        for w_hbm, w16 in zip(w_hbm_refs, w16_refs):
            _cast_weight(w_hbm, w16, stages[w_hbm.shape[1]], sem_in)
        for k in range(len(w16_refs)):
            publish(k).start()

    return publish


def _weights_epilogue(i, publish, n_weights):
    @pl.when(i == pl.num_programs(0) - 1)
    def _():
        for k in range(n_weights):
            publish(k).wait()


def _weight_plumbing(weights, cast):
    if not cast:
        return [_resident(w.shape) for w in weights], [], [], []
    any_spec = pl.BlockSpec(memory_space=pl.ANY)
    shapes = [w.shape for w in weights]
    stages = _stage_shape(shapes)
    scratch = ([pltpu.VMEM(sh, BF16) for sh in shapes] + [pltpu.VMEM(stages[c], F32) for c in sorted(stages)]
               + [pltpu.SemaphoreType.DMA((2,)), pltpu.SemaphoreType.DMA((len(weights),))])
    return ([any_spec] * len(weights), [any_spec] * len(weights),
            [jax.ShapeDtypeStruct(sh, BF16) for sh in shapes], scratch)


def _split_weight_refs(refs, n_weights, shapes):
    w16 = refs[:n_weights]
    cols = sorted({c for _, c in shapes})
    stage_refs = refs[n_weights:n_weights + len(cols)]
    sem_in, sem_out = refs[n_weights + len(cols):]
    return w16, dict(zip(cols, stage_refs)), sem_in, sem_out


def _ffn1_kernel(x_ref, gn_ref, w1_in, w2_in, o_ref, *rest, cast):
    if cast:
        i = pl.program_id(0)
        w_hbm = (w1_in, w2_in)
        w16, stages, sem_in, sem_out = _split_weight_refs(rest[2:], 2, [w.shape for w in w_hbm])
        publish = _weights_prologue(i, w_hbm, w16, rest[:2], stages, sem_in, sem_out)
        o_ref[...] = _swiglu_residual(x_ref[...], gn_ref, *w16)
        _weights_epilogue(i, publish, 2)
    else:
        o_ref[...] = _swiglu_residual(x_ref[...], gn_ref, w1_in, w2_in)


def _ffn1(x, gn, w1, w2, cast):
    n = x.shape[0]
    tm = min(TOKEN_TILE, n)
    w_specs, w_out_specs, w_out_shapes, scratch = _weight_plumbing([w1, w2], cast)
    out = pl.pallas_call(
        functools.partial(_ffn1_kernel, cast=cast),
        grid=(n // tm,),
        in_specs=[pl.BlockSpec((tm, D_MODEL), lambda i: (i, 0)), _resident((1, D_MODEL))] + w_specs,
        out_specs=[pl.BlockSpec((tm, D_MODEL), lambda i: (i, 0))] + w_out_specs,
        out_shape=[jax.ShapeDtypeStruct((n, D_MODEL), F32)] + w_out_shapes,
        scratch_shapes=scratch,
        compiler_params=_params(1),
        name="ffn1",
    )(x, gn, w1, w2)
    return out if cast else out[0]


def _post_kernel(x_ref, oa_ref, ob_ref, gn_ref, gf_ref, wo_in, w1_in, w2_in, y_ref, *rest, final, cast):
    if cast:
        i = pl.program_id(0)
        w_hbm = (wo_in, w1_in, w2_in)
        w16, stages, sem_in, sem_out = _split_weight_refs(rest[3:], 3, [w.shape for w in w_hbm])
        publish = _weights_prologue(i, w_hbm, w16, rest[:3], stages, sem_in, sem_out)
        wo_ref, w1_ref, w2_ref = w16
    else:
        wo_ref, w1_ref, w2_ref = wo_in, w1_in, w2_in
    x = x_ref[...]
    x = x + jnp.dot(oa_ref[...], wo_ref[:D_A, :], preferred_element_type=F32)
    x = x + jnp.dot(ob_ref[...], wo_ref[D_A:, :], preferred_element_type=F32)
    x = _swiglu_residual(x, gn_ref, w1_ref, w2_ref)
    y_ref[...] = _rms(x, gf_ref[...]) if final else x
    if cast:
        _weights_epilogue(i, publish, 3)


def _post(x, oa, ob, gn, gf, wo, w1, w2, final, cast):
    n = x.shape[0]
    tm = min(TOKEN_TILE, n)
    row = lambda i: (i, 0)
    w_specs, w_out_specs, w_out_shapes, scratch = _weight_plumbing([wo, w1, w2], cast)
    out = pl.pallas_call(
        functools.partial(_post_kernel, final=final, cast=cast),
        grid=(n // tm,),
        in_specs=[pl.BlockSpec((tm, D_MODEL), row), pl.BlockSpec((tm, D_A), row), pl.BlockSpec((tm, D_B), row),
                  _resident((1, D_MODEL)), _resident((1, D_MODEL))] + w_specs,
        out_specs=[pl.BlockSpec((tm, D_MODEL), row)] + w_out_specs,
        out_shape=[jax.ShapeDtypeStruct((n, D_MODEL), F32)] + w_out_shapes,
        scratch_shapes=scratch,
        compiler_params=_params(1),
        name="post",
    )(x, oa, ob, gn, gf, wo, w1, w2)
    return out if cast else out[0]


def _proj_kernel(x_ref, gn_ref, wa_ref, wf_ref, bf_ref, wb_ref, *out_refs, seg, kblk, tokens_minor):
    tm = x_ref.shape[0]
    h = _rms(x_ref[...], gn_ref[...]).astype(BF16)
    pa = jnp.dot(h, wa_ref[...], preferred_element_type=F32)
    pb = jnp.dot(h, wb_ref[...], preferred_element_type=F32)
    z = jnp.dot(h, wf_ref[...], preferred_element_type=F32) + bf_ref[...]
    lft = (jnp.minimum(z, 0.0) - jnp.log1p(jnp.exp(-jnp.abs(z)))).T
    ka, va = pa[:, D_A:2 * D_A], pa[:, 2 * D_A:]
    kb, vb = pb[:, D_B:2 * D_B], pb[:, 2 * D_B:]
    qa16 = (pa[:, :D_A] * (HD_A ** -0.5 * LOG2E)).astype(BF16)
    qb16 = (pb[:, :D_B] * (HD_B ** -0.5 * LOG2E)).astype(BF16)
    if tokens_minor:
        kat_ref, vat_ref, lft_ref, kbt_ref, vb_ref, qa16_ref, kat16_ref, va16_ref, qb16_ref, kbt16_ref, vb16_ref = out_refs
        kat, vat, kbt = ka.T, va.T, kb.T
    else:
        ka_ref, va_ref, lft_ref, kb_ref, vb_ref, qa16_ref, ka16_ref, va16_ref, qb16_ref, kb16_ref, vb16_ref = out_refs
        ka16_ref[...] = ka.astype(BF16)
        kb16_ref[...] = kb.astype(BF16)
        for hh in range(N_A):
            cols = slice(hh * HD_A, (hh + 1) * HD_A)
            ka_ref[pl.ds(hh, tm, stride=N_A), :] = ka[:, cols]
            va_ref[pl.ds(hh, tm, stride=N_A), :] = va[:, cols]
            kb_ref[pl.ds(hh, tm, stride=N_A), :] = kb[:, cols]
    qa16_ref[...] = qa16
    qb16_ref[...] = qb16
    va16_ref[...] = va.astype(BF16)
    vb16_ref[...] = vb.astype(BF16)
    for hh in range(N_B):
        vb_ref[pl.ds(hh, tm, stride=N_B), :] = vb[:, hh * LANES:(hh + 1) * LANES]
    for bb in range(tm // seg):
        cols = slice(bb * seg, (bb + 1) * seg)
        lft_ref[bb] = lft[:N_A, cols]
        if tokens_minor:
            kat_ref[bb] = kat[:, cols]
            vat_ref[bb] = vat[:, cols]
            kbt_ref[bb] = kbt[:, cols]
            for jj in range(seg // kblk):
                kc = slice(bb * seg + jj * kblk, bb * seg + (jj + 1) * kblk)
                kat16_ref[bb, jj] = kat[:, kc].astype(BF16)
                kbt16_ref[bb, jj] = kbt[:, kc].astype(BF16)


def _proj(x, gn, wa, wf, bf, wb, b, t, tokens_minor):
    n = x.shape[0]
    tm = min(TOKEN_TILE, n)
    seg = min(tm, t)
    nb = tm // seg
    nt = t // seg
    kblk = min(ATTN_BLOCK, seg)
    row = lambda i: (i, 0)
    tok = lambda i: (i // nt, 0, i % nt)
    blk = lambda i: (i // nt, i % nt, 0, 0)
    b16 = (jax.ShapeDtypeStruct((n, D_A), BF16), pl.BlockSpec((tm, D_A), row))
    lft = (jax.ShapeDtypeStruct((b, N_A, t), F32), pl.BlockSpec((nb, N_A, seg), tok))
    vb = (jax.ShapeDtypeStruct((n * N_B, LANES), F32), pl.BlockSpec((tm * N_B, LANES), row))
    if tokens_minor:
        kv = (jax.ShapeDtypeStruct((b, D_A, t), F32), pl.BlockSpec((nb, D_A, seg), tok))
        k16 = (jax.ShapeDtypeStruct((b, t // kblk, D_A, kblk), BF16),
               pl.BlockSpec((nb, seg // kblk, D_A, kblk), blk))
    else:
        kv = (jax.ShapeDtypeStruct((n * N_A, HD_A), F32), pl.BlockSpec((tm * N_A, HD_A), row))
        k16 = b16
    outs = [kv, kv, lft, kv, vb, b16, k16, b16, b16, k16, b16]
    return pl.pallas_call(
        functools.partial(_proj_kernel, seg=seg, kblk=kblk, tokens_minor=tokens_minor),
        grid=(n // tm,),
        in_specs=[pl.BlockSpec((tm, D_MODEL), row), _resident((1, D_MODEL)), _resident(wa.shape),
                  _resident(wf.shape), _resident(bf.shape), _resident(wb.shape)],
        out_specs=[o[1] for o in outs],
        out_shape=[o[0] for o in outs],
        compiler_params=_params(1),
        name="proj",
    )(x, gn, wa, wf, bf, wb)


def _split3(x):
    hi = x.astype(BF16).astype(F32)
    r = x - hi
    mid = r.astype(BF16).astype(F32)
    lo = (r - mid).astype(BF16).astype(F32)
    return hi, mid, lo


def _cum_kernel(lft_ref, aug_ref, pad_ref, *, t, blocked):
    w = ATTN_BLOCK
    nb = lft_ref.shape[0]
    rows = nb * N_A
    pad_ref[...] = jnp.zeros(pad_ref.shape, F32)
    for bb in range(nb):
        pad_ref[bb * N_A:(bb + 1) * N_A, 0:t] = lft_ref[bb]
    r = lax.broadcasted_iota(jnp.int32, (w, w), 0)
    c = lax.broadcasted_iota(jnp.int32, (w, w), 1)
    tri = jnp.where(r <= c, 1.0, 0.0).astype(BF16)
    zero = jnp.zeros((N_A, w), F32)
    carry = jnp.zeros((rows, 1), F32)
    for s in range(pad_ref.shape[1] // w):
        lo_col = s * w
        cols = min(w, t - lo_col)
        x = jnp.concatenate(_split3(pad_ref[:, lo_col:lo_col + w]), axis=0).astype(BF16)
        y = jnp.dot(x, tri, preferred_element_type=F32)
        cs = y[0:rows] + y[rows:2 * rows] + y[2 * rows:3 * rows] + carry
        carry = cs[:, w - 1:w]
        pieces = _split3(cs * -LOG2E)
        for bb in range(nb):
            sl = slice(bb * N_A, (bb + 1) * N_A)
            aug = jnp.concatenate([p[sl] for p in pieces] + [zero], axis=0).astype(BF16)
            if blocked:
                aug_ref[bb, s] = aug
            else:
                aug_ref[bb, :, lo_col:lo_col + cols] = aug[:, :cols]


def _cum_aug(lft, blocked):
    b, _, t = lft.shape
    w = ATTN_BLOCK
    t_pad = -(-t // w) * w
    nb = math.gcd(b, SCAN_BATCH)
    if blocked:
        out_shape = jax.ShapeDtypeStruct((b, t // w, AUG_ROWS, w), BF16)
        out_spec = pl.BlockSpec((nb, t // w, AUG_ROWS, w), lambda i: (i, 0, 0, 0))
    else:
        out_shape = jax.ShapeDtypeStruct((b, AUG_ROWS, t), BF16)
        out_spec = pl.BlockSpec((nb, AUG_ROWS, t), lambda i: (i, 0, 0))
    return pl.pallas_call(
        functools.partial(_cum_kernel, t=t, blocked=blocked),
        grid=(b // nb,),
        in_specs=[pl.BlockSpec((nb, N_A, t), lambda i: (i, 0, 0))],
        out_specs=out_spec,
        out_shape=out_shape,
        scratch_shapes=[pltpu.VMEM((nb * N_A, t_pad), F32)],
        compiler_params=_params(1),
        name="cum_aug",
    )(lft)


def _lane_halves(q):
    lane = lax.broadcasted_iota(jnp.int32, q.shape, 1)
    zero = jnp.zeros_like(q)
    return jnp.where(lane < HD_A, q, zero), jnp.where(lane >= HD_A, q, zero)


def _piece_selector(shape, head, n_heads):
    lane = lax.broadcasted_iota(jnp.int32, shape, 1)
    hit = (lane == head) | (lane == n_heads + head) | (lane == 2 * n_heads + head)
    return jnp.where(hit, 1.0, 0.0).astype(BF16)


def _alibi_rows(k_pos):
    w = k_pos.shape[1]
    row = lax.broadcasted_iota(jnp.int32, (4 * N_B, 1), 0)
    head = row % N_B
    slope = jnp.zeros((4 * N_B, 1), F32)
    for h in range(N_B):
        slope = jnp.where(head == h, ALIBI_SLOPES[h] * LOG2E, slope)
    hi, mid, lo = _split3(slope * k_pos)
    piece = jnp.where(row < N_B, hi, jnp.where(row < 2 * N_B, mid, jnp.where(row < 3 * N_B, lo, 0.0)))
    return jnp.concatenate([piece.astype(BF16), jnp.zeros((LANES - 4 * N_B, w), BF16)], axis=0)


def _alibi_fold(slope, q_pos, k_pos):
    return (-2.0 * LOG2E * slope) * jnp.maximum(k_pos - q_pos, 0).astype(F32)


def _diff_finish(o1, o2, lam, sub_ref, lambda_init):
    return _rms(o1 - lam * o2, sub_ref[...]) * (1.0 - lambda_init)


def _lambda(lq1_ref, lk1_ref, lq2_ref, lk2_ref, lambda_init):
    return (jnp.exp(jnp.sum(lq1_ref[...] * lk1_ref[...], axis=1, keepdims=True))
            - jnp.exp(jnp.sum(lq2_ref[...] * lk2_ref[...], axis=1, keepdims=True)) + lambda_init)


def _attn_prompt_kernel(qa_ref, qb_ref, kat_ref, kbt_ref, va_ref, vb_ref, aug_ref,
                        lq1_ref, lk1_ref, lq2_ref, lk2_ref, sub_ref, oa_ref, ob_ref,
                        qs_ref, m_ref, acc_ref, tbl_ref, *, lambda_init):
    tq = ATTN_BLOCK
    n_chain = N_PAIRS + N_B

    @pl.when(pl.program_id(0) == 0)
    def _():
        r = lax.broadcasted_iota(jnp.int32, (2 * tq, tq), 0)
        c = lax.broadcasted_iota(jnp.int32, (2 * tq, tq), 1)
        q_idx = jnp.where(r >= tq, r - tq, r)
        tbl_ref[0] = jnp.where(q_idx >= c, 0.0, NEG_INF)
        visible = (q_idx // CHUNK) >= (c // CHUNK)
        for h in range(N_B):
            tbl_ref[1 + h] = jnp.where(visible, _alibi_fold(ALIBI_SLOPES[h], q_idx, c), NEG_INF)
        for qset in range(Q_BLOCKS_PER_STEP):
            for hc in range(n_chain):
                ch = qset * n_chain + hc
                if hc < N_PAIRS:
                    qs_ref[ch, 0:tq, LANES:] = _piece_selector((tq, LANES), 2 * hc, N_A)
                    qs_ref[ch, tq:, LANES:] = _piece_selector((tq, LANES), 2 * hc + 1, N_A)
                else:
                    sel = _piece_selector((tq, LANES), hc - N_PAIRS, N_B)
                    qs_ref[ch, 0:tq, LANES:] = sel
                    qs_ref[ch, tq:, LANES:] = sel

    def step(sets, first, width=1):
        wk = width * tq
        ones = jnp.ones((wk, LANES), BF16)
        for qset, j in sets:
            rows = pl.ds(pl.multiple_of(j * tq, tq), wk)
            blocks = [j + d for d in range(width)]
            aug_a = jnp.concatenate([jnp.concatenate([aug_ref[0, jb] for jb in blocks], axis=1),
                                     jnp.zeros((LANES - AUG_ROWS, wk), BF16)], axis=0)
            aug_b = _alibi_rows((j * tq + lax.broadcasted_iota(jnp.int32, (1, wk), 1)).astype(F32))
            for hc in range(n_chain):
                ch = qset * n_chain + hc
                if hc < N_PAIRS:
                    kt_ref, v_ref, g, aug, tbl = kat_ref, va_ref, hc, aug_a, 0
                else:
                    kt_ref, v_ref, g, aug, tbl = kbt_ref, vb_ref, hc - N_PAIRS, aug_b, 1 + hc - N_PAIRS
                kt = jnp.concatenate([kt_ref[0, jb, g * LANES:(g + 1) * LANES, :] for jb in blocks], axis=1)
                s = jnp.dot(qs_ref[ch], jnp.concatenate([kt, aug], axis=0), preferred_element_type=F32)
                v = v_ref[rows, g * LANES:(g + 1) * LANES]
                if first:
                    s = s + tbl_ref[tbl]
                    m_new = jnp.broadcast_to(jnp.max(s, axis=1, keepdims=True), (2 * tq, LANES))
                else:
                    m_prev = m_ref[ch]
                    m_new = jnp.maximum(m_prev, jnp.max(s, axis=1, keepdims=True))
                p = jnp.exp2(s - jnp.concatenate([m_new] * (wk // LANES), axis=1))
                pv = jnp.dot(p.astype(BF16), jnp.concatenate([v, ones], axis=1), preferred_element_type=F32)
                if first:
                    acc_ref[ch] = pv
                else:
                    alpha = jnp.exp2(m_prev - m_new)
                    acc_ref[ch] = jnp.concatenate([alpha, alpha], axis=1) * acc_ref[ch] + pv
                m_ref[ch] = m_new

    def wide_body(jj, carry):
        j = jj * KEY_BLOCKS_PER_STEP
        step([(0, j), (1, j)], False, KEY_BLOCKS_PER_STEP)
        return carry

    lane = lax.broadcasted_iota(jnp.int32, (tq, LANES), 1)
    lam = _lambda(lq1_ref, lk1_ref, lq2_ref, lk2_ref, lambda_init)

    def query_blocks(pair, carry):
        even = pair * Q_BLOCKS_PER_STEP
        for qset in range(Q_BLOCKS_PER_STEP):
            q_rows = pl.ds(pl.multiple_of((even + qset) * tq, tq), tq)
            for hc in range(n_chain):
                q_ref, g = (qa_ref, hc) if hc < N_PAIRS else (qb_ref, hc - N_PAIRS)
                lo, hi = _lane_halves(q_ref[q_rows, g * LANES:(g + 1) * LANES])
                qs_ref[qset * n_chain + hc, 0:tq, 0:LANES] = lo
                qs_ref[qset * n_chain + hc, tq:, 0:LANES] = hi
        step([(0, even), (1, even + 1)], True)
        step([(1, even)], False)
        lax.fori_loop(0, pair, wide_body, 0)
        for qset in range(Q_BLOCKS_PER_STEP):
            q_rows = pl.ds(pl.multiple_of((even + qset) * tq, tq), tq)
            for hc in range(n_chain):
                ch = qset * n_chain + hc
                o = acc_ref[ch, :, 0:LANES] / acc_ref[ch, :, LANES:]
                if hc < N_PAIRS:
                    oa_ref[q_rows, hc * LANES:(hc + 1) * LANES] = jnp.where(
                        lane < HD_A, o[:tq], o[tq:]).astype(BF16)
                else:
                    h = hc - N_PAIRS
                    ob_ref[q_rows, h * LANES:(h + 1) * LANES] = _diff_finish(
                        o[:tq], o[tq:], lam, sub_ref, lambda_init).astype(BF16)
        return carry

    lax.fori_loop(0, qa_ref.shape[0] // (Q_BLOCKS_PER_STEP * tq), query_blocks, 0)


def _attn_prompt(qa16, qb16, kat16, kbt16, va16, vb16, aug, lams, sub, b, t, lambda_init):
    tq = ATTN_BLOCK
    nq = t // tq
    kspec = pl.BlockSpec((1, nq, D_A, tq), lambda bi: (bi, 0, 0, 0))
    vspec = pl.BlockSpec((t, D_A), lambda bi: (bi, 0))
    small = [_resident((1, HD_B))] * 4 + [_resident((1, 2 * HD_B))]
    n_chain = Q_BLOCKS_PER_STEP * (N_PAIRS + N_B)
    return pl.pallas_call(
        functools.partial(_attn_prompt_kernel, lambda_init=lambda_init),
        grid=(b,),
        in_specs=[vspec, vspec, kspec, kspec, vspec, vspec,
                  pl.BlockSpec((1, nq, AUG_ROWS, tq), lambda bi: (bi, 0, 0, 0))] + small,
        out_specs=[vspec, vspec],
        out_shape=[jax.ShapeDtypeStruct((b * t, D_A), BF16), jax.ShapeDtypeStruct((b * t, D_B), BF16)],
        scratch_shapes=[pltpu.VMEM((n_chain, 2 * tq, 2 * LANES), BF16),
                        pltpu.VMEM((n_chain, 2 * tq, LANES), F32),
                        pltpu.VMEM((n_chain, 2 * tq, 2 * LANES), F32),
                        pltpu.VMEM((1 + N_B, 2 * tq, tq), F32)],
        compiler_params=_params(1),
        name="attn_prompt",
    )(qa16, qb16, kat16, kbt16, va16, vb16, aug, *lams, sub)


def _nt_dot(a, b):
    return lax.dot_general(a, b, (((1,), (1,)), ((), ())), preferred_element_type=F32)


def _attn_cached_kernel(qa_ref, qb_ref, ckat_ref, cvat_ref, ckbt_ref, cvb_ref,
                        ka_ref, va_ref, kb_ref, vb_ref, aug_ref,
                        lq1_ref, lk1_ref, lq2_ref, lk2_ref, sub_ref, oa_ref, ob_ref,
                        kn_ref, *, lambda_init):
    tq = qa_ref.shape[0]
    past = ckat_ref.shape[2]
    t_all = past + tq
    n_chain = N_PAIRS + N_B
    r = lax.broadcasted_iota(jnp.int32, (2 * tq, LANES), 0)
    c = lax.broadcasted_iota(jnp.int32, (2 * tq, LANES), 1)
    q_idx = jnp.where(r >= tq, r - tq, r)
    is_key = c < tq
    causal_tbl = jnp.where((q_idx >= c) & is_key, 0.0, NEG_INF)
    visible = (((past + q_idx) // CHUNK) >= ((past + c) // CHUNK)) & is_key
    lane = lax.broadcasted_iota(jnp.int32, (tq, LANES), 1)
    lam = _lambda(lq1_ref, lk1_ref, lq2_ref, lk2_ref, lambda_init)
    aug_c = jnp.concatenate([aug_ref[0, :, 0:past], jnp.zeros((LANES - AUG_ROWS, past), BF16)], axis=0)
    alibi_c = _alibi_rows(lax.broadcasted_iota(jnp.int32, (1, past), 1).astype(F32))
    alibi_n = _alibi_rows((past + lax.broadcasted_iota(jnp.int32, (1, LANES), 1)).astype(F32))
    kn_ref[...] = jnp.zeros(kn_ref.shape, BF16)
    row_pad = jnp.zeros((LANES - tq, LANES), BF16)

    def new_keys_t(k_new):
        return jnp.concatenate([k_new, row_pad], axis=0).astype(F32).T.astype(BF16)

    for ch in range(n_chain):
        fox = ch < N_PAIRS
        g = ch if fox else ch - N_PAIRS
        rows = slice(g * LANES, (g + 1) * LANES)
        if fox:
            lo, hi = _lane_halves(qa_ref[:, rows])
            sel_lo = _piece_selector((tq, LANES), 2 * g, N_A)
            sel_hi = _piece_selector((tq, LANES), 2 * g + 1, N_A)
            k_old = jnp.concatenate([ckat_ref[0, rows, :].astype(BF16), aug_c], axis=0)
            kn_ref[ch, 0:LANES, :] = new_keys_t(ka_ref[:, rows])
            kn_ref[ch, LANES:LANES + AUG_ROWS, 0:tq] = aug_ref[0, :, past:t_all]
            tbl = causal_tbl
        else:
            lo, hi = _lane_halves(qb_ref[:, rows])
            sel_lo = sel_hi = _piece_selector((tq, LANES), g, N_B)
            k_old = jnp.concatenate([ckbt_ref[0, rows, :].astype(BF16), alibi_c], axis=0)
            kn_ref[ch, 0:LANES, :] = new_keys_t(kb_ref[:, rows])
            kn_ref[ch, LANES:, :] = alibi_n
            tbl = jnp.where(visible, _alibi_fold(ALIBI_SLOPES[g], q_idx, c), NEG_INF)
        qs = jnp.concatenate([jnp.concatenate([lo, sel_lo], axis=1), jnp.concatenate([hi, sel_hi], axis=1)], axis=0)
        s_old = jnp.dot(qs, k_old, preferred_element_type=F32)
        s_new = jnp.dot(qs, kn_ref[ch], preferred_element_type=F32) + tbl
        m = jnp.maximum(jnp.max(s_old, axis=1, keepdims=True), jnp.max(s_new, axis=1, keepdims=True))
        p_old = jnp.exp2(s_old - m)
        p_new = jnp.exp2(s_new - m)
        l = jnp.sum(p_old, axis=1, keepdims=True) + jnp.sum(p_new, axis=1, keepdims=True)
        v_new = jnp.concatenate([(va_ref if fox else vb_ref)[:, rows], row_pad], axis=0)
        o_new = jnp.dot(p_new.astype(BF16), v_new, preferred_element_type=F32)
        if fox:
            o = (_nt_dot(p_old.astype(BF16), cvat_ref[0, rows, :].astype(BF16)) + o_new) / l
            oa_ref[:, rows] = jnp.where(lane < HD_A, o[:tq], o[tq:]).astype(BF16)
        else:
            v_old = cvb_ref[0, pl.ds(g, past, stride=N_B), :].astype(BF16)
            o = (jnp.dot(p_old.astype(BF16), v_old, preferred_element_type=F32) + o_new) / l
            ob_ref[:, rows] = _diff_finish(o[:tq], o[tq:], lam, sub_ref, lambda_init).astype(BF16)


def _attn_cached(qa16, qb16, ckat, cvat, ckbt, cvb, ka16, va16, kb16, vb16, aug, lams, sub, b, tq, lambda_init):
    past = ckat.shape[2]
    t_all = past + tq
    n_chain = N_PAIRS + N_B
    qspec = pl.BlockSpec((tq, D_A), lambda bi: (bi, 0))
    old_t = pl.BlockSpec((1, D_A, past), lambda bi: (bi, 0, 0))
    small = [_resident((1, HD_B))] * 4 + [_resident((1, 2 * HD_B))]
    return pl.pallas_call(
        functools.partial(_attn_cached_kernel, lambda_init=lambda_init),
        grid=(b,),
        in_specs=[qspec, qspec, old_t, old_t, old_t,
                  pl.BlockSpec((1, past * N_B, LANES), lambda bi: (bi, 0, 0)),
                  qspec, qspec, qspec, qspec,
                  pl.BlockSpec((1, AUG_ROWS, t_all), lambda bi: (bi, 0, 0))] + small,
        out_specs=[qspec, qspec],
        out_shape=[jax.ShapeDtypeStruct((b * tq, D_A), BF16), jax.ShapeDtypeStruct((b * tq, D_B), BF16)],
        scratch_shapes=[pltpu.VMEM((n_chain, 2 * LANES, LANES), BF16)],
        compiler_params=_params(1),
        name="attn_cached",
    )(qa16, qb16, ckat, cvat, ckbt, cvb, ka16, va16, kb16, vb16, aug, *lams, sub)


def _tokens_minor(x):
    b, t = x.shape[:2]
    return jnp.moveaxis(x.reshape(b, t, -1), 1, 2)


def _tokens_major(xt, tail):
    b, _, t = xt.shape
    return jnp.moveaxis(xt, 2, 1).reshape(b, t, *tail)


def _layer(x, cache, w, lambda_init, final):
    b, t, _ = x.shape
    n = b * t
    cast = w["ffn16"] is None
    if cast:
        x1, w1a, w2a = _ffn1(x.reshape(n, D_MODEL), w["gn1"], w["w1a"], w["w2a"], cast=True)
    else:
        w1a, w2a, wo, w1b, w2b = w["ffn16"]
        x1 = _ffn1(x.reshape(n, D_MODEL), w["gn1"], w1a, w2a, cast=False)
    (k_a, v_a, lft, k_b, vb, qa16, ka16, va16, qb16, kb16, vb16) = _proj(
        x1, w["gmix"], w["wa"], w["wf"], w["bf"], w["wb"], b, t, tokens_minor=cache is None)
    if cache is None:
        aug = _cum_aug(lft, blocked=True)
        oa, ob = _attn_prompt(qa16, qb16, ka16, kb16, va16, vb16, aug, w["lams"], w["sub"], b, t, lambda_init)
        news = (_tokens_major(k_a, (N_A, HD_A)), _tokens_major(v_a, (N_A, HD_A)), _tokens_major(lft, (N_A,)),
                _tokens_major(k_b, (N_B, 2, HD_B)), vb.reshape(b, t, N_B, 2 * HD_B))
    else:
        cfk, cfv, cflf, cdk, cdv = cache
        past = cfk.shape[1]
        aug = _cum_aug(jnp.concatenate([_tokens_minor(cflf.astype(F32)), lft], axis=2), blocked=False)
        oa, ob = _attn_cached(qa16, qb16, _tokens_minor(cfk), _tokens_minor(cfv), _tokens_minor(cdk),
                              cdv.reshape(b, past * N_B, 2 * HD_B), ka16, va16, kb16, vb16, aug,
                              w["lams"], w["sub"], b, t, lambda_init)
        news = (k_a.reshape(b, t, N_A, HD_A), v_a.reshape(b, t, N_A, HD_A), _tokens_major(lft, (N_A,)),
                k_b.reshape(b, t, N_B, 2, HD_B), vb.reshape(b, t, N_B, 2 * HD_B))
    if cast:
        y, wo, w1b, w2b = _post(x1, oa, ob, w["gn2"], w["gfin"], w["wo"], w["w1b"], w["w2b"], final, cast=True)
        w["ffn16"] = (w1a, w2a, wo, w1b, w2b)
    else:
        y = _post(x1, oa, ob, w["gn2"], w["gfin"], wo, w1b, w2b, final, cast=False)
    return y.reshape(b, t, D_MODEL), news


def kernel(x_prompt, x_sample, cache_fox_k, cache_fox_v, cache_fox_logf, cache_diff_k, cache_diff_v,
           norm_ffn1, w_ffn1_in, w_ffn1_out, norm_mix, w_in, b_forget,
           lambda_q1, lambda_k1, lambda_q2, lambda_k2, diff_subln, w_out,
           norm_ffn2, w_ffn2_in, w_ffn2_out, norm_final):
    depth = w_in.shape[0]
    xp, xs = x_prompt, x_sample
    outs_p, outs_s = [], []
    gfin = norm_final.reshape(1, D_MODEL)
    for l in range(depth):
        lambda_init = 0.8 - 0.6 * math.exp(-0.3 * l)
        f_lo, f_hi = 3 * D_A, 3 * D_A + N_A
        w = {
            "gn1": norm_ffn1[l].reshape(1, D_MODEL),
            "w1a": w_ffn1_in[l], "w2a": w_ffn1_out[l],
            "gmix": norm_mix[l].reshape(1, D_MODEL),
            "wa": w_in[l][:, :f_lo].astype(BF16),
            "wf": jnp.pad(w_in[l][:, f_lo:f_hi], ((0, 0), (0, LANES - N_A))).astype(BF16),
            "bf": jnp.pad(b_forget[l], (0, LANES - N_A)).reshape(1, LANES),
            "wb": w_in[l][:, f_hi:].astype(BF16),
            "lams": [v[l].reshape(1, HD_B) for v in (lambda_q1, lambda_k1, lambda_q2, lambda_k2)],
            "sub": diff_subln[l].reshape(1, 2 * HD_B),
            "wo": w_out[l],
            "gn2": norm_ffn2[l].reshape(1, D_MODEL),
            "w1b": w_ffn2_in[l], "w2b": w_ffn2_out[l],
            "gfin": gfin,
            "ffn16": None,
        }
        streams = []
        for x, cache in ((xp, None),
                         (xs, (cache_fox_k[l], cache_fox_v[l], cache_fox_logf[l],
                               cache_diff_k[l], cache_diff_v[l]))):
            streams.append(_layer(x, cache, w, lambda_init, l == depth - 1))
        (xp, news_p), (xs, news_s) = streams
        outs_p.append(news_p)
        outs_s.append(news_s)
    stack = lambda outs, i: jnp.stack([o[i] for o in outs])
    return (xp, xs) + tuple(stack(outs_p, i) for i in range(5)) + tuple(stack(outs_s, i) for i in range(5))
```

```python
import functools
import math

import jax
import jax.numpy as jnp
from jax import lax
from jax.experimental import pallas as pl
from jax.experimental.pallas import tpu as pltpu

D_MODEL = 1024
D_FF = 2816
N_A = 8
HD_A = 64
N_B = 4
HD_B = 64
D_A = N_A * HD_A
D_B = N_B * 2 * HD_B
CHUNK = 64
NORM_EPS = 1e-6
NEG_INF = -1e30
ALIBI_SLOPES = (0.25, 0.0625, 0.015625, 0.00390625)

LOG2E = math.log2(math.e)
LANES = 128
MXU_WIDTH = 256
FF_CHUNKS = (1536, 1280)
assert sum(FF_CHUNKS) == D_FF and all(c % MXU_WIDTH == 0 for c in FF_CHUNKS)
SCAN_BATCH = 8
CAST_BYTES = 3 << 20
TOKEN_TILE = 512
PROJ_TILE = 1024
ATTN_BLOCK = 256
KEY_BLOCKS_PER_STEP = 2
Q_BLOCKS_PER_STEP = 2
assert KEY_BLOCKS_PER_STEP == Q_BLOCKS_PER_STEP == 2
AUG_ROWS = 32
VMEM_LIMIT = 56 * 1024 * 1024
N_PAIRS = N_A // 2

F32 = jnp.float32
BF16 = jnp.bfloat16


def _params(n_axes):
    return pltpu.CompilerParams(dimension_semantics=("arbitrary",) * n_axes,
                                vmem_limit_bytes=VMEM_LIMIT)


def _resident(shape):
    return pl.BlockSpec(shape, lambda *_: (0,) * len(shape), pipeline_mode=pl.Buffered(1))


def _rms(xf, g):
    ms = jnp.mean(xf * xf, axis=-1, keepdims=True)
    return xf * lax.rsqrt(ms + NORM_EPS) * g


def _swiglu_residual(x, gn_ref, w1_ref, w2_ref):
    h = _rms(x, gn_ref[...]).astype(BF16)
    acc = None
    lo = 0
    for width in FF_CHUNKS:
        g = jnp.dot(h, w1_ref[:, lo:lo + width], preferred_element_type=F32)
        u = jnp.dot(h, w1_ref[:, D_FF + lo:D_FF + lo + width], preferred_element_type=F32)
        a = (g * jax.nn.sigmoid(g) * u).astype(BF16)
        d = jnp.dot(a, w2_ref[lo:lo + width, :], preferred_element_type=F32)
        acc = d if acc is None else acc + d
        lo += width
    return x + 0.5 * acc


def _cast_rows(rows_total, cols):
    rows = max(16, CAST_BYTES // (4 * cols) // 16 * 16)
    while rows_total % rows:
        rows -= 16
    return rows


def _stage_shape(shapes):
    return {c: (2, max(_cast_rows(r, cc) for r, cc in shapes if cc == c), c) for _, c in shapes}


def _cast_weight(w_hbm, w16_ref, stage_ref, sem_ref):
    r_total, cols = w_hbm.shape
    rows = _cast_rows(r_total, cols)
    n = r_total // rows

    def copy(c):
        return pltpu.make_async_copy(w_hbm.at[pl.ds(c * rows, rows), :],
                                     stage_ref.at[c % 2, pl.ds(0, rows), :], sem_ref.at[c % 2])

    copy(0).start()
    for c in range(n):
        if c + 1 < n:
            copy(c + 1).start()
        copy(c).wait()
        w16_ref[pl.ds(c * rows, rows), :] = stage_ref[c % 2, 0:rows, :].astype(BF16)


def _weights_prologue(i, w_hbm_refs, w16_refs, w16_hbm_refs, stages, sem_in, sem_out):
    def publish(k):
        return pltpu.make_async_copy(w16_refs[k], w16_hbm_refs[k], sem_out.at[k])

    @pl.when(i == 0)
    def _():
        for w_hbm, w16 in zip(w_hbm_refs, w16_refs):
            _cast_weight(w_hbm, w16, stages[w_hbm.shape[1]], sem_in)
        for k in range(len(w16_refs)):
            publish(k).start()

    return publish


def _weights_epilogue(i, publish, n_weights):
    @pl.when(i == pl.num_programs(0) - 1)
    def _():
        for k in range(n_weights):
            publish(k).wait()


def _weight_plumbing(weights, cast):
    if not cast:
        return [_resident(w.shape) for w in weights], [], [], []
    any_spec = pl.BlockSpec(memory_space=pl.ANY)
    shapes = [w.shape for w in weights]
    stages = _stage_shape(shapes)
    scratch = ([pltpu.VMEM(sh, BF16) for sh in shapes] + [pltpu.VMEM(stages[c], F32) for c in sorted(stages)]
               + [pltpu.SemaphoreType.DMA((2,)), pltpu.SemaphoreType.DMA((len(weights),))])
    return ([any_spec] * len(weights), [any_spec] * len(weights),
            [jax.ShapeDtypeStruct(sh, BF16) for sh in shapes], scratch)


def _split_weight_refs(refs, n_weights, shapes):
    w16 = refs[:n_weights]
    cols = sorted({c for _, c in shapes})
    stage_refs = refs[n_weights:n_weights + len(cols)]
    sem_in, sem_out = refs[n_weights + len(cols):]
    return w16, dict(zip(cols, stage_refs)), sem_in, sem_out


def _ffn1_kernel(x_ref, gn_ref, w1_in, w2_in, o_ref, *rest, cast):
    if cast:
        i = pl.program_id(0)
        w_hbm = (w1_in, w2_in)
        w16, stages, sem_in, sem_out = _split_weight_refs(rest[2:], 2, [w.shape for w in w_hbm])
        publish = _weights_prologue(i, w_hbm, w16, rest[:2], stages, sem_in, sem_out)
        o_ref[...] = _swiglu_residual(x_ref[...], gn_ref, *w16)
        _weights_epilogue(i, publish, 2)
    else:
        o_ref[...] = _swiglu_residual(x_ref[...], gn_ref, w1_in, w2_in)


def _ffn1(x, gn, w1, w2, cast):
    n = x.shape[0]
    tm = min(TOKEN_TILE, n)
    w_specs, w_out_specs, w_out_shapes, scratch = _weight_plumbing([w1, w2], cast)
    out = pl.pallas_call(
        functools.partial(_ffn1_kernel, cast=cast),
        grid=(n // tm,),
        in_specs=[pl.BlockSpec((tm, D_MODEL), lambda i: (i, 0)), _resident((1, D_MODEL))] + w_specs,
        out_specs=[pl.BlockSpec((tm, D_MODEL), lambda i: (i, 0))] + w_out_specs,
        out_shape=[jax.ShapeDtypeStruct((n, D_MODEL), F32)] + w_out_shapes,
        scratch_shapes=scratch,
        compiler_params=_params(1),
        name="ffn1",
    )(x, gn, w1, w2)
    return out if cast else out[0]


def _post_kernel(x_ref, oa_ref, ob_ref, gn_ref, gf_ref, wo_in, w1_in, w2_in, y_ref, *rest, final, cast):
    if cast:
        i = pl.program_id(0)
        w_hbm = (wo_in, w1_in, w2_in)
        w16, stages, sem_in, sem_out = _split_weight_refs(rest[3:], 3, [w.shape for w in w_hbm])
        publish = _weights_prologue(i, w_hbm, w16, rest[:3], stages, sem_in, sem_out)
        wo_ref, w1_ref, w2_ref = w16
    else:
        wo_ref, w1_ref, w2_ref = wo_in, w1_in, w2_in
    x = x_ref[...]
    x = x + jnp.dot(oa_ref[...], wo_ref[:D_A, :], preferred_element_type=F32)
    x = x + jnp.dot(ob_ref[...], wo_ref[D_A:, :], preferred_element_type=F32)
    x = _swiglu_residual(x, gn_ref, w1_ref, w2_ref)
    y_ref[...] = _rms(x, gf_ref[...]) if final else x
    if cast:
        _weights_epilogue(i, publish, 3)


def _post(x, oa, ob, gn, gf, wo, w1, w2, final, cast):
    n = x.shape[0]
    tm = min(TOKEN_TILE, n)
    row = lambda i: (i, 0)
    w_specs, w_out_specs, w_out_shapes, scratch = _weight_plumbing([wo, w1, w2], cast)
    out = pl.pallas_call(
        functools.partial(_post_kernel, final=final, cast=cast),
        grid=(n // tm,),
        in_specs=[pl.BlockSpec((tm, D_MODEL), row), pl.BlockSpec((tm, D_A), row), pl.BlockSpec((tm, D_B), row),
                  _resident((1, D_MODEL)), _resident((1, D_MODEL))] + w_specs,
        out_specs=[pl.BlockSpec((tm, D_MODEL), row)] + w_out_specs,
        out_shape=[jax.ShapeDtypeStruct((n, D_MODEL), F32)] + w_out_shapes,
        scratch_shapes=scratch,
        compiler_params=_params(1),
        name="post",
    )(x, oa, ob, gn, gf, wo, w1, w2)
    return out if cast else out[0]


def _proj_weights_prologue(i, wt_hbm, w16_refs, w16_hbm_refs, stage_ref, stage_f_ref, sem_in, sem_out):
    wa16, wf16, wb16 = w16_refs
    rows = stage_ref.shape[1]

    def publish(k):
        return pltpu.make_async_copy(w16_refs[k], w16_hbm_refs[k], sem_out.at[k])

    @pl.when(i == 0)
    def _():
        f_lo = 3 * D_A
        jobs = [(r0 + c * rows, w16, c) for r0, w16 in ((0, wa16), (f_lo + N_A, wb16)) for c in range(f_lo // rows)]

        def copy(n):
            return pltpu.make_async_copy(wt_hbm.at[pl.ds(jobs[n][0], rows), :], stage_ref.at[n % 2], sem_in.at[n % 2])

        gate = pltpu.make_async_copy(wt_hbm.at[pl.ds(f_lo, N_A), :], stage_f_ref.at[pl.ds(0, N_A), :], sem_in.at[2])
        gate.start()
        copy(0).start()
        for n, (_, w16, c) in enumerate(jobs):
            if n + 1 < len(jobs):
                copy(n + 1).start()
            copy(n).wait()
            w16[:, c * rows:(c + 1) * rows] = stage_ref[n % 2].T.astype(BF16)
        stage_f_ref[N_A:, :] = jnp.zeros((LANES - N_A, D_MODEL), F32)
        gate.wait()
        wf16[...] = stage_f_ref[...].T.astype(BF16)
        for k in range(3):
            publish(k).start()

    return publish


def _proj_kernel(x_ref, gn_ref, bf_ref, *refs, seg, kblk, tokens_minor, cast):
    n_out = 11
    if cast:
        i = pl.program_id(0)
        wt_hbm, out_refs, w16_hbm = refs[0], refs[1:1 + n_out], refs[1 + n_out:4 + n_out]
        wa_ref, wf_ref, wb_ref, stage_ref, stage_f_ref, sem_in, sem_out = refs[4 + n_out:]
        publish = _proj_weights_prologue(i, wt_hbm, (wa_ref, wf_ref, wb_ref), w16_hbm,
                                         stage_ref, stage_f_ref, sem_in, sem_out)
    else:
        wa_ref, wf_ref, wb_ref = refs[:3]
        out_refs = refs[3:]
    tm = x_ref.shape[0]
    h = _rms(x_ref[...], gn_ref[...]).astype(BF16)
    pa = jnp.dot(h, wa_ref[...], preferred_element_type=F32)
    pb = jnp.dot(h, wb_ref[...], preferred_element_type=F32)
    z = jnp.dot(h, wf_ref[...], preferred_element_type=F32) + bf_ref[...]
    lft = (jnp.minimum(z, 0.0) - jnp.log1p(jnp.exp(-jnp.abs(z)))).T
    ka, va = pa[:, D_A:2 * D_A], pa[:, 2 * D_A:]
    kb, vb = pb[:, D_B:2 * D_B], pb[:, 2 * D_B:]
    qa16 = (pa[:, :D_A] * (HD_A ** -0.5 * LOG2E)).astype(BF16)
    qb16 = (pb[:, :D_B] * (HD_B ** -0.5 * LOG2E)).astype(BF16)
    if tokens_minor:
        kat_ref, vat_ref, lft_ref, kbt_ref, vb_ref, qa16_ref, kat16_ref, va16_ref, qb16_ref, kbt16_ref, vb16_ref = out_refs
        kat, vat, kbt = ka.T, va.T, kb.T
    else:
        ka_ref, va_ref, lft_ref, kb_ref, vb_ref, qa16_ref, ka16_ref, va16_ref, qb16_ref, kb16_ref, vb16_ref = out_refs
        ka16_ref[...] = ka.astype(BF16)
        kb16_ref[...] = kb.astype(BF16)
        for hh in range(N_A):
            cols = slice(hh * HD_A, (hh + 1) * HD_A)
            ka_ref[pl.ds(hh, tm, stride=N_A), :] = ka[:, cols]
            va_ref[pl.ds(hh, tm, stride=N_A), :] = va[:, cols]
            kb_ref[pl.ds(hh, tm, stride=N_A), :] = kb[:, cols]
    qa16_ref[...] = qa16
    qb16_ref[...] = qb16
    va16_ref[...] = va.astype(BF16)
    vb16_ref[...] = vb.astype(BF16)
    for hh in range(N_B):
        vb_ref[pl.ds(hh, tm, stride=N_B), :] = vb[:, hh * LANES:(hh + 1) * LANES]
    for bb in range(tm // seg):
        cols = slice(bb * seg, (bb + 1) * seg)
        lft_ref[bb] = lft[:N_A, cols]
        if tokens_minor:
            kat_ref[bb] = kat[:, cols]
            vat_ref[bb] = vat[:, cols]
            kbt_ref[bb] = kbt[:, cols]
            for jj in range(seg // kblk):
                kc = slice(bb * seg + jj * kblk, bb * seg + (jj + 1) * kblk)
                kat16_ref[bb, jj] = kat[:, kc].astype(BF16)
                kbt16_ref[bb, jj] = kbt[:, kc].astype(BF16)
    if cast:
        _weights_epilogue(i, publish, 3)


def _proj(x, gn, bf, weights, b, t, tokens_minor):
    cast = not isinstance(weights, (tuple, list))
    n = x.shape[0]
    tm = min(PROJ_TILE if tokens_minor else TOKEN_TILE, n)
    seg = min(tm, t)
    nb = tm // seg
    nt = t // seg
    kblk = min(ATTN_BLOCK, seg)
    row = lambda i: (i, 0)
    tok = lambda i: (i // nt, 0, i % nt)
    blk = lambda i: (i // nt, i % nt, 0, 0)
    b16 = (jax.ShapeDtypeStruct((n, D_A), BF16), pl.BlockSpec((tm, D_A), row))
    lft = (jax.ShapeDtypeStruct((b, N_A, t), F32), pl.BlockSpec((nb, N_A, seg), tok))
    vb = (jax.ShapeDtypeStruct((n * N_B, LANES), F32), pl.BlockSpec((tm * N_B, LANES), row))
    if tokens_minor:
        kv = (jax.ShapeDtypeStruct((b, D_A, t), F32), pl.BlockSpec((nb, D_A, seg), tok))
        k16 = (jax.ShapeDtypeStruct((b, t // kblk, D_A, kblk), BF16),
               pl.BlockSpec((nb, seg // kblk, D_A, kblk), blk))
    else:
        kv = (jax.ShapeDtypeStruct((n * N_A, HD_A), F32), pl.BlockSpec((tm * N_A, HD_A), row))
        k16 = b16
    outs = [kv, kv, lft, kv, vb, b16, k16, b16, b16, k16, b16]
    group_shapes = [(D_MODEL, 3 * D_A), (D_MODEL, LANES), (D_MODEL, 3 * D_B)]
    if cast:
        any_spec = pl.BlockSpec(memory_space=pl.ANY)
        w_specs = [any_spec]
        outs = outs + [(jax.ShapeDtypeStruct(sh, BF16), any_spec) for sh in group_shapes]
        scratch = ([pltpu.VMEM(sh, BF16) for sh in group_shapes]
                   + [pltpu.VMEM((2, 3 * D_A // 2, D_MODEL), F32), pltpu.VMEM((LANES, D_MODEL), F32),
                      pltpu.SemaphoreType.DMA((3,)), pltpu.SemaphoreType.DMA((3,))])
        weights = [weights]
    else:
        w_specs = [_resident(sh) for sh in group_shapes]
        scratch = []
    res = pl.pallas_call(
        functools.partial(_proj_kernel, seg=seg, kblk=kblk, tokens_minor=tokens_minor, cast=cast),
        grid=(n // tm,),
        in_specs=[pl.BlockSpec((tm, D_MODEL), row), _resident((1, D_MODEL)), _resident(bf.shape)] + w_specs,
        out_specs=[o[1] for o in outs],
        out_shape=[o[0] for o in outs],
        scratch_shapes=scratch,
        compiler_params=_params(1),
        name="proj",
    )(x, gn, bf, *weights)
    return res[:11], tuple(res[11:])


def _split3(x):
    hi = x.astype(BF16).astype(F32)
    r = x - hi
    mid = r.astype(BF16).astype(F32)
    lo = (r - mid).astype(BF16).astype(F32)
    return hi, mid, lo


def _cum_kernel(lft_ref, aug_ref, pad_ref, *, t, blocked):
    w = ATTN_BLOCK
    nb = lft_ref.shape[0]
    rows = nb * N_A
    pad_ref[...] = jnp.zeros(pad_ref.shape, F32)
    for bb in range(nb):
        pad_ref[bb * N_A:(bb + 1) * N_A, 0:t] = lft_ref[bb]
    r = lax.broadcasted_iota(jnp.int32, (w, w), 0)
    c = lax.broadcasted_iota(jnp.int32, (w, w), 1)
    tri = jnp.where(r <= c, 1.0, 0.0).astype(BF16)
    zero = jnp.zeros((N_A, w), F32)
    carry = jnp.zeros((rows, 1), F32)
    for s in range(pad_ref.shape[1] // w):
        lo_col = s * w
        cols = min(w, t - lo_col)
        x = jnp.concatenate(_split3(pad_ref[:, lo_col:lo_col + w]), axis=0).astype(BF16)
        y = jnp.dot(x, tri, preferred_element_type=F32)
        cs = y[0:rows] + y[rows:2 * rows] + y[2 * rows:3 * rows] + carry
        carry = cs[:, w - 1:w]
        pieces = _split3(cs * -LOG2E)
        for bb in range(nb):
            sl = slice(bb * N_A, (bb + 1) * N_A)
            aug = jnp.concatenate([p[sl] for p in pieces] + [zero], axis=0).astype(BF16)
            if blocked:
                aug_ref[bb, s] = aug
            else:
                aug_ref[bb, :, lo_col:lo_col + cols] = aug[:, :cols]


def _cum_aug(lft, blocked):
    b, _, t = lft.shape
    w = ATTN_BLOCK
    t_pad = -(-t // w) * w
    nb = math.gcd(b, SCAN_BATCH)
    if blocked:
        out_shape = jax.ShapeDtypeStruct((b, t // w, AUG_ROWS, w), BF16)
        out_spec = pl.BlockSpec((nb, t // w, AUG_ROWS, w), lambda i: (i, 0, 0, 0))
    else:
        out_shape = jax.ShapeDtypeStruct((b, AUG_ROWS, t), BF16)
        out_spec = pl.BlockSpec((nb, AUG_ROWS, t), lambda i: (i, 0, 0))
    return pl.pallas_call(
        functools.partial(_cum_kernel, t=t, blocked=blocked),
        grid=(b // nb,),
        in_specs=[pl.BlockSpec((nb, N_A, t), lambda i: (i, 0, 0))],
        out_specs=out_spec,
        out_shape=out_shape,
        scratch_shapes=[pltpu.VMEM((nb * N_A, t_pad), F32)],
        compiler_params=_params(1),
        name="cum_aug",
    )(lft)


def _lane_halves(q):
    lane = lax.broadcasted_iota(jnp.int32, q.shape, 1)
    zero = jnp.zeros_like(q)
    return jnp.where(lane < HD_A, q, zero), jnp.where(lane >= HD_A, q, zero)


def _piece_selector(shape, head, n_heads):
    lane = lax.broadcasted_iota(jnp.int32, shape, 1)
    hit = (lane == head) | (lane == n_heads + head) | (lane == 2 * n_heads + head)
    return jnp.where(hit, 1.0, 0.0).astype(BF16)


def _alibi_rows(k_pos):
    w = k_pos.shape[1]
    row = lax.broadcasted_iota(jnp.int32, (4 * N_B, 1), 0)
    head = row % N_B
    slope = jnp.zeros((4 * N_B, 1), F32)
    for h in range(N_B):
        slope = jnp.where(head == h, ALIBI_SLOPES[h] * LOG2E, slope)
    hi, mid, lo = _split3(slope * k_pos)
    piece = jnp.where(row < N_B, hi, jnp.where(row < 2 * N_B, mid, jnp.where(row < 3 * N_B, lo, 0.0)))
    return jnp.concatenate([piece.astype(BF16), jnp.zeros((LANES - 4 * N_B, w), BF16)], axis=0)


def _alibi_fold(slope, q_pos, k_pos):
    return (-2.0 * LOG2E * slope) * jnp.maximum(k_pos - q_pos, 0).astype(F32)


def _diff_finish(o1, o2, lam, sub_ref, lambda_init):
    return _rms(o1 - lam * o2, sub_ref[...]) * (1.0 - lambda_init)


def _lambda(lq1_ref, lk1_ref, lq2_ref, lk2_ref, lambda_init):
    return (jnp.exp(jnp.sum(lq1_ref[...] * lk1_ref[...], axis=1, keepdims=True))
            - jnp.exp(jnp.sum(lq2_ref[...] * lk2_ref[...], axis=1, keepdims=True)) + lambda_init)


def _attn_prompt_kernel(qa_ref, qb_ref, kat_ref, kbt_ref, va_ref, vb_ref, aug_ref,
                        lq1_ref, lk1_ref, lq2_ref, lk2_ref, sub_ref, oa_ref, ob_ref,
                        qs_ref, m_ref, acc_ref, tbl_ref, *, lambda_init):
    tq = ATTN_BLOCK
    n_chain = N_PAIRS + N_B

    @pl.when(pl.program_id(0) == 0)
    def _():
        r = lax.broadcasted_iota(jnp.int32, (2 * tq, tq), 0)
        c = lax.broadcasted_iota(jnp.int32, (2 * tq, tq), 1)
        q_idx = jnp.where(r >= tq, r - tq, r)
        tbl_ref[0] = jnp.where(q_idx >= c, 0.0, NEG_INF)
        visible = (q_idx // CHUNK) >= (c // CHUNK)
        for h in range(N_B):
            tbl_ref[1 + h] = jnp.where(visible, _alibi_fold(ALIBI_SLOPES[h], q_idx, c), NEG_INF)
        for qset in range(Q_BLOCKS_PER_STEP):
            for hc in range(n_chain):
                ch = qset * n_chain + hc
                if hc < N_PAIRS:
                    qs_ref[ch, 0:tq, LANES:] = _piece_selector((tq, LANES), 2 * hc, N_A)
                    qs_ref[ch, tq:, LANES:] = _piece_selector((tq, LANES), 2 * hc + 1, N_A)
                else:
                    sel = _piece_selector((tq, LANES), hc - N_PAIRS, N_B)
                    qs_ref[ch, 0:tq, LANES:] = sel
                    qs_ref[ch, tq:, LANES:] = sel

    def step(sets, first, width=1):
        wk = width * tq
        ones = jnp.ones((wk, LANES), BF16)
        for qset, j in sets:
            rows = pl.ds(pl.multiple_of(j * tq, tq), wk)
            blocks = [j + d for d in range(width)]
            aug_a = jnp.concatenate([jnp.concatenate([aug_ref[0, jb] for jb in blocks], axis=1),
                                     jnp.zeros((LANES - AUG_ROWS, wk), BF16)], axis=0)
            aug_b = _alibi_rows((j * tq + lax.broadcasted_iota(jnp.int32, (1, wk), 1)).astype(F32))
            for hc in range(n_chain):
                ch = qset * n_chain + hc
                if hc < N_PAIRS:
                    kt_ref, v_ref, g, aug, tbl = kat_ref, va_ref, hc, aug_a, 0
                else:
                    kt_ref, v_ref, g, aug, tbl = kbt_ref, vb_ref, hc - N_PAIRS, aug_b, 1 + hc - N_PAIRS
                kt = jnp.concatenate([kt_ref[0, jb, g * LANES:(g + 1) * LANES, :] for jb in blocks], axis=1)
                s = jnp.dot(qs_ref[ch], jnp.concatenate([kt, aug], axis=0), preferred_element_type=F32)
                v = v_ref[rows, g * LANES:(g + 1) * LANES]
                if first:
                    s = s + tbl_ref[tbl]
                    m_new = jnp.broadcast_to(jnp.max(s, axis=1, keepdims=True), (2 * tq, LANES))
                else:
                    m_prev = m_ref[ch]
                    m_new = jnp.maximum(m_prev, jnp.max(s, axis=1, keepdims=True))
                p = jnp.exp2(s - jnp.concatenate([m_new] * (wk // LANES), axis=1))
                pv = jnp.dot(p.astype(BF16), jnp.concatenate([v, ones], axis=1), preferred_element_type=F32)
                if first:
                    acc_ref[ch] = pv
                else:
                    alpha = jnp.exp2(m_prev - m_new)
                    acc_ref[ch] = jnp.concatenate([alpha, alpha], axis=1) * acc_ref[ch] + pv
                m_ref[ch] = m_new

    def wide_body(jj, carry):
        j = jj * KEY_BLOCKS_PER_STEP
        step([(0, j), (1, j)], False, KEY_BLOCKS_PER_STEP)
        return carry

    lane = lax.broadcasted_iota(jnp.int32, (tq, LANES), 1)
    lam = _lambda(lq1_ref, lk1_ref, lq2_ref, lk2_ref, lambda_init)

    def query_blocks(pair, carry):
        even = pair * Q_BLOCKS_PER_STEP
        for qset in range(Q_BLOCKS_PER_STEP):
            q_rows = pl.ds(pl.multiple_of((even + qset) * tq, tq), tq)
            for hc in range(n_chain):
                q_ref, g = (qa_ref, hc) if hc < N_PAIRS else (qb_ref, hc - N_PAIRS)
                lo, hi = _lane_halves(q_ref[q_rows, g * LANES:(g + 1) * LANES])
                qs_ref[qset * n_chain + hc, 0:tq, 0:LANES] = lo
                qs_ref[qset * n_chain + hc, tq:, 0:LANES] = hi
        step([(0, even), (1, even + 1)], True)
        step([(1, even)], False)
        lax.fori_loop(0, pair, wide_body, 0)
        for qset in range(Q_BLOCKS_PER_STEP):
            q_rows = pl.ds(pl.multiple_of((even + qset) * tq, tq), tq)
            for hc in range(n_chain):
                ch = qset * n_chain + hc
                o = acc_ref[ch, :, 0:LANES] / acc_ref[ch, :, LANES:]
                if hc < N_PAIRS:
                    oa_ref[q_rows, hc * LANES:(hc + 1) * LANES] = jnp.where(
                        lane < HD_A, o[:tq], o[tq:]).astype(BF16)
                else:
                    h = hc - N_PAIRS
                    ob_ref[q_rows, h * LANES:(h + 1) * LANES] = _diff_finish(
                        o[:tq], o[tq:], lam, sub_ref, lambda_init).astype(BF16)
        return carry

    lax.fori_loop(0, qa_ref.shape[0] // (Q_BLOCKS_PER_STEP * tq), query_blocks, 0)


def _attn_prompt(qa16, qb16, kat16, kbt16, va16, vb16, aug, lams, sub, b, t, lambda_init):
    tq = ATTN_BLOCK
    nq = t // tq
    kspec = pl.BlockSpec((1, nq, D_A, tq), lambda bi: (bi, 0, 0, 0))
    vspec = pl.BlockSpec((t, D_A), lambda bi: (bi, 0))
    small = [_resident((1, HD_B))] * 4 + [_resident((1, 2 * HD_B))]
    n_chain = Q_BLOCKS_PER_STEP * (N_PAIRS + N_B)
    return pl.pallas_call(
        functools.partial(_attn_prompt_kernel, lambda_init=lambda_init),
        grid=(b,),
        in_specs=[vspec, vspec, kspec, kspec, vspec, vspec,
                  pl.BlockSpec((1, nq, AUG_ROWS, tq), lambda bi: (bi, 0, 0, 0))] + small,
        out_specs=[vspec, vspec],
        out_shape=[jax.ShapeDtypeStruct((b * t, D_A), BF16), jax.ShapeDtypeStruct((b * t, D_B), BF16)],
        scratch_shapes=[pltpu.VMEM((n_chain, 2 * tq, 2 * LANES), BF16),
                        pltpu.VMEM((n_chain, 2 * tq, LANES), F32),
                        pltpu.VMEM((n_chain, 2 * tq, 2 * LANES), F32),
                        pltpu.VMEM((1 + N_B, 2 * tq, tq), F32)],
        compiler_params=_params(1),
        name="attn_prompt",
    )(qa16, qb16, kat16, kbt16, va16, vb16, aug, *lams, sub)


def _nt_dot(a, b):
    return lax.dot_general(a, b, (((1,), (1,)), ((), ())), preferred_element_type=F32)


def _attn_cached_kernel(qa_ref, qb_ref, ckat_ref, cvat_ref, ckbt_ref, cvb_ref,
                        ka_ref, va_ref, kb_ref, vb_ref, aug_ref,
                        lq1_ref, lk1_ref, lq2_ref, lk2_ref, sub_ref, oa_ref, ob_ref,
                        kn_ref, *, lambda_init):
    tq = qa_ref.shape[0]
    past = ckat_ref.shape[2]
    t_all = past + tq
    n_chain = N_PAIRS + N_B
    r = lax.broadcasted_iota(jnp.int32, (2 * tq, LANES), 0)
    c = lax.broadcasted_iota(jnp.int32, (2 * tq, LANES), 1)
    q_idx = jnp.where(r >= tq, r - tq, r)
    is_key = c < tq
    causal_tbl = jnp.where((q_idx >= c) & is_key, 0.0, NEG_INF)
    visible = (((past + q_idx) // CHUNK) >= ((past + c) // CHUNK)) & is_key
    lane = lax.broadcasted_iota(jnp.int32, (tq, LANES), 1)
    lam = _lambda(lq1_ref, lk1_ref, lq2_ref, lk2_ref, lambda_init)
    aug_c = jnp.concatenate([aug_ref[0, :, 0:past], jnp.zeros((LANES - AUG_ROWS, past), BF16)], axis=0)
    alibi_c = _alibi_rows(lax.broadcasted_iota(jnp.int32, (1, past), 1).astype(F32))
    alibi_n = _alibi_rows((past + lax.broadcasted_iota(jnp.int32, (1, LANES), 1)).astype(F32))
    kn_ref[...] = jnp.zeros(kn_ref.shape, BF16)
    row_pad = jnp.zeros((LANES - tq, LANES), BF16)

    def new_keys_t(k_new):
        return jnp.concatenate([k_new, row_pad], axis=0).astype(F32).T.astype(BF16)

    for ch in range(n_chain):
        fox = ch < N_PAIRS
        g = ch if fox else ch - N_PAIRS
        rows = slice(g * LANES, (g + 1) * LANES)
        if fox:
            lo, hi = _lane_halves(qa_ref[:, rows])
            sel_lo = _piece_selector((tq, LANES), 2 * g, N_A)
            sel_hi = _piece_selector((tq, LANES), 2 * g + 1, N_A)
            k_old = jnp.concatenate([ckat_ref[0, rows, :].astype(BF16), aug_c], axis=0)
            kn_ref[ch, 0:LANES, :] = new_keys_t(ka_ref[:, rows])
            kn_ref[ch, LANES:LANES + AUG_ROWS, 0:tq] = aug_ref[0, :, past:t_all]
            tbl = causal_tbl
        else:
            lo, hi = _lane_halves(qb_ref[:, rows])
            sel_lo = sel_hi = _piece_selector((tq, LANES), g, N_B)
            k_old = jnp.concatenate([ckbt_ref[0, rows, :].astype(BF16), alibi_c], axis=0)
            kn_ref[ch, 0:LANES, :] = new_keys_t(kb_ref[:, rows])
            kn_ref[ch, LANES:, :] = alibi_n
            tbl = jnp.where(visible, _alibi_fold(ALIBI_SLOPES[g], q_idx, c), NEG_INF)
        qs = jnp.concatenate([jnp.concatenate([lo, sel_lo], axis=1), jnp.concatenate([hi, sel_hi], axis=1)], axis=0)
        s_old = jnp.dot(qs, k_old, preferred_element_type=F32)
        s_new = jnp.dot(qs, kn_ref[ch], preferred_element_type=F32) + tbl
        m = jnp.maximum(jnp.max(s_old, axis=1, keepdims=True), jnp.max(s_new, axis=1, keepdims=True))
        p_old = jnp.exp2(s_old - m)
        p_new = jnp.exp2(s_new - m)
        l = jnp.sum(p_old, axis=1, keepdims=True) + jnp.sum(p_new, axis=1, keepdims=True)
        v_new = jnp.concatenate([(va_ref if fox else vb_ref)[:, rows], row_pad], axis=0)
        o_new = jnp.dot(p_new.astype(BF16), v_new, preferred_element_type=F32)
        if fox:
            o = (_nt_dot(p_old.astype(BF16), cvat_ref[0, rows, :].astype(BF16)) + o_new) / l
            oa_ref[:, rows] = jnp.where(lane < HD_A, o[:tq], o[tq:]).astype(BF16)
        else:
            v_old = cvb_ref[0, pl.ds(g, past, stride=N_B), :].astype(BF16)
            o = (jnp.dot(p_old.astype(BF16), v_old, preferred_element_type=F32) + o_new) / l
            ob_ref[:, rows] = _diff_finish(o[:tq], o[tq:], lam, sub_ref, lambda_init).astype(BF16)


def _attn_cached(qa16, qb16, ckat, cvat, ckbt, cvb, ka16, va16, kb16, vb16, aug, lams, sub, b, tq, lambda_init):
    past = ckat.shape[2]
    t_all = past + tq
    n_chain = N_PAIRS + N_B
    qspec = pl.BlockSpec((tq, D_A), lambda bi: (bi, 0))
    old_t = pl.BlockSpec((1, D_A, past), lambda bi: (bi, 0, 0))
    small = [_resident((1, HD_B))] * 4 + [_resident((1, 2 * HD_B))]
    return pl.pallas_call(
        functools.partial(_attn_cached_kernel, lambda_init=lambda_init),
        grid=(b,),
        in_specs=[qspec, qspec, old_t, old_t, old_t,
                  pl.BlockSpec((1, past * N_B, LANES), lambda bi: (bi, 0, 0)),
                  qspec, qspec, qspec, qspec,
                  pl.BlockSpec((1, AUG_ROWS, t_all), lambda bi: (bi, 0, 0))] + small,
        out_specs=[qspec, qspec],
        out_shape=[jax.ShapeDtypeStruct((b * tq, D_A), BF16), jax.ShapeDtypeStruct((b * tq, D_B), BF16)],
        scratch_shapes=[pltpu.VMEM((n_chain, 2 * LANES, LANES), BF16)],
        compiler_params=_params(1),
        name="attn_cached",
    )(qa16, qb16, ckat, cvat, ckbt, cvb, ka16, va16, kb16, vb16, aug, *lams, sub)


def _tokens_minor(x):
    b, t = x.shape[:2]
    return jnp.moveaxis(x.reshape(b, t, -1), 1, 2)


def _tokens_major(xt, tail):
    b, _, t = xt.shape
    return jnp.moveaxis(xt, 2, 1).reshape(b, t, *tail)


def _layer(x, cache, w, lambda_init, final):
    b, t, _ = x.shape
    n = b * t
    cast = w["ffn16"] is None
    if cast:
        x1, w1a, w2a = _ffn1(x.reshape(n, D_MODEL), w["gn1"], w["w1a"], w["w2a"], cast=True)
    else:
        w1a, w2a, wo, w1b, w2b = w["ffn16"]
        x1 = _ffn1(x.reshape(n, D_MODEL), w["gn1"], w1a, w2a, cast=False)
    (k_a, v_a, lft, k_b, vb, qa16, ka16, va16, qb16, kb16, vb16), w_in16 = _proj(
        x1, w["gmix"], w["bf"], w["w_in_t"] if cast else w["w_in16"], b, t, tokens_minor=cache is None)
    if cast:
        w["w_in16"] = w_in16
    if cache is None:
        aug = _cum_aug(lft, blocked=True)
        oa, ob = _attn_prompt(qa16, qb16, ka16, kb16, va16, vb16, aug, w["lams"], w["sub"], b, t, lambda_init)
        news = (_tokens_major(k_a, (N_A, HD_A)), _tokens_major(v_a, (N_A, HD_A)), _tokens_major(lft, (N_A,)),
                _tokens_major(k_b, (N_B, 2, HD_B)), vb.reshape(b, t, N_B, 2 * HD_B))
    else:
        cfk, cfv, cflf, cdk, cdv = cache
        past = cfk.shape[1]
        aug = _cum_aug(jnp.concatenate([_tokens_minor(cflf.astype(F32)), lft], axis=2), blocked=False)
        oa, ob = _attn_cached(qa16, qb16, _tokens_minor(cfk), _tokens_minor(cfv), _tokens_minor(cdk),
                              cdv.reshape(b, past * N_B, 2 * HD_B), ka16, va16, kb16, vb16, aug,
                              w["lams"], w["sub"], b, t, lambda_init)
        news = (k_a.reshape(b, t, N_A, HD_A), v_a.reshape(b, t, N_A, HD_A), _tokens_major(lft, (N_A,)),
                k_b.reshape(b, t, N_B, 2, HD_B), vb.reshape(b, t, N_B, 2 * HD_B))
    if cast:
        y, wo, w1b, w2b = _post(x1, oa, ob, w["gn2"], w["gfin"], w["wo"], w["w1b"], w["w2b"], final, cast=True)
        w["ffn16"] = (w1a, w2a, wo, w1b, w2b)
    else:
        y = _post(x1, oa, ob, w["gn2"], w["gfin"], wo, w1b, w2b, final, cast=False)
    return y.reshape(b, t, D_MODEL), news


def kernel(x_prompt, x_sample, cache_fox_k, cache_fox_v, cache_fox_logf, cache_diff_k, cache_diff_v,
           norm_ffn1, w_ffn1_in, w_ffn1_out, norm_mix, w_in, b_forget,
           lambda_q1, lambda_k1, lambda_q2, lambda_k2, diff_subln, w_out,
           norm_ffn2, w_ffn2_in, w_ffn2_out, norm_final):
    depth = w_in.shape[0]
    xp, xs = x_prompt, x_sample
    outs_p, outs_s = [], []
    gfin = norm_final.reshape(1, D_MODEL)
    for l in range(depth):
        lambda_init = 0.8 - 0.6 * math.exp(-0.3 * l)
        w = {
            "gn1": norm_ffn1[l].reshape(1, D_MODEL),
            "w1a": w_ffn1_in[l], "w2a": w_ffn1_out[l],
            "gmix": norm_mix[l].reshape(1, D_MODEL),
            "w_in_t": jnp.swapaxes(w_in[l], 0, 1),
            "bf": jnp.pad(b_forget[l], (0, LANES - N_A)).reshape(1, LANES),
            "lams": [v[l].reshape(1, HD_B) for v in (lambda_q1, lambda_k1, lambda_q2, lambda_k2)],
            "sub": diff_subln[l].reshape(1, 2 * HD_B),
            "wo": w_out[l],
            "gn2": norm_ffn2[l].reshape(1, D_MODEL),
            "w1b": w_ffn2_in[l], "w2b": w_ffn2_out[l],
            "gfin": gfin,
            "ffn16": None,
        }
        streams = []
        for x, cache in ((xp, None),
                         (xs, (cache_fox_k[l], cache_fox_v[l], cache_fox_logf[l],
                               cache_diff_k[l], cache_diff_v[l]))):
            streams.append(_layer(x, cache, w, lambda_init, l == depth - 1))
        (xp, news_p), (xs, news_s) = streams
        outs_p.append(news_p)
        outs_s.append(news_s)
    stack = lambda outs, i: jnp.stack([o[i] for o in outs])
    return (xp, xs) + tuple(stack(outs_p, i) for i in range(5)) + tuple(stack(outs_s, i) for i in range(5))
```

```python
import functools
import math

import jax
import jax.numpy as jnp
from jax import lax
from jax.experimental import pallas as pl
from jax.experimental.pallas import tpu as pltpu

D_MODEL = 1024
D_FF = 2816
N_A = 8
HD_A = 64
N_B = 4
HD_B = 64
D_A = N_A * HD_A
D_B = N_B * 2 * HD_B
CHUNK = 64
NORM_EPS = 1e-6
NEG_INF = -1e30
ALIBI_SLOPES = (0.25, 0.0625, 0.015625, 0.00390625)

LOG2E = math.log2(math.e)
LANES = 128
MXU_WIDTH = 256
FF_CHUNKS = (1536, 1280)
assert sum(FF_CHUNKS) == D_FF and all(c % MXU_WIDTH == 0 for c in FF_CHUNKS)
SCAN_BATCH = 8
CAST_BYTES = 3 << 20
TOKEN_TILE = 512
PROJ_TILE = 1024
ATTN_BLOCK = 256
KEY_BLOCKS_PER_STEP = 2
Q_BLOCKS_PER_STEP = 2
assert KEY_BLOCKS_PER_STEP == Q_BLOCKS_PER_STEP == 2
AUG_ROWS = 32
VMEM_LIMIT = 56 * 1024 * 1024
N_PAIRS = N_A // 2

F32 = jnp.float32
BF16 = jnp.bfloat16


def _params(n_axes):
    return pltpu.CompilerParams(dimension_semantics=("arbitrary",) * n_axes,
                                vmem_limit_bytes=VMEM_LIMIT)


def _resident(shape):
    return pl.BlockSpec(shape, lambda *_: (0,) * len(shape), pipeline_mode=pl.Buffered(1))


def _rms(xf, g):
    ms = jnp.mean(xf * xf, axis=-1, keepdims=True)
    return xf * lax.rsqrt(ms + NORM_EPS) * g


def _swiglu_residual(x, gn_ref, w1_ref, w2_ref):
    h = _rms(x, gn_ref[...]).astype(BF16)
    acc = None
    lo = 0
    for width in FF_CHUNKS:
        g = jnp.dot(h, w1_ref[:, lo:lo + width], preferred_element_type=F32)
        u = jnp.dot(h, w1_ref[:, D_FF + lo:D_FF + lo + width], preferred_element_type=F32)
        a = (g * jax.nn.sigmoid(g) * u).astype(BF16)
        d = jnp.dot(a, w2_ref[lo:lo + width, :], preferred_element_type=F32)
        acc = d if acc is None else acc + d
        lo += width
    return x + 0.5 * acc


def _cast_rows(rows_total, cols):
    rows = max(16, CAST_BYTES // (4 * cols) // 16 * 16)
    while rows_total % rows:
        rows -= 16
    return rows


def _stage_shape(shapes):
    return {c: (2, max(_cast_rows(r, cc) for r, cc in shapes if cc == c), c) for _, c in shapes}


def _cast_weight(w_hbm, w16_ref, stage_ref, sem_ref):
    r_total, cols = w_hbm.shape
    rows = _cast_rows(r_total, cols)
    n = r_total // rows

    def copy(c):
        return pltpu.make_async_copy(w_hbm.at[pl.ds(c * rows, rows), :],
                                     stage_ref.at[c % 2, pl.ds(0, rows), :], sem_ref.at[c % 2])

    copy(0).start()
    for c in range(n):
        if c + 1 < n:
            copy(c + 1).start()
        copy(c).wait()
        w16_ref[pl.ds(c * rows, rows), :] = stage_ref[c % 2, 0:rows, :].astype(BF16)


def _weights_prologue(i, w_hbm_refs, w16_refs, w16_hbm_refs, stages, sem_in, sem_out):
    def publish(k):
        return pltpu.make_async_copy(w16_refs[k], w16_hbm_refs[k], sem_out.at[k])

    @pl.when(i == 0)
    def _():
        for w_hbm, w16 in zip(w_hbm_refs, w16_refs):
            _cast_weight(w_hbm, w16, stages[w_hbm.shape[1]], sem_in)
        for k in range(len(w16_refs)):
            publish(k).start()

    return publish


def _weights_epilogue(i, publish, n_weights):
    @pl.when(i == pl.num_programs(0) - 1)
    def _():
        for k in range(n_weights):
            publish(k).wait()


def _weight_plumbing(weights, cast):
    if not cast:
        return [_resident(w.shape) for w in weights], [], [], []
    any_spec = pl.BlockSpec(memory_space=pl.ANY)
    shapes = [w.shape for w in weights]
    stages = _stage_shape(shapes)
    scratch = ([pltpu.VMEM(sh, BF16) for sh in shapes] + [pltpu.VMEM(stages[c], F32) for c in sorted(stages)]
               + [pltpu.SemaphoreType.DMA((2,)), pltpu.SemaphoreType.DMA((len(weights),))])
    return ([any_spec] * len(weights), [any_spec] * len(weights),
            [jax.ShapeDtypeStruct(sh, BF16) for sh in shapes], scratch)


def _split_weight_refs(refs, n_weights, shapes):
    w16 = refs[:n_weights]
    cols = sorted({c for _, c in shapes})
    stage_refs = refs[n_weights:n_weights + len(cols)]
    sem_in, sem_out = refs[n_weights + len(cols):]
    return w16, dict(zip(cols, stage_refs)), sem_in, sem_out


def _ffn1_kernel(x_ref, gn_ref, w1_in, w2_in, o_ref, *rest, cast):
    if cast:
        i = pl.program_id(0)
        w_hbm = (w1_in, w2_in)
        w16, stages, sem_in, sem_out = _split_weight_refs(rest[2:], 2, [w.shape for w in w_hbm])
        publish = _weights_prologue(i, w_hbm, w16, rest[:2], stages, sem_in, sem_out)
        o_ref[...] = _swiglu_residual(x_ref[...], gn_ref, *w16)
        _weights_epilogue(i, publish, 2)
    else:
        o_ref[...] = _swiglu_residual(x_ref[...], gn_ref, w1_in, w2_in)


def _ffn1(x, gn, w1, w2, cast):
    n = x.shape[0]
    tm = min(TOKEN_TILE, n)
    w_specs, w_out_specs, w_out_shapes, scratch = _weight_plumbing([w1, w2], cast)
    out = pl.pallas_call(
        functools.partial(_ffn1_kernel, cast=cast),
        grid=(n // tm,),
        in_specs=[pl.BlockSpec((tm, D_MODEL), lambda i: (i, 0)), _resident((1, D_MODEL))] + w_specs,
        out_specs=[pl.BlockSpec((tm, D_MODEL), lambda i: (i, 0))] + w_out_specs,
        out_shape=[jax.ShapeDtypeStruct((n, D_MODEL), F32)] + w_out_shapes,
        scratch_shapes=scratch,
        compiler_params=_params(1),
        name="ffn1",
    )(x, gn, w1, w2)
    return out if cast else out[0]


def _post_kernel(x_ref, oa_ref, ob_ref, gn_ref, gf_ref, wo_in, w1_in, w2_in, y_ref, *rest, final, cast):
    if cast:
        i = pl.program_id(0)
        w_hbm = (wo_in, w1_in, w2_in)
        w16, stages, sem_in, sem_out = _split_weight_refs(rest[3:], 3, [w.shape for w in w_hbm])
        publish = _weights_prologue(i, w_hbm, w16, rest[:3], stages, sem_in, sem_out)
        wo_ref, w1_ref, w2_ref = w16
    else:
        wo_ref, w1_ref, w2_ref = wo_in, w1_in, w2_in
    x = x_ref[...]
    x = x + jnp.dot(oa_ref[...], wo_ref[:D_A, :], preferred_element_type=F32)
    x = x + jnp.dot(ob_ref[...], wo_ref[D_A:, :], preferred_element_type=F32)
    x = _swiglu_residual(x, gn_ref, w1_ref, w2_ref)
    y_ref[...] = _rms(x, gf_ref[...]) if final else x
    if cast:
        _weights_epilogue(i, publish, 3)


def _post(x, oa, ob, gn, gf, wo, w1, w2, final, cast):
    n = x.shape[0]
    tm = min(TOKEN_TILE, n)
    row = lambda i: (i, 0)
    w_specs, w_out_specs, w_out_shapes, scratch = _weight_plumbing([wo, w1, w2], cast)
    out = pl.pallas_call(
        functools.partial(_post_kernel, final=final, cast=cast),
        grid=(n // tm,),
        in_specs=[pl.BlockSpec((tm, D_MODEL), row), pl.BlockSpec((tm, D_A), row), pl.BlockSpec((tm, D_B), row),
                  _resident((1, D_MODEL)), _resident((1, D_MODEL))] + w_specs,
        out_specs=[pl.BlockSpec((tm, D_MODEL), row)] + w_out_specs,
        out_shape=[jax.ShapeDtypeStruct((n, D_MODEL), F32)] + w_out_shapes,
        scratch_shapes=scratch,
        compiler_params=_params(1),
        name="post",
    )(x, oa, ob, gn, gf, wo, w1, w2)
    return out if cast else out[0]


def _proj_weights_prologue(i, wt_hbm, w16_refs, w16_hbm_refs, stage_ref, stage_f_ref, sem_in, sem_out):
    wa16, wf16, wb16 = w16_refs
    rows = stage_ref.shape[1]

    def publish(k):
        return pltpu.make_async_copy(w16_refs[k], w16_hbm_refs[k], sem_out.at[k])

    @pl.when(i == 0)
    def _():
        f_lo = 3 * D_A
        jobs = [(r0 + c * rows, w16, c) for r0, w16 in ((0, wa16), (f_lo + N_A, wb16)) for c in range(f_lo // rows)]

        def copy(n):
            return pltpu.make_async_copy(wt_hbm.at[pl.ds(jobs[n][0], rows), :], stage_ref.at[n % 2], sem_in.at[n % 2])

        gate = pltpu.make_async_copy(wt_hbm.at[pl.ds(f_lo, N_A), :], stage_f_ref.at[pl.ds(0, N_A), :], sem_in.at[2])
        gate.start()
        copy(0).start()
        for n, (_, w16, c) in enumerate(jobs):
            if n + 1 < len(jobs):
                copy(n + 1).start()
            copy(n).wait()
            w16[:, c * rows:(c + 1) * rows] = stage_ref[n % 2].T.astype(BF16)
        stage_f_ref[N_A:, :] = jnp.zeros((LANES - N_A, D_MODEL), F32)
        gate.wait()
        wf16[...] = stage_f_ref[...].T.astype(BF16)
        for k in range(3):
            publish(k).start()

    return publish


def _proj_kernel(x_ref, gn_ref, bf_ref, *refs, seg, kblk, tokens_minor, cast):
    n_out = 11
    if cast:
        i = pl.program_id(0)
        wt_hbm, out_refs, w16_hbm = refs[0], refs[1:1 + n_out], refs[1 + n_out:4 + n_out]
        wa_ref, wf_ref, wb_ref, stage_ref, stage_f_ref, sem_in, sem_out = refs[4 + n_out:]
        publish = _proj_weights_prologue(i, wt_hbm, (wa_ref, wf_ref, wb_ref), w16_hbm,
                                         stage_ref, stage_f_ref, sem_in, sem_out)
    else:
        wa_ref, wf_ref, wb_ref = refs[:3]
        out_refs = refs[3:]
    tm = x_ref.shape[0]
    h = _rms(x_ref[...], gn_ref[...]).astype(BF16)
    pa = jnp.dot(h, wa_ref[...], preferred_element_type=F32)
    pb = jnp.dot(h, wb_ref[...], preferred_element_type=F32)
    z = jnp.dot(h, wf_ref[...], preferred_element_type=F32) + bf_ref[...]
    lft = (jnp.minimum(z, 0.0) - jnp.log1p(jnp.exp(-jnp.abs(z)))).T
    ka, va = pa[:, D_A:2 * D_A], pa[:, 2 * D_A:]
    kb, vb = pb[:, D_B:2 * D_B], pb[:, 2 * D_B:]
    qa16 = (pa[:, :D_A] * (HD_A ** -0.5 * LOG2E)).astype(BF16)
    qb16 = (pb[:, :D_B] * (HD_B ** -0.5 * LOG2E)).astype(BF16)
    if tokens_minor:
        kat_ref, vat_ref, lft_ref, kbt_ref, vb_ref, qa16_ref, kat16_ref, va16_ref, qb16_ref, kbt16_ref, vb16_ref = out_refs
        kat, vat, kbt = ka.T, va.T, kb.T
    else:
        ka_ref, va_ref, lft_ref, kb_ref, vb_ref, qa16_ref, ka16_ref, va16_ref, qb16_ref, kb16_ref, vb16_ref = out_refs
        ka16_ref[...] = ka.astype(BF16)
        kb16_ref[...] = kb.astype(BF16)
        for hh in range(N_A):
            cols = slice(hh * HD_A, (hh + 1) * HD_A)
            ka_ref[pl.ds(hh, tm, stride=N_A), :] = ka[:, cols]
            va_ref[pl.ds(hh, tm, stride=N_A), :] = va[:, cols]
            kb_ref[pl.ds(hh, tm, stride=N_A), :] = kb[:, cols]
    qa16_ref[...] = qa16
    qb16_ref[...] = qb16
    va16_ref[...] = va.astype(BF16)
    vb16_ref[...] = vb.astype(BF16)
    for hh in range(N_B):
        vb_ref[pl.ds(hh, tm, stride=N_B), :] = vb[:, hh * LANES:(hh + 1) * LANES]
    for bb in range(tm // seg):
        cols = slice(bb * seg, (bb + 1) * seg)
        lft_ref[bb] = lft[:N_A, cols]
        if tokens_minor:
            kat_ref[bb] = kat[:, cols]
            vat_ref[bb] = vat[:, cols]
            kbt_ref[bb] = kbt[:, cols]
            for jj in range(seg // kblk):
                kc = slice(bb * seg + jj * kblk, bb * seg + (jj + 1) * kblk)
                kat16_ref[bb, jj] = kat[:, kc].astype(BF16)
                kbt16_ref[bb, jj] = kbt[:, kc].astype(BF16)
    if cast:
        _weights_epilogue(i, publish, 3)


def _proj(x, gn, bf, weights, b, t, tokens_minor):
    cast = not isinstance(weights, (tuple, list))
    n = x.shape[0]
    tm = min(PROJ_TILE if tokens_minor else TOKEN_TILE, n)
    seg = min(tm, t)
    nb = tm // seg
    nt = t // seg
    kblk = min(ATTN_BLOCK, seg)
    row = lambda i: (i, 0)
    tok = lambda i: (i // nt, 0, i % nt)
    blk = lambda i: (i // nt, i % nt, 0, 0)
    b16 = (jax.ShapeDtypeStruct((n, D_A), BF16), pl.BlockSpec((tm, D_A), row))
    lft = (jax.ShapeDtypeStruct((b, N_A, t), F32), pl.BlockSpec((nb, N_A, seg), tok))
    vb = (jax.ShapeDtypeStruct((n * N_B, LANES), F32), pl.BlockSpec((tm * N_B, LANES), row))
    if tokens_minor:
        kv = (jax.ShapeDtypeStruct((b, D_A, t), F32), pl.BlockSpec((nb, D_A, seg), tok))
        k16 = (jax.ShapeDtypeStruct((b, t // kblk, D_A, kblk), BF16),
               pl.BlockSpec((nb, seg // kblk, D_A, kblk), blk))
    else:
        kv = (jax.ShapeDtypeStruct((n * N_A, HD_A), F32), pl.BlockSpec((tm * N_A, HD_A), row))
        k16 = b16
    outs = [kv, kv, lft, kv, vb, b16, k16, b16, b16, k16, b16]
    group_shapes = [(D_MODEL, 3 * D_A), (D_MODEL, LANES), (D_MODEL, 3 * D_B)]
    if cast:
        any_spec = pl.BlockSpec(memory_space=pl.ANY)
        w_specs = [any_spec]
        outs = outs + [(jax.ShapeDtypeStruct(sh, BF16), any_spec) for sh in group_shapes]
        scratch = ([pltpu.VMEM(sh, BF16) for sh in group_shapes]
                   + [pltpu.VMEM((2, 3 * D_A // 2, D_MODEL), F32), pltpu.VMEM((LANES, D_MODEL), F32),
                      pltpu.SemaphoreType.DMA((3,)), pltpu.SemaphoreType.DMA((3,))])
        weights = [weights]
    else:
        w_specs = [_resident(sh) for sh in group_shapes]
        scratch = []
    res = pl.pallas_call(
        functools.partial(_proj_kernel, seg=seg, kblk=kblk, tokens_minor=tokens_minor, cast=cast),
        grid=(n // tm,),
        in_specs=[pl.BlockSpec((tm, D_MODEL), row), _resident((1, D_MODEL)), _resident(bf.shape)] + w_specs,
        out_specs=[o[1] for o in outs],
        out_shape=[o[0] for o in outs],
        scratch_shapes=scratch,
        compiler_params=_params(1),
        name="proj",
    )(x, gn, bf, *weights)
    return res[:11], tuple(res[11:])


def _split3(x):
    hi = x.astype(BF16).astype(F32)
    r = x - hi
    mid = r.astype(BF16).astype(F32)
    lo = (r - mid).astype(BF16).astype(F32)
    return hi, mid, lo


def _cum_kernel(lft_ref, aug_ref, pad_ref, *, t, blocked):
    w = ATTN_BLOCK
    nb = lft_ref.shape[0]
    rows = nb * N_A
    pad_ref[...] = jnp.zeros(pad_ref.shape, F32)
    for bb in range(nb):
        pad_ref[bb * N_A:(bb + 1) * N_A, 0:t] = lft_ref[bb]
    r = lax.broadcasted_iota(jnp.int32, (w, w), 0)
    c = lax.broadcasted_iota(jnp.int32, (w, w), 1)
    tri = jnp.where(r <= c, 1.0, 0.0).astype(BF16)
    zero = jnp.zeros((N_A, w), F32)
    carry = jnp.zeros((rows, 1), F32)
    for s in range(pad_ref.shape[1] // w):
        lo_col = s * w
        cols = min(w, t - lo_col)
        x = jnp.concatenate(_split3(pad_ref[:, lo_col:lo_col + w]), axis=0).astype(BF16)
        y = jnp.dot(x, tri, preferred_element_type=F32)
        cs = y[0:rows] + y[rows:2 * rows] + y[2 * rows:3 * rows] + carry
        carry = cs[:, w - 1:w]
        pieces = _split3(cs * -LOG2E)
        for bb in range(nb):
            sl = slice(bb * N_A, (bb + 1) * N_A)
            aug = jnp.concatenate([p[sl] for p in pieces] + [zero], axis=0).astype(BF16)
            if blocked:
                aug_ref[bb, s] = aug
            else:
                aug_ref[bb, :, lo_col:lo_col + cols] = aug[:, :cols]


def _cum_aug(lft, blocked):
    b, _, t = lft.shape
    w = ATTN_BLOCK
    t_pad = -(-t // w) * w
    nb = math.gcd(b, SCAN_BATCH)
    if blocked:
        out_shape = jax.ShapeDtypeStruct((b, t // w, AUG_ROWS, w), BF16)
        out_spec = pl.BlockSpec((nb, t // w, AUG_ROWS, w), lambda i: (i, 0, 0, 0))
    else:
        out_shape = jax.ShapeDtypeStruct((b, AUG_ROWS, t), BF16)
        out_spec = pl.BlockSpec((nb, AUG_ROWS, t), lambda i: (i, 0, 0))
    return pl.pallas_call(
        functools.partial(_cum_kernel, t=t, blocked=blocked),
        grid=(b // nb,),
        in_specs=[pl.BlockSpec((nb, N_A, t), lambda i: (i, 0, 0))],
        out_specs=out_spec,
        out_shape=out_shape,
        scratch_shapes=[pltpu.VMEM((nb * N_A, t_pad), F32)],
        compiler_params=_params(1),
        name="cum_aug",
    )(lft)


def _lane_halves(q):
    lane = lax.broadcasted_iota(jnp.int32, q.shape, 1)
    zero = jnp.zeros_like(q)
    return jnp.where(lane < HD_A, q, zero), jnp.where(lane >= HD_A, q, zero)


def _piece_selector(shape, head, n_heads):
    lane = lax.broadcasted_iota(jnp.int32, shape, 1)
    hit = (lane == head) | (lane == n_heads + head) | (lane == 2 * n_heads + head)
    return jnp.where(hit, 1.0, 0.0).astype(BF16)


def _alibi_rows(k_pos):
    w = k_pos.shape[1]
    row = lax.broadcasted_iota(jnp.int32, (4 * N_B, 1), 0)
    head = row % N_B
    slope = jnp.zeros((4 * N_B, 1), F32)
    for h in range(N_B):
        slope = jnp.where(head == h, ALIBI_SLOPES[h] * LOG2E, slope)
    hi, mid, lo = _split3(slope * k_pos)
    piece = jnp.where(row < N_B, hi, jnp.where(row < 2 * N_B, mid, jnp.where(row < 3 * N_B, lo, 0.0)))
    return jnp.concatenate([piece.astype(BF16), jnp.zeros((LANES - 4 * N_B, w), BF16)], axis=0)


def _alibi_fold(slope, q_pos, k_pos):
    return (-2.0 * LOG2E * slope) * jnp.maximum(k_pos - q_pos, 0).astype(F32)


def _diff_finish(o1, o2, lam, sub_ref, lambda_init):
    return _rms(o1 - lam * o2, sub_ref[...]) * (1.0 - lambda_init)


def _lambda(lq1_ref, lk1_ref, lq2_ref, lk2_ref, lambda_init):
    return (jnp.exp(jnp.sum(lq1_ref[...] * lk1_ref[...], axis=1, keepdims=True))
            - jnp.exp(jnp.sum(lq2_ref[...] * lk2_ref[...], axis=1, keepdims=True)) + lambda_init)


def _attn_prompt_kernel(qa_ref, qb_ref, kat_ref, kbt_ref, va_ref, vb_ref, aug_ref,
                        lq1_ref, lk1_ref, lq2_ref, lk2_ref, sub_ref, oa_ref, ob_ref,
                        qs_ref, m_ref, acc_ref, tbl_ref, *, lambda_init):
    tq = ATTN_BLOCK
    n_chain = N_PAIRS + N_B

    @pl.when(pl.program_id(0) == 0)
    def _():
        r = lax.broadcasted_iota(jnp.int32, (2 * tq, tq), 0)
        c = lax.broadcasted_iota(jnp.int32, (2 * tq, tq), 1)
        q_idx = jnp.where(r >= tq, r - tq, r)
        tbl_ref[0] = jnp.where(q_idx >= c, 0.0, NEG_INF)
        visible = (q_idx // CHUNK) >= (c // CHUNK)
        for h in range(N_B):
            tbl_ref[1 + h] = jnp.where(visible, _alibi_fold(ALIBI_SLOPES[h], q_idx, c), NEG_INF)
        for qset in range(Q_BLOCKS_PER_STEP):
            for hc in range(n_chain):
                ch = qset * n_chain + hc
                if hc < N_PAIRS:
                    qs_ref[ch, 0:tq, LANES:] = _piece_selector((tq, LANES), 2 * hc, N_A)
                    qs_ref[ch, tq:, LANES:] = _piece_selector((tq, LANES), 2 * hc + 1, N_A)
                else:
                    sel = _piece_selector((tq, LANES), hc - N_PAIRS, N_B)
                    qs_ref[ch, 0:tq, LANES:] = sel
                    qs_ref[ch, tq:, LANES:] = sel

    def step(sets, first, width=1):
        wk = width * tq
        ones = jnp.ones((wk, LANES), BF16)
        for qset, j in sets:
            rows = pl.ds(pl.multiple_of(j * tq, tq), wk)
            blocks = [j + d for d in range(width)]
            aug_a = jnp.concatenate([jnp.concatenate([aug_ref[0, jb] for jb in blocks], axis=1),
                                     jnp.zeros((LANES - AUG_ROWS, wk), BF16)], axis=0)
            aug_b = _alibi_rows((j * tq + lax.broadcasted_iota(jnp.int32, (1, wk), 1)).astype(F32))
            for hc in range(n_chain):
                ch = qset * n_chain + hc
                if hc < N_PAIRS:
                    kt_ref, v_ref, g, aug, tbl = kat_ref, va_ref, hc, aug_a, 0
                else:
                    kt_ref, v_ref, g, aug, tbl = kbt_ref, vb_ref, hc - N_PAIRS, aug_b, 1 + hc - N_PAIRS
                kt = jnp.concatenate([kt_ref[0, jb, g * LANES:(g + 1) * LANES, :] for jb in blocks], axis=1)
                s = jnp.dot(qs_ref[ch], jnp.concatenate([kt, aug], axis=0), preferred_element_type=F32)
                v = v_ref[rows, g * LANES:(g + 1) * LANES]
                if first:
                    s = s + tbl_ref[tbl]
                    m_new = jnp.broadcast_to(jnp.max(s, axis=1, keepdims=True), (2 * tq, LANES))
                else:
                    m_prev = m_ref[ch]
                    m_new = jnp.maximum(m_prev, jnp.max(s, axis=1, keepdims=True))
                p = jnp.exp2(s - jnp.concatenate([m_new] * (wk // LANES), axis=1))
                pv = jnp.dot(p.astype(BF16), jnp.concatenate([v, ones], axis=1), preferred_element_type=F32)
                if first:
                    acc_ref[ch] = pv
                else:
                    alpha = jnp.exp2(m_prev - m_new)
                    acc_ref[ch] = jnp.concatenate([alpha, alpha], axis=1) * acc_ref[ch] + pv
                m_ref[ch] = m_new

    def wide_body(jj, carry):
        j = jj * KEY_BLOCKS_PER_STEP
        step([(0, j), (1, j)], False, KEY_BLOCKS_PER_STEP)
        return carry

    lane = lax.broadcasted_iota(jnp.int32, (tq, LANES), 1)
    lam = _lambda(lq1_ref, lk1_ref, lq2_ref, lk2_ref, lambda_init)

    def query_blocks(pair, carry):
        even = pair * Q_BLOCKS_PER_STEP
        for qset in range(Q_BLOCKS_PER_STEP):
            q_rows = pl.ds(pl.multiple_of((even + qset) * tq, tq), tq)
            for hc in range(n_chain):
                q_ref, g = (qa_ref, hc) if hc < N_PAIRS else (qb_ref, hc - N_PAIRS)
                lo, hi = _lane_halves(q_ref[q_rows, g * LANES:(g + 1) * LANES])
                qs_ref[qset * n_chain + hc, 0:tq, 0:LANES] = lo
                qs_ref[qset * n_chain + hc, tq:, 0:LANES] = hi
        step([(0, even), (1, even + 1)], True)
        step([(1, even)], False)
        lax.fori_loop(0, pair, wide_body, 0)
        for qset in range(Q_BLOCKS_PER_STEP):
            q_rows = pl.ds(pl.multiple_of((even + qset) * tq, tq), tq)
            for hc in range(n_chain):
                ch = qset * n_chain + hc
                o = acc_ref[ch, :, 0:LANES] / acc_ref[ch, :, LANES:]
                if hc < N_PAIRS:
                    oa_ref[q_rows, hc * LANES:(hc + 1) * LANES] = jnp.where(
                        lane < HD_A, o[:tq], o[tq:]).astype(BF16)
                else:
                    h = hc - N_PAIRS
                    ob_ref[q_rows, h * LANES:(h + 1) * LANES] = _diff_finish(
                        o[:tq], o[tq:], lam, sub_ref, lambda_init).astype(BF16)
        return carry

    lax.fori_loop(0, qa_ref.shape[0] // (Q_BLOCKS_PER_STEP * tq), query_blocks, 0)


def _attn_prompt(qa16, qb16, kat16, kbt16, va16, vb16, aug, lams, sub, b, t, lambda_init):
    tq = ATTN_BLOCK
    nq = t // tq
    kspec = pl.BlockSpec((1, nq, D_A, tq), lambda bi: (bi, 0, 0, 0))
    vspec = pl.BlockSpec((t, D_A), lambda bi: (bi, 0))
    small = [_resident((1, HD_B))] * 4 + [_resident((1, 2 * HD_B))]
    n_chain = Q_BLOCKS_PER_STEP * (N_PAIRS + N_B)
    return pl.pallas_call(
        functools.partial(_attn_prompt_kernel, lambda_init=lambda_init),
        grid=(b,),
        in_specs=[vspec, vspec, kspec, kspec, vspec, vspec,
                  pl.BlockSpec((1, nq, AUG_ROWS, tq), lambda bi: (bi, 0, 0, 0))] + small,
        out_specs=[vspec, vspec],
        out_shape=[jax.ShapeDtypeStruct((b * t, D_A), BF16), jax.ShapeDtypeStruct((b * t, D_B), BF16)],
        scratch_shapes=[pltpu.VMEM((n_chain, 2 * tq, 2 * LANES), BF16),
                        pltpu.VMEM((n_chain, 2 * tq, LANES), F32),
                        pltpu.VMEM((n_chain, 2 * tq, 2 * LANES), F32),
                        pltpu.VMEM((1 + N_B, 2 * tq, tq), F32)],
        compiler_params=_params(1),
        name="attn_prompt",
    )(qa16, qb16, kat16, kbt16, va16, vb16, aug, *lams, sub)


def _nt_dot(a, b):
    return lax.dot_general(a, b, (((1,), (1,)), ((), ())), preferred_element_type=F32)


def _attn_cached_kernel(qa_ref, qb_ref, ckat_ref, cvat_ref, ckbt_ref, cvb_ref,
                        ka_ref, va_ref, kb_ref, vb_ref, aug_ref,
                        lq1_ref, lk1_ref, lq2_ref, lk2_ref, sub_ref, oa_ref, ob_ref,
                        kn_ref, *, lambda_init):
    tq = qa_ref.shape[0]
    past = ckat_ref.shape[2]
    t_all = past + tq
    n_chain = N_PAIRS + N_B
    r = lax.broadcasted_iota(jnp.int32, (2 * tq, LANES), 0)
    c = lax.broadcasted_iota(jnp.int32, (2 * tq, LANES), 1)
    q_idx = jnp.where(r >= tq, r - tq, r)
    is_key = c < tq
    causal_tbl = jnp.where((q_idx >= c) & is_key, 0.0, NEG_INF)
    visible = (((past + q_idx) // CHUNK) >= ((past + c) // CHUNK)) & is_key
    lane = lax.broadcasted_iota(jnp.int32, (tq, LANES), 1)
    lam = _lambda(lq1_ref, lk1_ref, lq2_ref, lk2_ref, lambda_init)
    aug_c = jnp.concatenate([aug_ref[0, :, 0:past], jnp.zeros((LANES - AUG_ROWS, past), BF16)], axis=0)
    alibi_c = _alibi_rows(lax.broadcasted_iota(jnp.int32, (1, past), 1).astype(F32))
    alibi_n = _alibi_rows((past + lax.broadcasted_iota(jnp.int32, (1, LANES), 1)).astype(F32))
    kn_ref[...] = jnp.zeros(kn_ref.shape, BF16)
    row_pad = jnp.zeros((LANES - tq, LANES), BF16)

    def new_keys_t(k_new):
        return jnp.concatenate([k_new, row_pad], axis=0).astype(F32).T.astype(BF16)

    probs = []
    for ch in range(n_chain):
        fox = ch < N_PAIRS
        g = ch if fox else ch - N_PAIRS
        rows = slice(g * LANES, (g + 1) * LANES)
        if fox:
            lo, hi = _lane_halves(qa_ref[:, rows])
            sel_lo = _piece_selector((tq, LANES), 2 * g, N_A)
            sel_hi = _piece_selector((tq, LANES), 2 * g + 1, N_A)
            k_old = jnp.concatenate([ckat_ref[0, rows, :].astype(BF16), aug_c], axis=0)
            kn_ref[ch, 0:LANES, :] = new_keys_t(ka_ref[:, rows])
            kn_ref[ch, LANES:LANES + AUG_ROWS, 0:tq] = aug_ref[0, :, past:t_all]
            tbl = causal_tbl
        else:
            lo, hi = _lane_halves(qb_ref[:, rows])
            sel_lo = sel_hi = _piece_selector((tq, LANES), g, N_B)
            k_old = jnp.concatenate([ckbt_ref[0, rows, :].astype(BF16), alibi_c], axis=0)
            kn_ref[ch, 0:LANES, :] = new_keys_t(kb_ref[:, rows])
            kn_ref[ch, LANES:, :] = alibi_n
            tbl = jnp.where(visible, _alibi_fold(ALIBI_SLOPES[g], q_idx, c), NEG_INF)
        qs = jnp.concatenate([jnp.concatenate([lo, sel_lo], axis=1), jnp.concatenate([hi, sel_hi], axis=1)], axis=0)
        s_old = jnp.dot(qs, k_old, preferred_element_type=F32)
        s_new = jnp.dot(qs, kn_ref[ch], preferred_element_type=F32) + tbl
        m = jnp.maximum(jnp.max(s_old, axis=1, keepdims=True), jnp.max(s_new, axis=1, keepdims=True))
        p_old = jnp.exp2(s_old - m)
        p_new = jnp.exp2(s_new - m)
        l = jnp.sum(p_old, axis=1, keepdims=True) + jnp.sum(p_new, axis=1, keepdims=True)
        probs.append((p_old.astype(BF16), p_new.astype(BF16), l))

    for ch in range(n_chain):
        fox = ch < N_PAIRS
        g = ch if fox else ch - N_PAIRS
        rows = slice(g * LANES, (g + 1) * LANES)
        p_old, p_new, l = probs[ch]
        v_new = jnp.concatenate([(va_ref if fox else vb_ref)[:, rows], row_pad], axis=0)
        o_new = jnp.dot(p_new, v_new, preferred_element_type=F32)
        if fox:
            o = (_nt_dot(p_old, cvat_ref[0, rows, :].astype(BF16)) + o_new) / l
            oa_ref[:, rows] = jnp.where(lane < HD_A, o[:tq], o[tq:]).astype(BF16)
        else:
            v_old = cvb_ref[0, pl.ds(g, past, stride=N_B), :].astype(BF16)
            o = (jnp.dot(p_old, v_old, preferred_element_type=F32) + o_new) / l
            ob_ref[:, rows] = _diff_finish(o[:tq], o[tq:], lam, sub_ref, lambda_init).astype(BF16)


def _attn_cached(qa16, qb16, ckat, cvat, ckbt, cvb, ka16, va16, kb16, vb16, aug, lams, sub, b, tq, lambda_init):
    past = ckat.shape[2]
    t_all = past + tq
    n_chain = N_PAIRS + N_B
    qspec = pl.BlockSpec((tq, D_A), lambda bi: (bi, 0))
    old_t = pl.BlockSpec((1, D_A, past), lambda bi: (bi, 0, 0))
    small = [_resident((1, HD_B))] * 4 + [_resident((1, 2 * HD_B))]
    return pl.pallas_call(
        functools.partial(_attn_cached_kernel, lambda_init=lambda_init),
        grid=(b,),
        in_specs=[qspec, qspec, old_t, old_t, old_t,
                  pl.BlockSpec((1, past * N_B, LANES), lambda bi: (bi, 0, 0)),
                  qspec, qspec, qspec, qspec,
                  pl.BlockSpec((1, AUG_ROWS, t_all), lambda bi: (bi, 0, 0))] + small,
        out_specs=[qspec, qspec],
        out_shape=[jax.ShapeDtypeStruct((b * tq, D_A), BF16), jax.ShapeDtypeStruct((b * tq, D_B), BF16)],
        scratch_shapes=[pltpu.VMEM((n_chain, 2 * LANES, LANES), BF16)],
        compiler_params=_params(1),
        name="attn_cached",
    )(qa16, qb16, ckat, cvat, ckbt, cvb, ka16, va16, kb16, vb16, aug, *lams, sub)


def _tokens_minor(x):
    b, t = x.shape[:2]
    return jnp.moveaxis(x.reshape(b, t, -1), 1, 2)


def _tokens_major(xt, tail):
    b, _, t = xt.shape
    return jnp.moveaxis(xt, 2, 1).reshape(b, t, *tail)


def _layer(x, cache, w, lambda_init, final):
    b, t, _ = x.shape
    n = b * t
    cast = w["ffn16"] is None
    if cast:
        x1, w1a, w2a = _ffn1(x.reshape(n, D_MODEL), w["gn1"], w["w1a"], w["w2a"], cast=True)
    else:
        w1a, w2a, wo, w1b, w2b = w["ffn16"]
        x1 = _ffn1(x.reshape(n, D_MODEL), w["gn1"], w1a, w2a, cast=False)
    (k_a, v_a, lft, k_b, vb, qa16, ka16, va16, qb16, kb16, vb16), w_in16 = _proj(
        x1, w["gmix"], w["bf"], w["w_in_t"] if cast else w["w_in16"], b, t, tokens_minor=cache is None)
    if cast:
        w["w_in16"] = w_in16
    if cache is None:
        aug = _cum_aug(lft, blocked=True)
        oa, ob = _attn_prompt(qa16, qb16, ka16, kb16, va16, vb16, aug, w["lams"], w["sub"], b, t, lambda_init)
        news = (_tokens_major(k_a, (N_A, HD_A)), _tokens_major(v_a, (N_A, HD_A)), _tokens_major(lft, (N_A,)),
                _tokens_major(k_b, (N_B, 2, HD_B)), vb.reshape(b, t, N_B, 2 * HD_B))
    else:
        cfk, cfv, cflf, cdk, cdv = cache
        past = cfk.shape[1]
        aug = _cum_aug(jnp.concatenate([_tokens_minor(cflf.astype(F32)), lft], axis=2), blocked=False)
        oa, ob = _attn_cached(qa16, qb16, _tokens_minor(cfk), _tokens_minor(cfv), _tokens_minor(cdk),
                              cdv.reshape(b, past * N_B, 2 * HD_B), ka16, va16, kb16, vb16, aug,
                              w["lams"], w["sub"], b, t, lambda_init)
        news = (k_a.reshape(b, t, N_A, HD_A), v_a.reshape(b, t, N_A, HD_A), _tokens_major(lft, (N_A,)),
                k_b.reshape(b, t, N_B, 2, HD_B), vb.reshape(b, t, N_B, 2 * HD_B))
    if cast:
        y, wo, w1b, w2b = _post(x1, oa, ob, w["gn2"], w["gfin"], w["wo"], w["w1b"], w["w2b"], final, cast=True)
        w["ffn16"] = (w1a, w2a, wo, w1b, w2b)
    else:
        y = _post(x1, oa, ob, w["gn2"], w["gfin"], wo, w1b, w2b, final, cast=False)
    return y.reshape(b, t, D_MODEL), news


def kernel(x_prompt, x_sample, cache_fox_k, cache_fox_v, cache_fox_logf, cache_diff_k, cache_diff_v,
           norm_ffn1, w_ffn1_in, w_ffn1_out, norm_mix, w_in, b_forget,
           lambda_q1, lambda_k1, lambda_q2, lambda_k2, diff_subln, w_out,
           norm_ffn2, w_ffn2_in, w_ffn2_out, norm_final):
    depth = w_in.shape[0]
    xp, xs = x_prompt, x_sample
    outs_p, outs_s = [], []
    gfin = norm_final.reshape(1, D_MODEL)
    for l in range(depth):
        lambda_init = 0.8 - 0.6 * math.exp(-0.3 * l)
        w = {
            "gn1": norm_ffn1[l].reshape(1, D_MODEL),
            "w1a": w_ffn1_in[l], "w2a": w_ffn1_out[l],
            "gmix": norm_mix[l].reshape(1, D_MODEL),
            "w_in_t": jnp.swapaxes(w_in[l], 0, 1),
            "bf": jnp.pad(b_forget[l], (0, LANES - N_A)).reshape(1, LANES),
            "lams": [v[l].reshape(1, HD_B) for v in (lambda_q1, lambda_k1, lambda_q2, lambda_k2)],
            "sub": diff_subln[l].reshape(1, 2 * HD_B),
            "wo": w_out[l],
            "gn2": norm_ffn2[l].reshape(1, D_MODEL),
            "w1b": w_ffn2_in[l], "w2b": w_ffn2_out[l],
            "gfin": gfin,
            "ffn16": None,
        }
        streams = []
        for x, cache in ((xp, None),
                         (xs, (cache_fox_k[l], cache_fox_v[l], cache_fox_logf[l],
                               cache_diff_k[l], cache_diff_v[l]))):
            streams.append(_layer(x, cache, w, lambda_init, l == depth - 1))
        (xp, news_p), (xs, news_s) = streams
        outs_p.append(news_p)
        outs_s.append(news_s)
    stack = lambda outs, i: jnp.stack([o[i] for o in outs])
    return (xp, xs) + tuple(stack(outs_p, i) for i in range(5)) + tuple(stack(outs_s, i) for i in range(5))
```

```python
import functools
import math

import jax
import jax.numpy as jnp
from jax import lax
from jax.experimental import pallas as pl
from jax.experimental.pallas import tpu as pltpu

D_MODEL = 1024
D_FF = 2816
N_A = 8
HD_A = 64
N_B = 4
HD_B = 64
D_A = N_A * HD_A
D_B = N_B * 2 * HD_B
CHUNK = 64
NORM_EPS = 1e-6
NEG_INF = -1e30
ALIBI_SLOPES = (0.25, 0.0625, 0.015625, 0.00390625)

LOG2E = math.log2(math.e)
LANES = 128
MXU_WIDTH = 256
FF_CHUNKS = (1536, 1280)
assert sum(FF_CHUNKS) == D_FF and all(c % MXU_WIDTH == 0 for c in FF_CHUNKS)
SCAN_BATCH = 8
CAST_BYTES = 3 << 20
TOKEN_TILE = 512
PROJ_TILE = 1024
ATTN_BLOCK = 256
KEY_BLOCKS_PER_STEP = 2
Q_BLOCKS_PER_STEP = 2
assert KEY_BLOCKS_PER_STEP == Q_BLOCKS_PER_STEP == 2
AUG_ROWS = 32
VMEM_LIMIT = 56 * 1024 * 1024
N_PAIRS = N_A // 2

F32 = jnp.float32
BF16 = jnp.bfloat16


def _params(n_axes):
    return pltpu.CompilerParams(dimension_semantics=("arbitrary",) * n_axes,
                                vmem_limit_bytes=VMEM_LIMIT)


def _resident(shape):
    return pl.BlockSpec(shape, lambda *_: (0,) * len(shape), pipeline_mode=pl.Buffered(1))


def _rms(xf, g):
    ms = jnp.mean(xf * xf, axis=-1, keepdims=True)
    return xf * lax.rsqrt(ms + NORM_EPS) * g


def _swiglu_residual(x, gn_ref, w1_ref, w2_ref):
    h = _rms(x, gn_ref[...]).astype(BF16)
    acc = None
    lo = 0
    for width in FF_CHUNKS:
        g = jnp.dot(h, w1_ref[:, lo:lo + width], preferred_element_type=F32)
        u = jnp.dot(h, w1_ref[:, D_FF + lo:D_FF + lo + width], preferred_element_type=F32)
        a = (g * jax.nn.sigmoid(g) * u).astype(BF16)
        d = jnp.dot(a, w2_ref[lo:lo + width, :], preferred_element_type=F32)
        acc = d if acc is None else acc + d
        lo += width
    return x + 0.5 * acc


def _cast_rows(rows_total, cols):
    rows = max(16, CAST_BYTES // (4 * cols) // 16 * 16)
    while rows_total % rows:
        rows -= 16
    return rows


def _stage_shape(shapes):
    return {c: (2, max(_cast_rows(r, cc) for r, cc in shapes if cc == c), c) for _, c in shapes}


def _cast_weight(w_hbm, w16_ref, stage_ref, sem_ref):
    r_total, cols = w_hbm.shape
    rows = _cast_rows(r_total, cols)
    n = r_total // rows

    def copy(c):
        return pltpu.make_async_copy(w_hbm.at[pl.ds(c * rows, rows), :],
                                     stage_ref.at[c % 2, pl.ds(0, rows), :], sem_ref.at[c % 2])

    copy(0).start()
    for c in range(n):
        if c + 1 < n:
            copy(c + 1).start()
        copy(c).wait()
        w16_ref[pl.ds(c * rows, rows), :] = stage_ref[c % 2, 0:rows, :].astype(BF16)


def _weights_prologue(i, w_hbm_refs, w16_refs, w16_hbm_refs, stages, sem_in, sem_out):
    def publish(k):
        return pltpu.make_async_copy(w16_refs[k], w16_hbm_refs[k], sem_out.at[k])

    @pl.when(i == 0)
    def _():
        for w_hbm, w16 in zip(w_hbm_refs, w16_refs):
            _cast_weight(w_hbm, w16, stages[w_hbm.shape[1]], sem_in)
        for k in range(len(w16_refs)):
            publish(k).start()

    return publish


def _weights_epilogue(i, publish, n_weights):
    @pl.when(i == pl.num_programs(0) - 1)
    def _():
        for k in range(n_weights):
            publish(k).wait()


def _weight_plumbing(weights, cast):
    if not cast:
        return [_resident(w.shape) for w in weights], [], [], []
    any_spec = pl.BlockSpec(memory_space=pl.ANY)
    shapes = [w.shape for w in weights]
    stages = _stage_shape(shapes)
    scratch = ([pltpu.VMEM(sh, BF16) for sh in shapes] + [pltpu.VMEM(stages[c], F32) for c in sorted(stages)]
               + [pltpu.SemaphoreType.DMA((2,)), pltpu.SemaphoreType.DMA((len(weights),))])
    return ([any_spec] * len(weights), [any_spec] * len(weights),
            [jax.ShapeDtypeStruct(sh, BF16) for sh in shapes], scratch)


def _split_weight_refs(refs, n_weights, shapes):
    w16 = refs[:n_weights]
    cols = sorted({c for _, c in shapes})
    stage_refs = refs[n_weights:n_weights + len(cols)]
    sem_in, sem_out = refs[n_weights + len(cols):]
    return w16, dict(zip(cols, stage_refs)), sem_in, sem_out


def _ffn1_kernel(x_ref, gn_ref, w1_in, w2_in, o_ref, *rest, cast):
    if cast:
        i = pl.program_id(0)
        w_hbm = (w1_in, w2_in)
        w16, stages, sem_in, sem_out = _split_weight_refs(rest[2:], 2, [w.shape for w in w_hbm])
        publish = _weights_prologue(i, w_hbm, w16, rest[:2], stages, sem_in, sem_out)
        o_ref[...] = _swiglu_residual(x_ref[...], gn_ref, *w16)
        _weights_epilogue(i, publish, 2)
    else:
        o_ref[...] = _swiglu_residual(x_ref[...], gn_ref, w1_in, w2_in)


def _ffn1(x, gn, w1, w2, cast):
    n = x.shape[0]
    tm = min(TOKEN_TILE, n)
    w_specs, w_out_specs, w_out_shapes, scratch = _weight_plumbing([w1, w2], cast)
    out = pl.pallas_call(
        functools.partial(_ffn1_kernel, cast=cast),
        grid=(n // tm,),
        in_specs=[pl.BlockSpec((tm, D_MODEL), lambda i: (i, 0)), _resident((1, D_MODEL))] + w_specs,
        out_specs=[pl.BlockSpec((tm, D_MODEL), lambda i: (i, 0))] + w_out_specs,
        out_shape=[jax.ShapeDtypeStruct((n, D_MODEL), F32)] + w_out_shapes,
        scratch_shapes=scratch,
        compiler_params=_params(1),
        name="ffn1",
    )(x, gn, w1, w2)
    return out if cast else out[0]


def _post_kernel(x_ref, oa_ref, ob_ref, gn_ref, gf_ref, wo_in, w1_in, w2_in, y_ref, *rest, final, cast):
    if cast:
        i = pl.program_id(0)
        w_hbm = (wo_in, w1_in, w2_in)
        w16, stages, sem_in, sem_out = _split_weight_refs(rest[3:], 3, [w.shape for w in w_hbm])
        publish = _weights_prologue(i, w_hbm, w16, rest[:3], stages, sem_in, sem_out)
        wo_ref, w1_ref, w2_ref = w16
    else:
        wo_ref, w1_ref, w2_ref = wo_in, w1_in, w2_in
    x = x_ref[...]
    x = x + jnp.dot(oa_ref[...], wo_ref[:D_A, :], preferred_element_type=F32)
    x = x + jnp.dot(ob_ref[...], wo_ref[D_A:, :], preferred_element_type=F32)
    x = _swiglu_residual(x, gn_ref, w1_ref, w2_ref)
    y_ref[...] = _rms(x, gf_ref[...]) if final else x
    if cast:
        _weights_epilogue(i, publish, 3)


def _post(x, oa, ob, gn, gf, wo, w1, w2, final, cast):
    n = x.shape[0]
    tm = min(TOKEN_TILE, n)
    row = lambda i: (i, 0)
    w_specs, w_out_specs, w_out_shapes, scratch = _weight_plumbing([wo, w1, w2], cast)
    out = pl.pallas_call(
        functools.partial(_post_kernel, final=final, cast=cast),
        grid=(n // tm,),
        in_specs=[pl.BlockSpec((tm, D_MODEL), row), pl.BlockSpec((tm, D_A), row), pl.BlockSpec((tm, D_B), row),
                  _resident((1, D_MODEL)), _resident((1, D_MODEL))] + w_specs,
        out_specs=[pl.BlockSpec((tm, D_MODEL), row)] + w_out_specs,
        out_shape=[jax.ShapeDtypeStruct((n, D_MODEL), F32)] + w_out_shapes,
        scratch_shapes=scratch,
        compiler_params=_params(1),
        name="post",
    )(x, oa, ob, gn, gf, wo, w1, w2)
    return out if cast else out[0]


PROJ_GATE_COL = 3 * D_A
PROJ_B_COL = PROJ_GATE_COL + LANES
PROJ_COLS = PROJ_B_COL + 3 * D_B


def _proj_weights_prologue(i, wt_hbm, w16_ref, w16_hbm, stage_ref, stage_f_ref, sem_in, sem_out):
    rows = stage_ref.shape[1]

    def publish(k):
        return pltpu.make_async_copy(w16_ref, w16_hbm, sem_out.at[k])

    @pl.when(i == 0)
    def _():
        jobs = [(r0 + c * rows, c0 + c * rows) for r0, c0 in ((0, 0), (PROJ_GATE_COL + N_A, PROJ_B_COL))
                for c in range(PROJ_GATE_COL // rows)]

        def copy(n):
            return pltpu.make_async_copy(wt_hbm.at[pl.ds(jobs[n][0], rows), :], stage_ref.at[n % 2], sem_in.at[n % 2])

        gate = pltpu.make_async_copy(wt_hbm.at[pl.ds(PROJ_GATE_COL, N_A), :], stage_f_ref.at[pl.ds(0, N_A), :],
                                     sem_in.at[2])
        gate.start()
        copy(0).start()
        for n, (_, col) in enumerate(jobs):
            if n + 1 < len(jobs):
                copy(n + 1).start()
            copy(n).wait()
            w16_ref[:, col:col + rows] = stage_ref[n % 2].T.astype(BF16)
        stage_f_ref[N_A:, :] = jnp.zeros((LANES - N_A, D_MODEL), F32)
        gate.wait()
        w16_ref[:, PROJ_GATE_COL:PROJ_B_COL] = stage_f_ref[...].T.astype(BF16)
        publish(0).start()

    return publish


def _proj_kernel(x_ref, gn_ref, bf_ref, *refs, seg, kblk, tokens_minor, cast):
    n_out = 11
    if cast:
        i = pl.program_id(0)
        wt_hbm, out_refs, w16_hbm = refs[0], refs[1:1 + n_out], refs[1 + n_out]
        w_ref, stage_ref, stage_f_ref, sem_in, sem_out = refs[2 + n_out:]
        publish = _proj_weights_prologue(i, wt_hbm, w_ref, w16_hbm, stage_ref, stage_f_ref, sem_in, sem_out)
    else:
        w_ref, out_refs = refs[0], refs[1:]
    tm = x_ref.shape[0]
    h = _rms(x_ref[...], gn_ref[...]).astype(BF16)
    p = jnp.dot(h, w_ref[...], preferred_element_type=F32)
    pa, pb = p[:, :PROJ_GATE_COL], p[:, PROJ_B_COL:]
    z = p[:, PROJ_GATE_COL:PROJ_B_COL] + bf_ref[...]
    lft = (jnp.minimum(z, 0.0) - jnp.log1p(jnp.exp(-jnp.abs(z)))).T
    ka, va = pa[:, D_A:2 * D_A], pa[:, 2 * D_A:]
    kb, vb = pb[:, D_B:2 * D_B], pb[:, 2 * D_B:]
    qa16 = (pa[:, :D_A] * (HD_A ** -0.5 * LOG2E)).astype(BF16)
    qb16 = (pb[:, :D_B] * (HD_B ** -0.5 * LOG2E)).astype(BF16)
    if tokens_minor:
        kat_ref, vat_ref, lft_ref, kbt_ref, vb_ref, qa16_ref, kat16_ref, va16_ref, qb16_ref, kbt16_ref, vb16_ref = out_refs
        kat, vat, kbt = ka.T, va.T, kb.T
    else:
        ka_ref, va_ref, lft_ref, kb_ref, vb_ref, qa16_ref, ka16_ref, va16_ref, qb16_ref, kb16_ref, vb16_ref = out_refs
        ka16_ref[...] = ka.astype(BF16)
        kb16_ref[...] = kb.astype(BF16)
        for hh in range(N_A):
            cols = slice(hh * HD_A, (hh + 1) * HD_A)
            ka_ref[pl.ds(hh, tm, stride=N_A), :] = ka[:, cols]
            va_ref[pl.ds(hh, tm, stride=N_A), :] = va[:, cols]
            kb_ref[pl.ds(hh, tm, stride=N_A), :] = kb[:, cols]
    qa16_ref[...] = qa16
    qb16_ref[...] = qb16
    va16_ref[...] = va.astype(BF16)
    vb16_ref[...] = vb.astype(BF16)
    for hh in range(N_B):
        vb_ref[pl.ds(hh, tm, stride=N_B), :] = vb[:, hh * LANES:(hh + 1) * LANES]
    for bb in range(tm // seg):
        cols = slice(bb * seg, (bb + 1) * seg)
        lft_ref[bb] = lft[:N_A, cols]
        if tokens_minor:
            kat_ref[bb] = kat[:, cols]
            vat_ref[bb] = vat[:, cols]
            kbt_ref[bb] = kbt[:, cols]
            for jj in range(seg // kblk):
                kc = slice(bb * seg + jj * kblk, bb * seg + (jj + 1) * kblk)
                kat16_ref[bb, jj] = kat[:, kc].astype(BF16)
                kbt16_ref[bb, jj] = kbt[:, kc].astype(BF16)
    if cast:
        _weights_epilogue(i, publish, 1)


def _proj(x, gn, bf, weight, b, t, tokens_minor):
    cast = weight.dtype != BF16
    n = x.shape[0]
    tm = min(PROJ_TILE if tokens_minor else TOKEN_TILE, n)
    seg = min(tm, t)
    nb = tm // seg
    nt = t // seg
    kblk = min(ATTN_BLOCK, seg)
    row = lambda i: (i, 0)
    tok = lambda i: (i // nt, 0, i % nt)
    blk = lambda i: (i // nt, i % nt, 0, 0)
    b16 = (jax.ShapeDtypeStruct((n, D_A), BF16), pl.BlockSpec((tm, D_A), row))
    lft = (jax.ShapeDtypeStruct((b, N_A, t), F32), pl.BlockSpec((nb, N_A, seg), tok))
    vb = (jax.ShapeDtypeStruct((n * N_B, LANES), F32), pl.BlockSpec((tm * N_B, LANES), row))
    if tokens_minor:
        kv = (jax.ShapeDtypeStruct((b, D_A, t), F32), pl.BlockSpec((nb, D_A, seg), tok))
        k16 = (jax.ShapeDtypeStruct((b, t // kblk, D_A, kblk), BF16),
               pl.BlockSpec((nb, seg // kblk, D_A, kblk), blk))
    else:
        kv = (jax.ShapeDtypeStruct((n * N_A, HD_A), F32), pl.BlockSpec((tm * N_A, HD_A), row))
        k16 = b16
    outs = [kv, kv, lft, kv, vb, b16, k16, b16, b16, k16, b16]
    w_shape = (D_MODEL, PROJ_COLS)
    if cast:
        any_spec = pl.BlockSpec(memory_space=pl.ANY)
        w_spec = any_spec
        outs = outs + [(jax.ShapeDtypeStruct(w_shape, BF16), any_spec)]
        scratch = [pltpu.VMEM(w_shape, BF16), pltpu.VMEM((2, 3 * D_A // 2, D_MODEL), F32),
                   pltpu.VMEM((LANES, D_MODEL), F32), pltpu.SemaphoreType.DMA((3,)), pltpu.SemaphoreType.DMA((1,))]
    else:
        w_spec = _resident(w_shape)
        scratch = []
    res = pl.pallas_call(
        functools.partial(_proj_kernel, seg=seg, kblk=kblk, tokens_minor=tokens_minor, cast=cast),
        grid=(n // tm,),
        in_specs=[pl.BlockSpec((tm, D_MODEL), row), _resident((1, D_MODEL)), _resident(bf.shape), w_spec],
        out_specs=[o[1] for o in outs],
        out_shape=[o[0] for o in outs],
        scratch_shapes=scratch,
        compiler_params=_params(1),
        name="proj",
    )(x, gn, bf, weight)
    return res[:11], (res[11] if cast else weight)


def _split3(x):
    hi = x.astype(BF16).astype(F32)
    r = x - hi
    mid = r.astype(BF16).astype(F32)
    lo = (r - mid).astype(BF16).astype(F32)
    return hi, mid, lo


def _cum_kernel(lft_ref, aug_ref, pad_ref, *, t, blocked):
    w = ATTN_BLOCK
    nb = lft_ref.shape[0]
    rows = nb * N_A
    pad_ref[...] = jnp.zeros(pad_ref.shape, F32)
    for bb in range(nb):
        pad_ref[bb * N_A:(bb + 1) * N_A, 0:t] = lft_ref[bb]
    r = lax.broadcasted_iota(jnp.int32, (w, w), 0)
    c = lax.broadcasted_iota(jnp.int32, (w, w), 1)
    tri = jnp.where(r <= c, 1.0, 0.0).astype(BF16)
    zero = jnp.zeros((N_A, w), F32)
    carry = jnp.zeros((rows, 1), F32)
    for s in range(pad_ref.shape[1] // w):
        lo_col = s * w
        cols = min(w, t - lo_col)
        x = jnp.concatenate(_split3(pad_ref[:, lo_col:lo_col + w]), axis=0).astype(BF16)
        y = jnp.dot(x, tri, preferred_element_type=F32)
        cs = y[0:rows] + y[rows:2 * rows] + y[2 * rows:3 * rows] + carry
        carry = cs[:, w - 1:w]
        pieces = _split3(cs * -LOG2E)
        for bb in range(nb):
            sl = slice(bb * N_A, (bb + 1) * N_A)
            aug = jnp.concatenate([p[sl] for p in pieces] + [zero], axis=0).astype(BF16)
            if blocked:
                aug_ref[bb, s] = aug
            else:
                aug_ref[bb, :, lo_col:lo_col + cols] = aug[:, :cols]


def _cum_aug(lft, blocked):
    b, _, t = lft.shape
    w = ATTN_BLOCK
    t_pad = -(-t // w) * w
    nb = math.gcd(b, SCAN_BATCH)
    if blocked:
        out_shape = jax.ShapeDtypeStruct((b, t // w, AUG_ROWS, w), BF16)
        out_spec = pl.BlockSpec((nb, t // w, AUG_ROWS, w), lambda i: (i, 0, 0, 0))
    else:
        out_shape = jax.ShapeDtypeStruct((b, AUG_ROWS, t), BF16)
        out_spec = pl.BlockSpec((nb, AUG_ROWS, t), lambda i: (i, 0, 0))
    return pl.pallas_call(
        functools.partial(_cum_kernel, t=t, blocked=blocked),
        grid=(b // nb,),
        in_specs=[pl.BlockSpec((nb, N_A, t), lambda i: (i, 0, 0))],
        out_specs=out_spec,
        out_shape=out_shape,
        scratch_shapes=[pltpu.VMEM((nb * N_A, t_pad), F32)],
        compiler_params=_params(1),
        name="cum_aug",
    )(lft)


def _lane_halves(q):
    lane = lax.broadcasted_iota(jnp.int32, q.shape, 1)
    zero = jnp.zeros_like(q)
    return jnp.where(lane < HD_A, q, zero), jnp.where(lane >= HD_A, q, zero)


def _piece_selector(shape, head, n_heads):
    lane = lax.broadcasted_iota(jnp.int32, shape, 1)
    hit = (lane == head) | (lane == n_heads + head) | (lane == 2 * n_heads + head)
    return jnp.where(hit, 1.0, 0.0).astype(BF16)


def _alibi_rows(k_pos):
    w = k_pos.shape[1]
    row = lax.broadcasted_iota(jnp.int32, (4 * N_B, 1), 0)
    head = row % N_B
    slope = jnp.zeros((4 * N_B, 1), F32)
    for h in range(N_B):
        slope = jnp.where(head == h, ALIBI_SLOPES[h] * LOG2E, slope)
    hi, mid, lo = _split3(slope * k_pos)
    piece = jnp.where(row < N_B, hi, jnp.where(row < 2 * N_B, mid, jnp.where(row < 3 * N_B, lo, 0.0)))
    return jnp.concatenate([piece.astype(BF16), jnp.zeros((LANES - 4 * N_B, w), BF16)], axis=0)


def _alibi_fold(slope, q_pos, k_pos):
    return (-2.0 * LOG2E * slope) * jnp.maximum(k_pos - q_pos, 0).astype(F32)


def _diff_finish(o1, o2, lam, sub_ref, lambda_init):
    return _rms(o1 - lam * o2, sub_ref[...]) * (1.0 - lambda_init)


def _lambda(lq1_ref, lk1_ref, lq2_ref, lk2_ref, lambda_init):
    return (jnp.exp(jnp.sum(lq1_ref[...] * lk1_ref[...], axis=1, keepdims=True))
            - jnp.exp(jnp.sum(lq2_ref[...] * lk2_ref[...], axis=1, keepdims=True)) + lambda_init)


def _attn_prompt_kernel(qa_ref, qb_ref, kat_ref, kbt_ref, va_ref, vb_ref, aug_ref,
                        lq1_ref, lk1_ref, lq2_ref, lk2_ref, sub_ref, oa_ref, ob_ref,
                        qs_ref, m_ref, acc_ref, tbl_ref, *, lambda_init):
    tq = ATTN_BLOCK
    n_chain = N_PAIRS + N_B

    @pl.when(pl.program_id(0) == 0)
    def _():
        r = lax.broadcasted_iota(jnp.int32, (2 * tq, tq), 0)
        c = lax.broadcasted_iota(jnp.int32, (2 * tq, tq), 1)
        q_idx = jnp.where(r >= tq, r - tq, r)
        tbl_ref[0] = jnp.where(q_idx >= c, 0.0, NEG_INF)
        visible = (q_idx // CHUNK) >= (c // CHUNK)
        for h in range(N_B):
            tbl_ref[1 + h] = jnp.where(visible, _alibi_fold(ALIBI_SLOPES[h], q_idx, c), NEG_INF)
        for qset in range(Q_BLOCKS_PER_STEP):
            for hc in range(n_chain):
                ch = qset * n_chain + hc
                if hc < N_PAIRS:
                    qs_ref[ch, 0:tq, LANES:] = _piece_selector((tq, LANES), 2 * hc, N_A)
                    qs_ref[ch, tq:, LANES:] = _piece_selector((tq, LANES), 2 * hc + 1, N_A)
                else:
                    sel = _piece_selector((tq, LANES), hc - N_PAIRS, N_B)
                    qs_ref[ch, 0:tq, LANES:] = sel
                    qs_ref[ch, tq:, LANES:] = sel

    def step(sets, first, width=1):
        wk = width * tq
        ones = jnp.ones((wk, LANES), BF16)
        for qset, j in sets:
            rows = pl.ds(pl.multiple_of(j * tq, tq), wk)
            blocks = [j + d for d in range(width)]
            aug_a = jnp.concatenate([jnp.concatenate([aug_ref[0, jb] for jb in blocks], axis=1),
                                     jnp.zeros((LANES - AUG_ROWS, wk), BF16)], axis=0)
            aug_b = _alibi_rows((j * tq + lax.broadcasted_iota(jnp.int32, (1, wk), 1)).astype(F32))
            for hc in range(n_chain):
                ch = qset * n_chain + hc
                if hc < N_PAIRS:
                    kt_ref, v_ref, g, aug, tbl = kat_ref, va_ref, hc, aug_a, 0
                else:
                    kt_ref, v_ref, g, aug, tbl = kbt_ref, vb_ref, hc - N_PAIRS, aug_b, 1 + hc - N_PAIRS
                kt = jnp.concatenate([kt_ref[0, jb, g * LANES:(g + 1) * LANES, :] for jb in blocks], axis=1)
                s = jnp.dot(qs_ref[ch], jnp.concatenate([kt, aug], axis=0), preferred_element_type=F32)
                v = v_ref[rows, g * LANES:(g + 1) * LANES]
                if first:
                    s = s + tbl_ref[tbl]
                    m_new = jnp.broadcast_to(jnp.max(s, axis=1, keepdims=True), (2 * tq, LANES))
                else:
                    m_prev = m_ref[ch]
                    m_new = jnp.maximum(m_prev, jnp.max(s, axis=1, keepdims=True))
                p = jnp.exp2(s - jnp.concatenate([m_new] * (wk // LANES), axis=1))
                pv = jnp.dot(p.astype(BF16), jnp.concatenate([v, ones], axis=1), preferred_element_type=F32)
                if first:
                    acc_ref[ch] = pv
                else:
                    alpha = jnp.exp2(m_prev - m_new)
                    acc_ref[ch] = jnp.concatenate([alpha, alpha], axis=1) * acc_ref[ch] + pv
                m_ref[ch] = m_new

    def wide_body(jj, carry):
        j = jj * KEY_BLOCKS_PER_STEP
        step([(0, j), (1, j)], False, KEY_BLOCKS_PER_STEP)
        return carry

    lane = lax.broadcasted_iota(jnp.int32, (tq, LANES), 1)
    lam = _lambda(lq1_ref, lk1_ref, lq2_ref, lk2_ref, lambda_init)

    def query_blocks(pair, carry):
        even = pair * Q_BLOCKS_PER_STEP
        for qset in range(Q_BLOCKS_PER_STEP):
            q_rows = pl.ds(pl.multiple_of((even + qset) * tq, tq), tq)
            for hc in range(n_chain):
                q_ref, g = (qa_ref, hc) if hc < N_PAIRS else (qb_ref, hc - N_PAIRS)
                lo, hi = _lane_halves(q_ref[q_rows, g * LANES:(g + 1) * LANES])
                qs_ref[qset * n_chain + hc, 0:tq, 0:LANES] = lo
                qs_ref[qset * n_chain + hc, tq:, 0:LANES] = hi
        step([(0, even), (1, even + 1)], True)
        step([(1, even)], False)
        lax.fori_loop(0, pair, wide_body, 0)
        for qset in range(Q_BLOCKS_PER_STEP):
            q_rows = pl.ds(pl.multiple_of((even + qset) * tq, tq), tq)
            for hc in range(n_chain):
                ch = qset * n_chain + hc
                o = acc_ref[ch, :, 0:LANES] / acc_ref[ch, :, LANES:]
                if hc < N_PAIRS:
                    oa_ref[q_rows, hc * LANES:(hc + 1) * LANES] = jnp.where(
                        lane < HD_A, o[:tq], o[tq:]).astype(BF16)
                else:
                    h = hc - N_PAIRS
                    ob_ref[q_rows, h * LANES:(h + 1) * LANES] = _diff_finish(
                        o[:tq], o[tq:], lam, sub_ref, lambda_init).astype(BF16)
        return carry

    lax.fori_loop(0, qa_ref.shape[0] // (Q_BLOCKS_PER_STEP * tq), query_blocks, 0)


def _attn_prompt(qa16, qb16, kat16, kbt16, va16, vb16, aug, lams, sub, b, t, lambda_init):
    tq = ATTN_BLOCK
    nq = t // tq
    kspec = pl.BlockSpec((1, nq, D_A, tq), lambda bi: (bi, 0, 0, 0))
    vspec = pl.BlockSpec((t, D_A), lambda bi: (bi, 0))
    small = [_resident((1, HD_B))] * 4 + [_resident((1, 2 * HD_B))]
    n_chain = Q_BLOCKS_PER_STEP * (N_PAIRS + N_B)
    return pl.pallas_call(
        functools.partial(_attn_prompt_kernel, lambda_init=lambda_init),
        grid=(b,),
        in_specs=[vspec, vspec, kspec, kspec, vspec, vspec,
                  pl.BlockSpec((1, nq, AUG_ROWS, tq), lambda bi: (bi, 0, 0, 0))] + small,
        out_specs=[vspec, vspec],
        out_shape=[jax.ShapeDtypeStruct((b * t, D_A), BF16), jax.ShapeDtypeStruct((b * t, D_B), BF16)],
        scratch_shapes=[pltpu.VMEM((n_chain, 2 * tq, 2 * LANES), BF16),
                        pltpu.VMEM((n_chain, 2 * tq, LANES), F32),
                        pltpu.VMEM((n_chain, 2 * tq, 2 * LANES), F32),
                        pltpu.VMEM((1 + N_B, 2 * tq, tq), F32)],
        compiler_params=_params(1),
        name="attn_prompt",
    )(qa16, qb16, kat16, kbt16, va16, vb16, aug, *lams, sub)


def _nt_dot(a, b):
    return lax.dot_general(a, b, (((1,), (1,)), ((), ())), preferred_element_type=F32)


def _attn_cached_kernel(qa_ref, qb_ref, ckat_ref, cvat_ref, ckbt_ref, cvb_ref,
                        ka_ref, va_ref, kb_ref, vb_ref, aug_ref,
                        lq1_ref, lk1_ref, lq2_ref, lk2_ref, sub_ref, oa_ref, ob_ref,
                        kn_ref, *, lambda_init):
    tq = qa_ref.shape[0]
    past = ckat_ref.shape[2]
    t_all = past + tq
    n_chain = N_PAIRS + N_B
    r = lax.broadcasted_iota(jnp.int32, (2 * tq, LANES), 0)
    c = lax.broadcasted_iota(jnp.int32, (2 * tq, LANES), 1)
    q_idx = jnp.where(r >= tq, r - tq, r)
    is_key = c < tq
    causal_tbl = jnp.where((q_idx >= c) & is_key, 0.0, NEG_INF)
    visible = (((past + q_idx) // CHUNK) >= ((past + c) // CHUNK)) & is_key
    lane = lax.broadcasted_iota(jnp.int32, (tq, LANES), 1)
    lam = _lambda(lq1_ref, lk1_ref, lq2_ref, lk2_ref, lambda_init)
    aug_c = jnp.concatenate([aug_ref[0, :, 0:past], jnp.zeros((LANES - AUG_ROWS, past), BF16)], axis=0)
    alibi_c = _alibi_rows(lax.broadcasted_iota(jnp.int32, (1, past), 1).astype(F32))
    alibi_n = _alibi_rows((past + lax.broadcasted_iota(jnp.int32, (1, LANES), 1)).astype(F32))
    kn_ref[...] = jnp.zeros(kn_ref.shape, BF16)
    row_pad = jnp.zeros((LANES - tq, LANES), BF16)

    def new_keys_t(k_new):
        return jnp.concatenate([k_new, row_pad], axis=0).astype(F32).T.astype(BF16)

    probs = []
    for ch in range(n_chain):
        fox = ch < N_PAIRS
        g = ch if fox else ch - N_PAIRS
        rows = slice(g * LANES, (g + 1) * LANES)
        if fox:
            lo, hi = _lane_halves(qa_ref[:, rows])
            sel_lo = _piece_selector((tq, LANES), 2 * g, N_A)
            sel_hi = _piece_selector((tq, LANES), 2 * g + 1, N_A)
            k_old = jnp.concatenate([ckat_ref[0, rows, :].astype(BF16), aug_c], axis=0)
            kn_ref[ch, 0:LANES, :] = new_keys_t(ka_ref[:, rows])
            kn_ref[ch, LANES:LANES + AUG_ROWS, 0:tq] = aug_ref[0, :, past:t_all]
            tbl = causal_tbl
        else:
            lo, hi = _lane_halves(qb_ref[:, rows])
            sel_lo = sel_hi = _piece_selector((tq, LANES), g, N_B)
            k_old = jnp.concatenate([ckbt_ref[0, rows, :].astype(BF16), alibi_c], axis=0)
            kn_ref[ch, 0:LANES, :] = new_keys_t(kb_ref[:, rows])
            kn_ref[ch, LANES:, :] = alibi_n
            tbl = jnp.where(visible, _alibi_fold(ALIBI_SLOPES[g], q_idx, c), NEG_INF)
        qs = jnp.concatenate([jnp.concatenate([lo, sel_lo], axis=1), jnp.concatenate([hi, sel_hi], axis=1)], axis=0)
        s_old = jnp.dot(qs, k_old, preferred_element_type=F32)
        s_new = jnp.dot(qs, kn_ref[ch], preferred_element_type=F32) + tbl
        m = jnp.maximum(jnp.max(s_old, axis=1, keepdims=True), jnp.max(s_new, axis=1, keepdims=True))
        p_old = jnp.exp2(s_old - m)
        p_new = jnp.exp2(s_new - m)
        l = jnp.sum(p_old, axis=1, keepdims=True) + jnp.sum(p_new, axis=1, keepdims=True)
        probs.append((p_old.astype(BF16), p_new.astype(BF16), l))

    for ch in range(n_chain):
        fox = ch < N_PAIRS
        g = ch if fox else ch - N_PAIRS
        rows = slice(g * LANES, (g + 1) * LANES)
        p_old, p_new, l = probs[ch]
        v_new = jnp.concatenate([(va_ref if fox else vb_ref)[:, rows], row_pad], axis=0)
        o_new = jnp.dot(p_new, v_new, preferred_element_type=F32)
        if fox:
            o = (_nt_dot(p_old, cvat_ref[0, rows, :].astype(BF16)) + o_new) / l
            oa_ref[:, rows] = jnp.where(lane < HD_A, o[:tq], o[tq:]).astype(BF16)
        else:
            v_old = cvb_ref[0, pl.ds(g, past, stride=N_B), :].astype(BF16)
            o = (jnp.dot(p_old, v_old, preferred_element_type=F32) + o_new) / l
            ob_ref[:, rows] = _diff_finish(o[:tq], o[tq:], lam, sub_ref, lambda_init).astype(BF16)


def _attn_cached(qa16, qb16, ckat, cvat, ckbt, cvb, ka16, va16, kb16, vb16, aug, lams, sub, b, tq, lambda_init):
    past = ckat.shape[2]
    t_all = past + tq
    n_chain = N_PAIRS + N_B
    qspec = pl.BlockSpec((tq, D_A), lambda bi: (bi, 0))
    old_t = pl.BlockSpec((1, D_A, past), lambda bi: (bi, 0, 0))
    small = [_resident((1, HD_B))] * 4 + [_resident((1, 2 * HD_B))]
    return pl.pallas_call(
        functools.partial(_attn_cached_kernel, lambda_init=lambda_init),
        grid=(b,),
        in_specs=[qspec, qspec, old_t, old_t, old_t,
                  pl.BlockSpec((1, past * N_B, LANES), lambda bi: (bi, 0, 0)),
                  qspec, qspec, qspec, qspec,
                  pl.BlockSpec((1, AUG_ROWS, t_all), lambda bi: (bi, 0, 0))] + small,
        out_specs=[qspec, qspec],
        out_shape=[jax.ShapeDtypeStruct((b * tq, D_A), BF16), jax.ShapeDtypeStruct((b * tq, D_B), BF16)],
        scratch_shapes=[pltpu.VMEM((n_chain, 2 * LANES, LANES), BF16)],
        compiler_params=_params(1),
        name="attn_cached",
    )(qa16, qb16, ckat, cvat, ckbt, cvb, ka16, va16, kb16, vb16, aug, *lams, sub)


def _tokens_minor(x):
    b, t = x.shape[:2]
    return jnp.moveaxis(x.reshape(b, t, -1), 1, 2)


def _tokens_major(xt, tail):
    b, _, t = xt.shape
    return jnp.moveaxis(xt, 2, 1).reshape(b, t, *tail)


def _layer(x, cache, w, lambda_init, final):
    b, t, _ = x.shape
    n = b * t
    cast = w["ffn16"] is None
    if cast:
        x1, w1a, w2a = _ffn1(x.reshape(n, D_MODEL), w["gn1"], w["w1a"], w["w2a"], cast=True)
    else:
        w1a, w2a, wo, w1b, w2b = w["ffn16"]
        x1 = _ffn1(x.reshape(n, D_MODEL), w["gn1"], w1a, w2a, cast=False)
    (k_a, v_a, lft, k_b, vb, qa16, ka16, va16, qb16, kb16, vb16), w_in16 = _proj(
        x1, w["gmix"], w["bf"], w["w_in_t"] if cast else w["w_in16"], b, t, tokens_minor=cache is None)
    if cast:
        w["w_in16"] = w_in16
    if cache is None:
        aug = _cum_aug(lft, blocked=True)
        oa, ob = _attn_prompt(qa16, qb16, ka16, kb16, va16, vb16, aug, w["lams"], w["sub"], b, t, lambda_init)
        news = (_tokens_major(k_a, (N_A, HD_A)), _tokens_major(v_a, (N_A, HD_A)), _tokens_major(lft, (N_A,)),
                _tokens_major(k_b, (N_B, 2, HD_B)), vb.reshape(b, t, N_B, 2 * HD_B))
    else:
        cfk, cfv, cflf, cdk, cdv = cache
        past = cfk.shape[1]
        aug = _cum_aug(jnp.concatenate([_tokens_minor(cflf.astype(F32)), lft], axis=2), blocked=False)
        oa, ob = _attn_cached(qa16, qb16, _tokens_minor(cfk), _tokens_minor(cfv), _tokens_minor(cdk),
                              cdv.reshape(b, past * N_B, 2 * HD_B), ka16, va16, kb16, vb16, aug,
                              w["lams"], w["sub"], b, t, lambda_init)
        news = (k_a.reshape(b, t, N_A, HD_A), v_a.reshape(b, t, N_A, HD_A), _tokens_major(lft, (N_A,)),
                k_b.reshape(b, t, N_B, 2, HD_B), vb.reshape(b, t, N_B, 2 * HD_B))
    if cast:
        y, wo, w1b, w2b = _post(x1, oa, ob, w["gn2"], w["gfin"], w["wo"], w["w1b"], w["w2b"], final, cast=True)
        w["ffn16"] = (w1a, w2a, wo, w1b, w2b)
    else:
        y = _post(x1, oa, ob, w["gn2"], w["gfin"], wo, w1b, w2b, final, cast=False)
    return y.reshape(b, t, D_MODEL), news


def kernel(x_prompt, x_sample, cache_fox_k, cache_fox_v, cache_fox_logf, cache_diff_k, cache_diff_v,
           norm_ffn1, w_ffn1_in, w_ffn1_out, norm_mix, w_in, b_forget,
           lambda_q1, lambda_k1, lambda_q2, lambda_k2, diff_subln, w_out,
           norm_ffn2, w_ffn2_in, w_ffn2_out, norm_final):
    depth = w_in.shape[0]
    xp, xs = x_prompt, x_sample
    outs_p, outs_s = [], []
    gfin = norm_final.reshape(1, D_MODEL)
    for l in range(depth):
        lambda_init = 0.8 - 0.6 * math.exp(-0.3 * l)
        w = {
            "gn1": norm_ffn1[l].reshape(1, D_MODEL),
            "w1a": w_ffn1_in[l], "w2a": w_ffn1_out[l],
            "gmix": norm_mix[l].reshape(1, D_MODEL),
            "w_in_t": jnp.swapaxes(w_in[l], 0, 1),
            "bf": jnp.pad(b_forget[l], (0, LANES - N_A)).reshape(1, LANES),
            "lams": [v[l].reshape(1, HD_B) for v in (lambda_q1, lambda_k1, lambda_q2, lambda_k2)],
            "sub": diff_subln[l].reshape(1, 2 * HD_B),
            "wo": w_out[l],
            "gn2": norm_ffn2[l].reshape(1, D_MODEL),
            "w1b": w_ffn2_in[l], "w2b": w_ffn2_out[l],
            "gfin": gfin,
            "ffn16": None,
        }
        streams = []
        for x, cache in ((xp, None),
                         (xs, (cache_fox_k[l], cache_fox_v[l], cache_fox_logf[l],
                               cache_diff_k[l], cache_diff_v[l]))):
            streams.append(_layer(x, cache, w, lambda_init, l == depth - 1))
        (xp, news_p), (xs, news_s) = streams
        outs_p.append(news_p)
        outs_s.append(news_s)
    stack = lambda outs, i: jnp.stack([o[i] for o in outs])
    return (xp, xs) + tuple(stack(outs_p, i) for i in range(5)) + tuple(stack(outs_s, i) for i in range(5))
```

```python
import functools
import math

import jax
import jax.numpy as jnp
from jax import lax
from jax.experimental import pallas as pl
from jax.experimental.pallas import tpu as pltpu

D_MODEL = 1024
D_FF = 2816
N_A = 8
HD_A = 64
N_B = 4
HD_B = 64
D_A = N_A * HD_A
D_B = N_B * 2 * HD_B
CHUNK = 64
NORM_EPS = 1e-6
NEG_INF = -1e30
ALIBI_SLOPES = (0.25, 0.0625, 0.015625, 0.00390625)

LOG2E = math.log2(math.e)
LANES = 128
MXU_WIDTH = 256
FF_CHUNKS = (1536, 1280)
assert sum(FF_CHUNKS) == D_FF and all(c % MXU_WIDTH == 0 for c in FF_CHUNKS)
SCAN_BATCH = 8
CAST_BYTES = 3 << 20
TOKEN_TILE = 512
PROJ_TILE = 1024
ATTN_BLOCK = 256
KEY_BLOCKS_PER_STEP = 2
Q_BLOCKS_PER_STEP = 2
assert KEY_BLOCKS_PER_STEP == Q_BLOCKS_PER_STEP == 2
AUG_ROWS = 32
VMEM_LIMIT = 56 * 1024 * 1024
N_PAIRS = N_A // 2

F32 = jnp.float32
BF16 = jnp.bfloat16


def _params(n_axes):
    return pltpu.CompilerParams(dimension_semantics=("arbitrary",) * n_axes,
                                vmem_limit_bytes=VMEM_LIMIT)


def _resident(shape):
    return pl.BlockSpec(shape, lambda *_: (0,) * len(shape), pipeline_mode=pl.Buffered(1))


def _rms(xf, g):
    ms = jnp.mean(xf * xf, axis=-1, keepdims=True)
    return xf * lax.rsqrt(ms + NORM_EPS) * g


def _swiglu_residual(x, gn_ref, w1_ref, w2_ref):
    h = _rms(x, gn_ref[...]).astype(BF16)
    acc = None
    lo = 0
    for width in FF_CHUNKS:
        g = jnp.dot(h, w1_ref[:, lo:lo + width], preferred_element_type=F32)
        u = jnp.dot(h, w1_ref[:, D_FF + lo:D_FF + lo + width], preferred_element_type=F32)
        a = (g * jax.nn.sigmoid(g) * u).astype(BF16)
        d = jnp.dot(a, w2_ref[lo:lo + width, :], preferred_element_type=F32)
        acc = d if acc is None else acc + d
        lo += width
    return x + 0.5 * acc


def _cast_rows(rows_total, cols):
    rows = max(16, CAST_BYTES // (4 * cols) // 16 * 16)
    while rows_total % rows:
        rows -= 16
    return rows


def _stage_shape(shapes):
    return {c: (2, max(_cast_rows(r, cc) for r, cc in shapes if cc == c), c) for _, c in shapes}


def _cast_weight(w_hbm, w16_ref, stage_ref, sem_ref):
    r_total, cols = w_hbm.shape
    rows = _cast_rows(r_total, cols)
    n = r_total // rows

    def copy(c):
        return pltpu.make_async_copy(w_hbm.at[pl.ds(c * rows, rows), :],
                                     stage_ref.at[c % 2, pl.ds(0, rows), :], sem_ref.at[c % 2])

    copy(0).start()
    for c in range(n):
        if c + 1 < n:
            copy(c + 1).start()
        copy(c).wait()
        w16_ref[pl.ds(c * rows, rows), :] = stage_ref[c % 2, 0:rows, :].astype(BF16)


def _weights_prologue(i, w_hbm_refs, w16_refs, w16_hbm_refs, stages, sem_in, sem_out):
    def publish(k):
        return pltpu.make_async_copy(w16_refs[k], w16_hbm_refs[k], sem_out.at[k])

    @pl.when(i == 0)
    def _():
        for w_hbm, w16 in zip(w_hbm_refs, w16_refs):
            _cast_weight(w_hbm, w16, stages[w_hbm.shape[1]], sem_in)
        for k in range(len(w16_refs)):
            publish(k).start()

    return publish


def _weights_epilogue(i, publish, n_weights):
    @pl.when(i == pl.num_programs(0) - 1)
    def _():
        for k in range(n_weights):
            publish(k).wait()


def _weight_plumbing(weights, cast):
    if not cast:
        return [_resident(w.shape) for w in weights], [], [], []
    any_spec = pl.BlockSpec(memory_space=pl.ANY)
    shapes = [w.shape for w in weights]
    stages = _stage_shape(shapes)
    scratch = ([pltpu.VMEM(sh, BF16) for sh in shapes] + [pltpu.VMEM(stages[c], F32) for c in sorted(stages)]
               + [pltpu.SemaphoreType.DMA((2,)), pltpu.SemaphoreType.DMA((len(weights),))])
    return ([any_spec] * len(weights), [any_spec] * len(weights),
            [jax.ShapeDtypeStruct(sh, BF16) for sh in shapes], scratch)


def _split_weight_refs(refs, n_weights, shapes):
    w16 = refs[:n_weights]
    cols = sorted({c for _, c in shapes})
    stage_refs = refs[n_weights:n_weights + len(cols)]
    sem_in, sem_out = refs[n_weights + len(cols):]
    return w16, dict(zip(cols, stage_refs)), sem_in, sem_out


def _ffn1_kernel(x_ref, gn_ref, w1_in, w2_in, o_ref, *rest, cast):
    if cast:
        i = pl.program_id(0)
        w_hbm = (w1_in, w2_in)
        w16, stages, sem_in, sem_out = _split_weight_refs(rest[2:], 2, [w.shape for w in w_hbm])
        publish = _weights_prologue(i, w_hbm, w16, rest[:2], stages, sem_in, sem_out)
        o_ref[...] = _swiglu_residual(x_ref[...], gn_ref, *w16)
        _weights_epilogue(i, publish, 2)
    else:
        o_ref[...] = _swiglu_residual(x_ref[...], gn_ref, w1_in, w2_in)


def _ffn1(x, gn, w1, w2, cast):
    n = x.shape[0]
    tm = min(TOKEN_TILE, n)
    w_specs, w_out_specs, w_out_shapes, scratch = _weight_plumbing([w1, w2], cast)
    out = pl.pallas_call(
        functools.partial(_ffn1_kernel, cast=cast),
        grid=(n // tm,),
        in_specs=[pl.BlockSpec((tm, D_MODEL), lambda i: (i, 0)), _resident((1, D_MODEL))] + w_specs,
        out_specs=[pl.BlockSpec((tm, D_MODEL), lambda i: (i, 0))] + w_out_specs,
        out_shape=[jax.ShapeDtypeStruct((n, D_MODEL), F32)] + w_out_shapes,
        scratch_shapes=scratch,
        compiler_params=_params(1),
        name="ffn1",
    )(x, gn, w1, w2)
    return out if cast else out[0]


def _post_kernel(x_ref, oa_ref, ob_ref, gn_ref, gf_ref, wo_in, w1_in, w2_in, y_ref, *rest, final, cast):
    if cast:
        i = pl.program_id(0)
        w_hbm = (wo_in, w1_in, w2_in)
        w16, stages, sem_in, sem_out = _split_weight_refs(rest[3:], 3, [w.shape for w in w_hbm])
        publish = _weights_prologue(i, w_hbm, w16, rest[:3], stages, sem_in, sem_out)
        wo_ref, w1_ref, w2_ref = w16
    else:
        wo_ref, w1_ref, w2_ref = wo_in, w1_in, w2_in
    x = x_ref[...]
    x = x + jnp.dot(oa_ref[...], wo_ref[:D_A, :], preferred_element_type=F32)
    x = x + jnp.dot(ob_ref[...], wo_ref[D_A:, :], preferred_element_type=F32)
    x = _swiglu_residual(x, gn_ref, w1_ref, w2_ref)
    y_ref[...] = _rms(x, gf_ref[...]) if final else x
    if cast:
        _weights_epilogue(i, publish, 3)


def _post(x, oa, ob, gn, gf, wo, w1, w2, final, cast):
    n = x.shape[0]
    tm = min(TOKEN_TILE, n)
    row = lambda i: (i, 0)
    w_specs, w_out_specs, w_out_shapes, scratch = _weight_plumbing([wo, w1, w2], cast)
    out = pl.pallas_call(
        functools.partial(_post_kernel, final=final, cast=cast),
        grid=(n // tm,),
        in_specs=[pl.BlockSpec((tm, D_MODEL), row), pl.BlockSpec((tm, D_A), row), pl.BlockSpec((tm, D_B), row),
                  _resident((1, D_MODEL)), _resident((1, D_MODEL))] + w_specs,
        out_specs=[pl.BlockSpec((tm, D_MODEL), row)] + w_out_specs,
        out_shape=[jax.ShapeDtypeStruct((n, D_MODEL), F32)] + w_out_shapes,
        scratch_shapes=scratch,
        compiler_params=_params(1),
        name="post",
    )(x, oa, ob, gn, gf, wo, w1, w2)
    return out if cast else out[0]


PROJ_GATE_COL = 3 * D_A
PROJ_B_COL = PROJ_GATE_COL + LANES
PROJ_COLS = PROJ_B_COL + 3 * D_B


def _proj_weights_prologue(i, wt_hbm, w16_ref, w16_hbm, stage_ref, stage_f_ref, sem_in, sem_out):
    rows = stage_ref.shape[1]

    def publish(k):
        return pltpu.make_async_copy(w16_ref, w16_hbm, sem_out.at[k])

    @pl.when(i == 0)
    def _():
        jobs = [(r0 + c * rows, c0 + c * rows) for r0, c0 in ((0, 0), (PROJ_GATE_COL + N_A, PROJ_B_COL))
                for c in range(PROJ_GATE_COL // rows)]

        def copy(n):
            return pltpu.make_async_copy(wt_hbm.at[pl.ds(jobs[n][0], rows), :], stage_ref.at[n % 2], sem_in.at[n % 2])

        gate = pltpu.make_async_copy(wt_hbm.at[pl.ds(PROJ_GATE_COL, N_A), :], stage_f_ref.at[pl.ds(0, N_A), :],
                                     sem_in.at[2])
        gate.start()
        copy(0).start()
        for n, (_, col) in enumerate(jobs):
            if n + 1 < len(jobs):
                copy(n + 1).start()
            copy(n).wait()
            w16_ref[:, col:col + rows] = stage_ref[n % 2].T.astype(BF16)
        stage_f_ref[N_A:, :] = jnp.zeros((LANES - N_A, D_MODEL), F32)
        gate.wait()
        w16_ref[:, PROJ_GATE_COL:PROJ_B_COL] = stage_f_ref[...].T.astype(BF16)
        publish(0).start()

    return publish


def _proj_kernel(x_ref, gn_ref, bf_ref, *refs, seg, kblk, tokens_minor, cast):
    n_out = 11
    if cast:
        i = pl.program_id(0)
        wt_hbm, out_refs, w16_hbm = refs[0], refs[1:1 + n_out], refs[1 + n_out]
        w_ref, stage_ref, stage_f_ref, sem_in, sem_out = refs[2 + n_out:]
        publish = _proj_weights_prologue(i, wt_hbm, w_ref, w16_hbm, stage_ref, stage_f_ref, sem_in, sem_out)
    else:
        w_ref, out_refs = refs[0], refs[1:]
    tm = x_ref.shape[0]
    h = _rms(x_ref[...], gn_ref[...]).astype(BF16)
    p = jnp.dot(h, w_ref[...], preferred_element_type=F32)
    pa, pb = p[:, :PROJ_GATE_COL], p[:, PROJ_B_COL:]
    z = p[:, PROJ_GATE_COL:PROJ_B_COL] + bf_ref[...]
    lft = (jnp.minimum(z, 0.0) - jnp.log1p(jnp.exp(-jnp.abs(z)))).T
    ka, va = pa[:, D_A:2 * D_A], pa[:, 2 * D_A:]
    kb, vb = pb[:, D_B:2 * D_B], pb[:, 2 * D_B:]
    qa16 = (pa[:, :D_A] * (HD_A ** -0.5 * LOG2E)).astype(BF16)
    qb16 = (pb[:, :D_B] * (HD_B ** -0.5 * LOG2E)).astype(BF16)
    if tokens_minor:
        kat_ref, vat_ref, lft_ref, kbt_ref, vb_ref, qa16_ref, kat16_ref, va16_ref, qb16_ref, kbt16_ref, vb16_ref = out_refs
        kat, vat, kbt = ka.T, va.T, kb.T
    else:
        ka_ref, va_ref, lft_ref, kb_ref, vb_ref, qa16_ref, ka16_ref, va16_ref, qb16_ref, kb16_ref, vb16_ref = out_refs
        ka16_ref[...] = ka.astype(BF16)
        kb16_ref[...] = kb.astype(BF16)
        for hh in range(N_A):
            cols = slice(hh * HD_A, (hh + 1) * HD_A)
            ka_ref[pl.ds(hh, tm, stride=N_A), :] = ka[:, cols]
            va_ref[pl.ds(hh, tm, stride=N_A), :] = va[:, cols]
            kb_ref[pl.ds(hh, tm, stride=N_A), :] = kb[:, cols]
    qa16_ref[...] = qa16
    qb16_ref[...] = qb16
    va16_ref[...] = va.astype(BF16)
    vb16_ref[...] = vb.astype(BF16)
    for hh in range(N_B):
        vb_ref[pl.ds(hh, tm, stride=N_B), :] = vb[:, hh * LANES:(hh + 1) * LANES]
    for bb in range(tm // seg):
        cols = slice(bb * seg, (bb + 1) * seg)
        lft_ref[bb] = lft[:N_A, cols]
        if tokens_minor:
            kat_ref[bb] = kat[:, cols]
            vat_ref[bb] = vat[:, cols]
            kbt_ref[bb] = kbt[:, cols]
            for jj in range(seg // kblk):
                kc = slice(bb * seg + jj * kblk, bb * seg + (jj + 1) * kblk)
                kat16_ref[bb, jj] = kat[:, kc].astype(BF16)
                kbt16_ref[bb, jj] = kbt[:, kc].astype(BF16)
    if cast:
        _weights_epilogue(i, publish, 1)


def _proj(x, gn, bf, weight, b, t, tokens_minor):
    cast = weight.dtype != BF16
    n = x.shape[0]
    tm = min(PROJ_TILE if tokens_minor else TOKEN_TILE, n)
    seg = min(tm, t)
    nb = tm // seg
    nt = t // seg
    kblk = min(ATTN_BLOCK, seg)
    row = lambda i: (i, 0)
    tok = lambda i: (i // nt, 0, i % nt)
    blk = lambda i: (i // nt, i % nt, 0, 0)
    b16 = (jax.ShapeDtypeStruct((n, D_A), BF16), pl.BlockSpec((tm, D_A), row))
    lft = (jax.ShapeDtypeStruct((b, N_A, t), F32), pl.BlockSpec((nb, N_A, seg), tok))
    vb = (jax.ShapeDtypeStruct((n * N_B, LANES), F32), pl.BlockSpec((tm * N_B, LANES), row))
    if tokens_minor:
        kv = (jax.ShapeDtypeStruct((b, D_A, t), F32), pl.BlockSpec((nb, D_A, seg), tok))
        k16 = (jax.ShapeDtypeStruct((b, t // kblk, D_A, kblk), BF16),
               pl.BlockSpec((nb, seg // kblk, D_A, kblk), blk))
    else:
        kv = (jax.ShapeDtypeStruct((n * N_A, HD_A), F32), pl.BlockSpec((tm * N_A, HD_A), row))
        k16 = b16
    outs = [kv, kv, lft, kv, vb, b16, k16, b16, b16, k16, b16]
    w_shape = (D_MODEL, PROJ_COLS)
    if cast:
        any_spec = pl.BlockSpec(memory_space=pl.ANY)
        w_spec = any_spec
        outs = outs + [(jax.ShapeDtypeStruct(w_shape, BF16), any_spec)]
        scratch = [pltpu.VMEM(w_shape, BF16), pltpu.VMEM((2, 3 * D_A // 2, D_MODEL), F32),
                   pltpu.VMEM((LANES, D_MODEL), F32), pltpu.SemaphoreType.DMA((3,)), pltpu.SemaphoreType.DMA((1,))]
    else:
        w_spec = _resident(w_shape)
        scratch = []
    res = pl.pallas_call(
        functools.partial(_proj_kernel, seg=seg, kblk=kblk, tokens_minor=tokens_minor, cast=cast),
        grid=(n // tm,),
        in_specs=[pl.BlockSpec((tm, D_MODEL), row), _resident((1, D_MODEL)), _resident(bf.shape), w_spec],
        out_specs=[o[1] for o in outs],
        out_shape=[o[0] for o in outs],
        scratch_shapes=scratch,
        compiler_params=_params(1),
        name="proj",
    )(x, gn, bf, weight)
    return res[:11], (res[11] if cast else weight)


def _split3(x):
    hi = x.astype(BF16).astype(F32)
    r = x - hi
    mid = r.astype(BF16).astype(F32)
    lo = (r - mid).astype(BF16).astype(F32)
    return hi, mid, lo


def _cum_kernel(lft_ref, aug_ref, pad_ref, *, t, blocked):
    w = ATTN_BLOCK
    nb = lft_ref.shape[0]
    rows = nb * N_A
    pad_ref[...] = jnp.zeros(pad_ref.shape, F32)
    for bb in range(nb):
        pad_ref[bb * N_A:(bb + 1) * N_A, 0:t] = lft_ref[bb]
    r = lax.broadcasted_iota(jnp.int32, (w, w), 0)
    c = lax.broadcasted_iota(jnp.int32, (w, w), 1)
    tri = jnp.where(r <= c, 1.0, 0.0).astype(BF16)
    zero = jnp.zeros((N_A, w), F32)
    carry = jnp.zeros((rows, 1), F32)
    for s in range(pad_ref.shape[1] // w):
        lo_col = s * w
        cols = min(w, t - lo_col)
        x = jnp.concatenate(_split3(pad_ref[:, lo_col:lo_col + w]), axis=0).astype(BF16)
        y = jnp.dot(x, tri, preferred_element_type=F32)
        cs = y[0:rows] + y[rows:2 * rows] + y[2 * rows:3 * rows] + carry
        carry = cs[:, w - 1:w]
        pieces = _split3(cs * -LOG2E)
        for bb in range(nb):
            sl = slice(bb * N_A, (bb + 1) * N_A)
            aug = jnp.concatenate([p[sl] for p in pieces] + [zero], axis=0).astype(BF16)
            if blocked:
                aug_ref[bb, s] = aug
            else:
                aug_ref[bb, :, lo_col:lo_col + cols] = aug[:, :cols]


def _cum_aug(lft, blocked):
    b, _, t = lft.shape
    w = ATTN_BLOCK
    t_pad = -(-t // w) * w
    nb = math.gcd(b, SCAN_BATCH)
    if blocked:
        out_shape = jax.ShapeDtypeStruct((b, t // w, AUG_ROWS, w), BF16)
        out_spec = pl.BlockSpec((nb, t // w, AUG_ROWS, w), lambda i: (i, 0, 0, 0))
    else:
        out_shape = jax.ShapeDtypeStruct((b, AUG_ROWS, t), BF16)
        out_spec = pl.BlockSpec((nb, AUG_ROWS, t), lambda i: (i, 0, 0))
    return pl.pallas_call(
        functools.partial(_cum_kernel, t=t, blocked=blocked),
        grid=(b // nb,),
        in_specs=[pl.BlockSpec((nb, N_A, t), lambda i: (i, 0, 0))],
        out_specs=out_spec,
        out_shape=out_shape,
        scratch_shapes=[pltpu.VMEM((nb * N_A, t_pad), F32)],
        compiler_params=_params(1),
        name="cum_aug",
    )(lft)


def _lane_halves(q):
    lane = lax.broadcasted_iota(jnp.int32, q.shape, 1)
    zero = jnp.zeros_like(q)
    return jnp.where(lane < HD_A, q, zero), jnp.where(lane >= HD_A, q, zero)


def _piece_selector(shape, head, n_heads):
    lane = lax.broadcasted_iota(jnp.int32, shape, 1)
    hit = (lane == head) | (lane == n_heads + head) | (lane == 2 * n_heads + head)
    return jnp.where(hit, 1.0, 0.0).astype(BF16)


def _alibi_rows(k_pos):
    w = k_pos.shape[1]
    row = lax.broadcasted_iota(jnp.int32, (4 * N_B, 1), 0)
    head = row % N_B
    slope = jnp.zeros((4 * N_B, 1), F32)
    for h in range(N_B):
        slope = jnp.where(head == h, ALIBI_SLOPES[h] * LOG2E, slope)
    hi, mid, lo = _split3(slope * k_pos)
    piece = jnp.where(row < N_B, hi, jnp.where(row < 2 * N_B, mid, jnp.where(row < 3 * N_B, lo, 0.0)))
    return jnp.concatenate([piece.astype(BF16), jnp.zeros((LANES - 4 * N_B, w), BF16)], axis=0)


def _alibi_fold(slope, q_pos, k_pos):
    return (-2.0 * LOG2E * slope) * jnp.maximum(k_pos - q_pos, 0).astype(F32)


def _diff_finish(o1, o2, lam, sub_ref, lambda_init):
    return _rms(o1 - lam * o2, sub_ref[...]) * (1.0 - lambda_init)


def _lambda(lq1_ref, lk1_ref, lq2_ref, lk2_ref, lambda_init):
    return (jnp.exp(jnp.sum(lq1_ref[...] * lk1_ref[...], axis=1, keepdims=True))
            - jnp.exp(jnp.sum(lq2_ref[...] * lk2_ref[...], axis=1, keepdims=True)) + lambda_init)


def _attn_prompt_kernel(qa_ref, qb_ref, kat_ref, kbt_ref, va_ref, vb_ref, aug_ref,
                        lq1_ref, lk1_ref, lq2_ref, lk2_ref, sub_ref, oa_ref, ob_ref,
                        qs_ref, m_ref, acc_ref, tbl_ref, *, lambda_init):
    tq = ATTN_BLOCK
    n_chain = N_PAIRS + N_B

    @pl.when(pl.program_id(0) == 0)
    def _():
        r = lax.broadcasted_iota(jnp.int32, (2 * tq, tq), 0)
        c = lax.broadcasted_iota(jnp.int32, (2 * tq, tq), 1)
        q_idx = jnp.where(r >= tq, r - tq, r)
        tbl_ref[0] = jnp.where(q_idx >= c, 0.0, NEG_INF)
        visible = (q_idx // CHUNK) >= (c // CHUNK)
        for h in range(N_B):
            tbl_ref[1 + h] = jnp.where(visible, _alibi_fold(ALIBI_SLOPES[h], q_idx, c), NEG_INF)
        for qset in range(Q_BLOCKS_PER_STEP):
            for hc in range(n_chain):
                ch = qset * n_chain + hc
                if hc < N_PAIRS:
                    qs_ref[ch, 0:tq, LANES:] = _piece_selector((tq, LANES), 2 * hc, N_A)
                    qs_ref[ch, tq:, LANES:] = _piece_selector((tq, LANES), 2 * hc + 1, N_A)
                else:
                    sel = _piece_selector((tq, LANES), hc - N_PAIRS, N_B)
                    qs_ref[ch, 0:tq, LANES:] = sel
                    qs_ref[ch, tq:, LANES:] = sel

    def step(sets, first, width=1):
        wk = width * tq
        ones = jnp.ones((wk, LANES), BF16)
        for qset, j in sets:
            rows = pl.ds(pl.multiple_of(j * tq, tq), wk)
            blocks = [j + d for d in range(width)]
            aug_a = jnp.concatenate([jnp.concatenate([aug_ref[0, jb] for jb in blocks], axis=1),
                                     jnp.zeros((LANES - AUG_ROWS, wk), BF16)], axis=0)
            aug_b = _alibi_rows((j * tq + lax.broadcasted_iota(jnp.int32, (1, wk), 1)).astype(F32))
            for hc in range(n_chain):
                ch = qset * n_chain + hc
                if hc < N_PAIRS:
                    kt_ref, v_ref, g, aug, tbl = kat_ref, va_ref, hc, aug_a, 0
                else:
                    kt_ref, v_ref, g, aug, tbl = kbt_ref, vb_ref, hc - N_PAIRS, aug_b, 1 + hc - N_PAIRS
                kt = jnp.concatenate([kt_ref[0, jb, g * LANES:(g + 1) * LANES, :] for jb in blocks], axis=1)
                s = jnp.dot(qs_ref[ch], jnp.concatenate([kt, aug], axis=0), preferred_element_type=F32)
                v = v_ref[rows, g * LANES:(g + 1) * LANES]
                if first:
                    s = s + tbl_ref[tbl]
                    m_new = jnp.broadcast_to(jnp.max(s, axis=1, keepdims=True), (2 * tq, LANES))
                else:
                    m_prev = m_ref[ch]
                    m_new = jnp.maximum(m_prev, jnp.max(s, axis=1, keepdims=True))
                p = jnp.exp2(s - jnp.concatenate([m_new] * (wk // LANES), axis=1))
                pv = jnp.dot(p.astype(BF16), jnp.concatenate([v, ones], axis=1), preferred_element_type=F32)
                if first:
                    acc_ref[ch] = pv
                else:
                    alpha = jnp.exp2(m_prev - m_new)
                    acc_ref[ch] = jnp.concatenate([alpha, alpha], axis=1) * acc_ref[ch] + pv
                m_ref[ch] = m_new

    def wide_body(jj, carry):
        j = jj * KEY_BLOCKS_PER_STEP
        step([(0, j), (1, j)], False, KEY_BLOCKS_PER_STEP)
        return carry

    lane = lax.broadcasted_iota(jnp.int32, (tq, LANES), 1)
    lam = _lambda(lq1_ref, lk1_ref, lq2_ref, lk2_ref, lambda_init)

    def query_blocks(pair, carry):
        even = pair * Q_BLOCKS_PER_STEP
        for qset in range(Q_BLOCKS_PER_STEP):
            q_rows = pl.ds(pl.multiple_of((even + qset) * tq, tq), tq)
            for hc in range(n_chain):
                q_ref, g = (qa_ref, hc) if hc < N_PAIRS else (qb_ref, hc - N_PAIRS)
                lo, hi = _lane_halves(q_ref[q_rows, g * LANES:(g + 1) * LANES])
                qs_ref[qset * n_chain + hc, 0:tq, 0:LANES] = lo
                qs_ref[qset * n_chain + hc, tq:, 0:LANES] = hi
        step([(0, even), (1, even + 1)], True)
        step([(1, even)], False)
        lax.fori_loop(0, pair, wide_body, 0)
        for qset in range(Q_BLOCKS_PER_STEP):
            q_rows = pl.ds(pl.multiple_of((even + qset) * tq, tq), tq)
            for hc in range(n_chain):
                ch = qset * n_chain + hc
                if hc < N_PAIRS:
                    picked = jnp.where(jnp.concatenate([lane, lane], axis=1) < HD_A,
                                       acc_ref[ch, 0:tq, :], acc_ref[ch, tq:, :])
                    oa_ref[q_rows, hc * LANES:(hc + 1) * LANES] = (picked[:, :LANES] / picked[:, LANES:]).astype(BF16)
                else:
                    o = acc_ref[ch, :, 0:LANES] / acc_ref[ch, :, LANES:]
                    h = hc - N_PAIRS
                    ob_ref[q_rows, h * LANES:(h + 1) * LANES] = _diff_finish(
                        o[:tq], o[tq:], lam, sub_ref, lambda_init).astype(BF16)
        return carry

    lax.fori_loop(0, qa_ref.shape[0] // (Q_BLOCKS_PER_STEP * tq), query_blocks, 0)


def _attn_prompt(qa16, qb16, kat16, kbt16, va16, vb16, aug, lams, sub, b, t, lambda_init):
    tq = ATTN_BLOCK
    nq = t // tq
    kspec = pl.BlockSpec((1, nq, D_A, tq), lambda bi: (bi, 0, 0, 0))
    vspec = pl.BlockSpec((t, D_A), lambda bi: (bi, 0))
    small = [_resident((1, HD_B))] * 4 + [_resident((1, 2 * HD_B))]
    n_chain = Q_BLOCKS_PER_STEP * (N_PAIRS + N_B)
    return pl.pallas_call(
        functools.partial(_attn_prompt_kernel, lambda_init=lambda_init),
        grid=(b,),
        in_specs=[vspec, vspec, kspec, kspec, vspec, vspec,
                  pl.BlockSpec((1, nq, AUG_ROWS, tq), lambda bi: (bi, 0, 0, 0))] + small,
        out_specs=[vspec, vspec],
        out_shape=[jax.ShapeDtypeStruct((b * t, D_A), BF16), jax.ShapeDtypeStruct((b * t, D_B), BF16)],
        scratch_shapes=[pltpu.VMEM((n_chain, 2 * tq, 2 * LANES), BF16),
                        pltpu.VMEM((n_chain, 2 * tq, LANES), F32),
                        pltpu.VMEM((n_chain, 2 * tq, 2 * LANES), F32),
                        pltpu.VMEM((1 + N_B, 2 * tq, tq), F32)],
        compiler_params=_params(1),
        name="attn_prompt",
    )(qa16, qb16, kat16, kbt16, va16, vb16, aug, *lams, sub)


def _nt_dot(a, b):
    return lax.dot_general(a, b, (((1,), (1,)), ((), ())), preferred_element_type=F32)


def _attn_cached_kernel(qa_ref, qb_ref, ckat_ref, cvat_ref, ckbt_ref, cvb_ref,
                        ka_ref, va_ref, kb_ref, vb_ref, aug_ref,
                        lq1_ref, lk1_ref, lq2_ref, lk2_ref, sub_ref, oa_ref, ob_ref,
                        kn_ref, *, lambda_init):
    tq = qa_ref.shape[0]
    past = ckat_ref.shape[2]
    t_all = past + tq
    n_chain = N_PAIRS + N_B
    r = lax.broadcasted_iota(jnp.int32, (2 * tq, LANES), 0)
    c = lax.broadcasted_iota(jnp.int32, (2 * tq, LANES), 1)
    q_idx = jnp.where(r >= tq, r - tq, r)
    is_key = c < tq
    causal_tbl = jnp.where((q_idx >= c) & is_key, 0.0, NEG_INF)
    visible = (((past + q_idx) // CHUNK) >= ((past + c) // CHUNK)) & is_key
    lane = lax.broadcasted_iota(jnp.int32, (tq, LANES), 1)
    lam = _lambda(lq1_ref, lk1_ref, lq2_ref, lk2_ref, lambda_init)
    aug_c = jnp.concatenate([aug_ref[0, :, 0:past], jnp.zeros((LANES - AUG_ROWS, past), BF16)], axis=0)
    alibi_c = _alibi_rows(lax.broadcasted_iota(jnp.int32, (1, past), 1).astype(F32))
    alibi_n = _alibi_rows((past + lax.broadcasted_iota(jnp.int32, (1, LANES), 1)).astype(F32))
    kn_ref[...] = jnp.zeros(kn_ref.shape, BF16)
    row_pad = jnp.zeros((LANES - tq, LANES), BF16)

    def new_keys_t(k_new):
        return jnp.concatenate([k_new, row_pad], axis=0).astype(F32).T.astype(BF16)

    probs = []
    for ch in range(n_chain):
        fox = ch < N_PAIRS
        g = ch if fox else ch - N_PAIRS
        rows = slice(g * LANES, (g + 1) * LANES)
        if fox:
            lo, hi = _lane_halves(qa_ref[:, rows])
            sel_lo = _piece_selector((tq, LANES), 2 * g, N_A)
            sel_hi = _piece_selector((tq, LANES), 2 * g + 1, N_A)
            k_old = jnp.concatenate([ckat_ref[0, rows, :].astype(BF16), aug_c], axis=0)
            kn_ref[ch, 0:LANES, :] = new_keys_t(ka_ref[:, rows])
            kn_ref[ch, LANES:LANES + AUG_ROWS, 0:tq] = aug_ref[0, :, past:t_all]
            tbl = causal_tbl
        else:
            lo, hi = _lane_halves(qb_ref[:, rows])
            sel_lo = sel_hi = _piece_selector((tq, LANES), g, N_B)
            k_old = jnp.concatenate([ckbt_ref[0, rows, :].astype(BF16), alibi_c], axis=0)
            kn_ref[ch, 0:LANES, :] = new_keys_t(kb_ref[:, rows])
            kn_ref[ch, LANES:, :] = alibi_n
            tbl = jnp.where(visible, _alibi_fold(ALIBI_SLOPES[g], q_idx, c), NEG_INF)
        qs = jnp.concatenate([jnp.concatenate([lo, sel_lo], axis=1), jnp.concatenate([hi, sel_hi], axis=1)], axis=0)
        s_old = jnp.dot(qs, k_old, preferred_element_type=F32)
        s_new = jnp.dot(qs, kn_ref[ch], preferred_element_type=F32) + tbl
        m = jnp.maximum(jnp.max(s_old, axis=1, keepdims=True), jnp.max(s_new, axis=1, keepdims=True))
        p_old = jnp.exp2(s_old - m)
        p_new = jnp.exp2(s_new - m)
        l = jnp.sum(p_old, axis=1, keepdims=True) + jnp.sum(p_new, axis=1, keepdims=True)
        probs.append((p_old.astype(BF16), p_new.astype(BF16), l))

    for ch in range(n_chain):
        fox = ch < N_PAIRS
        g = ch if fox else ch - N_PAIRS
        rows = slice(g * LANES, (g + 1) * LANES)
        p_old, p_new, l = probs[ch]
        v_new = jnp.concatenate([(va_ref if fox else vb_ref)[:, rows], row_pad], axis=0)
        o_new = jnp.dot(p_new, v_new, preferred_element_type=F32)
        if fox:
            o = (_nt_dot(p_old, cvat_ref[0, rows, :].astype(BF16)) + o_new) / l
            oa_ref[:, rows] = jnp.where(lane < HD_A, o[:tq], o[tq:]).astype(BF16)
        else:
            v_old = cvb_ref[0, pl.ds(g, past, stride=N_B), :].astype(BF16)
            o = (jnp.dot(p_old, v_old, preferred_element_type=F32) + o_new) / l
            ob_ref[:, rows] = _diff_finish(o[:tq], o[tq:], lam, sub_ref, lambda_init).astype(BF16)


def _attn_cached(qa16, qb16, ckat, cvat, ckbt, cvb, ka16, va16, kb16, vb16, aug, lams, sub, b, tq, lambda_init):
    past = ckat.shape[2]
    t_all = past + tq
    n_chain = N_PAIRS + N_B
    qspec = pl.BlockSpec((tq, D_A), lambda bi: (bi, 0))
    old_t = pl.BlockSpec((1, D_A, past), lambda bi: (bi, 0, 0))
    small = [_resident((1, HD_B))] * 4 + [_resident((1, 2 * HD_B))]
    return pl.pallas_call(
        functools.partial(_attn_cached_kernel, lambda_init=lambda_init),
        grid=(b,),
        in_specs=[qspec, qspec, old_t, old_t, old_t,
                  pl.BlockSpec((1, past * N_B, LANES), lambda bi: (bi, 0, 0)),
                  qspec, qspec, qspec, qspec,
                  pl.BlockSpec((1, AUG_ROWS, t_all), lambda bi: (bi, 0, 0))] + small,
        out_specs=[qspec, qspec],
        out_shape=[jax.ShapeDtypeStruct((b * tq, D_A), BF16), jax.ShapeDtypeStruct((b * tq, D_B), BF16)],
        scratch_shapes=[pltpu.VMEM((n_chain, 2 * LANES, LANES), BF16)],
        compiler_params=_params(1),
        name="attn_cached",
    )(qa16, qb16, ckat, cvat, ckbt, cvb, ka16, va16, kb16, vb16, aug, *lams, sub)


def _tokens_minor(x):
    b, t = x.shape[:2]
    return jnp.moveaxis(x.reshape(b, t, -1), 1, 2)


def _tokens_major(xt, tail):
    b, _, t = xt.shape
    return jnp.moveaxis(xt, 2, 1).reshape(b, t, *tail)


def _layer(x, cache, w, lambda_init, final):
    b, t, _ = x.shape
    n = b * t
    cast = w["ffn16"] is None
    if cast:
        x1, w1a, w2a = _ffn1(x.reshape(n, D_MODEL), w["gn1"], w["w1a"], w["w2a"], cast=True)
    else:
        w1a, w2a, wo, w1b, w2b = w["ffn16"]
        x1 = _ffn1(x.reshape(n, D_MODEL), w["gn1"], w1a, w2a, cast=False)
    (k_a, v_a, lft, k_b, vb, qa16, ka16, va16, qb16, kb16, vb16), w_in16 = _proj(
        x1, w["gmix"], w["bf"], w["w_in_t"] if cast else w["w_in16"], b, t, tokens_minor=cache is None)
    if cast:
        w["w_in16"] = w_in16
    if cache is None:
        aug = _cum_aug(lft, blocked=True)
        oa, ob = _attn_prompt(qa16, qb16, ka16, kb16, va16, vb16, aug, w["lams"], w["sub"], b, t, lambda_init)
        news = (_tokens_major(k_a, (N_A, HD_A)), _tokens_major(v_a, (N_A, HD_A)), _tokens_major(lft, (N_A,)),
                _tokens_major(k_b, (N_B, 2, HD_B)), vb.reshape(b, t, N_B, 2 * HD_B))
    else:
        cfk, cfv, cflf, cdk, cdv = cache
        past = cfk.shape[1]
        aug = _cum_aug(jnp.concatenate([_tokens_minor(cflf.astype(F32)), lft], axis=2), blocked=False)
        oa, ob = _attn_cached(qa16, qb16, _tokens_minor(cfk), _tokens_minor(cfv), _tokens_minor(cdk),
                              cdv.reshape(b, past * N_B, 2 * HD_B), ka16, va16, kb16, vb16, aug,
                              w["lams"], w["sub"], b, t, lambda_init)
        news = (k_a.reshape(b, t, N_A, HD_A), v_a.reshape(b, t, N_A, HD_A), _tokens_major(lft, (N_A,)),
                k_b.reshape(b, t, N_B, 2, HD_B), vb.reshape(b, t, N_B, 2 * HD_B))
    if cast:
        y, wo, w1b, w2b = _post(x1, oa, ob, w["gn2"], w["gfin"], w["wo"], w["w1b"], w["w2b"], final, cast=True)
        w["ffn16"] = (w1a, w2a, wo, w1b, w2b)
    else:
        y = _post(x1, oa, ob, w["gn2"], w["gfin"], wo, w1b, w2b, final, cast=False)
    return y.reshape(b, t, D_MODEL), news


def kernel(x_prompt, x_sample, cache_fox_k, cache_fox_v, cache_fox_logf, cache_diff_k, cache_diff_v,
           norm_ffn1, w_ffn1_in, w_ffn1_out, norm_mix, w_in, b_forget,
           lambda_q1, lambda_k1, lambda_q2, lambda_k2, diff_subln, w_out,
           norm_ffn2, w_ffn2_in, w_ffn2_out, norm_final):
    depth = w_in.shape[0]
    xp, xs = x_prompt, x_sample
    outs_p, outs_s = [], []
    gfin = norm_final.reshape(1, D_MODEL)
    for l in range(depth):
        lambda_init = 0.8 - 0.6 * math.exp(-0.3 * l)
        w = {
            "gn1": norm_ffn1[l].reshape(1, D_MODEL),
            "w1a": w_ffn1_in[l], "w2a": w_ffn1_out[l],
            "gmix": norm_mix[l].reshape(1, D_MODEL),
            "w_in_t": jnp.swapaxes(w_in[l], 0, 1),
            "bf": jnp.pad(b_forget[l], (0, LANES - N_A)).reshape(1, LANES),
            "lams": [v[l].reshape(1, HD_B) for v in (lambda_q1, lambda_k1, lambda_q2, lambda_k2)],
            "sub": diff_subln[l].reshape(1, 2 * HD_B),
            "wo": w_out[l],
            "gn2": norm_ffn2[l].reshape(1, D_MODEL),
            "w1b": w_ffn2_in[l], "w2b": w_ffn2_out[l],
            "gfin": gfin,
            "ffn16": None,
        }
        streams = []
        for x, cache in ((xp, None),
                         (xs, (cache_fox_k[l], cache_fox_v[l], cache_fox_logf[l],
                               cache_diff_k[l], cache_diff_v[l]))):
            streams.append(_layer(x, cache, w, lambda_init, l == depth - 1))
        (xp, news_p), (xs, news_s) = streams
        outs_p.append(news_p)
        outs_s.append(news_s)
    stack = lambda outs, i: jnp.stack([o[i] for o in outs])
    return (xp, xs) + tuple(stack(outs_p, i) for i in range(5)) + tuple(stack(outs_s, i) for i in range(5))
```

```python
import functools
import math

import jax
import jax.numpy as jnp
from jax import lax
from jax.experimental import pallas as pl
from jax.experimental.pallas import tpu as pltpu

D_MODEL = 1024
D_FF = 2816
N_A = 8
HD_A = 64
N_B = 4
HD_B = 64
D_A = N_A * HD_A
D_B = N_B * 2 * HD_B
CHUNK = 64
NORM_EPS = 1e-6
NEG_INF = -1e30
ALIBI_SLOPES = (0.25, 0.0625, 0.015625, 0.00390625)

LOG2E = math.log2(math.e)
LANES = 128
BF16_SUBLANES = 16
MXU_WIDTH = 256
FF_CHUNKS = (1536, 1280)
assert sum(FF_CHUNKS) == D_FF and all(c % MXU_WIDTH == 0 for c in FF_CHUNKS)
SCAN_BATCH = 8
CAST_BYTES = 3 << 20
TOKEN_TILE = 512
PROJ_TILE = 1024
ATTN_BLOCK = 256
KEY_BLOCKS_PER_STEP = 2
Q_BLOCKS_PER_STEP = 2
assert KEY_BLOCKS_PER_STEP == Q_BLOCKS_PER_STEP == 2
AUG_ROWS = 32
VMEM_LIMIT = 56 * 1024 * 1024
N_PAIRS = N_A // 2

F32 = jnp.float32
BF16 = jnp.bfloat16


def _params(n_axes):
    return pltpu.CompilerParams(dimension_semantics=("arbitrary",) * n_axes,
                                vmem_limit_bytes=VMEM_LIMIT)


def _resident(shape):
    return pl.BlockSpec(shape, lambda *_: (0,) * len(shape), pipeline_mode=pl.Buffered(1))


def _rms(xf, g):
    ms = jnp.mean(xf * xf, axis=-1, keepdims=True)
    return xf * lax.rsqrt(ms + NORM_EPS) * g


def _swiglu_residual(x, gn_ref, w1_ref, w2_ref):
    h = _rms(x, gn_ref[...]).astype(BF16)
    acc = None
    lo = 0
    for width in FF_CHUNKS:
        g = jnp.dot(h, w1_ref[:, lo:lo + width], preferred_element_type=F32)
        u = jnp.dot(h, w1_ref[:, D_FF + lo:D_FF + lo + width], preferred_element_type=F32)
        a = (g * jax.nn.sigmoid(g) * u).astype(BF16)
        d = jnp.dot(a, w2_ref[lo:lo + width, :], preferred_element_type=F32)
        acc = d if acc is None else acc + d
        lo += width
    return x + 0.5 * acc


def _cast_rows(rows_total, cols):
    rows = max(BF16_SUBLANES, CAST_BYTES // (4 * cols) // BF16_SUBLANES * BF16_SUBLANES)
    while rows_total % rows:
        rows -= BF16_SUBLANES
    return rows


def _stage_shape(shapes):
    return {c: (2, max(_cast_rows(r, cc) for r, cc in shapes if cc == c), c) for _, c in shapes}


def _cast_weight(w_hbm, w16_ref, stage_ref, sem_ref):
    r_total, cols = w_hbm.shape
    rows = _cast_rows(r_total, cols)
    n = r_total // rows

    def copy(c):
        return pltpu.make_async_copy(w_hbm.at[pl.ds(c * rows, rows), :],
                                     stage_ref.at[c % 2, pl.ds(0, rows), :], sem_ref.at[c % 2])

    copy(0).start()
    for c in range(n):
        if c + 1 < n:
            copy(c + 1).start()
        copy(c).wait()
        w16_ref[pl.ds(c * rows, rows), :] = stage_ref[c % 2, 0:rows, :].astype(BF16)


def _weights_prologue(i, w_hbm_refs, w16_refs, w16_hbm_refs, stages, sem_in, sem_out):
    def publish(k):
        return pltpu.make_async_copy(w16_refs[k], w16_hbm_refs[k], sem_out.at[k])

    @pl.when(i == 0)
    def _():
        for w_hbm, w16 in zip(w_hbm_refs, w16_refs):
            _cast_weight(w_hbm, w16, stages[w_hbm.shape[1]], sem_in)
        for k in range(len(w16_refs)):
            publish(k).start()

    return publish


def _weights_epilogue(i, publish, n_weights):
    @pl.when(i == pl.num_programs(0) - 1)
    def _():
        for k in range(n_weights):
            publish(k).wait()


def _weight_plumbing(weights, cast):
    if not cast:
        return [_resident(w.shape) for w in weights], [], [], []
    any_spec = pl.BlockSpec(memory_space=pl.ANY)
    shapes = [w.shape for w in weights]
    stages = _stage_shape(shapes)
    scratch = ([pltpu.VMEM(sh, BF16) for sh in shapes] + [pltpu.VMEM(stages[c], F32) for c in sorted(stages)]
               + [pltpu.SemaphoreType.DMA((2,)), pltpu.SemaphoreType.DMA((len(weights),))])
    return ([any_spec] * len(weights), [any_spec] * len(weights),
            [jax.ShapeDtypeStruct(sh, BF16) for sh in shapes], scratch)


def _split_weight_refs(refs, n_weights, shapes):
    w16 = refs[:n_weights]
    cols = sorted({c for _, c in shapes})
    stage_refs = refs[n_weights:n_weights + len(cols)]
    sem_in, sem_out = refs[n_weights + len(cols):]
    return w16, dict(zip(cols, stage_refs)), sem_in, sem_out


def _ffn1_kernel(x_ref, gn_ref, w1_in, w2_in, o_ref, *rest, cast):
    if cast:
        i = pl.program_id(0)
        w_hbm = (w1_in, w2_in)
        w16, stages, sem_in, sem_out = _split_weight_refs(rest[2:], 2, [w.shape for w in w_hbm])
        publish = _weights_prologue(i, w_hbm, w16, rest[:2], stages, sem_in, sem_out)
        o_ref[...] = _swiglu_residual(x_ref[...], gn_ref, *w16)
        _weights_epilogue(i, publish, 2)
    else:
        o_ref[...] = _swiglu_residual(x_ref[...], gn_ref, w1_in, w2_in)


def _ffn1(x, gn, w1, w2, cast):
    n = x.shape[0]
    tm = min(TOKEN_TILE, n)
    w_specs, w_out_specs, w_out_shapes, scratch = _weight_plumbing([w1, w2], cast)
    out = pl.pallas_call(
        functools.partial(_ffn1_kernel, cast=cast),
        grid=(n // tm,),
        in_specs=[pl.BlockSpec((tm, D_MODEL), lambda i: (i, 0)), _resident((1, D_MODEL))] + w_specs,
        out_specs=[pl.BlockSpec((tm, D_MODEL), lambda i: (i, 0))] + w_out_specs,
        out_shape=[jax.ShapeDtypeStruct((n, D_MODEL), F32)] + w_out_shapes,
        scratch_shapes=scratch,
        compiler_params=_params(1),
        name="ffn1",
    )(x, gn, w1, w2)
    return out if cast else out[0]


def _post_kernel(x_ref, oa_ref, ob_ref, gn_ref, gf_ref, wo_in, w1_in, w2_in, y_ref, *rest, final, cast):
    if cast:
        i = pl.program_id(0)
        w_hbm = (wo_in, w1_in, w2_in)
        w16, stages, sem_in, sem_out = _split_weight_refs(rest[3:], 3, [w.shape for w in w_hbm])
        publish = _weights_prologue(i, w_hbm, w16, rest[:3], stages, sem_in, sem_out)
        wo_ref, w1_ref, w2_ref = w16
    else:
        wo_ref, w1_ref, w2_ref = wo_in, w1_in, w2_in
    x = x_ref[...]
    x = x + jnp.dot(oa_ref[...], wo_ref[:D_A, :], preferred_element_type=F32)
    x = x + jnp.dot(ob_ref[...], wo_ref[D_A:, :], preferred_element_type=F32)
    x = _swiglu_residual(x, gn_ref, w1_ref, w2_ref)
    y_ref[...] = _rms(x, gf_ref[...]) if final else x
    if cast:
        _weights_epilogue(i, publish, 3)


def _post(x, oa, ob, gn, gf, wo, w1, w2, final, cast):
    n = x.shape[0]
    tm = min(TOKEN_TILE, n)
    row = lambda i: (i, 0)
    w_specs, w_out_specs, w_out_shapes, scratch = _weight_plumbing([wo, w1, w2], cast)
    out = pl.pallas_call(
        functools.partial(_post_kernel, final=final, cast=cast),
        grid=(n // tm,),
        in_specs=[pl.BlockSpec((tm, D_MODEL), row), pl.BlockSpec((tm, D_A), row), pl.BlockSpec((tm, D_B), row),
                  _resident((1, D_MODEL)), _resident((1, D_MODEL))] + w_specs,
        out_specs=[pl.BlockSpec((tm, D_MODEL), row)] + w_out_specs,
        out_shape=[jax.ShapeDtypeStruct((n, D_MODEL), F32)] + w_out_shapes,
        scratch_shapes=scratch,
        compiler_params=_params(1),
        name="post",
    )(x, oa, ob, gn, gf, wo, w1, w2)
    return out if cast else out[0]


PROJ_GATE_COL = 3 * D_A
PROJ_B_COL = PROJ_GATE_COL + LANES
PROJ_COLS = PROJ_B_COL + 3 * D_B


def _proj_weights_prologue(i, wt_hbm, w16_ref, w16_hbm, stage_ref, stage_f_ref, sem_in, sem_out):
    rows = stage_ref.shape[1]

    def publish(k):
        return pltpu.make_async_copy(w16_ref, w16_hbm, sem_out.at[k])

    @pl.when(i == 0)
    def _():
        jobs = [(r0 + c * rows, c0 + c * rows) for r0, c0 in ((0, 0), (PROJ_GATE_COL + N_A, PROJ_B_COL))
                for c in range(PROJ_GATE_COL // rows)]

        def copy(n):
            return pltpu.make_async_copy(wt_hbm.at[pl.ds(jobs[n][0], rows), :], stage_ref.at[n % 2], sem_in.at[n % 2])

        gate = pltpu.make_async_copy(wt_hbm.at[pl.ds(PROJ_GATE_COL, N_A), :], stage_f_ref.at[pl.ds(0, N_A), :],
                                     sem_in.at[2])
        gate.start()
        copy(0).start()
        for n, (_, col) in enumerate(jobs):
            if n + 1 < len(jobs):
                copy(n + 1).start()
            copy(n).wait()
            w16_ref[:, col:col + rows] = stage_ref[n % 2].T.astype(BF16)
        stage_f_ref[N_A:, :] = jnp.zeros((LANES - N_A, D_MODEL), F32)
        gate.wait()
        w16_ref[:, PROJ_GATE_COL:PROJ_B_COL] = stage_f_ref[...].T.astype(BF16)
        publish(0).start()

    return publish


def _proj_kernel(x_ref, gn_ref, bf_ref, *refs, seg, kblk, tokens_minor, cast):
    n_out = 11
    if cast:
        i = pl.program_id(0)
        wt_hbm, out_refs, w16_hbm = refs[0], refs[1:1 + n_out], refs[1 + n_out]
        w_ref, stage_ref, stage_f_ref, sem_in, sem_out = refs[2 + n_out:]
        publish = _proj_weights_prologue(i, wt_hbm, w_ref, w16_hbm, stage_ref, stage_f_ref, sem_in, sem_out)
    else:
        w_ref, out_refs = refs[0], refs[1:]
    tm = x_ref.shape[0]
    h = _rms(x_ref[...], gn_ref[...]).astype(BF16)
    p = jnp.dot(h, w_ref[...], preferred_element_type=F32)
    pa, pb = p[:, :PROJ_GATE_COL], p[:, PROJ_B_COL:]
    z = p[:, PROJ_GATE_COL:PROJ_B_COL] + bf_ref[...]
    lft = (jnp.minimum(z, 0.0) - jnp.log1p(jnp.exp(-jnp.abs(z)))).T
    ka, va = pa[:, D_A:2 * D_A], pa[:, 2 * D_A:]
    kb, vb = pb[:, D_B:2 * D_B], pb[:, 2 * D_B:]
    qa16 = (pa[:, :D_A] * (HD_A ** -0.5 * LOG2E)).astype(BF16)
    qb16 = (pb[:, :D_B] * (HD_B ** -0.5 * LOG2E)).astype(BF16)
    if tokens_minor:
        kat_ref, vat_ref, lft_ref, kbt_ref, vb_ref, qa16_ref, kat16_ref, va16_ref, qb16_ref, kbt16_ref, vb16_ref = out_refs
        kat, vat, kbt = ka.T, va.T, kb.T
    else:
        ka_ref, va_ref, lft_ref, kb_ref, vb_ref, qa16_ref, ka16_ref, va16_ref, qb16_ref, kb16_ref, vb16_ref = out_refs
        ka16_ref[...] = ka.astype(BF16)
        kb16_ref[...] = kb.astype(BF16)
        for hh in range(N_A):
            cols = slice(hh * HD_A, (hh + 1) * HD_A)
            ka_ref[pl.ds(hh, tm, stride=N_A), :] = ka[:, cols]
            va_ref[pl.ds(hh, tm, stride=N_A), :] = va[:, cols]
            kb_ref[pl.ds(hh, tm, stride=N_A), :] = kb[:, cols]
    qa16_ref[...] = qa16
    qb16_ref[...] = qb16
    va16_ref[...] = va.astype(BF16)
    vb16_ref[...] = vb.astype(BF16)
    for hh in range(N_B):
        vb_ref[pl.ds(hh, tm, stride=N_B), :] = vb[:, hh * LANES:(hh + 1) * LANES]
    for bb in range(tm // seg):
        cols = slice(bb * seg, (bb + 1) * seg)
        lft_ref[bb] = lft[:N_A, cols]
        if tokens_minor:
            kat_ref[bb] = kat[:, cols]
            vat_ref[bb] = vat[:, cols]
            kbt_ref[bb] = kbt[:, cols]
            for jj in range(seg // kblk):
                kc = slice(bb * seg + jj * kblk, bb * seg + (jj + 1) * kblk)
                kat16_ref[bb, jj] = kat[:, kc].astype(BF16)
                kbt16_ref[bb, jj] = kbt[:, kc].astype(BF16)
    if cast:
        _weights_epilogue(i, publish, 1)


def _proj(x, gn, bf, weight, b, t, tokens_minor):
    cast = weight.dtype != BF16
    n = x.shape[0]
    tm = min(PROJ_TILE if tokens_minor else TOKEN_TILE, n)
    seg = min(tm, t)
    nb = tm // seg
    nt = t // seg
    kblk = min(ATTN_BLOCK, seg)
    row = lambda i: (i, 0)
    tok = lambda i: (i // nt, 0, i % nt)
    blk = lambda i: (i // nt, i % nt, 0, 0)
    b16 = (jax.ShapeDtypeStruct((n, D_A), BF16), pl.BlockSpec((tm, D_A), row))
    lft = (jax.ShapeDtypeStruct((b, N_A, t), F32), pl.BlockSpec((nb, N_A, seg), tok))
    vb = (jax.ShapeDtypeStruct((n * N_B, LANES), F32), pl.BlockSpec((tm * N_B, LANES), row))
    if tokens_minor:
        kv = (jax.ShapeDtypeStruct((b, D_A, t), F32), pl.BlockSpec((nb, D_A, seg), tok))
        k16 = (jax.ShapeDtypeStruct((b, t // kblk, D_A, kblk), BF16),
               pl.BlockSpec((nb, seg // kblk, D_A, kblk), blk))
    else:
        kv = (jax.ShapeDtypeStruct((n * N_A, HD_A), F32), pl.BlockSpec((tm * N_A, HD_A), row))
        k16 = b16
    outs = [kv, kv, lft, kv, vb, b16, k16, b16, b16, k16, b16]
    w_shape = (D_MODEL, PROJ_COLS)
    if cast:
        any_spec = pl.BlockSpec(memory_space=pl.ANY)
        w_spec = any_spec
        outs = outs + [(jax.ShapeDtypeStruct(w_shape, BF16), any_spec)]
        scratch = [pltpu.VMEM(w_shape, BF16), pltpu.VMEM((2, 3 * D_A // 2, D_MODEL), F32),
                   pltpu.VMEM((LANES, D_MODEL), F32), pltpu.SemaphoreType.DMA((3,)), pltpu.SemaphoreType.DMA((1,))]
    else:
        w_spec = _resident(w_shape)
        scratch = []
    res = pl.pallas_call(
        functools.partial(_proj_kernel, seg=seg, kblk=kblk, tokens_minor=tokens_minor, cast=cast),
        grid=(n // tm,),
        in_specs=[pl.BlockSpec((tm, D_MODEL), row), _resident((1, D_MODEL)), _resident(bf.shape), w_spec],
        out_specs=[o[1] for o in outs],
        out_shape=[o[0] for o in outs],
        scratch_shapes=scratch,
        compiler_params=_params(1),
        name="proj",
    )(x, gn, bf, weight)
    return res[:11], (res[11] if cast else weight)


def _split3(x):
    hi = x.astype(BF16).astype(F32)
    r = x - hi
    mid = r.astype(BF16).astype(F32)
    lo = (r - mid).astype(BF16).astype(F32)
    return hi, mid, lo


def _cum_kernel(lft_ref, aug_ref, pad_ref, *, t, blocked):
    w = ATTN_BLOCK
    nb = lft_ref.shape[0]
    rows = nb * N_A
    pad_ref[...] = jnp.zeros(pad_ref.shape, F32)
    for bb in range(nb):
        pad_ref[bb * N_A:(bb + 1) * N_A, 0:t] = lft_ref[bb]
    r = lax.broadcasted_iota(jnp.int32, (w, w), 0)
    c = lax.broadcasted_iota(jnp.int32, (w, w), 1)
    tri = jnp.where(r <= c, 1.0, 0.0).astype(BF16)
    zero = jnp.zeros((N_A, w), F32)
    carry = jnp.zeros((rows, 1), F32)
    for s in range(pad_ref.shape[1] // w):
        lo_col = s * w
        cols = min(w, t - lo_col)
        x = jnp.concatenate(_split3(pad_ref[:, lo_col:lo_col + w]), axis=0).astype(BF16)
        y = jnp.dot(x, tri, preferred_element_type=F32)
        cs = y[0:rows] + y[rows:2 * rows] + y[2 * rows:3 * rows] + carry
        carry = cs[:, w - 1:w]
        pieces = _split3(cs * -LOG2E)
        for bb in range(nb):
            sl = slice(bb * N_A, (bb + 1) * N_A)
            aug = jnp.concatenate([p[sl] for p in pieces] + [zero], axis=0).astype(BF16)
            if blocked:
                aug_ref[bb, s] = aug
            else:
                aug_ref[bb, :, lo_col:lo_col + cols] = aug[:, :cols]


def _cum_aug(lft, blocked):
    b, _, t = lft.shape
    w = ATTN_BLOCK
    t_pad = -(-t // w) * w
    nb = math.gcd(b, SCAN_BATCH)
    if blocked:
        out_shape = jax.ShapeDtypeStruct((b, t // w, AUG_ROWS, w), BF16)
        out_spec = pl.BlockSpec((nb, t // w, AUG_ROWS, w), lambda i: (i, 0, 0, 0))
    else:
        out_shape = jax.ShapeDtypeStruct((b, AUG_ROWS, t), BF16)
        out_spec = pl.BlockSpec((nb, AUG_ROWS, t), lambda i: (i, 0, 0))
    return pl.pallas_call(
        functools.partial(_cum_kernel, t=t, blocked=blocked),
        grid=(b // nb,),
        in_specs=[pl.BlockSpec((nb, N_A, t), lambda i: (i, 0, 0))],
        out_specs=out_spec,
        out_shape=out_shape,
        scratch_shapes=[pltpu.VMEM((nb * N_A, t_pad), F32)],
        compiler_params=_params(1),
        name="cum_aug",
    )(lft)


def _lane_halves(q):
    lane = lax.broadcasted_iota(jnp.int32, q.shape, 1)
    zero = jnp.zeros_like(q)
    return jnp.where(lane < HD_A, q, zero), jnp.where(lane >= HD_A, q, zero)


def _piece_selector(shape, head, n_heads):
    lane = lax.broadcasted_iota(jnp.int32, shape, 1)
    hit = (lane == head) | (lane == n_heads + head) | (lane == 2 * n_heads + head)
    return jnp.where(hit, 1.0, 0.0).astype(BF16)


def _alibi_rows(k_pos):
    w = k_pos.shape[1]
    row = lax.broadcasted_iota(jnp.int32, (4 * N_B, 1), 0)
    head = row % N_B
    slope = jnp.zeros((4 * N_B, 1), F32)
    for h in range(N_B):
        slope = jnp.where(head == h, ALIBI_SLOPES[h] * LOG2E, slope)
    hi, mid, lo = _split3(slope * k_pos)
    piece = jnp.where(row < N_B, hi, jnp.where(row < 2 * N_B, mid, jnp.where(row < 3 * N_B, lo, 0.0)))
    return jnp.concatenate([piece.astype(BF16), jnp.zeros((LANES - 4 * N_B, w), BF16)], axis=0)


def _alibi_fold(slope, q_pos, k_pos):
    return (-2.0 * LOG2E * slope) * jnp.maximum(k_pos - q_pos, 0).astype(F32)


def _diff_finish(o1, o2, lam, sub_ref, lambda_init):
    return _rms(o1 - lam * o2, sub_ref[...]) * (1.0 - lambda_init)


def _lambda(lq1_ref, lk1_ref, lq2_ref, lk2_ref, lambda_init):
    return (jnp.exp(jnp.sum(lq1_ref[...] * lk1_ref[...], axis=1, keepdims=True))
            - jnp.exp(jnp.sum(lq2_ref[...] * lk2_ref[...], axis=1, keepdims=True)) + lambda_init)


def _attn_prompt_kernel(qa_ref, qb_ref, kat_ref, kbt_ref, va_ref, vb_ref, aug_ref,
                        lq1_ref, lk1_ref, lq2_ref, lk2_ref, sub_ref, oa_ref, ob_ref,
                        qs_ref, m_ref, acc_ref, tbl_ref, *, lambda_init):
    tq = ATTN_BLOCK
    n_chain = N_PAIRS + N_B

    @pl.when(pl.program_id(0) == 0)
    def _():
        r = lax.broadcasted_iota(jnp.int32, (2 * tq, tq), 0)
        c = lax.broadcasted_iota(jnp.int32, (2 * tq, tq), 1)
        q_idx = jnp.where(r >= tq, r - tq, r)
        tbl_ref[0] = jnp.where(q_idx >= c, 0.0, NEG_INF)
        visible = (q_idx // CHUNK) >= (c // CHUNK)
        for h in range(N_B):
            tbl_ref[1 + h] = jnp.where(visible, _alibi_fold(ALIBI_SLOPES[h], q_idx, c), NEG_INF)
        for qset in range(Q_BLOCKS_PER_STEP):
            for hc in range(n_chain):
                ch = qset * n_chain + hc
                if hc < N_PAIRS:
                    qs_ref[ch, 0:tq, LANES:] = _piece_selector((tq, LANES), 2 * hc, N_A)
                    qs_ref[ch, tq:, LANES:] = _piece_selector((tq, LANES), 2 * hc + 1, N_A)
                else:
                    sel = _piece_selector((tq, LANES), hc - N_PAIRS, N_B)
                    qs_ref[ch, 0:tq, LANES:] = sel
                    qs_ref[ch, tq:, LANES:] = sel

    def step(sets, first, width=1):
        wk = width * tq
        ones = jnp.ones((wk, LANES), BF16)
        for qset, j in sets:
            rows = pl.ds(pl.multiple_of(j * tq, tq), wk)
            blocks = [j + d for d in range(width)]
            aug_a = jnp.concatenate([jnp.concatenate([aug_ref[0, jb] for jb in blocks], axis=1),
                                     jnp.zeros((LANES - AUG_ROWS, wk), BF16)], axis=0)
            aug_b = _alibi_rows((j * tq + lax.broadcasted_iota(jnp.int32, (1, wk), 1)).astype(F32))
            for hc in range(n_chain):
                ch = qset * n_chain + hc
                if hc < N_PAIRS:
                    kt_ref, v_ref, g, aug, tbl = kat_ref, va_ref, hc, aug_a, 0
                else:
                    kt_ref, v_ref, g, aug, tbl = kbt_ref, vb_ref, hc - N_PAIRS, aug_b, 1 + hc - N_PAIRS
                kt = jnp.concatenate([kt_ref[0, jb, g * LANES:(g + 1) * LANES, :] for jb in blocks], axis=1)
                s = jnp.dot(qs_ref[ch], jnp.concatenate([kt, aug], axis=0), preferred_element_type=F32)
                v = v_ref[rows, g * LANES:(g + 1) * LANES]
                if first:
                    s = s + tbl_ref[tbl]
                    m_new = jnp.broadcast_to(jnp.max(s, axis=1, keepdims=True), (2 * tq, LANES))
                else:
                    m_prev = m_ref[ch]
                    m_new = jnp.maximum(m_prev, jnp.max(s, axis=1, keepdims=True))
                p = jnp.exp2(s - jnp.concatenate([m_new] * (wk // LANES), axis=1))
                pv = jnp.dot(p.astype(BF16), jnp.concatenate([v, ones], axis=1), preferred_element_type=F32)
                if first:
                    acc_ref[ch] = pv
                else:
                    alpha = jnp.exp2(m_prev - m_new)
                    acc_ref[ch] = jnp.concatenate([alpha, alpha], axis=1) * acc_ref[ch] + pv
                m_ref[ch] = m_new

    def wide_body(jj, carry):
        j = jj * KEY_BLOCKS_PER_STEP
        step([(0, j), (1, j)], False, KEY_BLOCKS_PER_STEP)
        return carry

    lane = lax.broadcasted_iota(jnp.int32, (tq, LANES), 1)
    lam = _lambda(lq1_ref, lk1_ref, lq2_ref, lk2_ref, lambda_init)

    def query_blocks(pair, carry):
        even = pair * Q_BLOCKS_PER_STEP
        for qset in range(Q_BLOCKS_PER_STEP):
            q_rows = pl.ds(pl.multiple_of((even + qset) * tq, tq), tq)
            for hc in range(n_chain):
                q_ref, g = (qa_ref, hc) if hc < N_PAIRS else (qb_ref, hc - N_PAIRS)
                lo, hi = _lane_halves(q_ref[q_rows, g * LANES:(g + 1) * LANES])
                qs_ref[qset * n_chain + hc, 0:tq, 0:LANES] = lo
                qs_ref[qset * n_chain + hc, tq:, 0:LANES] = hi
        step([(0, even), (1, even + 1)], True)
        step([(1, even)], False)
        lax.fori_loop(0, pair, wide_body, 0)
        for qset in range(Q_BLOCKS_PER_STEP):
            q_rows = pl.ds(pl.multiple_of((even + qset) * tq, tq), tq)
            for hc in range(n_chain):
                ch = qset * n_chain + hc
                if hc < N_PAIRS:
                    picked = jnp.where(jnp.concatenate([lane, lane], axis=1) < HD_A,
                                       acc_ref[ch, 0:tq, :], acc_ref[ch, tq:, :])
                    oa_ref[q_rows, hc * LANES:(hc + 1) * LANES] = (picked[:, :LANES] / picked[:, LANES:]).astype(BF16)
                else:
                    o = acc_ref[ch, :, 0:LANES] / acc_ref[ch, :, LANES:]
                    h = hc - N_PAIRS
                    ob_ref[q_rows, h * LANES:(h + 1) * LANES] = _diff_finish(
                        o[:tq], o[tq:], lam, sub_ref, lambda_init).astype(BF16)
        return carry

    lax.fori_loop(0, qa_ref.shape[0] // (Q_BLOCKS_PER_STEP * tq), query_blocks, 0)


def _attn_prompt(qa16, qb16, kat16, kbt16, va16, vb16, aug, lams, sub, b, t, lambda_init):
    tq = ATTN_BLOCK
    nq = t // tq
    kspec = pl.BlockSpec((1, nq, D_A, tq), lambda bi: (bi, 0, 0, 0))
    vspec = pl.BlockSpec((t, D_A), lambda bi: (bi, 0))
    small = [_resident((1, HD_B))] * 4 + [_resident((1, 2 * HD_B))]
    n_chain = Q_BLOCKS_PER_STEP * (N_PAIRS + N_B)
    return pl.pallas_call(
        functools.partial(_attn_prompt_kernel, lambda_init=lambda_init),
        grid=(b,),
        in_specs=[vspec, vspec, kspec, kspec, vspec, vspec,
                  pl.BlockSpec((1, nq, AUG_ROWS, tq), lambda bi: (bi, 0, 0, 0))] + small,
        out_specs=[vspec, vspec],
        out_shape=[jax.ShapeDtypeStruct((b * t, D_A), BF16), jax.ShapeDtypeStruct((b * t, D_B), BF16)],
        scratch_shapes=[pltpu.VMEM((n_chain, 2 * tq, 2 * LANES), BF16),
                        pltpu.VMEM((n_chain, 2 * tq, LANES), F32),
                        pltpu.VMEM((n_chain, 2 * tq, 2 * LANES), F32),
                        pltpu.VMEM((1 + N_B, 2 * tq, tq), F32)],
        compiler_params=_params(1),
        name="attn_prompt",
    )(qa16, qb16, kat16, kbt16, va16, vb16, aug, *lams, sub)


def _nt_dot(a, b):
    return lax.dot_general(a, b, (((1,), (1,)), ((), ())), preferred_element_type=F32)


def _attn_cached_kernel(qa_ref, qb_ref, ckat_ref, cvat_ref, ckbt_ref, cvb_ref,
                        ka_ref, va_ref, kb_ref, vb_ref, aug_ref,
                        lq1_ref, lk1_ref, lq2_ref, lk2_ref, sub_ref, oa_ref, ob_ref,
                        kn_ref, *, lambda_init):
    tq = qa_ref.shape[0]
    past = ckat_ref.shape[2]
    t_all = past + tq
    n_chain = N_PAIRS + N_B
    r = lax.broadcasted_iota(jnp.int32, (2 * tq, LANES), 0)
    c = lax.broadcasted_iota(jnp.int32, (2 * tq, LANES), 1)
    q_idx = jnp.where(r >= tq, r - tq, r)
    is_key = c < tq
    causal_tbl = jnp.where((q_idx >= c) & is_key, 0.0, NEG_INF)
    visible = (((past + q_idx) // CHUNK) >= ((past + c) // CHUNK)) & is_key
    lane = lax.broadcasted_iota(jnp.int32, (tq, LANES), 1)
    lam = _lambda(lq1_ref, lk1_ref, lq2_ref, lk2_ref, lambda_init)
    aug_c = jnp.concatenate([aug_ref[0, :, 0:past], jnp.zeros((LANES - AUG_ROWS, past), BF16)], axis=0)
    alibi_c = _alibi_rows(lax.broadcasted_iota(jnp.int32, (1, past), 1).astype(F32))
    alibi_n = _alibi_rows((past + lax.broadcasted_iota(jnp.int32, (1, LANES), 1)).astype(F32))
    kn_ref[...] = jnp.zeros(kn_ref.shape, BF16)
    row_pad = jnp.zeros((LANES - tq, LANES), BF16)

    def new_keys_t(k_new):
        return jnp.concatenate([k_new, row_pad], axis=0).astype(F32).T.astype(BF16)

    probs = []
    for ch in range(n_chain):
        fox = ch < N_PAIRS
        g = ch if fox else ch - N_PAIRS
        rows = slice(g * LANES, (g + 1) * LANES)
        if fox:
            lo, hi = _lane_halves(qa_ref[:, rows])
            sel_lo = _piece_selector((tq, LANES), 2 * g, N_A)
            sel_hi = _piece_selector((tq, LANES), 2 * g + 1, N_A)
            k_old = jnp.concatenate([ckat_ref[0, rows, :].astype(BF16), aug_c], axis=0)
            kn_ref[ch, 0:LANES, :] = new_keys_t(ka_ref[:, rows])
            kn_ref[ch, LANES:LANES + AUG_ROWS, 0:tq] = aug_ref[0, :, past:t_all]
            tbl = causal_tbl
        else:
            lo, hi = _lane_halves(qb_ref[:, rows])
            sel_lo = sel_hi = _piece_selector((tq, LANES), g, N_B)
            k_old = jnp.concatenate([ckbt_ref[0, rows, :].astype(BF16), alibi_c], axis=0)
            kn_ref[ch, 0:LANES, :] = new_keys_t(kb_ref[:, rows])
            kn_ref[ch, LANES:, :] = alibi_n
            tbl = jnp.where(visible, _alibi_fold(ALIBI_SLOPES[g], q_idx, c), NEG_INF)
        qs = jnp.concatenate([jnp.concatenate([lo, sel_lo], axis=1), jnp.concatenate([hi, sel_hi], axis=1)], axis=0)
        s_old = jnp.dot(qs, k_old, preferred_element_type=F32)
        s_new = jnp.dot(qs, kn_ref[ch], preferred_element_type=F32) + tbl
        m = jnp.maximum(jnp.max(s_old, axis=1, keepdims=True), jnp.max(s_new, axis=1, keepdims=True))
        p_old = jnp.exp2(s_old - m)
        p_new = jnp.exp2(s_new - m)
        l = jnp.sum(p_old, axis=1, keepdims=True) + jnp.sum(p_new, axis=1, keepdims=True)
        probs.append((p_old.astype(BF16), p_new.astype(BF16), l))

    for ch in range(n_chain):
        fox = ch < N_PAIRS
        g = ch if fox else ch - N_PAIRS
        rows = slice(g * LANES, (g + 1) * LANES)
        p_old, p_new, l = probs[ch]
        v_new = jnp.concatenate([(va_ref if fox else vb_ref)[:, rows], row_pad], axis=0)
        o_new = jnp.dot(p_new, v_new, preferred_element_type=F32)
        if fox:
            o = (_nt_dot(p_old, cvat_ref[0, rows, :].astype(BF16)) + o_new) / l
            oa_ref[:, rows] = jnp.where(lane < HD_A, o[:tq], o[tq:]).astype(BF16)
        else:
            v_old = cvb_ref[0, pl.ds(g, past, stride=N_B), :].astype(BF16)
            o = (jnp.dot(p_old, v_old, preferred_element_type=F32) + o_new) / l
            ob_ref[:, rows] = _diff_finish(o[:tq], o[tq:], lam, sub_ref, lambda_init).astype(BF16)


def _attn_cached(qa16, qb16, ckat, cvat, ckbt, cvb, ka16, va16, kb16, vb16, aug, lams, sub, b, tq, lambda_init):
    past = ckat.shape[2]
    t_all = past + tq
    n_chain = N_PAIRS + N_B
    qspec = pl.BlockSpec((tq, D_A), lambda bi: (bi, 0))
    old_t = pl.BlockSpec((1, D_A, past), lambda bi: (bi, 0, 0))
    small = [_resident((1, HD_B))] * 4 + [_resident((1, 2 * HD_B))]
    return pl.pallas_call(
        functools.partial(_attn_cached_kernel, lambda_init=lambda_init),
        grid=(b,),
        in_specs=[qspec, qspec, old_t, old_t, old_t,
                  pl.BlockSpec((1, past * N_B, LANES), lambda bi: (bi, 0, 0)),
                  qspec, qspec, qspec, qspec,
                  pl.BlockSpec((1, AUG_ROWS, t_all), lambda bi: (bi, 0, 0))] + small,
        out_specs=[qspec, qspec],
        out_shape=[jax.ShapeDtypeStruct((b * tq, D_A), BF16), jax.ShapeDtypeStruct((b * tq, D_B), BF16)],
        scratch_shapes=[pltpu.VMEM((n_chain, 2 * LANES, LANES), BF16)],
        compiler_params=_params(1),
        name="attn_cached",
    )(qa16, qb16, ckat, cvat, ckbt, cvb, ka16, va16, kb16, vb16, aug, *lams, sub)


def _tokens_minor(x):
    b, t = x.shape[:2]
    return jnp.moveaxis(x.reshape(b, t, -1), 1, 2)


def _tokens_major(xt, tail):
    b, _, t = xt.shape
    return jnp.moveaxis(xt, 2, 1).reshape(b, t, *tail)


def _layer(x, cache, w, lambda_init, final):
    b, t, _ = x.shape
    n = b * t
    cast = w["ffn16"] is None
    if cast:
        x1, w1a, w2a = _ffn1(x.reshape(n, D_MODEL), w["gn1"], w["w1a"], w["w2a"], cast=True)
    else:
        w1a, w2a, wo, w1b, w2b = w["ffn16"]
        x1 = _ffn1(x.reshape(n, D_MODEL), w["gn1"], w1a, w2a, cast=False)
    (k_a, v_a, lft, k_b, vb, qa16, ka16, va16, qb16, kb16, vb16), w_in16 = _proj(
        x1, w["gmix"], w["bf"], w["w_in_t"] if cast else w["w_in16"], b, t, tokens_minor=cache is None)
    if cast:
        w["w_in16"] = w_in16
    if cache is None:
        aug = _cum_aug(lft, blocked=True)
        oa, ob = _attn_prompt(qa16, qb16, ka16, kb16, va16, vb16, aug, w["lams"], w["sub"], b, t, lambda_init)
        news = (_tokens_major(k_a, (N_A, HD_A)), _tokens_major(v_a, (N_A, HD_A)), _tokens_major(lft, (N_A,)),
                _tokens_major(k_b, (N_B, 2, HD_B)), vb.reshape(b, t, N_B, 2 * HD_B))
    else:
        cfk, cfv, cflf, cdk, cdv = cache
        past = cfk.shape[1]
        aug = _cum_aug(jnp.concatenate([_tokens_minor(cflf.astype(F32)), lft], axis=2), blocked=False)
        oa, ob = _attn_cached(qa16, qb16, _tokens_minor(cfk), _tokens_minor(cfv), _tokens_minor(cdk),
                              cdv.reshape(b, past * N_B, 2 * HD_B), ka16, va16, kb16, vb16, aug,
                              w["lams"], w["sub"], b, t, lambda_init)
        news = (k_a.reshape(b, t, N_A, HD_A), v_a.reshape(b, t, N_A, HD_A), _tokens_major(lft, (N_A,)),
                k_b.reshape(b, t, N_B, 2, HD_B), vb.reshape(b, t, N_B, 2 * HD_B))
    if cast:
        y, wo, w1b, w2b = _post(x1, oa, ob, w["gn2"], w["gfin"], w["wo"], w["w1b"], w["w2b"], final, cast=True)
        w["ffn16"] = (w1a, w2a, wo, w1b, w2b)
    else:
        y = _post(x1, oa, ob, w["gn2"], w["gfin"], wo, w1b, w2b, final, cast=False)
    return y.reshape(b, t, D_MODEL), news


def kernel(x_prompt, x_sample, cache_fox_k, cache_fox_v, cache_fox_logf, cache_diff_k, cache_diff_v,
           norm_ffn1, w_ffn1_in, w_ffn1_out, norm_mix, w_in, b_forget,
           lambda_q1, lambda_k1, lambda_q2, lambda_k2, diff_subln, w_out,
           norm_ffn2, w_ffn2_in, w_ffn2_out, norm_final):
    depth = w_in.shape[0]
    xp, xs = x_prompt, x_sample
    outs_p, outs_s = [], []
    gfin = norm_final.reshape(1, D_MODEL)
    for l in range(depth):
        lambda_init = 0.8 - 0.6 * math.exp(-0.3 * l)
        w = {
            "gn1": norm_ffn1[l].reshape(1, D_MODEL),
            "w1a": w_ffn1_in[l], "w2a": w_ffn1_out[l],
            "gmix": norm_mix[l].reshape(1, D_MODEL),
            "w_in_t": jnp.swapaxes(w_in[l], 0, 1),
            "bf": jnp.pad(b_forget[l], (0, LANES - N_A)).reshape(1, LANES),
            "lams": [v[l].reshape(1, HD_B) for v in (lambda_q1, lambda_k1, lambda_q2, lambda_k2)],
            "sub": diff_subln[l].reshape(1, 2 * HD_B),
            "wo": w_out[l],
            "gn2": norm_ffn2[l].reshape(1, D_MODEL),
            "w1b": w_ffn2_in[l], "w2b": w_ffn2_out[l],
            "gfin": gfin,
            "ffn16": None,
        }
        streams = []
        for x, cache in ((xp, None),
                         (xs, (cache_fox_k[l], cache_fox_v[l], cache_fox_logf[l],
                               cache_diff_k[l], cache_diff_v[l]))):
            streams.append(_layer(x, cache, w, lambda_init, l == depth - 1))
        (xp, news_p), (xs, news_s) = streams
        outs_p.append(news_p)
        outs_s.append(news_s)
    stack = lambda outs, i: jnp.stack([o[i] for o in outs])
    return (xp, xs) + tuple(stack(outs_p, i) for i in range(5)) + tuple(stack(outs_s, i) for i in range(5))
```
